```python
import math
import jax, jax.numpy as jnp
from jax import lax
import numpy as np

D_MODEL = 1024
BATCH = 4
SEQ = 8192
DEPTH = 2

HEAD_DIM = 64
RWKV_HEADS = (D_MODEL // 2) // HEAD_DIM
FOX_HEADS = (D_MODEL // 2) // HEAD_DIM
RWKV_WIDTH = RWKV_HEADS * HEAD_DIM
FOX_WIDTH = FOX_HEADS * HEAD_DIM
MIX_WIDTH = RWKV_WIDTH + FOX_WIDTH
DECAY_LORA = int(max(32, round(1.8 * RWKV_WIDTH ** 0.5 / 32) * 32))
AAA_LORA = int(max(32, round(1.8 * RWKV_WIDTH ** 0.5 / 32) * 32))
GATE_LORA = int(max(32, round(0.6 * RWKV_WIDTH ** 0.8 / 32) * 32))
RWKV_COLS = 3 * RWKV_WIDTH + DECAY_LORA + AAA_LORA + GATE_LORA
FOX_COLS = 3 * FOX_WIDTH + FOX_HEADS
IN_COLS = RWKV_COLS + FOX_COLS
FOX_BLOCK = 128
CONV_WIDTH = 31
D_FF = 2816
N_EXPERTS = 8
TOP_K = 2
D_FF_EXPERT = 3584
LN_EPS = 1e-5
GN_EPS = 64e-5
DEEPNORM_ALPHA = (2.0 * DEPTH) ** 0.25
DEEPNORM_BETA = (8.0 * DEPTH) ** -0.25
N_EVEN = (DEPTH + 1) // 2
N_ODD = DEPTH // 2

kernel_name = "hybrid_rwkv7_fox_conformer_moe_deepnorm"


def layer_norm(x, g, b, eps=LN_EPS):
    xf = x.astype(jnp.float32)
    mu = jnp.mean(xf, axis=-1, keepdims=True)
    var = jnp.mean(jnp.square(xf - mu), axis=-1, keepdims=True)
    return ((xf - mu) * lax.rsqrt(var + eps) * g + b).astype(x.dtype)


def token_shift(p, mu):
    prev = jnp.pad(p, ((0, 0), (1, 0), (0, 0)))[:, :-1]
    return p + mu * (prev - p)


def rwkv7_time_mix(p, mu, w0, w_up, a0, a_up, g_up, k_k, k_a, r_k, gn_g, gn_b):
    B, T, _ = p.shape
    f32 = jnp.float32
    p = token_shift(p, mu)
    o1 = RWKV_WIDTH
    o2 = 2 * RWKV_WIDTH
    o3 = 3 * RWKV_WIDTH
    o4 = o3 + DECAY_LORA
    o5 = o4 + AAA_LORA
    r, k, v, wd, ad, gd = jnp.split(p, [o1, o2, o3, o4, o5], axis=-1)
    w = -jax.nn.softplus(-(w0 + jnp.tanh(wd) @ w_up).astype(f32)) - 0.5
    decay = jnp.exp(-jnp.exp(w))
    a = jax.nn.sigmoid((a0 + ad @ a_up).astype(f32))
    g = jax.nn.sigmoid(gd) @ g_up

    hs = lambda t: t.astype(f32).reshape(B, T, RWKV_HEADS, HEAD_DIM)
    r, k, v, decay, a = hs(r), hs(k), hs(v), hs(decay), hs(a)
    kk = k * k_k
    kk = kk / jnp.maximum(jnp.linalg.norm(kk, axis=-1, keepdims=True), 1e-12)
    k = k * (1.0 + (a - 1.0) * k_a)
    b = kk * a

    def step(S, inp):
        r_t, w_t, k_t, v_t, kk_t, b_t = inp
        sa = jnp.einsum('bhij,bhj->bhi', S, -kk_t)
        S = (S * w_t[:, :, None, :] + sa[..., None] * b_t[:, :, None, :]
             + v_t[..., None] * k_t[:, :, None, :])
        y_t = jnp.einsum('bhij,bhj->bhi', S, r_t)
        return S, y_t

    tm = lambda t: jnp.transpose(t, (1, 0, 2, 3))
    S0 = jnp.zeros((B, RWKV_HEADS, HEAD_DIM, HEAD_DIM), f32)
    _, ys = lax.scan(step, S0, (tm(r), tm(decay), tm(k), tm(v), tm(kk), tm(b)))
    y = jnp.transpose(ys, (1, 0, 2, 3))

    mu_y = jnp.mean(y, axis=-1, keepdims=True)
    var_y = jnp.mean(jnp.square(y - mu_y), axis=-1, keepdims=True)
    y = (y - mu_y) * lax.rsqrt(var_y + GN_EPS) * gn_g + gn_b
    y = y + jnp.sum(r * k * r_k, axis=-1, keepdims=True) * v
    y = y.reshape(B, T, RWKV_WIDTH) * g.astype(f32)
    return y.astype(p.dtype)


def forgetting_attention(p, b_f):
    B, T, _ = p.shape
    f32 = jnp.float32
    q, k, v, f_logit = jnp.split(p, [FOX_WIDTH, 2 * FOX_WIDTH, 3 * FOX_WIDTH], axis=-1)
    heads = lambda t: jnp.transpose(t.reshape(B, T, FOX_HEADS, HEAD_DIM), (0, 2, 1, 3))
    q, k, v = heads(q), heads(k), heads(v)
    log_f = jax.nn.log_sigmoid(f_logit.astype(f32) + b_f)
    c = jnp.transpose(jnp.cumsum(log_f, axis=1), (0, 2, 1))
    scale = 1.0 / math.sqrt(HEAD_DIM)
    key_pos = jnp.arange(T)

    def block(i):
        start = i * FOX_BLOCK
        qb = lax.dynamic_slice_in_dim(q, start, FOX_BLOCK, axis=2)
        cb = lax.dynamic_slice_in_dim(c, start, FOX_BLOCK, axis=2)
        s = jnp.einsum('bhqd,bhkd->bhqk', qb, k, preferred_element_type=f32) * scale
        s = s + cb[..., None] - c[:, :, None, :]
        q_pos = start + jnp.arange(FOX_BLOCK)
        s = jnp.where(key_pos[None, :] <= q_pos[:, None], s, -jnp.inf)
        pr = jax.nn.softmax(s, axis=-1)
        return jnp.einsum('bhqk,bhkd->bhqd', pr.astype(v.dtype), v)

    out = lax.map(block, jnp.arange(T // FOX_BLOCK))
    out = jnp.transpose(out, (1, 0, 3, 2, 4)).reshape(B, T, FOX_WIDTH)
    return out


def conformer_conv(x, w_pw1, b_pw1, w_dw, b_dw, ln_g, ln_b, w_pw2, b_pw2):
    h = x @ w_pw1 + b_pw1
    h = h[..., :D_MODEL] * jax.nn.sigmoid(h[..., D_MODEL:])
    h = lax.conv_general_dilated(
        h, w_dw[:, None, :].astype(h.dtype), window_strides=(1,),
        padding=[(CONV_WIDTH - 1, 0)], dimension_numbers=('NWC', 'WIO', 'NWC'),
        feature_group_count=D_MODEL) + b_dw
    h = jax.nn.silu(layer_norm(h, ln_g, ln_b))
    return h @ w_pw2 + b_pw2


def swiglu(x, w_gate, w_up, w_down):
    return (jax.nn.silu(x @ w_gate) * (x @ w_up)) @ w_down


def moe_swiglu(x, w_router, w_gate, w_up, w_down):
    B, T, D = x.shape
    xt = x.reshape(B * T, D)
    logits = (xt @ w_router).astype(jnp.float32)
    top_v, top_i = lax.top_k(logits, TOP_K)
    top_w = jax.nn.softmax(top_v, axis=-1)
    gates = jnp.sum(jax.nn.one_hot(top_i, N_EXPERTS, dtype=jnp.float32) * top_w[..., None], axis=1)
    gates = gates.astype(x.dtype)
    y = jnp.zeros_like(xt)
    for e in range(N_EXPERTS):
        y = y + gates[:, e:e + 1] * swiglu(xt, w_gate[e], w_up[e], w_down[e])
    return y.reshape(B, T, D)


def setup_inputs(seed: int = 0) -> dict:
    key = jax.random.key(seed)
    ks = iter(jax.random.split(key, 64))
    f32 = jnp.float32

    def nrm(shape, scale):
        return jax.random.normal(next(ks), shape, f32) * scale

    def uni(shape, lo, hi):
        return jax.random.uniform(next(ks), shape, f32, lo, hi)

    LE, LO, H, N, D = N_EVEN, N_ODD, RWKV_HEADS, HEAD_DIM, D_MODEL
    beta = DEEPNORM_BETA
    col_scale = jnp.concatenate([
        jnp.ones((2 * RWKV_WIDTH,), f32), jnp.full((RWKV_WIDTH,), beta, f32),
        jnp.ones((DECAY_LORA + AAA_LORA + GATE_LORA,), f32),
        jnp.ones((2 * FOX_WIDTH,), f32), jnp.full((FOX_WIDTH,), beta, f32),
        jnp.ones((FOX_HEADS,), f32)])

    inp = {}
    inp["x"] = nrm((BATCH, SEQ, D), 1.0)
    inp["mix_w_in"] = nrm((LE, D, IN_COLS), D ** -0.5) * col_scale
    inp["rwkv_mu"] = uni((LE, RWKV_COLS), 0.0, 1.0)
    inp["rwkv_w0"] = uni((LE, RWKV_WIDTH), -6.0, -1.0)
    inp["rwkv_w_up"] = nrm((LE, DECAY_LORA, RWKV_WIDTH), 0.5 * DECAY_LORA ** -0.5)
    inp["rwkv_a0"] = nrm((LE, RWKV_WIDTH), 0.1)
    inp["rwkv_a_up"] = nrm((LE, AAA_LORA, RWKV_WIDTH), 0.5 * AAA_LORA ** -0.5)
    inp["rwkv_g_up"] = nrm((LE, GATE_LORA, RWKV_WIDTH), GATE_LORA ** -0.5)
    inp["rwkv_k_k"] = 0.85 + nrm((LE, H, N), 0.05)
    inp["rwkv_k_a"] = 1.0 + nrm((LE, H, N), 0.05)
    inp["rwkv_r_k"] = nrm((LE, H, N), 0.1)
    inp["rwkv_gn_g"] = 1.0 + nrm((LE, H, N), 0.02)
    inp["rwkv_gn_b"] = nrm((LE, H, N), 0.02)
    inp["fox_b_f"] = uni((LE, FOX_HEADS), 1.0, 5.0)
    inp["mix_w_out"] = nrm((LE, MIX_WIDTH, D), beta * MIX_WIDTH ** -0.5)
    inp["mix_ln_g"] = 1.0 + nrm((LE, D), 0.02)
    inp["mix_ln_b"] = nrm((LE, D), 0.02)
    inp["ffn_w_gate"] = nrm((LE, D, D_FF), beta * D ** -0.5)
    inp["ffn_w_up"] = nrm((LE, D, D_FF), beta * D ** -0.5)
    inp["ffn_w_down"] = nrm((LE, D_FF, D), beta * D_FF ** -0.5)
    inp["ffn_ln_g"] = 1.0 + nrm((LE, D), 0.02)
    inp["ffn_ln_b"] = nrm((LE, D), 0.02)
    inp["conv_w_pw1"] = nrm((LO, D, 2 * D), D ** -0.5)
    inp["conv_b_pw1"] = nrm((LO, 2 * D), 0.02)
    inp["conv_w_dw"] = nrm((LO, CONV_WIDTH, D), CONV_WIDTH ** -0.5)
    inp["conv_b_dw"] = nrm((LO, D), 0.02)
    inp["conv_ln_g"] = 1.0 + nrm((LO, D), 0.02)
    inp["conv_ln_b"] = nrm((LO, D), 0.02)
    inp["conv_w_pw2"] = nrm((LO, D, D), beta * D ** -0.5)
    inp["conv_b_pw2"] = nrm((LO, D), 0.02)
    inp["conv_post_ln_g"] = 1.0 + nrm((LO, D), 0.02)
    inp["conv_post_ln_b"] = nrm((LO, D), 0.02)
    inp["moe_w_router"] = nrm((LO, D, N_EXPERTS), D ** -0.5)
    inp["moe_w_gate"] = nrm((LO, N_EXPERTS, D, D_FF_EXPERT), beta * D ** -0.5)
    inp["moe_w_up"] = nrm((LO, N_EXPERTS, D, D_FF_EXPERT), beta * D ** -0.5)
    inp["moe_w_down"] = nrm((LO, N_EXPERTS, D_FF_EXPERT, D), beta * D_FF_EXPERT ** -0.5)
    inp["moe_ln_g"] = 1.0 + nrm((LO, D), 0.02)
    inp["moe_ln_b"] = nrm((LO, D), 0.02)
    return inp


def reference(x, mix_w_in, rwkv_mu, rwkv_w0, rwkv_w_up, rwkv_a0, rwkv_a_up, rwkv_g_up,
              rwkv_k_k, rwkv_k_a, rwkv_r_k, rwkv_gn_g, rwkv_gn_b, fox_b_f, mix_w_out,
              mix_ln_g, mix_ln_b, ffn_w_gate, ffn_w_up, ffn_w_down, ffn_ln_g, ffn_ln_b,
              conv_w_pw1, conv_b_pw1, conv_w_dw, conv_b_dw, conv_ln_g, conv_ln_b,
              conv_w_pw2, conv_b_pw2, conv_post_ln_g, conv_post_ln_b,
              moe_w_router, moe_w_gate, moe_w_up, moe_w_down, moe_ln_g, moe_ln_b):
    alpha = DEEPNORM_ALPHA
    for i in range(DEPTH):
        j = i // 2
        if i % 2 == 0:
            p = x @ mix_w_in[j]
            y_rwkv = rwkv7_time_mix(p[..., :RWKV_COLS], rwkv_mu[j], rwkv_w0[j], rwkv_w_up[j],
                                    rwkv_a0[j], rwkv_a_up[j], rwkv_g_up[j], rwkv_k_k[j],
                                    rwkv_k_a[j], rwkv_r_k[j], rwkv_gn_g[j], rwkv_gn_b[j])
            y_fox = forgetting_attention(p[..., RWKV_COLS:], fox_b_f[j])
            mixed = jnp.concatenate([y_rwkv, y_fox], axis=-1) @ mix_w_out[j]
            x = layer_norm(alpha * x + mixed, mix_ln_g[j], mix_ln_b[j])
            x = layer_norm(alpha * x + swiglu(x, ffn_w_gate[j], ffn_w_up[j], ffn_w_down[j]),
                           ffn_ln_g[j], ffn_ln_b[j])
        else:
            conv = conformer_conv(x, conv_w_pw1[j], conv_b_pw1[j], conv_w_dw[j], conv_b_dw[j],
                                  conv_ln_g[j], conv_ln_b[j], conv_w_pw2[j], conv_b_pw2[j])
            x = layer_norm(alpha * x + conv, conv_post_ln_g[j], conv_post_ln_b[j])
            moe = moe_swiglu(x, moe_w_router[j], moe_w_gate[j], moe_w_up[j], moe_w_down[j])
            x = layer_norm(alpha * x + moe, moe_ln_g[j], moe_ln_b[j])
    return x
```

```python
import functools
import math

import jax
import jax.numpy as jnp
from jax import lax
from jax.experimental import pallas as pl
from jax.experimental.pallas import tpu as pltpu

F32 = jnp.float32
BF16 = jnp.bfloat16
HIGHEST = lax.Precision.HIGHEST

HEAD_DIM = 64
N_HEADS = 8
GROUP_W = N_HEADS * HEAD_DIM
LORA_PAD = 128
DECAY_LORA = 32
AAA_LORA = 32
GATE_LORA = 96
CONV_WIDTH = 31
CONV_HALO = 32
N_EXPERTS = 8
LN_EPS = 1e-5
GN_EPS = 64e-5
DEPTH = 2
ALPHA = (2.0 * DEPTH) ** 0.25
NEG_BIG = -1e30
VMEM_LIMIT = 56 * 1024 * 1024


def _dot(a, b, **kw):
    return jnp.dot(a, b, preferred_element_type=F32, **kw)


def _dot_nt(a, b):
    return lax.dot_general(a, b, (((1,), (1,)), ((), ())), preferred_element_type=F32)


def _dot_tn(a, b):
    return lax.dot_general(a, b, (((0,), (0,)), ((), ())), preferred_element_type=F32)


def _sigmoid(z):
    return 1.0 / (1.0 + jnp.exp(-z))


def _softplus(z):
    return jnp.maximum(z, 0.0) + jnp.log1p(jnp.exp(-jnp.abs(z)))


def _layer_norm(h, g, b):
    mu = jnp.mean(h, axis=-1, keepdims=True)
    d = h - mu
    var = jnp.mean(d * d, axis=-1, keepdims=True)
    return d * lax.rsqrt(var + LN_EPS) * g + b


def _params(*sem):
    return pltpu.CompilerParams(dimension_semantics=sem, vmem_limit_bytes=VMEM_LIMIT)


def _full(shape):
    return pl.BlockSpec(shape, lambda *_: (0,) * len(shape))


def _inproj_kernel(x_ref, wa_ref, wb_ref, wf_ref, mu_ref, bf_ref, tri_ref,
                   pr_ref, qkv_ref, c_ref, last_ref, carry_ref, *, tiles_per_seq):
    i = pl.program_id(0)

    @pl.when(i % tiles_per_seq == 0)
    def _():
        last_ref[...] = jnp.zeros_like(last_ref)
        carry_ref[...] = jnp.zeros_like(carry_ref)

    xb = x_ref[...].astype(BF16)
    tm = xb.shape[0]
    row0 = lax.broadcasted_iota(jnp.int32, (tm, 1), 0) == 0
    ca = wa_ref.shape[1]
    cw = 256 if ca % 256 == 0 else 128
    for c0 in range(0, ca, cw):
        p = _dot(xb, wa_ref[:, c0:c0 + cw])
        prev = jnp.where(row0, last_ref[:, c0:c0 + cw], pltpu.roll(p, 1, 0))
        last_ref[:, c0:c0 + cw] = p[tm - 1:tm, :]
        pr_ref[:, c0:c0 + cw] = p + mu_ref[:, c0:c0 + cw] * (prev - p)
    cb = wb_ref.shape[1]
    for c0 in range(0, cb, GROUP_W):
        qkv_ref[:, c0:c0 + GROUP_W] = _dot(xb, wb_ref[:, c0:c0 + GROUP_W]).astype(BF16)
    fl = _dot(xb, wf_ref[...]) + bf_ref[...]
    log_f = jnp.minimum(fl, 0.0) - jnp.log1p(jnp.exp(-jnp.abs(fl)))
    c = _dot(tri_ref[...], log_f, precision=HIGHEST) + carry_ref[...]
    c_ref[...] = c
    carry_ref[...] = c[tm - 1:tm, :]


def _inproj(x2, wa, wb, wf, mu, bf, *, seq, tm):
    n, d = x2.shape
    ca, cb = wa.shape[1], wb.shape[1]
    tri = (lax.broadcasted_iota(jnp.int32, (tm, tm), 1)
           <= lax.broadcasted_iota(jnp.int32, (tm, tm), 0)).astype(F32)
    return pl.pallas_call(
        functools.partial(_inproj_kernel, tiles_per_seq=seq // tm),
        grid=(n // tm,),
        in_specs=[pl.BlockSpec((tm, d), lambda i: (i, 0)),
                  _full((d, ca)), _full((d, cb)), _full((d, 128)),
                  _full((1, ca)), _full((1, 128)), _full((tm, tm))],
        out_specs=[pl.BlockSpec((tm, ca), lambda i: (i, 0)),
                   pl.BlockSpec((tm, cb), lambda i: (i, 0)),
                   pl.BlockSpec((tm, 128), lambda i: (i, 0))],
        out_shape=[jax.ShapeDtypeStruct((n, ca), F32),
                   jax.ShapeDtypeStruct((n, cb), BF16),
                   jax.ShapeDtypeStruct((n, 128), F32)],
        scratch_shapes=[pltpu.VMEM((1, ca), F32), pltpu.VMEM((1, 128), F32)],
        compiler_params=_params("arbitrary"),
        name="inproj",
    )(x2, wa, wb, wf, mu, bf, tri)


def _rwkv_kernel(pr_ref, w0_ref, wup_ref, a0_ref, aup_ref, kk_ref, ka_ref,
                 gsum_ref, tri_ref, y_ref, h_ref, *, chunk):
    @pl.when(pl.program_id(1) == 0)
    def _():
        h_ref[...] = jnp.zeros_like(h_ref)

    gw = GROUP_W
    r = pr_ref[:, 0:gw]
    k = pr_ref[:, gw:2 * gw]
    v = pr_ref[:, 2 * gw:3 * gw]
    wd = pr_ref[:, 3 * gw:3 * gw + LORA_PAD]
    ad = pr_ref[:, 3 * gw + LORA_PAD:3 * gw + 2 * LORA_PAD]

    w_pre = w0_ref[...] + _dot(jnp.tanh(wd).astype(BF16), wup_ref[...])
    w = -_softplus(-w_pre) - 0.5
    log_decay = -jnp.exp(w)
    a = _sigmoid(a0_ref[...] + _dot(ad.astype(BF16), aup_ref[...]))
    kk = k * kk_ref[...]
    norm = jnp.sqrt(_dot((kk * kk).astype(BF16), gsum_ref[...]))
    kk = kk / jnp.maximum(norm, 1e-12)
    k_mod = k * (1.0 + (a - 1.0) * ka_ref[...])
    b_vec = kk * a

    cum = _dot(tri_ref[...], log_decay, precision=HIGHEST)
    last = cum[chunk - 1:chunk, :]
    p_in = jnp.exp(cum)
    p_inv = jnp.exp(-cum)
    p_tail = jnp.exp(last - cum)
    a_t = (-kk * jnp.exp(cum - log_decay)).astype(BF16)
    r_t = (r * p_in).astype(BF16)
    b_t = (b_vec * p_inv).astype(BF16)
    k_t = (k_mod * p_inv).astype(BF16)
    b_h = (b_vec * p_tail).astype(BF16)
    k_h = (k_mod * p_tail).astype(BF16)
    p_last = jnp.exp(last)
    vb = v.astype(BF16)

    row = lax.broadcasted_iota(jnp.int32, (chunk, chunk), 0)
    col = lax.broadcasted_iota(jnp.int32, (chunk, chunk), 1)
    strict = col < row
    incl = col <= row
    eye = (col == row).astype(F32)
    hrow = lax.broadcasted_iota(jnp.int32, (HEAD_DIM, HEAD_DIM), 0)
    hcol = lax.broadcasted_iota(jnp.int32, (HEAD_DIM, HEAD_DIM), 1)
    heye = hrow == hcol

    for h in range(N_HEADS):
        sl = slice(h * HEAD_DIM, (h + 1) * HEAD_DIM)
        ar = jnp.concatenate([a_t[:, sl], r_t[:, sl]], axis=0)
        g_b = _dot_nt(ar, b_t[:, sl])
        g_k = _dot_nt(ar, k_t[:, sl])
        l_ab = jnp.where(strict, g_b[:chunk], 0.0)
        l_ak = jnp.where(strict, g_k[:chunk], 0.0)
        m_rb = jnp.where(incl, g_b[chunk:], 0.0).astype(BF16)
        m_rk = jnp.where(incl, g_k[chunk:], 0.0).astype(BF16)
        xp = l_ab
        t_inv = eye + xp
        steps = int(math.log2(chunk)) - 1
        for _ in range(steps):
            xb = xp.astype(BF16)
            xp = _dot(xb, xb)
            t_inv = t_inv + _dot(t_inv.astype(BF16), xp.astype(BF16))
        tb = t_inv.astype(BF16)
        vh = vb[:, sl]
        w1 = _dot(tb, a_t[:, sl]).astype(BF16)
        u0 = _dot(tb, _dot(l_ak.astype(BF16), vh).astype(BF16))
        y0 = _dot(m_rk, vh)
        dh = _dot_tn(k_h[:, sl], vh)
        hs = h_ref[h]
        hsb = hs.astype(BF16)
        u = _dot(w1, hsb) + u0
        ub = u.astype(BF16)
        y = _dot(r_t[:, sl], hsb) + _dot(m_rb, ub) + y0
        p_col = jnp.sum(jnp.where(heye, p_last[:, sl], 0.0), axis=1, keepdims=True)
        h_ref[h] = p_col * hs + _dot_tn(b_h[:, sl], ub) + dh
        y_ref[:, sl] = y


def _rwkv(pr, w0, wup, a0, aup, k_k, k_a, *, batch, seq, chunk):
    n, ca = pr.shape
    nch = seq // chunk
    gidx = lax.broadcasted_iota(jnp.int32, (GROUP_W, GROUP_W), 0) // HEAD_DIM
    gsum = (gidx == gidx.T).astype(BF16)
    tri = (lax.broadcasted_iota(jnp.int32, (chunk, chunk), 1)
           <= lax.broadcasted_iota(jnp.int32, (chunk, chunk), 0)).astype(F32)
    return pl.pallas_call(
        functools.partial(_rwkv_kernel, chunk=chunk),
        grid=(batch, nch),
        in_specs=[pl.BlockSpec((chunk, ca), lambda b, c: (b * nch + c, 0)),
                  _full((1, GROUP_W)), _full((LORA_PAD, GROUP_W)),
                  _full((1, GROUP_W)), _full((LORA_PAD, GROUP_W)),
                  _full((1, GROUP_W)), _full((1, GROUP_W)),
                  _full((GROUP_W, GROUP_W)), _full((chunk, chunk))],
        out_specs=pl.BlockSpec((chunk, GROUP_W), lambda b, c: (b * nch + c, 0)),
        out_shape=jax.ShapeDtypeStruct((n, GROUP_W), F32),
        scratch_shapes=[pltpu.VMEM((N_HEADS, HEAD_DIM, HEAD_DIM), F32)],
        compiler_params=_params("arbitrary", "arbitrary"),
        name="rwkv_scan",
    )(pr, w0, wup, a0, aup, k_k, k_a, gsum, tri)


def _fox_kernel(q_ref, k_ref, v_ref, c_ref, o_ref, acc_ref, m_ref, l_ref, *, tq):
    qi = pl.program_id(2)
    lane = lax.broadcasted_iota(jnp.int32, (1, 2 * HEAD_DIM), 1)
    first = lane < HEAD_DIM
    q = q_ref[...]
    zero = jnp.zeros_like(q)
    qh = (jnp.where(first, q, zero), jnp.where(first, zero, q))
    acc_ref[...] = jnp.zeros_like(acc_ref)
    m_ref[...] = jnp.full_like(m_ref, NEG_BIG)
    l_ref[...] = jnp.zeros_like(l_ref)
    causal = (lax.broadcasted_iota(jnp.int32, (tq, tq), 1)
              <= lax.broadcasted_iota(jnp.int32, (tq, tq), 0))

    def step(kb, masked):
        ks = pl.multiple_of(kb * tq, tq)
        kblk = k_ref[pl.ds(ks, tq), :]
        vblk = v_ref[pl.ds(ks, tq), :]
        alphas, pvs = [], []
        for hh in range(2):
            z = _dot_nt(qh[hh], kblk) - c_ref[hh:hh + 1, pl.ds(ks, tq)]
            if masked:
                z = jnp.where(causal, z, NEG_BIG)
            m_prev = m_ref[hh]
            m_new = jnp.maximum(m_prev, jnp.max(z, axis=-1, keepdims=True))
            alpha = jnp.exp(m_prev - m_new)
            p = jnp.exp(z - m_new)
            l_ref[hh] = alpha * l_ref[hh] + jnp.sum(p, axis=-1, keepdims=True)
            m_ref[hh] = m_new
            alphas.append(alpha)
            pvs.append(_dot(p.astype(BF16), vblk))
        acc_ref[...] = (acc_ref[...] * jnp.where(first, alphas[0], alphas[1])
                        + jnp.where(first, pvs[0], pvs[1]))

    def body(kb, carry):
        step(kb, False)
        return carry

    lax.fori_loop(0, qi, body, 0)
    step(qi, True)
    o_ref[...] = (acc_ref[...] / jnp.where(first, l_ref[0], l_ref[1])).astype(BF16)


def _fox(qkv, c_t, *, batch, seq, tq):
    n = qkv.shape[0]
    nq = seq // tq
    npair = N_HEADS // 2
    lanes = 2 * HEAD_DIM
    return pl.pallas_call(
        functools.partial(_fox_kernel, tq=tq),
        grid=(batch, npair, nq),
        in_specs=[pl.BlockSpec((tq, lanes), lambda b, j, i: (b * nq + i, j)),
                  pl.BlockSpec((seq, lanes), lambda b, j, i: (b, npair + j)),
                  pl.BlockSpec((seq, lanes), lambda b, j, i: (b, 2 * npair + j)),
                  pl.BlockSpec((None, 2, seq), lambda b, j, i: (b * npair + j, 0, 0))],
        out_specs=pl.BlockSpec((tq, lanes), lambda b, j, i: (b * nq + i, j)),
        out_shape=jax.ShapeDtypeStruct((n, GROUP_W), BF16),
        scratch_shapes=[pltpu.VMEM((tq, lanes), F32),
                        pltpu.VMEM((2, tq, 1), F32), pltpu.VMEM((2, tq, 1), F32)],
        compiler_params=_params("arbitrary", "arbitrary", "arbitrary"),
        name="fox_attention",
    )(qkv, qkv, qkv, c_t)


def _mixout_kernel(x_ref, pr_ref, yr_ref, yf_ref, a0_ref, aup_ref, gup_ref, ka_ref,
                   rk_ref, gng_ref, gnb_ref, gsum_ref, wr_ref, wf_ref, lng_ref, lnb_ref,
                   o_ref):
    gw = GROUP_W
    r = pr_ref[:, 0:gw]
    k = pr_ref[:, gw:2 * gw]
    v = pr_ref[:, 2 * gw:3 * gw]
    ad = pr_ref[:, 3 * gw + LORA_PAD:3 * gw + 2 * LORA_PAD]
    gd = pr_ref[:, 3 * gw + 2 * LORA_PAD:3 * gw + 3 * LORA_PAD]
    a = _sigmoid(a0_ref[...] + _dot(ad.astype(BF16), aup_ref[...]))
    k_mod = k * (1.0 + (a - 1.0) * ka_ref[...])
    gate = _dot(_sigmoid(gd).astype(BF16), gup_ref[...])
    gsum = gsum_ref[...]

    def group_sum(t):
        hi = t.astype(BF16)
        lo = (t - hi.astype(F32)).astype(BF16)
        return _dot(hi, gsum) + _dot(lo, gsum)

    y = yr_ref[...]
    mean = group_sum(y) * (1.0 / HEAD_DIM)
    d = y - mean
    var = group_sum(d * d) * (1.0 / HEAD_DIM)
    yn = d * lax.rsqrt(var + GN_EPS) * gng_ref[...] + gnb_ref[...]
    bonus = group_sum(r * k_mod * rk_ref[...])
    y_rwkv = ((yn + bonus * v) * gate).astype(BF16)
    mixed = _dot(y_rwkv, wr_ref[...]) + _dot(yf_ref[...], wf_ref[...])
    o_ref[...] = _layer_norm(ALPHA * x_ref[...] + mixed, lng_ref[...], lnb_ref[...])


def _mixout(x2, pr, yr, yf, a0, aup, gup, k_a, r_k, gn_g, gn_b, w_r, w_f, ln_g, ln_b, *, tm):
    n, d = x2.shape
    ca = pr.shape[1]
    gidx = lax.broadcasted_iota(jnp.int32, (GROUP_W, GROUP_W), 0) // HEAD_DIM
    gsum = (gidx == gidx.T).astype(BF16)
    vec = _full((1, GROUP_W))
    return pl.pallas_call(
        _mixout_kernel,
        grid=(n // tm,),
        in_specs=[pl.BlockSpec((tm, d), lambda i: (i, 0)),
                  pl.BlockSpec((tm, ca), lambda i: (i, 0)),
                  pl.BlockSpec((tm, GROUP_W), lambda i: (i, 0)),
                  pl.BlockSpec((tm, GROUP_W), lambda i: (i, 0)),
                  vec, _full((LORA_PAD, GROUP_W)), _full((LORA_PAD, GROUP_W)),
                  vec, vec, vec, vec, _full((GROUP_W, GROUP_W)),
                  _full((GROUP_W, d)), _full((GROUP_W, d)),
                  _full((1, d)), _full((1, d))],
        out_specs=pl.BlockSpec((tm, d), lambda i: (i, 0)),
        out_shape=jax.ShapeDtypeStruct((n, d), F32),
        compiler_params=_params("parallel"),
        name="mix_out",
    )(x2, pr, yr, yf, a0, aup, gup, k_a, r_k, gn_g, gn_b, gsum, w_r, w_f, ln_g, ln_b)


def _ffn_kernel(x_ref, wg_ref, wu_ref, wd_ref, lng_ref, lnb_ref, o_ref, acc_ref):
    f = pl.program_id(1)

    @pl.when(f == 0)
    def _():
        acc_ref[...] = jnp.zeros_like(acc_ref)

    xb = x_ref[...].astype(BF16)
    g = _dot(xb, wg_ref[...])
    u = _dot(xb, wu_ref[...])
    h = (g * _sigmoid(g) * u).astype(BF16)
    acc_ref[...] += _dot(h, wd_ref[...])

    @pl.when(f == pl.num_programs(1) - 1)
    def _():
        o_ref[...] = _layer_norm(ALPHA * x_ref[...] + acc_ref[...], lng_ref[...], lnb_ref[...])


def _ffn(x2, wg, wu, wd, ln_g, ln_b, *, tm, tf):
    n, d = x2.shape
    ff = wg.shape[1]
    return pl.pallas_call(
        _ffn_kernel,
        grid=(n // tm, ff // tf),
        in_specs=[pl.BlockSpec((tm, d), lambda i, f: (i, 0)),
                  pl.BlockSpec((d, tf), lambda i, f: (0, f)),
                  pl.BlockSpec((d, tf), lambda i, f: (0, f)),
                  pl.BlockSpec((tf, d), lambda i, f: (f, 0)),
                  _full((1, d)), _full((1, d))],
        out_specs=pl.BlockSpec((tm, d), lambda i, f: (i, 0)),
        out_shape=jax.ShapeDtypeStruct((n, d), F32),
        scratch_shapes=[pltpu.VMEM((tm, d), F32)],
        compiler_params=_params("parallel", "arbitrary"),
        name="ffn_swiglu",
    )(x2, wg, wu, wd, ln_g, ln_b)


def _glu_kernel(x_ref, w_ref, b_ref, o_ref):
    d = o_ref.shape[1]
    xb = x_ref[...].astype(BF16)
    val = _dot(xb, w_ref[:, 0:d]) + b_ref[:, 0:d]
    gat = _dot(xb, w_ref[:, d:2 * d]) + b_ref[:, d:2 * d]
    o_ref[...] = val * _sigmoid(gat)


def _glu(x2, w, b, *, tm):
    n, d = x2.shape
    return pl.pallas_call(
        _glu_kernel,
        grid=(n // tm,),
        in_specs=[pl.BlockSpec((tm, d), lambda i: (i, 0)), _full((d, 2 * d)), _full((1, 2 * d))],
        out_specs=pl.BlockSpec((tm, d), lambda i: (i, 0)),
        out_shape=jax.ShapeDtypeStruct((n, d), F32),
        compiler_params=_params("parallel"),
        name="conv_glu",
    )(x2, w, b)


def _top2_gates(logits):
    lane = lax.broadcasted_iota(jnp.int32, logits.shape, 1).astype(F32)
    lg = jnp.where(lane < N_EXPERTS, logits, NEG_BIG)
    m1 = jnp.max(lg, axis=-1, keepdims=True)
    i1 = jnp.min(jnp.where(lg == m1, lane, 128.0), axis=-1, keepdims=True)
    lg2 = jnp.where(lane == i1, NEG_BIG, lg)
    m2 = jnp.max(lg2, axis=-1, keepdims=True)
    i2 = jnp.min(jnp.where(lg2 == m2, lane, 128.0), axis=-1, keepdims=True)
    e2 = jnp.exp(m2 - m1)
    w1 = 1.0 / (1.0 + e2)
    w2 = e2 / (1.0 + e2)
    return jnp.where(lane == i1, w1, 0.0) + jnp.where(lane == i2, w2, 0.0)


def _conv_kernel(hc_ref, hp_ref, x_ref, wdw_ref, bdw_ref, lng_ref, lnb_ref, w2_ref, b2_ref,
                 pg_ref, pb_ref, wr_ref, x3_ref, x3b_ref, gates_ref, ext_ref, cv_ref,
                 *, tiles_per_seq):
    tm, d = x_ref.shape
    first = pl.program_id(0) % tiles_per_seq == 0
    ext_ref[0:CONV_HALO, :] = jnp.where(first, 0.0, hp_ref[...])
    ext_ref[CONV_HALO:CONV_HALO + tm, :] = hc_ref[...]
    off = CONV_HALO - (CONV_WIDTH - 1)
    rc, cc = 64, 256
    for r0 in range(0, tm, rc):
        for c0 in range(0, d, cc):
            acc = jnp.broadcast_to(bdw_ref[:, c0:c0 + cc], (rc, cc))
            for t in range(CONV_WIDTH):
                acc = acc + wdw_ref[t:t + 1, c0:c0 + cc] * ext_ref[r0 + off + t:r0 + off + t + rc, c0:c0 + cc]
            cv_ref[r0:r0 + rc, c0:c0 + cc] = acc
    hn = _layer_norm(cv_ref[...], lng_ref[...], lnb_ref[...])
    hs = (hn * _sigmoid(hn)).astype(BF16)
    conv = _dot(hs, w2_ref[...]) + b2_ref[...]
    x3 = _layer_norm(ALPHA * x_ref[...] + conv, pg_ref[...], pb_ref[...])
    x3_ref[...] = x3
    x3b_ref[...] = x3.astype(BF16)
    gates_ref[...] = _top2_gates(_dot(x3, wr_ref[...], precision=HIGHEST))


def _conv(hg, x2, w_dw, b_dw, ln_g, ln_b, w2, b2, pg, pb, w_router, *, seq, tm):
    n, d = x2.shape
    ratio = tm // CONV_HALO
    vec = _full((1, d))
    return pl.pallas_call(
        functools.partial(_conv_kernel, tiles_per_seq=seq // tm),
        grid=(n // tm,),
        in_specs=[pl.BlockSpec((tm, d), lambda i: (i, 0)),
                  pl.BlockSpec((CONV_HALO, d), lambda i: (jnp.maximum(i * ratio - 1, 0), 0)),
                  pl.BlockSpec((tm, d), lambda i: (i, 0)),
                  _full((CONV_HALO, d)), vec, vec, vec, _full((d, d)), vec, vec, vec,
                  _full((d, 128))],
        out_specs=[pl.BlockSpec((tm, d), lambda i: (i, 0)),
                   pl.BlockSpec((tm, d), lambda i: (i, 0)),
                   pl.BlockSpec((tm, 128), lambda i: (i, 0))],
        out_shape=[jax.ShapeDtypeStruct((n, d), F32),
                   jax.ShapeDtypeStruct((n, d), BF16),
                   jax.ShapeDtypeStruct((n, 128), F32)],
        scratch_shapes=[pltpu.VMEM((tm + CONV_HALO, d), F32), pltpu.VMEM((tm, d), F32)],
        compiler_params=_params("parallel"),
        name="conv_module",
    )(hg, hg, x2, w_dw, b_dw, ln_g, ln_b, w2, b2, pg, pb, w_router)


def _moe_kernel(xb_ref, gates_ref, wg_ref, wu_ref, wd_ref, x_ref, lng_ref, lnb_ref,
                o_ref, acc_ref):
    e = pl.program_id(1)
    f = pl.program_id(2)

    @pl.when(jnp.logical_and(e == 0, f == 0))
    def _():
        acc_ref[...] = jnp.zeros_like(acc_ref)

    gates = gates_ref[...]
    lane = lax.broadcasted_iota(jnp.int32, gates.shape, 1)
    gate = jnp.sum(jnp.where(lane == e, gates, 0.0), axis=-1, keepdims=True)
    xb = xb_ref[...]
    g = _dot(xb, wg_ref[...])
    u = _dot(xb, wu_ref[...])
    h = (g * _sigmoid(g) * u * gate).astype(BF16)
    acc_ref[...] += _dot(h, wd_ref[...])

    @pl.when(jnp.logical_and(e == pl.num_programs(1) - 1, f == pl.num_programs(2) - 1))
    def _():
        o_ref[...] = _layer_norm(ALPHA * x_ref[...] + acc_ref[...], lng_ref[...], lnb_ref[...])


def _moe(x3, x3b, gates, wg, wu, wd, ln_g, ln_b, *, tm, tf):
    n, d = x3.shape
    ne, _, ff = wg.shape
    return pl.pallas_call(
        _moe_kernel,
        grid=(n // tm, ne, ff // tf),
        in_specs=[pl.BlockSpec((tm, d), lambda i, e, f: (i, 0)),
                  pl.BlockSpec((tm, 128), lambda i, e, f: (i, 0)),
                  pl.BlockSpec((None, d, tf), lambda i, e, f: (e, 0, f)),
                  pl.BlockSpec((None, d, tf), lambda i, e, f: (e, 0, f)),
                  pl.BlockSpec((None, tf, d), lambda i, e, f: (e, f, 0)),
                  pl.BlockSpec((tm, d), lambda i, e, f: (i, 0)),
                  _full((1, d)), _full((1, d))],
        out_specs=pl.BlockSpec((tm, d), lambda i, e, f: (i, 0)),
        out_shape=jax.ShapeDtypeStruct((n, d), F32),
        scratch_shapes=[pltpu.VMEM((tm, d), F32)],
        compiler_params=_params("parallel", "arbitrary", "arbitrary"),
        name="moe_swiglu",
    )(x3b, gates, wg, wu, wd, x3, ln_g, ln_b)


def _pad_cols(w, width):
    return jnp.pad(w, ((0, 0), (0, width - w.shape[1])))


def _pad_rows(w, height):
    return jnp.pad(w, ((0, height - w.shape[0]), (0, 0)))


def _forward(x, mix_w_in, rwkv_mu, rwkv_w0, rwkv_w_up, rwkv_a0, rwkv_a_up, rwkv_g_up,
             rwkv_k_k, rwkv_k_a, rwkv_r_k, rwkv_gn_g, rwkv_gn_b, fox_b_f, mix_w_out,
             mix_ln_g, mix_ln_b, ffn_w_gate, ffn_w_up, ffn_w_down, ffn_ln_g, ffn_ln_b,
             conv_w_pw1, conv_b_pw1, conv_w_dw, conv_b_dw, conv_ln_g, conv_ln_b,
             conv_w_pw2, conv_b_pw2, conv_post_ln_g, conv_post_ln_b,
             moe_w_router, moe_w_gate, moe_w_up, moe_w_down, moe_ln_g, moe_ln_b,
             *, tm=512, chunk=64, tq=512, tf_ffn=1408, tm_moe=1024, tf_moe=896):
    batch, seq, d = x.shape
    n = batch * seq
    gw = GROUP_W
    x2 = x.reshape(n, d)
    row = lambda t: t.reshape(1, -1)

    w_in = mix_w_in[0]
    mu = rwkv_mu[0]
    o_w, o_a, o_g = 3 * gw, 3 * gw + DECAY_LORA, 3 * gw + DECAY_LORA + AAA_LORA
    o_fox = o_g + GATE_LORA

    def lora_layout(t):
        return jnp.concatenate([t[..., :o_w],
                                _pad_cols(t[..., o_w:o_a], LORA_PAD),
                                _pad_cols(t[..., o_a:o_g], LORA_PAD),
                                _pad_cols(t[..., o_g:o_fox], LORA_PAD)], axis=-1)

    wa = lora_layout(w_in).astype(BF16)
    mu_a = lora_layout(row(mu))
    scale = 1.0 / math.sqrt(HEAD_DIM)
    wb = jnp.concatenate([w_in[:, o_fox:o_fox + gw] * scale,
                          w_in[:, o_fox + gw:o_fox + 3 * gw]], axis=1).astype(BF16)
    wf = _pad_cols(w_in[:, o_fox + 3 * gw:], 128).astype(BF16)
    bf = _pad_cols(row(fox_b_f[0]), 128)

    pr, qkv, c = _inproj(x2, wa, wb, wf, mu_a, bf, seq=seq, tm=tm)

    wup = _pad_rows(rwkv_w_up[0], LORA_PAD).astype(BF16)
    aup = _pad_rows(rwkv_a_up[0], LORA_PAD).astype(BF16)
    gup = _pad_rows(rwkv_g_up[0], LORA_PAD).astype(BF16)
    k_k, k_a, r_k = row(rwkv_k_k[0]), row(rwkv_k_a[0]), row(rwkv_r_k[0])
    yr = _rwkv(pr, row(rwkv_w0[0]), wup, row(rwkv_a0[0]), aup, k_k, k_a,
               batch=batch, seq=seq, chunk=chunk)

    c_t = jnp.transpose(c[:, :N_HEADS].reshape(batch, seq, N_HEADS // 2, 2), (0, 2, 3, 1))
    c_t = c_t.reshape(batch * (N_HEADS // 2), 2, seq)
    yf = _fox(qkv, c_t, batch=batch, seq=seq, tq=tq)

    w_out = mix_w_out[0].astype(BF16)
    x1 = _mixout(x2, pr, yr, yf, row(rwkv_a0[0]), aup, gup, k_a, r_k,
                 row(rwkv_gn_g[0]), row(rwkv_gn_b[0]), w_out[:gw], w_out[gw:],
                 row(mix_ln_g[0]), row(mix_ln_b[0]), tm=tm)
    x2b = _ffn(x1, ffn_w_gate[0].astype(BF16), ffn_w_up[0].astype(BF16),
               ffn_w_down[0].astype(BF16), row(ffn_ln_g[0]), row(ffn_ln_b[0]), tm=tm, tf=tf_ffn)

    hg = _glu(x2b, conv_w_pw1[0].astype(BF16), row(conv_b_pw1[0]), tm=tm)
    w_router = _pad_cols(moe_w_router[0], 128)
    x3, x3b, gates = _conv(hg, x2b, _pad_rows(conv_w_dw[0], CONV_HALO), row(conv_b_dw[0]),
                           row(conv_ln_g[0]), row(conv_ln_b[0]), conv_w_pw2[0].astype(BF16),
                           row(conv_b_pw2[0]), row(conv_post_ln_g[0]), row(conv_post_ln_b[0]),
                           w_router, seq=seq, tm=tm)
    out = _moe(x3, x3b, gates, moe_w_gate[0].astype(BF16), moe_w_up[0].astype(BF16),
               moe_w_down[0].astype(BF16), row(moe_ln_g[0]), row(moe_ln_b[0]),
               tm=tm_moe, tf=tf_moe)
    return out.reshape(batch, seq, d)


def kernel(x, mix_w_in, rwkv_mu, rwkv_w0, rwkv_w_up, rwkv_a0, rwkv_a_up, rwkv_g_up, rwkv_k_k, rwkv_k_a, rwkv_r_k, rwkv_gn_g, rwkv_gn_b, fox_b_f, mix_w_out, mix_ln_g, mix_ln_b, ffn_w_gate, ffn_w_up, ffn_w_down, ffn_ln_g, ffn_ln_b, conv_w_pw1, conv_b_pw1, conv_w_dw, conv_b_dw, conv_ln_g, conv_ln_b, conv_w_pw2, conv_b_pw2, conv_post_ln_g, conv_post_ln_b, moe_w_router, moe_w_gate, moe_w_up, moe_w_down, moe_ln_g, moe_ln_b):
    return _forward(x, mix_w_in, rwkv_mu, rwkv_w0, rwkv_w_up, rwkv_a0, rwkv_a_up, rwkv_g_up,
                    rwkv_k_k, rwkv_k_a, rwkv_r_k, rwkv_gn_g, rwkv_gn_b, fox_b_f, mix_w_out,
                    mix_ln_g, mix_ln_b, ffn_w_gate, ffn_w_up, ffn_w_down, ffn_ln_g, ffn_ln_b,
                    conv_w_pw1, conv_b_pw1, conv_w_dw, conv_b_dw, conv_ln_g, conv_ln_b,
                    conv_w_pw2, conv_b_pw2, conv_post_ln_g, conv_post_ln_b,
                    moe_w_router, moe_w_gate, moe_w_up, moe_w_down, moe_ln_g, moe_ln_b)
```

```python
import functools
import math

import jax
import jax.numpy as jnp
from jax import lax
from jax.experimental import pallas as pl
from jax.experimental.pallas import tpu as pltpu

F32 = jnp.float32
BF16 = jnp.bfloat16
HIGHEST = lax.Precision.HIGHEST

HEAD_DIM = 64
N_HEADS = 8
GROUP_W = N_HEADS * HEAD_DIM
LORA_PAD = 128
DECAY_LORA = 32
AAA_LORA = 32
GATE_LORA = 96
CONV_WIDTH = 31
CONV_HALO = 32
N_EXPERTS = 8
LN_EPS = 1e-5
GN_EPS = 64e-5
DEPTH = 2
ALPHA = (2.0 * DEPTH) ** 0.25
NEG_BIG = -1e30
LOG2E = math.log2(math.e)
VMEM_LIMIT = 56 * 1024 * 1024


def _dot(a, b, **kw):
    return jnp.dot(a, b, preferred_element_type=F32, **kw)


def _dot_nt(a, b):
    return lax.dot_general(a, b, (((1,), (1,)), ((), ())), preferred_element_type=F32)


def _dot_tn(a, b):
    return lax.dot_general(a, b, (((0,), (0,)), ((), ())), preferred_element_type=F32)


def _sigmoid(z):
    return 1.0 / (1.0 + jnp.exp(-z))


def _softplus(z):
    return jnp.maximum(z, 0.0) + jnp.log1p(jnp.exp(-jnp.abs(z)))


def _layer_norm(h, g, b):
    mu = jnp.mean(h, axis=-1, keepdims=True)
    d = h - mu
    var = jnp.mean(d * d, axis=-1, keepdims=True)
    return d * lax.rsqrt(var + LN_EPS) * g + b


def _params(*sem):
    return pltpu.CompilerParams(dimension_semantics=sem, vmem_limit_bytes=VMEM_LIMIT)


def _full(shape):
    return pl.BlockSpec(shape, lambda *_: (0,) * len(shape))


def _inproj_kernel(x_ref, wa_ref, wb_ref, wf_ref, mu_ref, bf_ref, tri_ref,
                   pr_ref, qkv_ref, c_ref, last_ref, carry_ref, *, tiles_per_seq):
    i = pl.program_id(0)

    @pl.when(i % tiles_per_seq == 0)
    def _():
        last_ref[...] = jnp.zeros_like(last_ref)
        carry_ref[...] = jnp.zeros_like(carry_ref)

    xb = x_ref[...].astype(BF16)
    tm = xb.shape[0]
    row0 = lax.broadcasted_iota(jnp.int32, (tm, 1), 0) == 0
    ca = wa_ref.shape[1]
    cw = 256 if ca % 256 == 0 else 128
    for c0 in range(0, ca, cw):
        p = _dot(xb, wa_ref[:, c0:c0 + cw])
        prev = jnp.where(row0, last_ref[:, c0:c0 + cw], pltpu.roll(p, 1, 0))
        last_ref[:, c0:c0 + cw] = p[tm - 1:tm, :]
        pr_ref[:, c0:c0 + cw] = p + mu_ref[:, c0:c0 + cw] * (prev - p)
    cb = wb_ref.shape[1]
    for c0 in range(0, cb, GROUP_W):
        qkv_ref[:, c0:c0 + GROUP_W] = _dot(xb, wb_ref[:, c0:c0 + GROUP_W]).astype(BF16)
    fl = _dot(xb, wf_ref[...]) + bf_ref[...]
    log_f = jnp.minimum(fl, 0.0) - jnp.log1p(jnp.exp(-jnp.abs(fl)))
    c = _dot(tri_ref[...], log_f, precision=HIGHEST) + carry_ref[...]
    c_ref[...] = c
    carry_ref[...] = c[tm - 1:tm, :]


def _inproj(x2, wa, wb, wf, mu, bf, *, seq, tm):
    n, d = x2.shape
    ca, cb = wa.shape[1], wb.shape[1]
    tri = (lax.broadcasted_iota(jnp.int32, (tm, tm), 1)
           <= lax.broadcasted_iota(jnp.int32, (tm, tm), 0)).astype(F32)
    return pl.pallas_call(
        functools.partial(_inproj_kernel, tiles_per_seq=seq // tm),
        grid=(n // tm,),
        in_specs=[pl.BlockSpec((tm, d), lambda i: (i, 0)),
                  _full((d, ca)), _full((d, cb)), _full((d, 128)),
                  _full((1, ca)), _full((1, 128)), _full((tm, tm))],
        out_specs=[pl.BlockSpec((tm, ca), lambda i: (i, 0)),
                   pl.BlockSpec((tm, cb), lambda i: (i, 0)),
                   pl.BlockSpec((tm, 128), lambda i: (i, 0))],
        out_shape=[jax.ShapeDtypeStruct((n, ca), F32),
                   jax.ShapeDtypeStruct((n, cb), BF16),
                   jax.ShapeDtypeStruct((n, 128), F32)],
        scratch_shapes=[pltpu.VMEM((1, ca), F32), pltpu.VMEM((1, 128), F32)],
        compiler_params=_params("arbitrary"),
        name="inproj",
    )(x2, wa, wb, wf, mu, bf, tri)


def _rwkv_kernel(pr_ref, w0_ref, wup_ref, a0_ref, aup_ref, kk_ref, ka_ref,
                 gsum_ref, tri_ref, y_ref, h_ref, *, chunk, nb):
    @pl.when(pl.program_id(1) == 0)
    def _():
        h_ref[...] = jnp.zeros_like(h_ref)

    gw = GROUP_W
    pw = 2 * HEAD_DIM
    npair = N_HEADS // 2
    rows = 2 * chunk
    log_chunk = int(math.log2(chunk))
    head0 = lax.broadcasted_iota(jnp.int32, (1, pw), 1) < HEAD_DIM
    row = lax.broadcasted_iota(jnp.int32, (rows, rows), 0)
    col = lax.broadcasted_iota(jnp.int32, (rows, rows), 1)
    strict = (col & (chunk - 1)) < (row & (chunk - 1))
    incl = (col & (chunk - 1)) <= (row & (chunk - 1))
    eye = (col == row).astype(F32)
    peye = (lax.broadcasted_iota(jnp.int32, (pw, pw), 0)
            == lax.broadcasted_iota(jnp.int32, (pw, pw), 1))

    def stack(x):
        return jnp.concatenate([jnp.where(head0, x, 0.0), jnp.where(head0, 0.0, x)],
                               axis=0).astype(BF16)

    units = [(b, j) for b in range(nb) for j in range(npair)]
    nu = len(units)
    ar, bk, vs, bhs, khs, pcs = [], [], [], [], [], []
    for b in range(nb):
        r = pr_ref[b, :, 0:gw]
        k = pr_ref[b, :, gw:2 * gw]
        v = pr_ref[b, :, 2 * gw:3 * gw]
        wd = pr_ref[b, :, 3 * gw:3 * gw + LORA_PAD]
        ad = pr_ref[b, :, 3 * gw + LORA_PAD:3 * gw + 2 * LORA_PAD]
        w_pre = w0_ref[...] + _dot(jnp.tanh(wd).astype(BF16), wup_ref[...])
        w = -_softplus(-w_pre) - 0.5
        log_decay = -jnp.exp(w)
        a = _sigmoid(a0_ref[...] + _dot(ad.astype(BF16), aup_ref[...]))
        kk = k * kk_ref[...]
        norm = jnp.sqrt(_dot((kk * kk).astype(BF16), gsum_ref[...]))
        kk = kk / jnp.maximum(norm, 1e-12)
        k_mod = k * (1.0 + (a - 1.0) * ka_ref[...])
        b_vec = kk * a
        cum = _dot(tri_ref[...], log_decay, precision=HIGHEST)
        last = cum[chunk - 1:chunk, :]
        p_inv = jnp.exp(-cum)
        p_tail = jnp.exp(last - cum)
        a_t = -kk * jnp.exp(cum - log_decay)
        r_t = r * jnp.exp(cum)
        b_t = b_vec * p_inv
        k_t = k_mod * p_inv
        b_h = b_vec * p_tail
        k_h = k_mod * p_tail
        p_last = jnp.exp(last)
        for j in range(npair):
            sl = slice(j * pw, (j + 1) * pw)
            ar.append(jnp.concatenate([stack(a_t[:, sl]), stack(r_t[:, sl])], axis=0))
            bk.append(jnp.concatenate([stack(b_t[:, sl]), stack(k_t[:, sl])], axis=0))
            vs.append(stack(v[:, sl]))
            bhs.append(stack(b_h[:, sl]))
            khs.append(stack(k_h[:, sl]))
            pcs.append(jnp.sum(jnp.where(peye, p_last[:, sl], 0.0), axis=1, keepdims=True))

    gram = [_dot_nt(ar[u], bk[u]) for u in range(nu)]
    l_ab = [jnp.where(strict, gram[u][:rows, :rows], 0.0) for u in range(nu)]
    l_akv = [_dot(jnp.where(strict, gram[u][:rows, rows:], 0.0).astype(BF16), vs[u])
             for u in range(nu)]
    m_rb = [jnp.where(incl, gram[u][rows:, :rows], 0.0).astype(BF16) for u in range(nu)]
    m_rkv = [_dot(jnp.where(incl, gram[u][rows:, rows:], 0.0).astype(BF16), vs[u])
             for u in range(nu)]
    t_inv = [eye + l_ab[u] for u in range(nu)]
    xb = [l_ab[u].astype(BF16) for u in range(nu)]
    xp = [_dot(xb[u], xb[u]) for u in range(nu)]
    for step in range(log_chunk - 1):
        xb = [xp[u].astype(BF16) for u in range(nu)]
        if step < log_chunk - 2:
            both = [_dot(jnp.concatenate([t_inv[u].astype(BF16), xb[u]], axis=0), xb[u])
                    for u in range(nu)]
            t_inv = [t_inv[u] + both[u][:rows] for u in range(nu)]
            xp = [both[u][rows:] for u in range(nu)]
        else:
            t_inv = [t_inv[u] + _dot(t_inv[u].astype(BF16), xb[u]) for u in range(nu)]
    tw = [_dot(t_inv[u].astype(BF16),
               jnp.concatenate([ar[u][:rows], l_akv[u].astype(BF16)], axis=1)).astype(BF16)
          for u in range(nu)]
    mw = [_dot(m_rb[u], tw[u]) for u in range(nu)]
    bw = [_dot_tn(bhs[u], tw[u]) for u in range(nu)]
    kv = [_dot_tn(khs[u], vs[u]) for u in range(nu)]
    for u, (b, j) in enumerate(units):
        wy = ar[u][rows:].astype(F32) + mw[u][:, :pw]
        yc = mw[u][:, pw:] + m_rkv[u]
        hf = h_ref[u]
        yh = _dot(jnp.concatenate([wy.astype(BF16), bw[u][:, :pw].astype(BF16)], axis=0),
                  hf.astype(BF16))
        h_ref[u] = pcs[u] * hf + yh[rows:] + bw[u][:, pw:] + kv[u]
        ys = yh[:rows] + yc
        y_ref[b, :, j * pw:(j + 1) * pw] = ys[:chunk] + ys[chunk:]


def _rwkv(pr, w0, wup, a0, aup, k_k, k_a, *, batch, seq, chunk, nb):
    n, ca = pr.shape
    nch = seq // chunk
    gidx = lax.broadcasted_iota(jnp.int32, (GROUP_W, GROUP_W), 0) // HEAD_DIM
    gsum = (gidx == gidx.T).astype(BF16)
    tri = (lax.broadcasted_iota(jnp.int32, (chunk, chunk), 1)
           <= lax.broadcasted_iota(jnp.int32, (chunk, chunk), 0)).astype(F32)
    y = pl.pallas_call(
        functools.partial(_rwkv_kernel, chunk=chunk, nb=nb),
        grid=(batch // nb, nch),
        in_specs=[pl.BlockSpec((nb, chunk, ca), lambda g, c: (g, c, 0)),
                  _full((1, GROUP_W)), _full((LORA_PAD, GROUP_W)),
                  _full((1, GROUP_W)), _full((LORA_PAD, GROUP_W)),
                  _full((1, GROUP_W)), _full((1, GROUP_W)),
                  _full((GROUP_W, GROUP_W)), _full((chunk, chunk))],
        out_specs=pl.BlockSpec((nb, chunk, GROUP_W), lambda g, c: (g, c, 0)),
        out_shape=jax.ShapeDtypeStruct((batch, seq, GROUP_W), F32),
        scratch_shapes=[pltpu.VMEM((nb * (N_HEADS // 2), 2 * HEAD_DIM, 2 * HEAD_DIM), F32)],
        compiler_params=_params("arbitrary", "arbitrary"),
        name="rwkv_scan",
    )(pr.reshape(batch, seq, ca), w0, wup, a0, aup, k_k, k_a, gsum, tri)
    return y.reshape(n, GROUP_W)


def _fox_kernel(q_ref, k_ref, vt_ref, c_ref, o_ref, acc_ref, m_ref, l_ref, cb_ref,
                sa_ref, sb_ref, *, tq):
    j = pl.program_id(1)
    qi = pl.program_id(2)
    pw = 2 * HEAD_DIM
    seq = k_ref.shape[0]
    lane = lax.broadcasted_iota(jnp.int32, (1, pw), 1)

    @pl.when(qi == 0)
    def _():
        def fill(rb, carry):
            rs = pl.multiple_of(rb * tq, tq)
            cblk = c_ref[pl.ds(rs, tq), :]
            for hh in range(2):
                colv = jnp.sum(jnp.where(lane == 2 * j + hh, cblk, 0.0), axis=1, keepdims=True)
                cb_ref[hh, pl.ds(rs, tq), :] = jnp.broadcast_to(colv * LOG2E, (tq, pw))
            return carry
        lax.fori_loop(0, seq // tq, fill, 0)

    first = lane < HEAD_DIM
    q = q_ref[...]
    zero = jnp.zeros_like(q)
    qh = (jnp.where(first, q, zero), jnp.where(first, zero, q))
    acc_ref[...] = jnp.zeros_like(acc_ref)
    m_ref[...] = jnp.full_like(m_ref, NEG_BIG)
    l_ref[...] = jnp.zeros_like(l_ref)
    causal = (lax.broadcasted_iota(jnp.int32, (tq, tq), 0)
              <= lax.broadcasted_iota(jnp.int32, (tq, tq), 1))
    top = lax.broadcasted_iota(jnp.int32, (pw, 1), 0) < HEAD_DIM

    def scores(kb, s_ref):
        kblk = k_ref[pl.ds(pl.multiple_of(kb * tq, tq), tq), :]
        for hh in range(2):
            s_ref[hh] = _dot_nt(kblk, qh[hh])

    def softmax_pv(kb, s_ref, masked):
        ks = pl.multiple_of(kb * tq, tq)
        vt = vt_ref[kb]
        alphas, pvs = [], []
        for hh in range(2):
            cb = cb_ref[hh, pl.ds(ks, tq), :]
            z = s_ref[hh] - jnp.concatenate([cb] * (tq // pw), axis=1)
            if masked:
                z = jnp.where(causal, z, NEG_BIG)
            m_prev = m_ref[hh]
            m_new = jnp.maximum(m_prev, jnp.max(z, axis=0, keepdims=True))
            alpha = jnp.exp2(m_prev - m_new)
            p = jnp.exp2(z - m_new)
            l_ref[hh] = alpha * l_ref[hh] + jnp.sum(p, axis=0, keepdims=True)
            m_ref[hh] = m_new
            alphas.append(alpha)
            pvs.append(_dot(vt, p.astype(BF16)))
        acc_ref[...] = (acc_ref[...] * jnp.where(top, alphas[0], alphas[1])
                        + jnp.where(top, pvs[0], pvs[1]))

    scores(0, sa_ref)

    def body(i, carry):
        scores(2 * i + 1, sb_ref)
        softmax_pv(2 * i, sa_ref, False)
        scores(2 * i + 2, sa_ref)
        softmax_pv(2 * i + 1, sb_ref, False)
        return carry

    lax.fori_loop(0, qi // 2, body, 0)

    @pl.when(qi % 2 == 0)
    def _():
        softmax_pv(qi, sa_ref, True)

    @pl.when(qi % 2 == 1)
    def _():
        scores(qi, sb_ref)
        softmax_pv(qi - 1, sa_ref, False)
        softmax_pv(qi, sb_ref, True)

    out_t = acc_ref[...] / jnp.where(top, l_ref[0], l_ref[1])
    o_ref[...] = out_t.T.astype(BF16)


def _fox(qkv, c, *, batch, seq, tq):
    n = qkv.shape[0]
    nq = seq // tq
    npair = N_HEADS // 2
    pw = 2 * HEAD_DIM
    vt = qkv[:, 2 * GROUP_W:].reshape(batch, nq, tq, npair, pw)
    vt = jnp.transpose(vt, (0, 3, 1, 4, 2)).reshape(batch * npair, nq, pw, tq)
    return pl.pallas_call(
        functools.partial(_fox_kernel, tq=tq),
        grid=(batch, npair, nq),
        in_specs=[pl.BlockSpec((tq, pw), lambda b, j, i: (b * nq + i, j)),
                  pl.BlockSpec((seq, pw), lambda b, j, i: (b, npair + j)),
                  pl.BlockSpec((None, nq, pw, tq), lambda b, j, i: (b * npair + j, 0, 0, 0)),
                  pl.BlockSpec((seq, 128), lambda b, j, i: (b, 0))],
        out_specs=pl.BlockSpec((tq, pw), lambda b, j, i: (b * nq + i, j)),
        out_shape=jax.ShapeDtypeStruct((n, GROUP_W), BF16),
        scratch_shapes=[pltpu.VMEM((pw, tq), F32),
                        pltpu.VMEM((2, 1, tq), F32), pltpu.VMEM((2, 1, tq), F32),
                        pltpu.VMEM((2, seq, pw), F32),
                        pltpu.VMEM((2, tq, tq), F32), pltpu.VMEM((2, tq, tq), F32)],
        compiler_params=_params("arbitrary", "arbitrary", "arbitrary"),
        name="fox_attention",
    )(qkv, qkv, vt, c)


def _mixout_kernel(x_ref, pr_ref, yr_ref, yf_ref, a0_ref, aup_ref, gup_ref, ka_ref,
                   rk_ref, gng_ref, gnb_ref, gsum_ref, wr_ref, wf_ref, lng_ref, lnb_ref,
                   o_ref):
    gw = GROUP_W
    r = pr_ref[:, 0:gw]
    k = pr_ref[:, gw:2 * gw]
    v = pr_ref[:, 2 * gw:3 * gw]
    ad = pr_ref[:, 3 * gw + LORA_PAD:3 * gw + 2 * LORA_PAD]
    gd = pr_ref[:, 3 * gw + 2 * LORA_PAD:3 * gw + 3 * LORA_PAD]
    a = _sigmoid(a0_ref[...] + _dot(ad.astype(BF16), aup_ref[...]))
    k_mod = k * (1.0 + (a - 1.0) * ka_ref[...])
    gate = _dot(_sigmoid(gd).astype(BF16), gup_ref[...])
    gsum = gsum_ref[...]

    def group_sum(t):
        hi = t.astype(BF16)
        lo = (t - hi.astype(F32)).astype(BF16)
        return _dot(hi, gsum) + _dot(lo, gsum)

    y = yr_ref[...]
    mean = group_sum(y) * (1.0 / HEAD_DIM)
    d = y - mean
    var = group_sum(d * d) * (1.0 / HEAD_DIM)
    yn = d * lax.rsqrt(var + GN_EPS) * gng_ref[...] + gnb_ref[...]
    bonus = group_sum(r * k_mod * rk_ref[...])
    y_rwkv = ((yn + bonus * v) * gate).astype(BF16)
    mixed = _dot(y_rwkv, wr_ref[...]) + _dot(yf_ref[...], wf_ref[...])
    o_ref[...] = _layer_norm(ALPHA * x_ref[...] + mixed, lng_ref[...], lnb_ref[...])


def _mixout(x2, pr, yr, yf, a0, aup, gup, k_a, r_k, gn_g, gn_b, w_r, w_f, ln_g, ln_b, *, tm):
    n, d = x2.shape
    ca = pr.shape[1]
    gidx = lax.broadcasted_iota(jnp.int32, (GROUP_W, GROUP_W), 0) // HEAD_DIM
    gsum = (gidx == gidx.T).astype(BF16)
    vec = _full((1, GROUP_W))
    return pl.pallas_call(
        _mixout_kernel,
        grid=(n // tm,),
        in_specs=[pl.BlockSpec((tm, d), lambda i: (i, 0)),
                  pl.BlockSpec((tm, ca), lambda i: (i, 0)),
                  pl.BlockSpec((tm, GROUP_W), lambda i: (i, 0)),
                  pl.BlockSpec((tm, GROUP_W), lambda i: (i, 0)),
                  vec, _full((LORA_PAD, GROUP_W)), _full((LORA_PAD, GROUP_W)),
                  vec, vec, vec, vec, _full((GROUP_W, GROUP_W)),
                  _full((GROUP_W, d)), _full((GROUP_W, d)),
                  _full((1, d)), _full((1, d))],
        out_specs=pl.BlockSpec((tm, d), lambda i: (i, 0)),
        out_shape=jax.ShapeDtypeStruct((n, d), F32),
        compiler_params=_params("parallel"),
        name="mix_out",
    )(x2, pr, yr, yf, a0, aup, gup, k_a, r_k, gn_g, gn_b, gsum, w_r, w_f, ln_g, ln_b)


def _ffn_kernel(x_ref, wg_ref, wu_ref, wd_ref, lng_ref, lnb_ref, o_ref, acc_ref):
    f = pl.program_id(1)

    @pl.when(f == 0)
    def _():
        acc_ref[...] = jnp.zeros_like(acc_ref)

    xb = x_ref[...].astype(BF16)
    g = _dot(xb, wg_ref[...])
    u = _dot(xb, wu_ref[...])
    h = (g * _sigmoid(g) * u).astype(BF16)
    acc_ref[...] += _dot(h, wd_ref[...])

    @pl.when(f == pl.num_programs(1) - 1)
    def _():
        o_ref[...] = _layer_norm(ALPHA * x_ref[...] + acc_ref[...], lng_ref[...], lnb_ref[...])


def _ffn(x2, wg, wu, wd, ln_g, ln_b, *, tm, tf):
    n, d = x2.shape
    ff = wg.shape[1]
    return pl.pallas_call(
        _ffn_kernel,
        grid=(n // tm, ff // tf),
        in_specs=[pl.BlockSpec((tm, d), lambda i, f: (i, 0)),
                  pl.BlockSpec((d, tf), lambda i, f: (0, f)),
                  pl.BlockSpec((d, tf), lambda i, f: (0, f)),
                  pl.BlockSpec((tf, d), lambda i, f: (f, 0)),
                  _full((1, d)), _full((1, d))],
        out_specs=pl.BlockSpec((tm, d), lambda i, f: (i, 0)),
        out_shape=jax.ShapeDtypeStruct((n, d), F32),
        scratch_shapes=[pltpu.VMEM((tm, d), F32)],
        compiler_params=_params("parallel", "arbitrary"),
        name="ffn_swiglu",
    )(x2, wg, wu, wd, ln_g, ln_b)


def _glu_kernel(x_ref, w_ref, b_ref, o_ref):
    d = o_ref.shape[1]
    xb = x_ref[...].astype(BF16)
    val = _dot(xb, w_ref[:, 0:d]) + b_ref[:, 0:d]
    gat = _dot(xb, w_ref[:, d:2 * d]) + b_ref[:, d:2 * d]
    o_ref[...] = val * _sigmoid(gat)


def _glu(x2, w, b, *, tm):
    n, d = x2.shape
    return pl.pallas_call(
        _glu_kernel,
        grid=(n // tm,),
        in_specs=[pl.BlockSpec((tm, d), lambda i: (i, 0)), _full((d, 2 * d)), _full((1, 2 * d))],
        out_specs=pl.BlockSpec((tm, d), lambda i: (i, 0)),
        out_shape=jax.ShapeDtypeStruct((n, d), F32),
        compiler_params=_params("parallel"),
        name="conv_glu",
    )(x2, w, b)


def _top2_gates(logits):
    lane = lax.broadcasted_iota(jnp.int32, logits.shape, 1).astype(F32)
    lg = jnp.where(lane < N_EXPERTS, logits, NEG_BIG)
    m1 = jnp.max(lg, axis=-1, keepdims=True)
    i1 = jnp.min(jnp.where(lg == m1, lane, 128.0), axis=-1, keepdims=True)
    lg2 = jnp.where(lane == i1, NEG_BIG, lg)
    m2 = jnp.max(lg2, axis=-1, keepdims=True)
    i2 = jnp.min(jnp.where(lg2 == m2, lane, 128.0), axis=-1, keepdims=True)
    e2 = jnp.exp(m2 - m1)
    w1 = 1.0 / (1.0 + e2)
    w2 = e2 / (1.0 + e2)
    return jnp.where(lane == i1, w1, 0.0) + jnp.where(lane == i2, w2, 0.0)


def _conv_kernel(hc_ref, hp_ref, x_ref, wdw_ref, bdw_ref, lng_ref, lnb_ref, w2_ref, b2_ref,
                 pg_ref, pb_ref, wr_ref, x3_ref, x3b_ref, gates_ref, ext_ref, cv_ref,
                 *, tiles_per_seq):
    tm, d = x_ref.shape
    first = pl.program_id(0) % tiles_per_seq == 0
    ext_ref[0:CONV_HALO, :] = jnp.where(first, 0.0, hp_ref[...])
    ext_ref[CONV_HALO:CONV_HALO + tm, :] = hc_ref[...]
    off = CONV_HALO - (CONV_WIDTH - 1)
    rc, cc = 64, 256
    for r0 in range(0, tm, rc):
        for c0 in range(0, d, cc):
            acc = jnp.broadcast_to(bdw_ref[:, c0:c0 + cc], (rc, cc))
            for t in range(CONV_WIDTH):
                acc = acc + wdw_ref[t:t + 1, c0:c0 + cc] * ext_ref[r0 + off + t:r0 + off + t + rc, c0:c0 + cc]
            cv_ref[r0:r0 + rc, c0:c0 + cc] = acc
    hn = _layer_norm(cv_ref[...], lng_ref[...], lnb_ref[...])
    hs = (hn * _sigmoid(hn)).astype(BF16)
    conv = _dot(hs, w2_ref[...]) + b2_ref[...]
    x3 = _layer_norm(ALPHA * x_ref[...] + conv, pg_ref[...], pb_ref[...])
    x3_ref[...] = x3
    x3b_ref[...] = x3.astype(BF16)
    gates_ref[...] = _top2_gates(_dot(x3, wr_ref[...], precision=HIGHEST))


def _conv(hg, x2, w_dw, b_dw, ln_g, ln_b, w2, b2, pg, pb, w_router, *, seq, tm):
    n, d = x2.shape
    ratio = tm // CONV_HALO
    vec = _full((1, d))
    return pl.pallas_call(
        functools.partial(_conv_kernel, tiles_per_seq=seq // tm),
        grid=(n // tm,),
        in_specs=[pl.BlockSpec((tm, d), lambda i: (i, 0)),
                  pl.BlockSpec((CONV_HALO, d), lambda i: (jnp.maximum(i * ratio - 1, 0), 0)),
                  pl.BlockSpec((tm, d), lambda i: (i, 0)),
                  _full((CONV_HALO, d)), vec, vec, vec, _full((d, d)), vec, vec, vec,
                  _full((d, 128))],
        out_specs=[pl.BlockSpec((tm, d), lambda i: (i, 0)),
                   pl.BlockSpec((tm, d), lambda i: (i, 0)),
                   pl.BlockSpec((tm, 128), lambda i: (i, 0))],
        out_shape=[jax.ShapeDtypeStruct((n, d), F32),
                   jax.ShapeDtypeStruct((n, d), BF16),
                   jax.ShapeDtypeStruct((n, 128), F32)],
        scratch_shapes=[pltpu.VMEM((tm + CONV_HALO, d), F32), pltpu.VMEM((tm, d), F32)],
        compiler_params=_params("parallel"),
        name="conv_module",
    )(hg, hg, x2, w_dw, b_dw, ln_g, ln_b, w2, b2, pg, pb, w_router)


def _moe_kernel(xb_ref, gates_ref, wg_ref, wu_ref, wd_ref, x_ref, lng_ref, lnb_ref,
                o_ref, acc_ref):
    e = pl.program_id(1)
    f = pl.program_id(2)

    @pl.when(jnp.logical_and(e == 0, f == 0))
    def _():
        acc_ref[...] = jnp.zeros_like(acc_ref)

    gates = gates_ref[...]
    lane = lax.broadcasted_iota(jnp.int32, gates.shape, 1)
    gate = jnp.sum(jnp.where(lane == e, gates, 0.0), axis=-1, keepdims=True)
    xb = xb_ref[...]
    g = _dot(xb, wg_ref[...])
    u = _dot(xb, wu_ref[...])
    h = (g * _sigmoid(g) * u * gate).astype(BF16)
    acc_ref[...] += _dot(h, wd_ref[...])

    @pl.when(jnp.logical_and(e == pl.num_programs(1) - 1, f == pl.num_programs(2) - 1))
    def _():
        o_ref[...] = _layer_norm(ALPHA * x_ref[...] + acc_ref[...], lng_ref[...], lnb_ref[...])


def _moe(x3, x3b, gates, wg, wu, wd, ln_g, ln_b, *, tm, tf):
    n, d = x3.shape
    ne, _, ff = wg.shape
    return pl.pallas_call(
        _moe_kernel,
        grid=(n // tm, ne, ff // tf),
        in_specs=[pl.BlockSpec((tm, d), lambda i, e, f: (i, 0)),
                  pl.BlockSpec((tm, 128), lambda i, e, f: (i, 0)),
                  pl.BlockSpec((None, d, tf), lambda i, e, f: (e, 0, f)),
                  pl.BlockSpec((None, d, tf), lambda i, e, f: (e, 0, f)),
                  pl.BlockSpec((None, tf, d), lambda i, e, f: (e, f, 0)),
                  pl.BlockSpec((tm, d), lambda i, e, f: (i, 0)),
                  _full((1, d)), _full((1, d))],
        out_specs=pl.BlockSpec((tm, d), lambda i, e, f: (i, 0)),
        out_shape=jax.ShapeDtypeStruct((n, d), F32),
        scratch_shapes=[pltpu.VMEM((tm, d), F32)],
        compiler_params=_params("parallel", "arbitrary", "arbitrary"),
        name="moe_swiglu",
    )(x3b, gates, wg, wu, wd, x3, ln_g, ln_b)


def _pad_cols(w, width):
    return jnp.pad(w, ((0, 0), (0, width - w.shape[1])))


def _pad_rows(w, height):
    return jnp.pad(w, ((0, height - w.shape[0]), (0, 0)))


def _forward(x, mix_w_in, rwkv_mu, rwkv_w0, rwkv_w_up, rwkv_a0, rwkv_a_up, rwkv_g_up,
             rwkv_k_k, rwkv_k_a, rwkv_r_k, rwkv_gn_g, rwkv_gn_b, fox_b_f, mix_w_out,
             mix_ln_g, mix_ln_b, ffn_w_gate, ffn_w_up, ffn_w_down, ffn_ln_g, ffn_ln_b,
             conv_w_pw1, conv_b_pw1, conv_w_dw, conv_b_dw, conv_ln_g, conv_ln_b,
             conv_w_pw2, conv_b_pw2, conv_post_ln_g, conv_post_ln_b,
             moe_w_router, moe_w_gate, moe_w_up, moe_w_down, moe_ln_g, moe_ln_b,
             *, tm=512, chunk=64, nb_rwkv=4, tq=512, tf_ffn=1408, tm_moe=1024, tf_moe=896):
    batch, seq, d = x.shape
    n = batch * seq
    gw = GROUP_W
    x2 = x.reshape(n, d)
    row = lambda t: t.reshape(1, -1)

    w_in = mix_w_in[0]
    mu = rwkv_mu[0]
    o_w, o_a, o_g = 3 * gw, 3 * gw + DECAY_LORA, 3 * gw + DECAY_LORA + AAA_LORA
    o_fox = o_g + GATE_LORA

    def lora_layout(t):
        return jnp.concatenate([t[..., :o_w],
                                _pad_cols(t[..., o_w:o_a], LORA_PAD),
                                _pad_cols(t[..., o_a:o_g], LORA_PAD),
                                _pad_cols(t[..., o_g:o_fox], LORA_PAD)], axis=-1)

    wa = lora_layout(w_in).astype(BF16)
    mu_a = lora_layout(row(mu))
    scale = LOG2E / math.sqrt(HEAD_DIM)
    wb = jnp.concatenate([w_in[:, o_fox:o_fox + gw] * scale,
                          w_in[:, o_fox + gw:o_fox + 3 * gw]], axis=1).astype(BF16)
    wf = _pad_cols(w_in[:, o_fox + 3 * gw:], 128).astype(BF16)
    bf = _pad_cols(row(fox_b_f[0]), 128)

    pr, qkv, c = _inproj(x2, wa, wb, wf, mu_a, bf, seq=seq, tm=tm)

    wup = _pad_rows(rwkv_w_up[0], LORA_PAD).astype(BF16)
    aup = _pad_rows(rwkv_a_up[0], LORA_PAD).astype(BF16)
    gup = _pad_rows(rwkv_g_up[0], LORA_PAD).astype(BF16)
    k_k, k_a, r_k = row(rwkv_k_k[0]), row(rwkv_k_a[0]), row(rwkv_r_k[0])
    yr = _rwkv(pr, row(rwkv_w0[0]), wup, row(rwkv_a0[0]), aup, k_k, k_a,
               batch=batch, seq=seq, chunk=chunk, nb=nb_rwkv)

    yf = _fox(qkv, c, batch=batch, seq=seq, tq=tq)

    w_out = mix_w_out[0].astype(BF16)
    x1 = _mixout(x2, pr, yr, yf, row(rwkv_a0[0]), aup, gup, k_a, r_k,
                 row(rwkv_gn_g[0]), row(rwkv_gn_b[0]), w_out[:gw], w_out[gw:],
                 row(mix_ln_g[0]), row(mix_ln_b[0]), tm=tm)
    x2b = _ffn(x1, ffn_w_gate[0].astype(BF16), ffn_w_up[0].astype(BF16),
               ffn_w_down[0].astype(BF16), row(ffn_ln_g[0]), row(ffn_ln_b[0]), tm=tm, tf=tf_ffn)

    hg = _glu(x2b, conv_w_pw1[0].astype(BF16), row(conv_b_pw1[0]), tm=tm)
    w_router = _pad_cols(moe_w_router[0], 128)
    x3, x3b, gates = _conv(hg, x2b, _pad_rows(conv_w_dw[0], CONV_HALO), row(conv_b_dw[0]),
                           row(conv_ln_g[0]), row(conv_ln_b[0]), conv_w_pw2[0].astype(BF16),
                           row(conv_b_pw2[0]), row(conv_post_ln_g[0]), row(conv_post_ln_b[0]),
                           w_router, seq=seq, tm=tm)
    out = _moe(x3, x3b, gates, moe_w_gate[0].astype(BF16), moe_w_up[0].astype(BF16),
               moe_w_down[0].astype(BF16), row(moe_ln_g[0]), row(moe_ln_b[0]),
               tm=tm_moe, tf=tf_moe)
    return out.reshape(batch, seq, d)


def kernel(x, mix_w_in, rwkv_mu, rwkv_w0, rwkv_w_up, rwkv_a0, rwkv_a_up, rwkv_g_up, rwkv_k_k, rwkv_k_a, rwkv_r_k, rwkv_gn_g, rwkv_gn_b, fox_b_f, mix_w_out, mix_ln_g, mix_ln_b, ffn_w_gate, ffn_w_up, ffn_w_down, ffn_ln_g, ffn_ln_b, conv_w_pw1, conv_b_pw1, conv_w_dw, conv_b_dw, conv_ln_g, conv_ln_b, conv_w_pw2, conv_b_pw2, conv_post_ln_g, conv_post_ln_b, moe_w_router, moe_w_gate, moe_w_up, moe_w_down, moe_ln_g, moe_ln_b):
    return _forward(x, mix_w_in, rwkv_mu, rwkv_w0, rwkv_w_up, rwkv_a0, rwkv_a_up, rwkv_g_up,
                    rwkv_k_k, rwkv_k_a, rwkv_r_k, rwkv_gn_g, rwkv_gn_b, fox_b_f, mix_w_out,
                    mix_ln_g, mix_ln_b, ffn_w_gate, ffn_w_up, ffn_w_down, ffn_ln_g, ffn_ln_b,
                    conv_w_pw1, conv_b_pw1, conv_w_dw, conv_b_dw, conv_ln_g, conv_ln_b,
                    conv_w_pw2, conv_b_pw2, conv_post_ln_g, conv_post_ln_b,
                    moe_w_router, moe_w_gate, moe_w_up, moe_w_down, moe_ln_g, moe_ln_b)
```

```python
import functools
import math

import jax
import jax.numpy as jnp
from jax import lax
from jax.experimental import pallas as pl
from jax.experimental.pallas import tpu as pltpu

F32 = jnp.float32
BF16 = jnp.bfloat16
HIGHEST = lax.Precision.HIGHEST

HEAD_DIM = 64
N_HEADS = 8
GROUP_W = N_HEADS * HEAD_DIM
LORA_PAD = 128
DECAY_LORA = 32
AAA_LORA = 32
GATE_LORA = 96
CONV_WIDTH = 31
CONV_HALO = 32
N_EXPERTS = 8
LN_EPS = 1e-5
GN_EPS = 64e-5
DEPTH = 2
ALPHA = (2.0 * DEPTH) ** 0.25
NEG_BIG = -1e30
LOG2E = math.log2(math.e)
VMEM_LIMIT = 56 * 1024 * 1024


def _dot(a, b, **kw):
    return jnp.dot(a, b, preferred_element_type=F32, **kw)


def _dot_nt(a, b):
    return lax.dot_general(a, b, (((1,), (1,)), ((), ())), preferred_element_type=F32)


def _dot_tn(a, b):
    return lax.dot_general(a, b, (((0,), (0,)), ((), ())), preferred_element_type=F32)


def _sigmoid(z):
    return 1.0 / (1.0 + jnp.exp(-z))


def _softplus(z):
    return jnp.maximum(z, 0.0) + jnp.log1p(jnp.exp(-jnp.abs(z)))


def _layer_norm(h, g, b):
    mu = jnp.mean(h, axis=-1, keepdims=True)
    d = h - mu
    var = jnp.mean(d * d, axis=-1, keepdims=True)
    return d * lax.rsqrt(var + LN_EPS) * g + b


def _params(*sem):
    return pltpu.CompilerParams(dimension_semantics=sem, vmem_limit_bytes=VMEM_LIMIT)


def _full(shape):
    return pl.BlockSpec(shape, lambda *_: (0,) * len(shape))


def _inproj_kernel(x_ref, wa_ref, wb_ref, wf_ref, mu_ref, bf_ref, tri_ref,
                   pr_ref, qkv_ref, c_ref, last_ref, carry_ref, *, tiles_per_seq):
    i = pl.program_id(0)

    @pl.when(i % tiles_per_seq == 0)
    def _():
        last_ref[...] = jnp.zeros_like(last_ref)
        carry_ref[...] = jnp.zeros_like(carry_ref)

    xb = x_ref[...].astype(BF16)
    tm = xb.shape[0]
    row0 = lax.broadcasted_iota(jnp.int32, (tm, 1), 0) == 0
    ca = wa_ref.shape[1]
    cw = 256 if ca % 256 == 0 else 128
    for c0 in range(0, ca, cw):
        p = _dot(xb, wa_ref[:, c0:c0 + cw])
        prev = jnp.where(row0, last_ref[:, c0:c0 + cw], pltpu.roll(p, 1, 0))
        last_ref[:, c0:c0 + cw] = p[tm - 1:tm, :]
        pr_ref[:, c0:c0 + cw] = p + mu_ref[:, c0:c0 + cw] * (prev - p)
    cb = wb_ref.shape[1]
    for c0 in range(0, cb, GROUP_W):
        qkv_ref[:, c0:c0 + GROUP_W] = _dot(xb, wb_ref[:, c0:c0 + GROUP_W]).astype(BF16)
    fl = _dot(xb, wf_ref[...]) + bf_ref[...]
    log_f = jnp.minimum(fl, 0.0) - jnp.log1p(jnp.exp(-jnp.abs(fl)))
    c = _dot(tri_ref[...], log_f, precision=HIGHEST) + carry_ref[...]
    c_ref[...] = c
    carry_ref[...] = c[tm - 1:tm, :]


def _inproj(x2, wa, wb, wf, mu, bf, *, seq, tm):
    n, d = x2.shape
    ca, cb = wa.shape[1], wb.shape[1]
    tri = (lax.broadcasted_iota(jnp.int32, (tm, tm), 1)
           <= lax.broadcasted_iota(jnp.int32, (tm, tm), 0)).astype(F32)
    return pl.pallas_call(
        functools.partial(_inproj_kernel, tiles_per_seq=seq // tm),
        grid=(n // tm,),
        in_specs=[pl.BlockSpec((tm, d), lambda i: (i, 0)),
                  _full((d, ca)), _full((d, cb)), _full((d, 128)),
                  _full((1, ca)), _full((1, 128)), _full((tm, tm))],
        out_specs=[pl.BlockSpec((tm, ca), lambda i: (i, 0)),
                   pl.BlockSpec((tm, cb), lambda i: (i, 0)),
                   pl.BlockSpec((tm, 128), lambda i: (i, 0))],
        out_shape=[jax.ShapeDtypeStruct((n, ca), F32),
                   jax.ShapeDtypeStruct((n, cb), BF16),
                   jax.ShapeDtypeStruct((n, 128), F32)],
        scratch_shapes=[pltpu.VMEM((1, ca), F32), pltpu.VMEM((1, 128), F32)],
        compiler_params=_params("arbitrary"),
        name="inproj",
    )(x2, wa, wb, wf, mu, bf, tri)


def _rwkv_kernel(pr_ref, w0_ref, wup_ref, a0_ref, aup_ref, kk_ref, ka_ref,
                 gsum_ref, tri_ref, y_ref, h_ref, *, chunk, nb):
    @pl.when(pl.program_id(1) == 0)
    def _():
        h_ref[...] = jnp.zeros_like(h_ref)

    gw = GROUP_W
    pw = 2 * HEAD_DIM
    npair = N_HEADS // 2
    rows = 2 * chunk
    log_chunk = int(math.log2(chunk))
    head0 = lax.broadcasted_iota(jnp.int32, (1, pw), 1) < HEAD_DIM
    row = lax.broadcasted_iota(jnp.int32, (rows, rows), 0)
    col = lax.broadcasted_iota(jnp.int32, (rows, rows), 1)
    strict = (col & (chunk - 1)) < (row & (chunk - 1))
    incl = (col & (chunk - 1)) <= (row & (chunk - 1))
    eye = (col == row).astype(F32)
    peye = (lax.broadcasted_iota(jnp.int32, (pw, pw), 0)
            == lax.broadcasted_iota(jnp.int32, (pw, pw), 1))

    def stack(x):
        return jnp.concatenate([jnp.where(head0, x, 0.0), jnp.where(head0, 0.0, x)],
                               axis=0).astype(BF16)

    units = [(b, j) for b in range(nb) for j in range(npair)]
    nu = len(units)
    ar, bk, vs, bhs, khs, pcs = [], [], [], [], [], []
    for b in range(nb):
        r = pr_ref[b, :, 0:gw]
        k = pr_ref[b, :, gw:2 * gw]
        v = pr_ref[b, :, 2 * gw:3 * gw]
        wd = pr_ref[b, :, 3 * gw:3 * gw + LORA_PAD]
        ad = pr_ref[b, :, 3 * gw + LORA_PAD:3 * gw + 2 * LORA_PAD]
        w_pre = w0_ref[...] + _dot(jnp.tanh(wd).astype(BF16), wup_ref[...])
        w = -_softplus(-w_pre) - 0.5
        log_decay = -jnp.exp(w)
        a = _sigmoid(a0_ref[...] + _dot(ad.astype(BF16), aup_ref[...]))
        kk = k * kk_ref[...]
        norm = jnp.sqrt(_dot((kk * kk).astype(BF16), gsum_ref[...]))
        kk = kk / jnp.maximum(norm, 1e-12)
        k_mod = k * (1.0 + (a - 1.0) * ka_ref[...])
        b_vec = kk * a
        cum = _dot(tri_ref[...], log_decay, precision=HIGHEST)
        last = cum[chunk - 1:chunk, :]
        p_inv = jnp.exp(-cum)
        p_tail = jnp.exp(last - cum)
        a_t = -kk * jnp.exp(cum - log_decay)
        r_t = r * jnp.exp(cum)
        b_t = b_vec * p_inv
        k_t = k_mod * p_inv
        b_h = b_vec * p_tail
        k_h = k_mod * p_tail
        p_last = jnp.exp(last)
        for j in range(npair):
            sl = slice(j * pw, (j + 1) * pw)
            ar.append(jnp.concatenate([stack(a_t[:, sl]), stack(r_t[:, sl])], axis=0))
            bk.append(jnp.concatenate([stack(b_t[:, sl]), stack(k_t[:, sl])], axis=0))
            vs.append(stack(v[:, sl]))
            bhs.append(stack(b_h[:, sl]))
            khs.append(stack(k_h[:, sl]))
            pcs.append(jnp.sum(jnp.where(peye, p_last[:, sl], 0.0), axis=1, keepdims=True))

    gram = [_dot_nt(ar[u], bk[u]) for u in range(nu)]
    l_ab = [jnp.where(strict, gram[u][:rows, :rows], 0.0) for u in range(nu)]
    l_akv = [_dot(jnp.where(strict, gram[u][:rows, rows:], 0.0).astype(BF16), vs[u])
             for u in range(nu)]
    m_rb = [jnp.where(incl, gram[u][rows:, :rows], 0.0).astype(BF16) for u in range(nu)]
    m_rkv = [_dot(jnp.where(incl, gram[u][rows:, rows:], 0.0).astype(BF16), vs[u])
             for u in range(nu)]
    t_inv = [eye + l_ab[u] for u in range(nu)]
    xb = [l_ab[u].astype(BF16) for u in range(nu)]
    xp = [_dot(xb[u], xb[u]) for u in range(nu)]
    for step in range(log_chunk - 1):
        xb = [xp[u].astype(BF16) for u in range(nu)]
        if step < log_chunk - 2:
            both = [_dot(jnp.concatenate([t_inv[u].astype(BF16), xb[u]], axis=0), xb[u])
                    for u in range(nu)]
            t_inv = [t_inv[u] + both[u][:rows] for u in range(nu)]
            xp = [both[u][rows:] for u in range(nu)]
        else:
            t_inv = [t_inv[u] + _dot(t_inv[u].astype(BF16), xb[u]) for u in range(nu)]
    tw = [_dot(t_inv[u].astype(BF16),
               jnp.concatenate([ar[u][:rows], l_akv[u].astype(BF16)], axis=1)).astype(BF16)
          for u in range(nu)]
    mw = [_dot(m_rb[u], tw[u]) for u in range(nu)]
    bw = [_dot_tn(bhs[u], tw[u]) for u in range(nu)]
    kv = [_dot_tn(khs[u], vs[u]) for u in range(nu)]
    for u, (b, j) in enumerate(units):
        wy = ar[u][rows:].astype(F32) + mw[u][:, :pw]
        yc = mw[u][:, pw:] + m_rkv[u]
        hf = h_ref[u]
        yh = _dot(jnp.concatenate([wy.astype(BF16), bw[u][:, :pw].astype(BF16)], axis=0),
                  hf.astype(BF16))
        h_ref[u] = pcs[u] * hf + yh[rows:] + bw[u][:, pw:] + kv[u]
        ys = yh[:rows] + yc
        y_ref[b, :, j * pw:(j + 1) * pw] = ys[:chunk] + ys[chunk:]


def _rwkv(pr, w0, wup, a0, aup, k_k, k_a, *, batch, seq, chunk, nb):
    n, ca = pr.shape
    nch = seq // chunk
    gidx = lax.broadcasted_iota(jnp.int32, (GROUP_W, GROUP_W), 0) // HEAD_DIM
    gsum = (gidx == gidx.T).astype(BF16)
    tri = (lax.broadcasted_iota(jnp.int32, (chunk, chunk), 1)
           <= lax.broadcasted_iota(jnp.int32, (chunk, chunk), 0)).astype(F32)
    y = pl.pallas_call(
        functools.partial(_rwkv_kernel, chunk=chunk, nb=nb),
        grid=(batch // nb, nch),
        in_specs=[pl.BlockSpec((nb, chunk, ca), lambda g, c: (g, c, 0)),
                  _full((1, GROUP_W)), _full((LORA_PAD, GROUP_W)),
                  _full((1, GROUP_W)), _full((LORA_PAD, GROUP_W)),
                  _full((1, GROUP_W)), _full((1, GROUP_W)),
                  _full((GROUP_W, GROUP_W)), _full((chunk, chunk))],
        out_specs=pl.BlockSpec((nb, chunk, GROUP_W), lambda g, c: (g, c, 0)),
        out_shape=jax.ShapeDtypeStruct((batch, seq, GROUP_W), F32),
        scratch_shapes=[pltpu.VMEM((nb * (N_HEADS // 2), 2 * HEAD_DIM, 2 * HEAD_DIM), F32)],
        compiler_params=_params("arbitrary", "arbitrary"),
        name="rwkv_scan",
    )(pr.reshape(batch, seq, ca), w0, wup, a0, aup, k_k, k_a, gsum, tri)
    return y.reshape(n, GROUP_W)


def _fox_kernel(q_ref, k_ref, vt_ref, c_ref, o_ref, acc_ref, m_ref, l_ref, cb_ref,
                sa_ref, sb_ref, *, tq):
    j = pl.program_id(1)
    qi = pl.program_id(2)
    pw = 2 * HEAD_DIM
    seq = k_ref.shape[0]
    lane = lax.broadcasted_iota(jnp.int32, (1, pw), 1)

    @pl.when(qi == 0)
    def _():
        def fill(rb, carry):
            rs = pl.multiple_of(rb * tq, tq)
            cblk = c_ref[pl.ds(rs, tq), :]
            for hh in range(2):
                colv = jnp.sum(jnp.where(lane == 2 * j + hh, cblk, 0.0), axis=1, keepdims=True)
                cb_ref[hh, pl.ds(rs, tq), :] = jnp.broadcast_to(colv * LOG2E, (tq, pw))
            return carry
        lax.fori_loop(0, seq // tq, fill, 0)

    first = lane < HEAD_DIM
    q = q_ref[...]
    zero = jnp.zeros_like(q)
    qh = (jnp.where(first, q, zero), jnp.where(first, zero, q))
    acc_ref[...] = jnp.zeros_like(acc_ref)
    m_ref[...] = jnp.full_like(m_ref, NEG_BIG)
    l_ref[...] = jnp.zeros_like(l_ref)
    causal = (lax.broadcasted_iota(jnp.int32, (tq, tq), 0)
              <= lax.broadcasted_iota(jnp.int32, (tq, tq), 1))
    top = lax.broadcasted_iota(jnp.int32, (pw, 1), 0) < HEAD_DIM

    def scores(kb, s_ref):
        kblk = k_ref[pl.ds(pl.multiple_of(kb * tq, tq), tq), :]
        for hh in range(2):
            s_ref[hh] = _dot_nt(kblk, qh[hh])

    def softmax_pv(kb, s_ref, masked):
        ks = pl.multiple_of(kb * tq, tq)
        vt = vt_ref[kb]
        alphas, pvs = [], []
        for hh in range(2):
            cb = cb_ref[hh, pl.ds(ks, tq), :]
            z = s_ref[hh] - jnp.concatenate([cb] * (tq // pw), axis=1)
            if masked:
                z = jnp.where(causal, z, NEG_BIG)
            m_prev = m_ref[hh]
            m_new = jnp.maximum(m_prev, jnp.max(z, axis=0, keepdims=True))
            alpha = jnp.exp2(m_prev - m_new)
            p = jnp.exp2(z - m_new)
            l_ref[hh] = alpha * l_ref[hh] + jnp.sum(p, axis=0, keepdims=True)
            m_ref[hh] = m_new
            alphas.append(alpha)
            pvs.append(_dot(vt, p.astype(BF16)))
        acc_ref[...] = (acc_ref[...] * jnp.where(top, alphas[0], alphas[1])
                        + jnp.where(top, pvs[0], pvs[1]))

    scores(0, sa_ref)

    def body(i, carry):
        scores(2 * i + 1, sb_ref)
        softmax_pv(2 * i, sa_ref, False)
        scores(2 * i + 2, sa_ref)
        softmax_pv(2 * i + 1, sb_ref, False)
        return carry

    lax.fori_loop(0, qi // 2, body, 0)

    @pl.when(qi % 2 == 0)
    def _():
        softmax_pv(qi, sa_ref, True)

    @pl.when(qi % 2 == 1)
    def _():
        scores(qi, sb_ref)
        softmax_pv(qi - 1, sa_ref, False)
        softmax_pv(qi, sb_ref, True)

    out_t = acc_ref[...] / jnp.where(top, l_ref[0], l_ref[1])
    o_ref[...] = out_t.T.astype(BF16)


def _fox(qkv, c, *, batch, seq, tq):
    n = qkv.shape[0]
    nq = seq // tq
    npair = N_HEADS // 2
    pw = 2 * HEAD_DIM
    vt = qkv[:, 2 * GROUP_W:].reshape(batch, nq, tq, npair, pw)
    vt = jnp.transpose(vt, (0, 3, 1, 4, 2)).reshape(batch * npair, nq, pw, tq)
    return pl.pallas_call(
        functools.partial(_fox_kernel, tq=tq),
        grid=(batch, npair, nq),
        in_specs=[pl.BlockSpec((tq, pw), lambda b, j, i: (b * nq + i, j)),
                  pl.BlockSpec((seq, pw), lambda b, j, i: (b, npair + j)),
                  pl.BlockSpec((None, nq, pw, tq), lambda b, j, i: (b * npair + j, 0, 0, 0)),
                  pl.BlockSpec((seq, 128), lambda b, j, i: (b, 0))],
        out_specs=pl.BlockSpec((tq, pw), lambda b, j, i: (b * nq + i, j)),
        out_shape=jax.ShapeDtypeStruct((n, GROUP_W), BF16),
        scratch_shapes=[pltpu.VMEM((pw, tq), F32),
                        pltpu.VMEM((2, 1, tq), F32), pltpu.VMEM((2, 1, tq), F32),
                        pltpu.VMEM((2, seq, pw), F32),
                        pltpu.VMEM((2, tq, tq), F32), pltpu.VMEM((2, tq, tq), F32)],
        compiler_params=_params("arbitrary", "arbitrary", "arbitrary"),
        name="fox_attention",
    )(qkv, qkv, vt, c)


def _mixout_kernel(x_ref, pr_ref, yr_ref, yf_ref, a0_ref, aup_ref, gup_ref, ka_ref,
                   rk_ref, gng_ref, gnb_ref, gsum_ref, wr_ref, wf_ref, lng_ref, lnb_ref,
                   o_ref):
    gw = GROUP_W
    r = pr_ref[:, 0:gw]
    k = pr_ref[:, gw:2 * gw]
    v = pr_ref[:, 2 * gw:3 * gw]
    ad = pr_ref[:, 3 * gw + LORA_PAD:3 * gw + 2 * LORA_PAD]
    gd = pr_ref[:, 3 * gw + 2 * LORA_PAD:3 * gw + 3 * LORA_PAD]
    a = _sigmoid(a0_ref[...] + _dot(ad.astype(BF16), aup_ref[...]))
    k_mod = k * (1.0 + (a - 1.0) * ka_ref[...])
    gate = _dot(_sigmoid(gd).astype(BF16), gup_ref[...])
    gsum = gsum_ref[...]

    def group_sum(t):
        hi = t.astype(BF16)
        lo = (t - hi.astype(F32)).astype(BF16)
        return _dot(hi, gsum) + _dot(lo, gsum)

    y = yr_ref[...]
    mean = group_sum(y) * (1.0 / HEAD_DIM)
    d = y - mean
    var = group_sum(d * d) * (1.0 / HEAD_DIM)
    yn = d * lax.rsqrt(var + GN_EPS) * gng_ref[...] + gnb_ref[...]
    bonus = group_sum(r * k_mod * rk_ref[...])
    y_rwkv = ((yn + bonus * v) * gate).astype(BF16)
    mixed = _dot(y_rwkv, wr_ref[...]) + _dot(yf_ref[...], wf_ref[...])
    o_ref[...] = _layer_norm(ALPHA * x_ref[...] + mixed, lng_ref[...], lnb_ref[...])


def _mixout(x2, pr, yr, yf, a0, aup, gup, k_a, r_k, gn_g, gn_b, w_r, w_f, ln_g, ln_b, *, tm):
    n, d = x2.shape
    ca = pr.shape[1]
    gidx = lax.broadcasted_iota(jnp.int32, (GROUP_W, GROUP_W), 0) // HEAD_DIM
    gsum = (gidx == gidx.T).astype(BF16)
    vec = _full((1, GROUP_W))
    return pl.pallas_call(
        _mixout_kernel,
        grid=(n // tm,),
        in_specs=[pl.BlockSpec((tm, d), lambda i: (i, 0)),
                  pl.BlockSpec((tm, ca), lambda i: (i, 0)),
                  pl.BlockSpec((tm, GROUP_W), lambda i: (i, 0)),
                  pl.BlockSpec((tm, GROUP_W), lambda i: (i, 0)),
                  vec, _full((LORA_PAD, GROUP_W)), _full((LORA_PAD, GROUP_W)),
                  vec, vec, vec, vec, _full((GROUP_W, GROUP_W)),
                  _full((GROUP_W, d)), _full((GROUP_W, d)),
                  _full((1, d)), _full((1, d))],
        out_specs=pl.BlockSpec((tm, d), lambda i: (i, 0)),
        out_shape=jax.ShapeDtypeStruct((n, d), F32),
        compiler_params=_params("parallel"),
        name="mix_out",
    )(x2, pr, yr, yf, a0, aup, gup, k_a, r_k, gn_g, gn_b, gsum, w_r, w_f, ln_g, ln_b)


def _ffn_kernel(x_ref, wg_ref, wu_ref, wd_ref, lng_ref, lnb_ref, o_ref, acc_ref):
    f = pl.program_id(1)

    @pl.when(f == 0)
    def _():
        acc_ref[...] = jnp.zeros_like(acc_ref)

    xb = x_ref[...].astype(BF16)
    g = _dot(xb, wg_ref[...])
    u = _dot(xb, wu_ref[...])
    h = (g * _sigmoid(g) * u).astype(BF16)
    acc_ref[...] += _dot(h, wd_ref[...])

    @pl.when(f == pl.num_programs(1) - 1)
    def _():
        o_ref[...] = _layer_norm(ALPHA * x_ref[...] + acc_ref[...], lng_ref[...], lnb_ref[...])


def _ffn(x2, wg, wu, wd, ln_g, ln_b, *, tm, tf):
    n, d = x2.shape
    ff = wg.shape[1]
    return pl.pallas_call(
        _ffn_kernel,
        grid=(n // tm, ff // tf),
        in_specs=[pl.BlockSpec((tm, d), lambda i, f: (i, 0)),
                  pl.BlockSpec((d, tf), lambda i, f: (0, f)),
                  pl.BlockSpec((d, tf), lambda i, f: (0, f)),
                  pl.BlockSpec((tf, d), lambda i, f: (f, 0)),
                  _full((1, d)), _full((1, d))],
        out_specs=pl.BlockSpec((tm, d), lambda i, f: (i, 0)),
        out_shape=jax.ShapeDtypeStruct((n, d), F32),
        scratch_shapes=[pltpu.VMEM((tm, d), F32)],
        compiler_params=_params("parallel", "arbitrary"),
        name="ffn_swiglu",
    )(x2, wg, wu, wd, ln_g, ln_b)


def _glu_kernel(x_ref, w_ref, b_ref, o_ref):
    d = o_ref.shape[1]
    xb = x_ref[...].astype(BF16)
    val = _dot(xb, w_ref[:, 0:d]) + b_ref[:, 0:d]
    gat = _dot(xb, w_ref[:, d:2 * d]) + b_ref[:, d:2 * d]
    o_ref[...] = val * _sigmoid(gat)


def _glu(x2, w, b, *, tm):
    n, d = x2.shape
    return pl.pallas_call(
        _glu_kernel,
        grid=(n // tm,),
        in_specs=[pl.BlockSpec((tm, d), lambda i: (i, 0)), _full((d, 2 * d)), _full((1, 2 * d))],
        out_specs=pl.BlockSpec((tm, d), lambda i: (i, 0)),
        out_shape=jax.ShapeDtypeStruct((n, d), F32),
        compiler_params=_params("parallel"),
        name="conv_glu",
    )(x2, w, b)


def _top2_gates(logits):
    lane = lax.broadcasted_iota(jnp.int32, logits.shape, 1).astype(F32)
    lg = jnp.where(lane < N_EXPERTS, logits, NEG_BIG)
    m1 = jnp.max(lg, axis=-1, keepdims=True)
    i1 = jnp.min(jnp.where(lg == m1, lane, 128.0), axis=-1, keepdims=True)
    lg2 = jnp.where(lane == i1, NEG_BIG, lg)
    m2 = jnp.max(lg2, axis=-1, keepdims=True)
    i2 = jnp.min(jnp.where(lg2 == m2, lane, 128.0), axis=-1, keepdims=True)
    e2 = jnp.exp(m2 - m1)
    w1 = 1.0 / (1.0 + e2)
    w2 = e2 / (1.0 + e2)
    return jnp.where(lane == i1, w1, 0.0) + jnp.where(lane == i2, w2, 0.0)


def _conv_kernel(hc_ref, hp_ref, x_ref, wdw_ref, bdw_ref, lng_ref, lnb_ref, w2_ref, b2_ref,
                 pg_ref, pb_ref, wr_ref, x3_ref, x3b_ref, gates_ref, ext_ref, cv_ref,
                 *, tiles_per_seq):
    tm, d = x_ref.shape
    first = pl.program_id(0) % tiles_per_seq == 0
    ext_ref[0:CONV_HALO, :] = jnp.where(first, 0.0, hp_ref[...])
    ext_ref[CONV_HALO:CONV_HALO + tm, :] = hc_ref[...]
    off = CONV_HALO - (CONV_WIDTH - 1)
    rc, cc = 64, 256
    for r0 in range(0, tm, rc):
        for c0 in range(0, d, cc):
            acc = jnp.broadcast_to(bdw_ref[:, c0:c0 + cc], (rc, cc))
            for t in range(CONV_WIDTH):
                acc = acc + wdw_ref[t:t + 1, c0:c0 + cc] * ext_ref[r0 + off + t:r0 + off + t + rc, c0:c0 + cc]
            cv_ref[r0:r0 + rc, c0:c0 + cc] = acc
    hn = _layer_norm(cv_ref[...], lng_ref[...], lnb_ref[...])
    hs = (hn * _sigmoid(hn)).astype(BF16)
    conv = _dot(hs, w2_ref[...]) + b2_ref[...]
    x3 = _layer_norm(ALPHA * x_ref[...] + conv, pg_ref[...], pb_ref[...])
    x3_ref[...] = x3
    x3b_ref[...] = x3.astype(BF16)
    gates_ref[...] = _top2_gates(_dot(x3, wr_ref[...], precision=HIGHEST))


def _conv(hg, x2, w_dw, b_dw, ln_g, ln_b, w2, b2, pg, pb, w_router, *, seq, tm):
    n, d = x2.shape
    ratio = tm // CONV_HALO
    vec = _full((1, d))
    return pl.pallas_call(
        functools.partial(_conv_kernel, tiles_per_seq=seq // tm),
        grid=(n // tm,),
        in_specs=[pl.BlockSpec((tm, d), lambda i: (i, 0)),
                  pl.BlockSpec((CONV_HALO, d), lambda i: (jnp.maximum(i * ratio - 1, 0), 0)),
                  pl.BlockSpec((tm, d), lambda i: (i, 0)),
                  _full((CONV_HALO, d)), vec, vec, vec, _full((d, d)), vec, vec, vec,
                  _full((d, 128))],
        out_specs=[pl.BlockSpec((tm, d), lambda i: (i, 0)),
                   pl.BlockSpec((tm, d), lambda i: (i, 0)),
                   pl.BlockSpec((tm, 128), lambda i: (i, 0))],
        out_shape=[jax.ShapeDtypeStruct((n, d), F32),
                   jax.ShapeDtypeStruct((n, d), BF16),
                   jax.ShapeDtypeStruct((n, 128), F32)],
        scratch_shapes=[pltpu.VMEM((tm + CONV_HALO, d), F32), pltpu.VMEM((tm, d), F32)],
        compiler_params=_params("parallel"),
        name="conv_module",
    )(hg, hg, x2, w_dw, b_dw, ln_g, ln_b, w2, b2, pg, pb, w_router)


MOE_SUB = 256
MOE_CUM = 256


def _moe_kernel(xb_ref, gates_ref, tri_ref, wg_ref, wu_ref, wd_ref, x_ref, lng_ref, lnb_ref,
                o_ref, xg_ref, ya_ref, gg_ref, pos_ref, tot_ref, nsb_ref):
    e = pl.program_id(1)
    f = pl.program_id(2)
    last_e = pl.num_programs(1) - 1
    last_f = pl.num_programs(2) - 1
    tm = xb_ref.shape[0]
    lane = lax.broadcasted_iota(jnp.int32, (1, 128), 1)

    @pl.when(jnp.logical_and(e == 0, f == 0))
    def _():
        o_ref[...] = jnp.zeros_like(o_ref)
        carry = jnp.zeros((1, 128), F32)
        for r0 in range(0, tm, MOE_CUM):
            sel = jnp.where(gates_ref[r0:r0 + MOE_CUM, :] > 0.0, 1.0, 0.0)
            pos_ref[r0:r0 + MOE_CUM, :] = _dot(tri_ref[...], sel.astype(BF16)) + carry
            carry = carry + jnp.sum(sel, axis=0, keepdims=True)
        tot_ref[...] = carry

    def expert_columns():
        gate = jnp.sum(jnp.where(lane == e, gates_ref[...], 0.0), axis=1, keepdims=True)
        pos = jnp.sum(jnp.where(lane == e, pos_ref[...], 0.0), axis=1, keepdims=True)
        return gate, jnp.where(gate > 0.0, pos, -1.0)

    def one_hot(pos, sbi):
        slot = (lax.broadcasted_iota(jnp.int32, (1, MOE_SUB), 1) + sbi * MOE_SUB).astype(F32)
        return jnp.where(pos == slot, 1.0, 0.0).astype(BF16)

    @pl.when(f == 0)
    def _():
        count = jnp.sum(jnp.where(lane == e, tot_ref[...], 0.0)).astype(jnp.int32)
        nsb = (count + (MOE_SUB - 1)) // MOE_SUB
        nsb_ref[0] = nsb
        gate, pos = expert_columns()
        g_hi = gate.astype(BF16).astype(F32)
        g_mid = (gate - g_hi).astype(BF16).astype(F32)
        g_lo = gate - g_hi - g_mid
        g_parts = jnp.where(lane == 0, g_hi, jnp.where(lane == 1, g_mid,
                                                      jnp.where(lane == 2, g_lo, 0.0))).astype(BF16)

        def gather(sbi, carry):
            rs = pl.multiple_of(sbi * MOE_SUB, MOE_SUB)
            oh = one_hot(pos, sbi)
            xg_ref[pl.ds(rs, MOE_SUB), :] = _dot_tn(oh, xb_ref[...]).astype(BF16)
            gg = jnp.sum(_dot_tn(oh, g_parts), axis=1, keepdims=True)
            gg_ref[pl.ds(rs, MOE_SUB), :] = jnp.broadcast_to(gg, (MOE_SUB, 128))
            ya_ref[pl.ds(rs, MOE_SUB), :] = jnp.zeros((MOE_SUB, ya_ref.shape[1]), F32)
            return carry
        lax.fori_loop(0, nsb, gather, 0)

    nsb = nsb_ref[0]

    def ffn(sbi, carry):
        rs = pl.multiple_of(sbi * MOE_SUB, MOE_SUB)
        xg = xg_ref[pl.ds(rs, MOE_SUB), :]
        g = _dot(xg, wg_ref[...])
        u = _dot(xg, wu_ref[...])
        h = (g * _sigmoid(g) * u * gg_ref[pl.ds(rs, MOE_SUB), 0:1]).astype(BF16)
        ya_ref[pl.ds(rs, MOE_SUB), :] += _dot(h, wd_ref[...])
        return carry
    lax.fori_loop(0, nsb, ffn, 0)

    @pl.when(f == last_f)
    def _():
        _, pos = expert_columns()

        def scatter(sbi, carry):
            rs = pl.multiple_of(sbi * MOE_SUB, MOE_SUB)
            o_ref[...] += _dot(one_hot(pos, sbi), ya_ref[pl.ds(rs, MOE_SUB), :].astype(BF16))
            return carry
        lax.fori_loop(0, nsb, scatter, 0)

    @pl.when(jnp.logical_and(e == last_e, f == last_f))
    def _():
        o_ref[...] = _layer_norm(ALPHA * x_ref[...] + o_ref[...], lng_ref[...], lnb_ref[...])


def _moe(x3, x3b, gates, wg, wu, wd, ln_g, ln_b, *, tm, tf):
    n, d = x3.shape
    ne, _, ff = wg.shape
    tri = (lax.broadcasted_iota(jnp.int32, (MOE_CUM, MOE_CUM), 1)
           < lax.broadcasted_iota(jnp.int32, (MOE_CUM, MOE_CUM), 0)).astype(BF16)
    once = pl.Buffered(1)
    return pl.pallas_call(
        _moe_kernel,
        grid=(n // tm, ne, ff // tf),
        in_specs=[pl.BlockSpec((tm, d), lambda i, e, f: (i, 0), pipeline_mode=once),
                  pl.BlockSpec((tm, 128), lambda i, e, f: (i, 0)),
                  _full((MOE_CUM, MOE_CUM)),
                  pl.BlockSpec((None, d, tf), lambda i, e, f: (e, 0, f)),
                  pl.BlockSpec((None, d, tf), lambda i, e, f: (e, 0, f)),
                  pl.BlockSpec((None, tf, d), lambda i, e, f: (e, f, 0)),
                  pl.BlockSpec((tm, d), lambda i, e, f: (i, 0), pipeline_mode=once),
                  _full((1, d)), _full((1, d))],
        out_specs=pl.BlockSpec((tm, d), lambda i, e, f: (i, 0), pipeline_mode=once),
        out_shape=jax.ShapeDtypeStruct((n, d), F32),
        scratch_shapes=[pltpu.VMEM((tm, d), BF16), pltpu.VMEM((tm, d), F32),
                        pltpu.VMEM((tm, 128), F32), pltpu.VMEM((tm, 128), F32),
                        pltpu.VMEM((1, 128), F32), pltpu.SMEM((1,), jnp.int32)],
        compiler_params=_params("arbitrary", "arbitrary", "arbitrary"),
        name="moe_swiglu",
    )(x3b, gates, tri, wg, wu, wd, x3, ln_g, ln_b)


def _pad_cols(w, width):
    return jnp.pad(w, ((0, 0), (0, width - w.shape[1])))


def _pad_rows(w, height):
    return jnp.pad(w, ((0, height - w.shape[0]), (0, 0)))


def _forward(x, mix_w_in, rwkv_mu, rwkv_w0, rwkv_w_up, rwkv_a0, rwkv_a_up, rwkv_g_up,
             rwkv_k_k, rwkv_k_a, rwkv_r_k, rwkv_gn_g, rwkv_gn_b, fox_b_f, mix_w_out,
             mix_ln_g, mix_ln_b, ffn_w_gate, ffn_w_up, ffn_w_down, ffn_ln_g, ffn_ln_b,
             conv_w_pw1, conv_b_pw1, conv_w_dw, conv_b_dw, conv_ln_g, conv_ln_b,
             conv_w_pw2, conv_b_pw2, conv_post_ln_g, conv_post_ln_b,
             moe_w_router, moe_w_gate, moe_w_up, moe_w_down, moe_ln_g, moe_ln_b,
             *, tm=512, chunk=64, nb_rwkv=4, tq=512, tf_ffn=1408, tm_moe=2048, tf_moe=896):
    batch, seq, d = x.shape
    n = batch * seq
    gw = GROUP_W
    x2 = x.reshape(n, d)
    row = lambda t: t.reshape(1, -1)

    w_in = mix_w_in[0]
    mu = rwkv_mu[0]
    o_w, o_a, o_g = 3 * gw, 3 * gw + DECAY_LORA, 3 * gw + DECAY_LORA + AAA_LORA
    o_fox = o_g + GATE_LORA

    def lora_layout(t):
        return jnp.concatenate([t[..., :o_w],
                                _pad_cols(t[..., o_w:o_a], LORA_PAD),
                                _pad_cols(t[..., o_a:o_g], LORA_PAD),
                                _pad_cols(t[..., o_g:o_fox], LORA_PAD)], axis=-1)

    wa = lora_layout(w_in).astype(BF16)
    mu_a = lora_layout(row(mu))
    scale = LOG2E / math.sqrt(HEAD_DIM)
    wb = jnp.concatenate([w_in[:, o_fox:o_fox + gw] * scale,
                          w_in[:, o_fox + gw:o_fox + 3 * gw]], axis=1).astype(BF16)
    wf = _pad_cols(w_in[:, o_fox + 3 * gw:], 128).astype(BF16)
    bf = _pad_cols(row(fox_b_f[0]), 128)

    pr, qkv, c = _inproj(x2, wa, wb, wf, mu_a, bf, seq=seq, tm=tm)

    wup = _pad_rows(rwkv_w_up[0], LORA_PAD).astype(BF16)
    aup = _pad_rows(rwkv_a_up[0], LORA_PAD).astype(BF16)
    gup = _pad_rows(rwkv_g_up[0], LORA_PAD).astype(BF16)
    k_k, k_a, r_k = row(rwkv_k_k[0]), row(rwkv_k_a[0]), row(rwkv_r_k[0])
    yr = _rwkv(pr, row(rwkv_w0[0]), wup, row(rwkv_a0[0]), aup, k_k, k_a,
               batch=batch, seq=seq, chunk=chunk, nb=nb_rwkv)

    yf = _fox(qkv, c, batch=batch, seq=seq, tq=tq)

    w_out = mix_w_out[0].astype(BF16)
    x1 = _mixout(x2, pr, yr, yf, row(rwkv_a0[0]), aup, gup, k_a, r_k,
                 row(rwkv_gn_g[0]), row(rwkv_gn_b[0]), w_out[:gw], w_out[gw:],
                 row(mix_ln_g[0]), row(mix_ln_b[0]), tm=tm)
    x2b = _ffn(x1, ffn_w_gate[0].astype(BF16), ffn_w_up[0].astype(BF16),
               ffn_w_down[0].astype(BF16), row(ffn_ln_g[0]), row(ffn_ln_b[0]), tm=tm, tf=tf_ffn)

    hg = _glu(x2b, conv_w_pw1[0].astype(BF16), row(conv_b_pw1[0]), tm=tm)
    w_router = _pad_cols(moe_w_router[0], 128)
    x3, x3b, gates = _conv(hg, x2b, _pad_rows(conv_w_dw[0], CONV_HALO), row(conv_b_dw[0]),
                           row(conv_ln_g[0]), row(conv_ln_b[0]), conv_w_pw2[0].astype(BF16),
                           row(conv_b_pw2[0]), row(conv_post_ln_g[0]), row(conv_post_ln_b[0]),
                           w_router, seq=seq, tm=tm)
    out = _moe(x3, x3b, gates, moe_w_gate[0].astype(BF16), moe_w_up[0].astype(BF16),
               moe_w_down[0].astype(BF16), row(moe_ln_g[0]), row(moe_ln_b[0]),
               tm=tm_moe, tf=tf_moe)
    return out.reshape(batch, seq, d)


def kernel(x, mix_w_in, rwkv_mu, rwkv_w0, rwkv_w_up, rwkv_a0, rwkv_a_up, rwkv_g_up, rwkv_k_k, rwkv_k_a, rwkv_r_k, rwkv_gn_g, rwkv_gn_b, fox_b_f, mix_w_out, mix_ln_g, mix_ln_b, ffn_w_gate, ffn_w_up, ffn_w_down, ffn_ln_g, ffn_ln_b, conv_w_pw1, conv_b_pw1, conv_w_dw, conv_b_dw, conv_ln_g, conv_ln_b, conv_w_pw2, conv_b_pw2, conv_post_ln_g, conv_post_ln_b, moe_w_router, moe_w_gate, moe_w_up, moe_w_down, moe_ln_g, moe_ln_b):
    return _forward(x, mix_w_in, rwkv_mu, rwkv_w0, rwkv_w_up, rwkv_a0, rwkv_a_up, rwkv_g_up,
                    rwkv_k_k, rwkv_k_a, rwkv_r_k, rwkv_gn_g, rwkv_gn_b, fox_b_f, mix_w_out,
                    mix_ln_g, mix_ln_b, ffn_w_gate, ffn_w_up, ffn_w_down, ffn_ln_g, ffn_ln_b,
                    conv_w_pw1, conv_b_pw1, conv_w_dw, conv_b_dw, conv_ln_g, conv_ln_b,
                    conv_w_pw2, conv_b_pw2, conv_post_ln_g, conv_post_ln_b,
                    moe_w_router, moe_w_gate, moe_w_up, moe_w_down, moe_ln_g, moe_ln_b)
```

```python
import functools
import math

import jax
import jax.numpy as jnp
from jax import lax
from jax.experimental import pallas as pl
from jax.experimental.pallas import tpu as pltpu

F32 = jnp.float32
BF16 = jnp.bfloat16
HIGHEST = lax.Precision.HIGHEST

HEAD_DIM = 64
N_HEADS = 8
GROUP_W = N_HEADS * HEAD_DIM
LORA_PAD = 128
DECAY_LORA = 32
AAA_LORA = 32
GATE_LORA = 96
CONV_WIDTH = 31
CONV_HALO = 32
SUBLANES = 8
N_EXPERTS = 8
LN_EPS = 1e-5
GN_EPS = 64e-5
DEPTH = 2
ALPHA = (2.0 * DEPTH) ** 0.25
NEG_BIG = -1e30
LOG2E = math.log2(math.e)
VMEM_LIMIT = 56 * 1024 * 1024


def _dot(a, b, **kw):
    return jnp.dot(a, b, preferred_element_type=F32, **kw)


def _dot_nt(a, b):
    return lax.dot_general(a, b, (((1,), (1,)), ((), ())), preferred_element_type=F32)


def _dot_tn(a, b):
    return lax.dot_general(a, b, (((0,), (0,)), ((), ())), preferred_element_type=F32)


def _dot_exact_lhs(a, v):
    hi = v.astype(BF16)
    rem = v - hi.astype(F32)
    mid = rem.astype(BF16)
    lo = (rem - mid.astype(F32)).astype(BF16)
    w = v.shape[1]
    out = _dot(a, jnp.concatenate([hi, mid, lo], axis=1))
    return out[:, :w] + out[:, w:2 * w] + out[:, 2 * w:]


def _sigmoid(z):
    return 1.0 / (1.0 + jnp.exp(-z))


def _softplus(z):
    return jnp.maximum(z, 0.0) + jnp.log1p(jnp.exp(-jnp.abs(z)))


def _layer_norm(h, g, b):
    mu = jnp.mean(h, axis=-1, keepdims=True)
    d = h - mu
    var = jnp.mean(d * d, axis=-1, keepdims=True)
    return d * lax.rsqrt(var + LN_EPS) * g + b


def _params(*sem):
    return pltpu.CompilerParams(dimension_semantics=sem, vmem_limit_bytes=VMEM_LIMIT)


def _full(shape):
    return pl.BlockSpec(shape, lambda *_: (0,) * len(shape))


def _inproj_kernel(x_ref, wa_ref, wb_ref, wf_ref, mu_ref, bf_ref, tri_ref,
                   pr_ref, qkv_ref, c_ref, last_ref, carry_ref, *, tiles_per_seq):
    i = pl.program_id(0)

    @pl.when(i % tiles_per_seq == 0)
    def _():
        last_ref[...] = jnp.zeros_like(last_ref)
        carry_ref[...] = jnp.zeros_like(carry_ref)

    xb = x_ref[...].astype(BF16)
    tm = xb.shape[0]
    row0 = lax.broadcasted_iota(jnp.int32, (tm, 1), 0) == 0
    ca = wa_ref.shape[1]
    for c0 in range(0, ca, GROUP_W):
        cw = min(GROUP_W, ca - c0)
        p = _dot(xb, wa_ref[:, c0:c0 + cw])
        prev = jnp.where(row0, last_ref[:, c0:c0 + cw], pltpu.roll(p, 1, 0))
        last_ref[:, c0:c0 + cw] = p[tm - 1:tm, :]
        pr_ref[:, c0:c0 + cw] = p + mu_ref[:, c0:c0 + cw] * (prev - p)
    cb = wb_ref.shape[1]
    for c0 in range(0, cb, GROUP_W):
        qkv_ref[:, c0:c0 + GROUP_W] = _dot(xb, wb_ref[:, c0:c0 + GROUP_W]).astype(BF16)
    fl = _dot(xb, wf_ref[...]) + bf_ref[...]
    log_f = jnp.minimum(fl, 0.0) - jnp.log1p(jnp.exp(-jnp.abs(fl)))
    c = _dot_exact_lhs(tri_ref[...], log_f) + carry_ref[...]
    c_ref[...] = c
    carry_ref[...] = c[tm - 1:tm, :]


def _inproj(x2, wa, wb, wf, mu, bf, *, seq, tm):
    n, d = x2.shape
    ca, cb = wa.shape[1], wb.shape[1]
    tri = (lax.broadcasted_iota(jnp.int32, (tm, tm), 1)
           <= lax.broadcasted_iota(jnp.int32, (tm, tm), 0)).astype(BF16)
    return pl.pallas_call(
        functools.partial(_inproj_kernel, tiles_per_seq=seq // tm),
        grid=(n // tm,),
        in_specs=[pl.BlockSpec((tm, d), lambda i: (i, 0)),
                  _full((d, ca)), _full((d, cb)), _full((d, 128)),
                  _full((1, ca)), _full((1, 128)), _full((tm, tm))],
        out_specs=[pl.BlockSpec((tm, ca), lambda i: (i, 0)),
                   pl.BlockSpec((tm, cb), lambda i: (i, 0)),
                   pl.BlockSpec((tm, 128), lambda i: (i, 0))],
        out_shape=[jax.ShapeDtypeStruct((n, ca), F32),
                   jax.ShapeDtypeStruct((n, cb), BF16),
                   jax.ShapeDtypeStruct((n, 128), F32)],
        scratch_shapes=[pltpu.VMEM((1, ca), F32), pltpu.VMEM((1, 128), F32)],
        compiler_params=_params("arbitrary"),
        name="inproj",
    )(x2, wa, wb, wf, mu, bf, tri)


def _rwkv_kernel(pr_ref, w0_ref, wup_ref, a0_ref, aup_ref, kk_ref, ka_ref,
                 gsum_ref, tri_ref, y_ref, h_ref, *, chunk, nb):
    @pl.when(pl.program_id(1) == 0)
    def _():
        h_ref[...] = jnp.zeros_like(h_ref)

    gw = GROUP_W
    pw = 2 * HEAD_DIM
    npair = N_HEADS // 2
    rows = 2 * chunk
    log_chunk = int(math.log2(chunk))
    head0 = lax.broadcasted_iota(jnp.int32, (1, pw), 1) < HEAD_DIM
    row = lax.broadcasted_iota(jnp.int32, (rows, rows), 0)
    col = lax.broadcasted_iota(jnp.int32, (rows, rows), 1)
    strict = (col & (chunk - 1)) < (row & (chunk - 1))
    incl = (col & (chunk - 1)) <= (row & (chunk - 1))
    eye = (col == row).astype(F32)
    peye = (lax.broadcasted_iota(jnp.int32, (pw, pw), 0)
            == lax.broadcasted_iota(jnp.int32, (pw, pw), 1))

    def stack(x):
        return jnp.concatenate([jnp.where(head0, x, 0.0), jnp.where(head0, 0.0, x)],
                               axis=0).astype(BF16)

    units = [(b, j) for b in range(nb) for j in range(npair)]
    nu = len(units)
    ar, bk, vs, bhs, khs, pcs = [], [], [], [], [], []
    for b in range(nb):
        r = pr_ref[b, :, 0:gw]
        k = pr_ref[b, :, gw:2 * gw]
        v = pr_ref[b, :, 2 * gw:3 * gw]
        wd = pr_ref[b, :, 3 * gw:3 * gw + LORA_PAD]
        ad = pr_ref[b, :, 3 * gw + LORA_PAD:3 * gw + 2 * LORA_PAD]
        w_pre = w0_ref[...] + _dot(jnp.tanh(wd).astype(BF16), wup_ref[...])
        w = -_softplus(-w_pre) - 0.5
        log_decay = -jnp.exp(w)
        a = _sigmoid(a0_ref[...] + _dot(ad.astype(BF16), aup_ref[...]))
        kk = k * kk_ref[...]
        norm = jnp.sqrt(_dot((kk * kk).astype(BF16), gsum_ref[...]))
        kk = kk / jnp.maximum(norm, 1e-12)
        k_mod = k * (1.0 + (a - 1.0) * ka_ref[...])
        b_vec = kk * a
        cum = _dot_exact_lhs(tri_ref[...], log_decay)
        last = cum[chunk - 1:chunk, :]
        p_inv = jnp.exp(-cum)
        p_tail = jnp.exp(last - cum)
        a_t = -kk * jnp.exp(cum - log_decay)
        r_t = r * jnp.exp(cum)
        b_t = b_vec * p_inv
        k_t = k_mod * p_inv
        b_h = b_vec * p_tail
        k_h = k_mod * p_tail
        p_last = jnp.exp(last)
        for j in range(npair):
            sl = slice(j * pw, (j + 1) * pw)
            ar.append(jnp.concatenate([stack(a_t[:, sl]), stack(r_t[:, sl])], axis=0))
            bk.append(jnp.concatenate([stack(b_t[:, sl]), stack(k_t[:, sl])], axis=0))
            vs.append(stack(v[:, sl]))
            bhs.append(stack(b_h[:, sl]))
            khs.append(stack(k_h[:, sl]))
            pcs.append(jnp.sum(jnp.where(peye, p_last[:, sl], 0.0), axis=1, keepdims=True))

    gram = [_dot_nt(ar[u], bk[u]) for u in range(nu)]
    l_ab = [jnp.where(strict, gram[u][:rows, :rows], 0.0) for u in range(nu)]
    l_akv = [_dot(jnp.where(strict, gram[u][:rows, rows:], 0.0).astype(BF16), vs[u])
             for u in range(nu)]
    m_rb = [jnp.where(incl, gram[u][rows:, :rows], 0.0).astype(BF16) for u in range(nu)]
    m_rkv = [_dot(jnp.where(incl, gram[u][rows:, rows:], 0.0).astype(BF16), vs[u])
             for u in range(nu)]
    t_inv = [eye + l_ab[u] for u in range(nu)]
    xb = [l_ab[u].astype(BF16) for u in range(nu)]
    xp = [_dot(xb[u], xb[u]) for u in range(nu)]
    for step in range(log_chunk - 1):
        xb = [xp[u].astype(BF16) for u in range(nu)]
        if step < log_chunk - 2:
            both = [_dot(jnp.concatenate([t_inv[u].astype(BF16), xb[u]], axis=0), xb[u])
                    for u in range(nu)]
            t_inv = [t_inv[u] + both[u][:rows] for u in range(nu)]
            xp = [both[u][rows:] for u in range(nu)]
        else:
            t_inv = [t_inv[u] + _dot(t_inv[u].astype(BF16), xb[u]) for u in range(nu)]
    tw = [_dot(t_inv[u].astype(BF16),
               jnp.concatenate([ar[u][:rows], l_akv[u].astype(BF16)], axis=1)).astype(BF16)
          for u in range(nu)]
    mw = [_dot(m_rb[u], tw[u]) for u in range(nu)]
    bw = [_dot_tn(bhs[u], tw[u]) for u in range(nu)]
    kv = [_dot_tn(khs[u], vs[u]) for u in range(nu)]
    for u, (b, j) in enumerate(units):
        wy = ar[u][rows:].astype(F32) + mw[u][:, :pw]
        yc = mw[u][:, pw:] + m_rkv[u]
        hf = h_ref[u]
        yh = _dot(jnp.concatenate([wy.astype(BF16), bw[u][:, :pw].astype(BF16)], axis=0),
                  hf.astype(BF16))
        h_ref[u] = pcs[u] * hf + yh[rows:] + bw[u][:, pw:] + kv[u]
        ys = yh[:rows] + yc
        y_ref[b, :, j * pw:(j + 1) * pw] = ys[:chunk] + ys[chunk:]


def _rwkv(pr, w0, wup, a0, aup, k_k, k_a, *, batch, seq, chunk, nb):
    n, ca = pr.shape
    nch = seq // chunk
    gidx = lax.broadcasted_iota(jnp.int32, (GROUP_W, GROUP_W), 0) // HEAD_DIM
    gsum = (gidx == gidx.T).astype(BF16)
    tri = (lax.broadcasted_iota(jnp.int32, (chunk, chunk), 1)
           <= lax.broadcasted_iota(jnp.int32, (chunk, chunk), 0)).astype(BF16)
    y = pl.pallas_call(
        functools.partial(_rwkv_kernel, chunk=chunk, nb=nb),
        grid=(batch // nb, nch),
        in_specs=[pl.BlockSpec((nb, chunk, ca), lambda g, c: (g, c, 0)),
                  _full((1, GROUP_W)), _full((LORA_PAD, GROUP_W)),
                  _full((1, GROUP_W)), _full((LORA_PAD, GROUP_W)),
                  _full((1, GROUP_W)), _full((1, GROUP_W)),
                  _full((GROUP_W, GROUP_W)), _full((chunk, chunk))],
        out_specs=pl.BlockSpec((nb, chunk, GROUP_W), lambda g, c: (g, c, 0)),
        out_shape=jax.ShapeDtypeStruct((batch, seq, GROUP_W), F32),
        scratch_shapes=[pltpu.VMEM((nb * (N_HEADS // 2), 2 * HEAD_DIM, 2 * HEAD_DIM), F32)],
        compiler_params=_params("arbitrary", "arbitrary"),
        name="rwkv_scan",
    )(pr.reshape(batch, seq, ca), w0, wup, a0, aup, k_k, k_a, gsum, tri)
    return y.reshape(n, GROUP_W)


def _fox_kernel(q_ref, k_ref, vt_ref, c_ref, o_ref, acc_ref, m_ref, l_ref, cb_ref,
                sa_ref, sb_ref, *, tq):
    j = pl.program_id(1)
    qi = pl.program_id(2)
    pw = 2 * HEAD_DIM
    seq = k_ref.shape[0]
    lane = lax.broadcasted_iota(jnp.int32, (1, pw), 1)

    @pl.when(qi == 0)
    def _():
        def fill(rb, carry):
            rs = pl.multiple_of(rb * tq, tq)
            cblk = c_ref[pl.ds(rs, tq), :]
            for hh in range(2):
                colv = jnp.sum(jnp.where(lane == 2 * j + hh, cblk, 0.0), axis=1, keepdims=True)
                cb_ref[hh, pl.ds(rs, tq), :] = jnp.broadcast_to(colv * LOG2E, (tq, pw))
            return carry
        lax.fori_loop(0, seq // tq, fill, 0)

    first = lane < HEAD_DIM
    q = q_ref[...]
    zero = jnp.zeros_like(q)
    qh = (jnp.where(first, q, zero), jnp.where(first, zero, q))
    acc_ref[...] = jnp.zeros_like(acc_ref)
    m_ref[...] = jnp.full_like(m_ref, NEG_BIG)
    l_ref[...] = jnp.zeros_like(l_ref)
    causal = (lax.broadcasted_iota(jnp.int32, (tq, tq), 0)
              <= lax.broadcasted_iota(jnp.int32, (tq, tq), 1))
    top = lax.broadcasted_iota(jnp.int32, (pw, 1), 0) < HEAD_DIM

    def scores(kb, s_ref):
        kblk = k_ref[pl.ds(pl.multiple_of(kb * tq, tq), tq), :]
        for hh in range(2):
            s_ref[hh] = _dot_nt(kblk, qh[hh])

    def softmax_pv(kb, s_ref, masked):
        ks = pl.multiple_of(kb * tq, tq)
        vt = vt_ref[kb].astype(F32)
        vts = (jnp.where(top, vt, 1.0).astype(BF16), jnp.where(top, 1.0, vt).astype(BF16))
        alphas, pvs = [], []
        for hh in range(2):
            cb = cb_ref[hh, pl.ds(ks, tq), :]
            z = s_ref[hh] - jnp.concatenate([cb] * (tq // pw), axis=1)
            if masked:
                z = jnp.where(causal, z, NEG_BIG)
            m_prev = m_ref[hh]
            m_new = jnp.maximum(m_prev, jnp.max(z, axis=0, keepdims=True))
            alpha = jnp.exp2(m_prev - m_new)
            p = jnp.exp2(z - m_new)
            pv = _dot(vts[hh], p.astype(BF16))
            ones_row = (1 - hh) * HEAD_DIM
            l_ref[hh] = alpha * l_ref[hh] + pv[ones_row:ones_row + 1, :]
            m_ref[hh] = m_new
            alphas.append(alpha)
            pvs.append(pv)
        acc_ref[...] = (acc_ref[...] * jnp.where(top, alphas[0], alphas[1])
                        + jnp.where(top, pvs[0], pvs[1]))

    scores(0, sa_ref)

    def body(i, carry):
        scores(2 * i + 1, sb_ref)
        softmax_pv(2 * i, sa_ref, False)
        scores(2 * i + 2, sa_ref)
        softmax_pv(2 * i + 1, sb_ref, False)
        return carry

    lax.fori_loop(0, qi // 2, body, 0)

    @pl.when(qi % 2 == 0)
    def _():
        softmax_pv(qi, sa_ref, True)

    @pl.when(qi % 2 == 1)
    def _():
        scores(qi, sb_ref)
        softmax_pv(qi - 1, sa_ref, False)
        softmax_pv(qi, sb_ref, True)

    out_t = acc_ref[...] / jnp.where(top, l_ref[0], l_ref[1])
    o_ref[...] = out_t.T.astype(BF16)


def _fox(qkv, c, *, batch, seq, tq):
    n = qkv.shape[0]
    nq = seq // tq
    npair = N_HEADS // 2
    pw = 2 * HEAD_DIM
    vt = qkv[:, 2 * GROUP_W:].reshape(batch, nq, tq, npair, pw)
    vt = jnp.transpose(vt, (0, 3, 1, 4, 2)).reshape(batch * npair, nq, pw, tq)
    return pl.pallas_call(
        functools.partial(_fox_kernel, tq=tq),
        grid=(batch, npair, nq),
        in_specs=[pl.BlockSpec((tq, pw), lambda b, j, i: (b * nq + i, j)),
                  pl.BlockSpec((seq, pw), lambda b, j, i: (b, npair + j)),
                  pl.BlockSpec((None, nq, pw, tq), lambda b, j, i: (b * npair + j, 0, 0, 0)),
                  pl.BlockSpec((seq, 128), lambda b, j, i: (b, 0))],
        out_specs=pl.BlockSpec((tq, pw), lambda b, j, i: (b * nq + i, j)),
        out_shape=jax.ShapeDtypeStruct((n, GROUP_W), BF16),
        scratch_shapes=[pltpu.VMEM((pw, tq), F32),
                        pltpu.VMEM((2, 1, tq), F32), pltpu.VMEM((2, 1, tq), F32),
                        pltpu.VMEM((2, seq, pw), F32),
                        pltpu.VMEM((2, tq, tq), F32), pltpu.VMEM((2, tq, tq), F32)],
        compiler_params=_params("arbitrary", "arbitrary", "arbitrary"),
        name="fox_attention",
    )(qkv, qkv, vt, c)


def _mixout_kernel(x_ref, pr_ref, yr_ref, yf_ref, a0_ref, aup_ref, gup_ref, ka_ref,
                   rk_ref, gng_ref, gnb_ref, gsum_ref, wr_ref, wf_ref, lng_ref, lnb_ref,
                   o_ref):
    gw = GROUP_W
    r = pr_ref[:, 0:gw]
    k = pr_ref[:, gw:2 * gw]
    v = pr_ref[:, 2 * gw:3 * gw]
    ad = pr_ref[:, 3 * gw + LORA_PAD:3 * gw + 2 * LORA_PAD]
    gd = pr_ref[:, 3 * gw + 2 * LORA_PAD:3 * gw + 3 * LORA_PAD]
    a = _sigmoid(a0_ref[...] + _dot(ad.astype(BF16), aup_ref[...]))
    k_mod = k * (1.0 + (a - 1.0) * ka_ref[...])
    gate = _dot(_sigmoid(gd).astype(BF16), gup_ref[...])
    gsum = gsum_ref[...]

    def group_sum(t):
        return _dot(t.astype(BF16), gsum)

    y = yr_ref[...]
    y_hi = y.astype(BF16)
    y_lo = (y - y_hi.astype(F32)).astype(BF16)
    mean = (_dot(y_hi, gsum) + _dot(y_lo, gsum)) * (1.0 / HEAD_DIM)
    d = y - mean
    var = group_sum(d * d) * (1.0 / HEAD_DIM)
    yn = d * lax.rsqrt(var + GN_EPS) * gng_ref[...] + gnb_ref[...]
    bonus = group_sum(r * k_mod * rk_ref[...])
    y_rwkv = ((yn + bonus * v) * gate).astype(BF16)
    mixed = _dot(y_rwkv, wr_ref[...]) + _dot(yf_ref[...], wf_ref[...])
    o_ref[...] = _layer_norm(ALPHA * x_ref[...] + mixed, lng_ref[...], lnb_ref[...])


def _mixout(x2, pr, yr, yf, a0, aup, gup, k_a, r_k, gn_g, gn_b, w_r, w_f, ln_g, ln_b, *, tm):
    n, d = x2.shape
    ca = pr.shape[1]
    gidx = lax.broadcasted_iota(jnp.int32, (GROUP_W, GROUP_W), 0) // HEAD_DIM
    gsum = (gidx == gidx.T).astype(BF16)
    vec = _full((1, GROUP_W))
    return pl.pallas_call(
        _mixout_kernel,
        grid=(n // tm,),
        in_specs=[pl.BlockSpec((tm, d), lambda i: (i, 0)),
                  pl.BlockSpec((tm, ca), lambda i: (i, 0)),
                  pl.BlockSpec((tm, GROUP_W), lambda i: (i, 0)),
                  pl.BlockSpec((tm, GROUP_W), lambda i: (i, 0)),
                  vec, _full((LORA_PAD, GROUP_W)), _full((LORA_PAD, GROUP_W)),
                  vec, vec, vec, vec, _full((GROUP_W, GROUP_W)),
                  _full((GROUP_W, d)), _full((GROUP_W, d)),
                  _full((1, d)), _full((1, d))],
        out_specs=pl.BlockSpec((tm, d), lambda i: (i, 0)),
        out_shape=jax.ShapeDtypeStruct((n, d), F32),
        compiler_params=_params("parallel"),
        name="mix_out",
    )(x2, pr, yr, yf, a0, aup, gup, k_a, r_k, gn_g, gn_b, gsum, w_r, w_f, ln_g, ln_b)


def _ffn_kernel(x_ref, wg_ref, wu_ref, wd_ref, lng_ref, lnb_ref, o_ref, acc_ref):
    f = pl.program_id(1)

    @pl.when(f == 0)
    def _():
        acc_ref[...] = jnp.zeros_like(acc_ref)

    xb = x_ref[...].astype(BF16)
    g = _dot(xb, wg_ref[...])
    u = _dot(xb, wu_ref[...])
    h = (g * _sigmoid(g) * u).astype(BF16)
    acc_ref[...] += _dot(h, wd_ref[...])

    @pl.when(f == pl.num_programs(1) - 1)
    def _():
        o_ref[...] = _layer_norm(ALPHA * x_ref[...] + acc_ref[...], lng_ref[...], lnb_ref[...])


def _ffn(x2, wg, wu, wd, ln_g, ln_b, *, tm, tf):
    n, d = x2.shape
    ff = wg.shape[1]
    return pl.pallas_call(
        _ffn_kernel,
        grid=(n // tm, ff // tf),
        in_specs=[pl.BlockSpec((tm, d), lambda i, f: (i, 0)),
                  pl.BlockSpec((d, tf), lambda i, f: (0, f)),
                  pl.BlockSpec((d, tf), lambda i, f: (0, f)),
                  pl.BlockSpec((tf, d), lambda i, f: (f, 0)),
                  _full((1, d)), _full((1, d))],
        out_specs=pl.BlockSpec((tm, d), lambda i, f: (i, 0)),
        out_shape=jax.ShapeDtypeStruct((n, d), F32),
        scratch_shapes=[pltpu.VMEM((tm, d), F32)],
        compiler_params=_params("parallel", "arbitrary"),
        name="ffn_swiglu",
    )(x2, wg, wu, wd, ln_g, ln_b)


def _glu_kernel(x_ref, w_ref, b_ref, o_ref):
    d = o_ref.shape[1]
    xb = x_ref[...].astype(BF16)
    val = _dot(xb, w_ref[:, 0:d]) + b_ref[:, 0:d]
    gat = _dot(xb, w_ref[:, d:2 * d]) + b_ref[:, d:2 * d]
    o_ref[...] = val * _sigmoid(gat)


def _glu(x2, w, b, *, tm):
    n, d = x2.shape
    return pl.pallas_call(
        _glu_kernel,
        grid=(n // tm,),
        in_specs=[pl.BlockSpec((tm, d), lambda i: (i, 0)), _full((d, 2 * d)), _full((1, 2 * d))],
        out_specs=pl.BlockSpec((tm, d), lambda i: (i, 0)),
        out_shape=jax.ShapeDtypeStruct((n, d), F32),
        compiler_params=_params("parallel"),
        name="conv_glu",
    )(x2, w, b)


def _top2_gates(logits):
    lane = lax.broadcasted_iota(jnp.int32, logits.shape, 1).astype(F32)
    lg = jnp.where(lane < N_EXPERTS, logits, NEG_BIG)
    m1 = jnp.max(lg, axis=-1, keepdims=True)
    i1 = jnp.min(jnp.where(lg == m1, lane, 128.0), axis=-1, keepdims=True)
    lg2 = jnp.where(lane == i1, NEG_BIG, lg)
    m2 = jnp.max(lg2, axis=-1, keepdims=True)
    i2 = jnp.min(jnp.where(lg2 == m2, lane, 128.0), axis=-1, keepdims=True)
    e2 = jnp.exp(m2 - m1)
    w1 = 1.0 / (1.0 + e2)
    w2 = e2 / (1.0 + e2)
    return jnp.where(lane == i1, w1, 0.0) + jnp.where(lane == i2, w2, 0.0)


def _conv_kernel(hc_ref, hp_ref, x_ref, wdw_ref, bdw_ref, lng_ref, lnb_ref, w2_ref, b2_ref,
                 pg_ref, pb_ref, wr_ref, x3_ref, x3b_ref, gates_ref, ext_ref, cv_ref,
                 *, tiles_per_seq):
    tm, d = x_ref.shape
    first = pl.program_id(0) % tiles_per_seq == 0
    ext_ref[0, 0:CONV_HALO, :] = jnp.where(first, 0.0, hp_ref[...])
    ext_ref[0, CONV_HALO:CONV_HALO + tm, :] = hc_ref[...]
    nrows = tm + CONV_HALO
    for c0 in range(0, d, 256):
        base = ext_ref[0, :, c0:c0 + 256]
        for j in range(1, SUBLANES):
            ext_ref[j, :, c0:c0 + 256] = pltpu.roll(base, nrows - j, 0)
    off = CONV_HALO - (CONV_WIDTH - 1)
    rc, cc = 64, 256
    for r0 in range(0, tm, rc):
        for c0 in range(0, d, cc):
            acc = jnp.broadcast_to(bdw_ref[:, c0:c0 + cc], (rc, cc))
            for t in range(CONV_WIDTH):
                base, j = divmod(off + t, SUBLANES)
                rs = r0 + base * SUBLANES
                acc = acc + wdw_ref[t:t + 1, c0:c0 + cc] * ext_ref[j, rs:rs + rc, c0:c0 + cc]
            cv_ref[r0:r0 + rc, c0:c0 + cc] = acc
    hn = _layer_norm(cv_ref[...], lng_ref[...], lnb_ref[...])
    hs = (hn * _sigmoid(hn)).astype(BF16)
    conv = _dot(hs, w2_ref[...]) + b2_ref[...]
    x3 = _layer_norm(ALPHA * x_ref[...] + conv, pg_ref[...], pb_ref[...])
    x3_ref[...] = x3
    x3b_ref[...] = x3.astype(BF16)
    x_hi = x3.astype(BF16)
    x_lo = (x3 - x_hi.astype(F32)).astype(BF16)
    hi_part = _dot(x_hi, wr_ref[...])
    logits = hi_part[:, :128] + hi_part[:, 128:] + _dot(x_lo, wr_ref[:, 0:128])
    gates_ref[...] = _top2_gates(logits)


def _conv(hg, x2, w_dw, b_dw, ln_g, ln_b, w2, b2, pg, pb, w_router, *, seq, tm):
    n, d = x2.shape
    ratio = tm // CONV_HALO
    vec = _full((1, d))
    return pl.pallas_call(
        functools.partial(_conv_kernel, tiles_per_seq=seq // tm),
        grid=(n // tm,),
        in_specs=[pl.BlockSpec((tm, d), lambda i: (i, 0)),
                  pl.BlockSpec((CONV_HALO, d), lambda i: (jnp.maximum(i * ratio - 1, 0), 0)),
                  pl.BlockSpec((tm, d), lambda i: (i, 0)),
                  _full((CONV_HALO, d)), vec, vec, vec, _full((d, d)), vec, vec, vec,
                  _full((d, 256))],
        out_specs=[pl.BlockSpec((tm, d), lambda i: (i, 0)),
                   pl.BlockSpec((tm, d), lambda i: (i, 0)),
                   pl.BlockSpec((tm, 128), lambda i: (i, 0))],
        out_shape=[jax.ShapeDtypeStruct((n, d), F32),
                   jax.ShapeDtypeStruct((n, d), BF16),
                   jax.ShapeDtypeStruct((n, 128), F32)],
        scratch_shapes=[pltpu.VMEM((SUBLANES, tm + CONV_HALO, d), F32), pltpu.VMEM((tm, d), F32)],
        compiler_params=_params("parallel"),
        name="conv_module",
    )(hg, hg, x2, w_dw, b_dw, ln_g, ln_b, w2, b2, pg, pb, w_router)


MOE_SUB = 256
MOE_CUM = 256


def _moe_kernel(xb_ref, gates_ref, tri_ref, wg_ref, wu_ref, wd_ref, x_ref, lng_ref, lnb_ref,
                o_ref, xg_ref, ya_ref, gg_ref, pos_ref, tot_ref, nsb_ref):
    e = pl.program_id(1)
    f = pl.program_id(2)
    last_e = pl.num_programs(1) - 1
    last_f = pl.num_programs(2) - 1
    tm = xb_ref.shape[0]
    lane = lax.broadcasted_iota(jnp.int32, (1, 128), 1)

    @pl.when(jnp.logical_and(e == 0, f == 0))
    def _():
        o_ref[...] = jnp.zeros_like(o_ref)
        carry = jnp.zeros((1, 128), F32)
        for r0 in range(0, tm, MOE_CUM):
            sel = jnp.where(gates_ref[r0:r0 + MOE_CUM, :] > 0.0, 1.0, 0.0)
            pos_ref[r0:r0 + MOE_CUM, :] = _dot(tri_ref[...], sel.astype(BF16)) + carry
            carry = carry + jnp.sum(sel, axis=0, keepdims=True)
        tot_ref[...] = carry

    def expert_columns():
        gate = jnp.sum(jnp.where(lane == e, gates_ref[...], 0.0), axis=1, keepdims=True)
        pos = jnp.sum(jnp.where(lane == e, pos_ref[...], 0.0), axis=1, keepdims=True)
        return gate, jnp.where(gate > 0.0, pos, -1.0)

    def one_hot(pos, sbi):
        slot = (lax.broadcasted_iota(jnp.int32, (1, MOE_SUB), 1) + sbi * MOE_SUB).astype(F32)
        return jnp.where(pos == slot, 1.0, 0.0).astype(BF16)

    @pl.when(f == 0)
    def _():
        count = jnp.sum(jnp.where(lane == e, tot_ref[...], 0.0)).astype(jnp.int32)
        nsb = (count + (MOE_SUB - 1)) // MOE_SUB
        nsb_ref[0] = nsb
        gate, pos = expert_columns()
        g_hi = gate.astype(BF16).astype(F32)
        g_mid = (gate - g_hi).astype(BF16).astype(F32)
        g_lo = gate - g_hi - g_mid
        g_parts = jnp.where(lane == 0, g_hi, jnp.where(lane == 1, g_mid,
                                                      jnp.where(lane == 2, g_lo, 0.0))).astype(BF16)

        def gather(sbi, carry):
            rs = pl.multiple_of(sbi * MOE_SUB, MOE_SUB)
            oh = one_hot(pos, sbi)
            xg_ref[pl.ds(rs, MOE_SUB), :] = _dot_tn(oh, xb_ref[...]).astype(BF16)
            gg = jnp.sum(_dot_tn(oh, g_parts), axis=1, keepdims=True)
            gg_ref[pl.ds(rs, MOE_SUB), :] = jnp.broadcast_to(gg, (MOE_SUB, 128))
            ya_ref[pl.ds(rs, MOE_SUB), :] = jnp.zeros((MOE_SUB, ya_ref.shape[1]), F32)
            return carry
        lax.fori_loop(0, nsb, gather, 0)

    nsb = nsb_ref[0]

    def ffn(sbi, carry):
        rs = pl.multiple_of(sbi * MOE_SUB, MOE_SUB)
        xg = xg_ref[pl.ds(rs, MOE_SUB), :]
        g = _dot(xg, wg_ref[...])
        u = _dot(xg, wu_ref[...])
        h = (g * _sigmoid(g) * u * gg_ref[pl.ds(rs, MOE_SUB), 0:1]).astype(BF16)
        ya_ref[pl.ds(rs, MOE_SUB), :] += _dot(h, wd_ref[...])
        return carry
    lax.fori_loop(0, nsb, ffn, 0)

    @pl.when(f == last_f)
    def _():
        _, pos = expert_columns()

        def scatter(sbi, carry):
            rs = pl.multiple_of(sbi * MOE_SUB, MOE_SUB)
            o_ref[...] += _dot(one_hot(pos, sbi), ya_ref[pl.ds(rs, MOE_SUB), :].astype(BF16))
            return carry
        lax.fori_loop(0, nsb, scatter, 0)

    @pl.when(jnp.logical_and(e == last_e, f == last_f))
    def _():
        o_ref[...] = _layer_norm(ALPHA * x_ref[...] + o_ref[...], lng_ref[...], lnb_ref[...])


def _moe(x3, x3b, gates, wg, wu, wd, ln_g, ln_b, *, tm, tf):
    n, d = x3.shape
    ne, _, ff = wg.shape
    tri = (lax.broadcasted_iota(jnp.int32, (MOE_CUM, MOE_CUM), 1)
           < lax.broadcasted_iota(jnp.int32, (MOE_CUM, MOE_CUM), 0)).astype(BF16)
    once = pl.Buffered(1)
    return pl.pallas_call(
        _moe_kernel,
        grid=(n // tm, ne, ff // tf),
        in_specs=[pl.BlockSpec((tm, d), lambda i, e, f: (i, 0), pipeline_mode=once),
                  pl.BlockSpec((tm, 128), lambda i, e, f: (i, 0)),
                  _full((MOE_CUM, MOE_CUM)),
                  pl.BlockSpec((None, d, tf), lambda i, e, f: (e, 0, f)),
                  pl.BlockSpec((None, d, tf), lambda i, e, f: (e, 0, f)),
                  pl.BlockSpec((None, tf, d), lambda i, e, f: (e, f, 0)),
                  pl.BlockSpec((tm, d), lambda i, e, f: (i, 0), pipeline_mode=once),
                  _full((1, d)), _full((1, d))],
        out_specs=pl.BlockSpec((tm, d), lambda i, e, f: (i, 0), pipeline_mode=once),
        out_shape=jax.ShapeDtypeStruct((n, d), F32),
        scratch_shapes=[pltpu.VMEM((tm, d), BF16), pltpu.VMEM((tm, d), F32),
                        pltpu.VMEM((tm, 128), F32), pltpu.VMEM((tm, 128), F32),
                        pltpu.VMEM((1, 128), F32), pltpu.SMEM((1,), jnp.int32)],
        compiler_params=_params("arbitrary", "arbitrary", "arbitrary"),
        name="moe_swiglu",
    )(x3b, gates, tri, wg, wu, wd, x3, ln_g, ln_b)


def _pad_cols(w, width):
    return jnp.pad(w, ((0, 0), (0, width - w.shape[1])))


def _pad_rows(w, height):
    return jnp.pad(w, ((0, height - w.shape[0]), (0, 0)))


def _forward(x, mix_w_in, rwkv_mu, rwkv_w0, rwkv_w_up, rwkv_a0, rwkv_a_up, rwkv_g_up,
             rwkv_k_k, rwkv_k_a, rwkv_r_k, rwkv_gn_g, rwkv_gn_b, fox_b_f, mix_w_out,
             mix_ln_g, mix_ln_b, ffn_w_gate, ffn_w_up, ffn_w_down, ffn_ln_g, ffn_ln_b,
             conv_w_pw1, conv_b_pw1, conv_w_dw, conv_b_dw, conv_ln_g, conv_ln_b,
             conv_w_pw2, conv_b_pw2, conv_post_ln_g, conv_post_ln_b,
             moe_w_router, moe_w_gate, moe_w_up, moe_w_down, moe_ln_g, moe_ln_b,
             *, tm=512, chunk=64, nb_rwkv=4, tq=512, tf_ffn=1408, tm_moe=2048, tf_moe=896):
    batch, seq, d = x.shape
    n = batch * seq
    gw = GROUP_W
    x2 = x.reshape(n, d)
    row = lambda t: t.reshape(1, -1)

    w_in = mix_w_in[0]
    mu = rwkv_mu[0]
    o_w, o_a, o_g = 3 * gw, 3 * gw + DECAY_LORA, 3 * gw + DECAY_LORA + AAA_LORA
    o_fox = o_g + GATE_LORA

    def lora_layout(t):
        return jnp.concatenate([t[..., :o_w],
                                _pad_cols(t[..., o_w:o_a], LORA_PAD),
                                _pad_cols(t[..., o_a:o_g], LORA_PAD),
                                _pad_cols(t[..., o_g:o_fox], LORA_PAD)], axis=-1)

    wa = lora_layout(w_in).astype(BF16)
    mu_a = lora_layout(row(mu))
    scale = LOG2E / math.sqrt(HEAD_DIM)
    wb = jnp.concatenate([w_in[:, o_fox:o_fox + gw] * scale,
                          w_in[:, o_fox + gw:o_fox + 3 * gw]], axis=1).astype(BF16)
    wf = _pad_cols(w_in[:, o_fox + 3 * gw:], 128).astype(BF16)
    bf = _pad_cols(row(fox_b_f[0]), 128)

    pr, qkv, c = _inproj(x2, wa, wb, wf, mu_a, bf, seq=seq, tm=tm)

    wup = _pad_rows(rwkv_w_up[0], LORA_PAD).astype(BF16)
    aup = _pad_rows(rwkv_a_up[0], LORA_PAD).astype(BF16)
    gup = _pad_rows(rwkv_g_up[0], LORA_PAD).astype(BF16)
    k_k, k_a, r_k = row(rwkv_k_k[0]), row(rwkv_k_a[0]), row(rwkv_r_k[0])
    yr = _rwkv(pr, row(rwkv_w0[0]), wup, row(rwkv_a0[0]), aup, k_k, k_a,
               batch=batch, seq=seq, chunk=chunk, nb=nb_rwkv)

    yf = _fox(qkv, c, batch=batch, seq=seq, tq=tq)

    w_out = mix_w_out[0].astype(BF16)
    x1 = _mixout(x2, pr, yr, yf, row(rwkv_a0[0]), aup, gup, k_a, r_k,
                 row(rwkv_gn_g[0]), row(rwkv_gn_b[0]), w_out[:gw], w_out[gw:],
                 row(mix_ln_g[0]), row(mix_ln_b[0]), tm=tm)
    x2b = _ffn(x1, ffn_w_gate[0].astype(BF16), ffn_w_up[0].astype(BF16),
               ffn_w_down[0].astype(BF16), row(ffn_ln_g[0]), row(ffn_ln_b[0]), tm=tm, tf=tf_ffn)

    hg = _glu(x2b, conv_w_pw1[0].astype(BF16), row(conv_b_pw1[0]), tm=tm)
    w_router = _pad_cols(moe_w_router[0], 128)
    wr_hi = w_router.astype(BF16)
    w_router = jnp.concatenate([wr_hi, (w_router - wr_hi.astype(F32)).astype(BF16)], axis=1)
    x3, x3b, gates = _conv(hg, x2b, _pad_rows(conv_w_dw[0], CONV_HALO), row(conv_b_dw[0]),
                           row(conv_ln_g[0]), row(conv_ln_b[0]), conv_w_pw2[0].astype(BF16),
                           row(conv_b_pw2[0]), row(conv_post_ln_g[0]), row(conv_post_ln_b[0]),
                           w_router, seq=seq, tm=tm)
    out = _moe(x3, x3b, gates, moe_w_gate[0].astype(BF16), moe_w_up[0].astype(BF16),
               moe_w_down[0].astype(BF16), row(moe_ln_g[0]), row(moe_ln_b[0]),
               tm=tm_moe, tf=tf_moe)
    return out.reshape(batch, seq, d)


def kernel(x, mix_w_in, rwkv_mu, rwkv_w0, rwkv_w_up, rwkv_a0, rwkv_a_up, rwkv_g_up, rwkv_k_k, rwkv_k_a, rwkv_r_k, rwkv_gn_g, rwkv_gn_b, fox_b_f, mix_w_out, mix_ln_g, mix_ln_b, ffn_w_gate, ffn_w_up, ffn_w_down, ffn_ln_g, ffn_ln_b, conv_w_pw1, conv_b_pw1, conv_w_dw, conv_b_dw, conv_ln_g, conv_ln_b, conv_w_pw2, conv_b_pw2, conv_post_ln_g, conv_post_ln_b, moe_w_router, moe_w_gate, moe_w_up, moe_w_down, moe_ln_g, moe_ln_b):
    return _forward(x, mix_w_in, rwkv_mu, rwkv_w0, rwkv_w_up, rwkv_a0, rwkv_a_up, rwkv_g_up,
                    rwkv_k_k, rwkv_k_a, rwkv_r_k, rwkv_gn_g, rwkv_gn_b, fox_b_f, mix_w_out,
                    mix_ln_g, mix_ln_b, ffn_w_gate, ffn_w_up, ffn_w_down, ffn_ln_g, ffn_ln_b,
                    conv_w_pw1, conv_b_pw1, conv_w_dw, conv_b_dw, conv_ln_g, conv_ln_b,
                    conv_w_pw2, conv_b_pw2, conv_post_ln_g, conv_post_ln_b,
                    moe_w_router, moe_w_gate, moe_w_up, moe_w_down, moe_ln_g, moe_ln_b)
```

```python
import functools
import math

import jax
import jax.numpy as jnp
from jax import lax
from jax.experimental import pallas as pl
from jax.experimental.pallas import tpu as pltpu

F32 = jnp.float32
BF16 = jnp.bfloat16
HIGHEST = lax.Precision.HIGHEST

HEAD_DIM = 64
N_HEADS = 8
GROUP_W = N_HEADS * HEAD_DIM
LORA_PAD = 128
DECAY_LORA = 32
AAA_LORA = 32
GATE_LORA = 96
CONV_WIDTH = 31
CONV_HALO = 32
SUBLANES = 8
N_EXPERTS = 8
LN_EPS = 1e-5
GN_EPS = 64e-5
DEPTH = 2
ALPHA = (2.0 * DEPTH) ** 0.25
NEG_BIG = -1e30
LOG2E = math.log2(math.e)
VMEM_LIMIT = 56 * 1024 * 1024


def _dot(a, b, **kw):
    return jnp.dot(a, b, preferred_element_type=F32, **kw)


def _dot_nt(a, b):
    return lax.dot_general(a, b, (((1,), (1,)), ((), ())), preferred_element_type=F32)


def _dot_tn(a, b):
    return lax.dot_general(a, b, (((0,), (0,)), ((), ())), preferred_element_type=F32)


def _dot_exact_lhs(a, v):
    hi = v.astype(BF16)
    rem = v - hi.astype(F32)
    mid = rem.astype(BF16)
    lo = (rem - mid.astype(F32)).astype(BF16)
    w = v.shape[1]
    out = _dot(a, jnp.concatenate([hi, mid, lo], axis=1))
    return out[:, :w] + out[:, w:2 * w] + out[:, 2 * w:]


def _sigmoid(z):
    return 1.0 / (1.0 + jnp.exp(-z))


def _softplus(z):
    return jnp.maximum(z, 0.0) + jnp.log1p(jnp.exp(-jnp.abs(z)))


def _layer_norm(h, g, b):
    mu = jnp.mean(h, axis=-1, keepdims=True)
    d = h - mu
    var = jnp.mean(d * d, axis=-1, keepdims=True)
    return d * lax.rsqrt(var + LN_EPS) * g + b


def _params(*sem):
    return pltpu.CompilerParams(dimension_semantics=sem, vmem_limit_bytes=VMEM_LIMIT)


def _full(shape):
    return pl.BlockSpec(shape, lambda *_: (0,) * len(shape))


def _inproj_kernel(x_ref, wa_ref, wb_ref, wf_ref, mu_ref, bf_ref, tri_ref,
                   pr_ref, qk_ref, vt_ref, c_ref, last_ref, carry_ref, *, tiles_per_seq):
    i = pl.program_id(0)

    @pl.when(i % tiles_per_seq == 0)
    def _():
        last_ref[...] = jnp.zeros_like(last_ref)
        carry_ref[...] = jnp.zeros_like(carry_ref)

    xb = x_ref[...].astype(BF16)
    tm = xb.shape[0]
    row0 = lax.broadcasted_iota(jnp.int32, (tm, 1), 0) == 0
    ca = wa_ref.shape[1]
    for c0 in range(0, ca, GROUP_W):
        cw = min(GROUP_W, ca - c0)
        p = _dot(xb, wa_ref[:, c0:c0 + cw])
        prev = jnp.where(row0, last_ref[:, c0:c0 + cw], pltpu.roll(p, 1, 0))
        last_ref[:, c0:c0 + cw] = p[tm - 1:tm, :]
        pr_ref[:, c0:c0 + cw] = p + mu_ref[:, c0:c0 + cw] * (prev - p)
    for c0 in range(0, 2 * GROUP_W, GROUP_W):
        qk_ref[:, c0:c0 + GROUP_W] = _dot(xb, wb_ref[:, c0:c0 + GROUP_W]).astype(BF16)
    vt_ref[...] = _dot(xb, wb_ref[:, 2 * GROUP_W:3 * GROUP_W]).T.astype(BF16)
    fl = _dot(xb, wf_ref[...]) + bf_ref[...]
    log_f = jnp.minimum(fl, 0.0) - jnp.log1p(jnp.exp(-jnp.abs(fl)))
    c = _dot_exact_lhs(tri_ref[...], log_f) + carry_ref[...]
    c_ref[...] = c
    carry_ref[...] = c[tm - 1:tm, :]


def _inproj(x2, wa, wb, wf, mu, bf, *, seq, tm):
    n, d = x2.shape
    ca, cb = wa.shape[1], wb.shape[1]
    tps = seq // tm
    tri = (lax.broadcasted_iota(jnp.int32, (tm, tm), 1)
           <= lax.broadcasted_iota(jnp.int32, (tm, tm), 0)).astype(BF16)
    return pl.pallas_call(
        functools.partial(_inproj_kernel, tiles_per_seq=tps),
        grid=(n // tm,),
        in_specs=[pl.BlockSpec((tm, d), lambda i: (i, 0)),
                  _full((d, ca)), _full((d, cb)), _full((d, 128)),
                  _full((1, ca)), _full((1, 128)), _full((tm, tm))],
        out_specs=[pl.BlockSpec((tm, ca), lambda i: (i, 0)),
                   pl.BlockSpec((tm, 2 * GROUP_W), lambda i: (i, 0)),
                   pl.BlockSpec((None, None, GROUP_W, tm), lambda i: (i // tps, i % tps, 0, 0)),
                   pl.BlockSpec((tm, 128), lambda i: (i, 0))],
        out_shape=[jax.ShapeDtypeStruct((n, ca), F32),
                   jax.ShapeDtypeStruct((n, 2 * GROUP_W), BF16),
                   jax.ShapeDtypeStruct((n // seq, tps, GROUP_W, tm), BF16),
                   jax.ShapeDtypeStruct((n, 128), F32)],
        scratch_shapes=[pltpu.VMEM((1, ca), F32), pltpu.VMEM((1, 128), F32)],
        compiler_params=_params("arbitrary"),
        name="inproj",
    )(x2, wa, wb, wf, mu, bf, tri)


def _rwkv_kernel(pr_ref, w0_ref, wup_ref, a0_ref, aup_ref, kk_ref, ka_ref,
                 gsum_ref, tri_ref, y_ref, h_ref, *, chunk, nb):
    @pl.when(pl.program_id(1) == 0)
    def _():
        h_ref[...] = jnp.zeros_like(h_ref)

    gw = GROUP_W
    pw = 2 * HEAD_DIM
    npair = N_HEADS // 2
    rows = 2 * chunk
    log_chunk = int(math.log2(chunk))
    head0 = lax.broadcasted_iota(jnp.int32, (1, pw), 1) < HEAD_DIM
    row = lax.broadcasted_iota(jnp.int32, (rows, rows), 0)
    col = lax.broadcasted_iota(jnp.int32, (rows, rows), 1)
    strict = (col & (chunk - 1)) < (row & (chunk - 1))
    incl = (col & (chunk - 1)) <= (row & (chunk - 1))
    eye = (col == row).astype(F32)
    peye = (lax.broadcasted_iota(jnp.int32, (pw, pw), 0)
            == lax.broadcasted_iota(jnp.int32, (pw, pw), 1))

    def stack(x):
        return jnp.concatenate([jnp.where(head0, x, 0.0), jnp.where(head0, 0.0, x)],
                               axis=0).astype(BF16)

    units = [(b, j) for b in range(nb) for j in range(npair)]
    nu = len(units)
    ar, bk, vs, bhs, khs, pcs = [], [], [], [], [], []
    for b in range(nb):
        r = pr_ref[b, :, 0:gw]
        k = pr_ref[b, :, gw:2 * gw]
        v = pr_ref[b, :, 2 * gw:3 * gw]
        wd = pr_ref[b, :, 3 * gw:3 * gw + LORA_PAD]
        ad = pr_ref[b, :, 3 * gw + LORA_PAD:3 * gw + 2 * LORA_PAD]
        w_pre = w0_ref[...] + _dot(jnp.tanh(wd).astype(BF16), wup_ref[...])
        w = -_softplus(-w_pre) - 0.5
        log_decay = -jnp.exp(w)
        a = _sigmoid(a0_ref[...] + _dot(ad.astype(BF16), aup_ref[...]))
        kk = k * kk_ref[...]
        norm = jnp.sqrt(_dot((kk * kk).astype(BF16), gsum_ref[...]))
        kk = kk / jnp.maximum(norm, 1e-12)
        k_mod = k * (1.0 + (a - 1.0) * ka_ref[...])
        b_vec = kk * a
        cum = _dot_exact_lhs(tri_ref[...], log_decay)
        last = cum[chunk - 1:chunk, :]
        p_inv = jnp.exp(-cum)
        p_tail = jnp.exp(last - cum)
        a_t = -kk * jnp.exp(cum - log_decay)
        r_t = r * jnp.exp(cum)
        b_t = b_vec * p_inv
        k_t = k_mod * p_inv
        b_h = b_vec * p_tail
        k_h = k_mod * p_tail
        p_last = jnp.exp(last)
        for j in range(npair):
            sl = slice(j * pw, (j + 1) * pw)
            ar.append(jnp.concatenate([stack(a_t[:, sl]), stack(r_t[:, sl])], axis=0))
            bk.append(jnp.concatenate([stack(b_t[:, sl]), stack(k_t[:, sl])], axis=0))
            vs.append(stack(v[:, sl]))
            bhs.append(stack(b_h[:, sl]))
            khs.append(stack(k_h[:, sl]))
            pcs.append(jnp.sum(jnp.where(peye, p_last[:, sl], 0.0), axis=1, keepdims=True))

    gram = [_dot_nt(ar[u], bk[u]) for u in range(nu)]
    l_ab = [jnp.where(strict, gram[u][:rows, :rows], 0.0) for u in range(nu)]
    l_akv = [_dot(jnp.where(strict, gram[u][:rows, rows:], 0.0).astype(BF16), vs[u])
             for u in range(nu)]
    m_rb = [jnp.where(incl, gram[u][rows:, :rows], 0.0).astype(BF16) for u in range(nu)]
    m_rkv = [_dot(jnp.where(incl, gram[u][rows:, rows:], 0.0).astype(BF16), vs[u])
             for u in range(nu)]
    t_inv = [eye + l_ab[u] for u in range(nu)]
    xb = [l_ab[u].astype(BF16) for u in range(nu)]
    xp = [_dot(xb[u], xb[u]) for u in range(nu)]
    for step in range(log_chunk - 1):
        xb = [xp[u].astype(BF16) for u in range(nu)]
        if step < log_chunk - 2:
            both = [_dot(jnp.concatenate([t_inv[u].astype(BF16), xb[u]], axis=0), xb[u])
                    for u in range(nu)]
            t_inv = [t_inv[u] + both[u][:rows] for u in range(nu)]
            xp = [both[u][rows:] for u in range(nu)]
        else:
            t_inv = [t_inv[u] + _dot(t_inv[u].astype(BF16), xb[u]) for u in range(nu)]
    tw = [_dot(t_inv[u].astype(BF16),
               jnp.concatenate([ar[u][:rows], l_akv[u].astype(BF16)], axis=1)).astype(BF16)
          for u in range(nu)]
    mw = [_dot(m_rb[u], tw[u]) for u in range(nu)]
    bw = [_dot_tn(bhs[u], tw[u]) for u in range(nu)]
    kv = [_dot_tn(khs[u], vs[u]) for u in range(nu)]
    for u, (b, j) in enumerate(units):
        wy = ar[u][rows:].astype(F32) + mw[u][:, :pw]
        yc = mw[u][:, pw:] + m_rkv[u]
        hf = h_ref[u]
        yh = _dot(jnp.concatenate([wy.astype(BF16), bw[u][:, :pw].astype(BF16)], axis=0),
                  hf.astype(BF16))
        h_ref[u] = pcs[u] * hf + yh[rows:] + bw[u][:, pw:] + kv[u]
        ys = yh[:rows] + yc
        y_ref[b, :, j * pw:(j + 1) * pw] = ys[:chunk] + ys[chunk:]


def _rwkv(pr, w0, wup, a0, aup, k_k, k_a, *, batch, seq, chunk, nb):
    n, ca = pr.shape
    nch = seq // chunk
    gidx = lax.broadcasted_iota(jnp.int32, (GROUP_W, GROUP_W), 0) // HEAD_DIM
    gsum = (gidx == gidx.T).astype(BF16)
    tri = (lax.broadcasted_iota(jnp.int32, (chunk, chunk), 1)
           <= lax.broadcasted_iota(jnp.int32, (chunk, chunk), 0)).astype(BF16)
    y = pl.pallas_call(
        functools.partial(_rwkv_kernel, chunk=chunk, nb=nb),
        grid=(batch // nb, nch),
        in_specs=[pl.BlockSpec((nb, chunk, ca), lambda g, c: (g, c, 0)),
                  _full((1, GROUP_W)), _full((LORA_PAD, GROUP_W)),
                  _full((1, GROUP_W)), _full((LORA_PAD, GROUP_W)),
                  _full((1, GROUP_W)), _full((1, GROUP_W)),
                  _full((GROUP_W, GROUP_W)), _full((chunk, chunk))],
        out_specs=pl.BlockSpec((nb, chunk, GROUP_W), lambda g, c: (g, c, 0)),
        out_shape=jax.ShapeDtypeStruct((batch, seq, GROUP_W), F32),
        scratch_shapes=[pltpu.VMEM((nb * (N_HEADS // 2), 2 * HEAD_DIM, 2 * HEAD_DIM), F32)],
        compiler_params=_params("arbitrary", "arbitrary"),
        name="rwkv_scan",
    )(pr.reshape(batch, seq, ca), w0, wup, a0, aup, k_k, k_a, gsum, tri)
    return y.reshape(n, GROUP_W)


def _fox_kernel(q_ref, k_ref, vt_ref, c_ref, o_ref, acc_ref, m_ref, l_ref, kb_ref,
                sa_ref, sb_ref, *, tq):
    j = pl.program_id(1)
    qi = pl.program_id(2)
    pw = 2 * HEAD_DIM
    seq = k_ref.shape[0]
    lane = lax.broadcasted_iota(jnp.int32, (1, pw), 1)

    first = lane < HEAD_DIM
    bias_lane = (HEAD_DIM, 0)

    @pl.when(qi == 0)
    def _():
        def fill(rb, carry):
            rs = pl.multiple_of(rb * tq, tq)
            cblk = c_ref[pl.ds(rs, tq), :]
            for hh in range(2):
                bias = -LOG2E * jnp.sum(jnp.where(lane == 2 * j + hh, cblk, 0.0),
                                        axis=1, keepdims=True)
                b_hi = bias.astype(BF16).astype(F32)
                b_mid = (bias - b_hi).astype(BF16).astype(F32)
                b_lo = bias - b_hi - b_mid
                l0 = bias_lane[hh]
                kb_ref[hh, pl.ds(rs, tq), :] = jnp.where(
                    lane == l0, b_hi, jnp.where(lane == l0 + 1, b_mid,
                                                jnp.where(lane == l0 + 2, b_lo, 0.0))).astype(BF16)
            return carry
        lax.fori_loop(0, seq // tq, fill, 0)

    q = q_ref[...]
    ones3 = [jnp.where(jnp.logical_and(lane >= l0, lane < l0 + 3), 1.0, 0.0).astype(BF16)
             for l0 in bias_lane]
    own = (first, jnp.logical_not(first))
    qh = tuple(jnp.where(own[hh], q, ones3[hh]) for hh in range(2))
    acc_ref[...] = jnp.zeros_like(acc_ref)
    m_ref[...] = jnp.full_like(m_ref, NEG_BIG)
    l_ref[...] = jnp.zeros_like(l_ref)
    causal = (lax.broadcasted_iota(jnp.int32, (tq, tq), 0)
              <= lax.broadcasted_iota(jnp.int32, (tq, tq), 1))
    top = lax.broadcasted_iota(jnp.int32, (pw, 1), 0) < HEAD_DIM

    def scores(kb, s_ref):
        ks = pl.multiple_of(kb * tq, tq)
        kblk = k_ref[pl.ds(ks, tq), :]
        for hh in range(2):
            k_aug = jnp.where(own[hh], kblk, kb_ref[hh, pl.ds(ks, tq), :])
            s_ref[hh] = _dot_nt(k_aug, qh[hh])

    def softmax_pv(kb, s_ref, masked):
        vt = vt_ref[kb].astype(F32)
        vts = (jnp.where(top, vt, 1.0).astype(BF16), jnp.where(top, 1.0, vt).astype(BF16))
        alphas, pvs = [], []
        for hh in range(2):
            z = s_ref[hh]
            if masked:
                z = jnp.where(causal, z, NEG_BIG)
            m_prev = m_ref[hh]
            m_new = jnp.maximum(m_prev, jnp.max(z, axis=0, keepdims=True))
            alpha = jnp.exp2(m_prev - m_new)
            p = jnp.exp2(z - m_new)
            pv = _dot(vts[hh], p.astype(BF16))
            ones_row = (1 - hh) * HEAD_DIM
            l_ref[hh] = alpha * l_ref[hh] + pv[ones_row:ones_row + 1, :]
            m_ref[hh] = m_new
            alphas.append(alpha)
            pvs.append(pv)
        acc_ref[...] = (acc_ref[...] * jnp.where(top, alphas[0], alphas[1])
                        + jnp.where(top, pvs[0], pvs[1]))

    scores(0, sa_ref)

    def body(i, carry):
        scores(2 * i + 1, sb_ref)
        softmax_pv(2 * i, sa_ref, False)
        scores(2 * i + 2, sa_ref)
        softmax_pv(2 * i + 1, sb_ref, False)
        return carry

    lax.fori_loop(0, qi // 2, body, 0)

    @pl.when(qi % 2 == 0)
    def _():
        softmax_pv(qi, sa_ref, True)

    @pl.when(qi % 2 == 1)
    def _():
        scores(qi, sb_ref)
        softmax_pv(qi - 1, sa_ref, False)
        softmax_pv(qi, sb_ref, True)

    out_t = acc_ref[...] / jnp.where(top, l_ref[0], l_ref[1])
    o_ref[...] = out_t.T.astype(BF16)


def _fox(qk, vt, c, *, batch, seq, tq):
    n = qk.shape[0]
    nq = seq // tq
    npair = N_HEADS // 2
    pw = 2 * HEAD_DIM
    assert vt.shape == (batch, nq, GROUP_W, tq)
    return pl.pallas_call(
        functools.partial(_fox_kernel, tq=tq),
        grid=(batch, npair, nq),
        in_specs=[pl.BlockSpec((tq, pw), lambda b, j, i: (b * nq + i, j)),
                  pl.BlockSpec((seq, pw), lambda b, j, i: (b, npair + j)),
                  pl.BlockSpec((None, nq, pw, tq), lambda b, j, i: (b, 0, j, 0)),
                  pl.BlockSpec((seq, 128), lambda b, j, i: (b, 0))],
        out_specs=pl.BlockSpec((tq, pw), lambda b, j, i: (b * nq + i, j)),
        out_shape=jax.ShapeDtypeStruct((n, GROUP_W), BF16),
        scratch_shapes=[pltpu.VMEM((pw, tq), F32),
                        pltpu.VMEM((2, 1, tq), F32), pltpu.VMEM((2, 1, tq), F32),
                        pltpu.VMEM((2, seq, pw), BF16),
                        pltpu.VMEM((2, tq, tq), F32), pltpu.VMEM((2, tq, tq), F32)],
        compiler_params=_params("arbitrary", "arbitrary", "arbitrary"),
        name="fox_attention",
    )(qk, qk, vt, c)


def _mixout_kernel(x_ref, pr_ref, yr_ref, yf_ref, a0_ref, aup_ref, gup_ref, ka_ref,
                   rk_ref, gng_ref, gnb_ref, gsum_ref, wr_ref, wf_ref, lng_ref, lnb_ref,
                   o_ref):
    gw = GROUP_W
    r = pr_ref[:, 0:gw]
    k = pr_ref[:, gw:2 * gw]
    v = pr_ref[:, 2 * gw:3 * gw]
    ad = pr_ref[:, 3 * gw + LORA_PAD:3 * gw + 2 * LORA_PAD]
    gd = pr_ref[:, 3 * gw + 2 * LORA_PAD:3 * gw + 3 * LORA_PAD]
    a = _sigmoid(a0_ref[...] + _dot(ad.astype(BF16), aup_ref[...]))
    k_mod = k * (1.0 + (a - 1.0) * ka_ref[...])
    gate = _dot(_sigmoid(gd).astype(BF16), gup_ref[...])
    gsum = gsum_ref[...]

    def group_sum(t):
        return _dot(t.astype(BF16), gsum)

    y = yr_ref[...]
    y_hi = y.astype(BF16)
    y_lo = (y - y_hi.astype(F32)).astype(BF16)
    mean = (_dot(y_hi, gsum) + _dot(y_lo, gsum)) * (1.0 / HEAD_DIM)
    d = y - mean
    var = group_sum(d * d) * (1.0 / HEAD_DIM)
    yn = d * lax.rsqrt(var + GN_EPS) * gng_ref[...] + gnb_ref[...]
    bonus = group_sum(r * k_mod * rk_ref[...])
    y_rwkv = ((yn + bonus * v) * gate).astype(BF16)
    mixed = _dot(y_rwkv, wr_ref[...]) + _dot(yf_ref[...], wf_ref[...])
    o_ref[...] = _layer_norm(ALPHA * x_ref[...] + mixed, lng_ref[...], lnb_ref[...])


def _mixout(x2, pr, yr, yf, a0, aup, gup, k_a, r_k, gn_g, gn_b, w_r, w_f, ln_g, ln_b, *, tm):
    n, d = x2.shape
    ca = pr.shape[1]
    gidx = lax.broadcasted_iota(jnp.int32, (GROUP_W, GROUP_W), 0) // HEAD_DIM
    gsum = (gidx == gidx.T).astype(BF16)
    vec = _full((1, GROUP_W))
    return pl.pallas_call(
        _mixout_kernel,
        grid=(n // tm,),
        in_specs=[pl.BlockSpec((tm, d), lambda i: (i, 0)),
                  pl.BlockSpec((tm, ca), lambda i: (i, 0)),
                  pl.BlockSpec((tm, GROUP_W), lambda i: (i, 0)),
                  pl.BlockSpec((tm, GROUP_W), lambda i: (i, 0)),
                  vec, _full((LORA_PAD, GROUP_W)), _full((LORA_PAD, GROUP_W)),
                  vec, vec, vec, vec, _full((GROUP_W, GROUP_W)),
                  _full((GROUP_W, d)), _full((GROUP_W, d)),
                  _full((1, d)), _full((1, d))],
        out_specs=pl.BlockSpec((tm, d), lambda i: (i, 0)),
        out_shape=jax.ShapeDtypeStruct((n, d), F32),
        compiler_params=_params("parallel"),
        name="mix_out",
    )(x2, pr, yr, yf, a0, aup, gup, k_a, r_k, gn_g, gn_b, gsum, w_r, w_f, ln_g, ln_b)


def _ffn_kernel(x_ref, wg_ref, wu_ref, wd_ref, lng_ref, lnb_ref, o_ref, acc_ref):
    f = pl.program_id(1)

    @pl.when(f == 0)
    def _():
        acc_ref[...] = jnp.zeros_like(acc_ref)

    xb = x_ref[...].astype(BF16)
    g = _dot(xb, wg_ref[...])
    u = _dot(xb, wu_ref[...])
    h = (g * _sigmoid(g) * u).astype(BF16)
    acc_ref[...] += _dot(h, wd_ref[...])

    @pl.when(f == pl.num_programs(1) - 1)
    def _():
        o_ref[...] = _layer_norm(ALPHA * x_ref[...] + acc_ref[...], lng_ref[...], lnb_ref[...])


def _ffn(x2, wg, wu, wd, ln_g, ln_b, *, tm, tf):
    n, d = x2.shape
    ff = wg.shape[1]
    return pl.pallas_call(
        _ffn_kernel,
        grid=(n // tm, ff // tf),
        in_specs=[pl.BlockSpec((tm, d), lambda i, f: (i, 0)),
                  pl.BlockSpec((d, tf), lambda i, f: (0, f)),
                  pl.BlockSpec((d, tf), lambda i, f: (0, f)),
                  pl.BlockSpec((tf, d), lambda i, f: (f, 0)),
                  _full((1, d)), _full((1, d))],
        out_specs=pl.BlockSpec((tm, d), lambda i, f: (i, 0)),
        out_shape=jax.ShapeDtypeStruct((n, d), F32),
        scratch_shapes=[pltpu.VMEM((tm, d), F32)],
        compiler_params=_params("parallel", "arbitrary"),
        name="ffn_swiglu",
    )(x2, wg, wu, wd, ln_g, ln_b)


def _glu_kernel(x_ref, w_ref, b_ref, o_ref):
    d = o_ref.shape[1]
    xb = x_ref[...].astype(BF16)
    val = _dot(xb, w_ref[:, 0:d]) + b_ref[:, 0:d]
    gat = _dot(xb, w_ref[:, d:2 * d]) + b_ref[:, d:2 * d]
    o_ref[...] = val * _sigmoid(gat)


def _glu(x2, w, b, *, tm):
    n, d = x2.shape
    return pl.pallas_call(
        _glu_kernel,
        grid=(n // tm,),
        in_specs=[pl.BlockSpec((tm, d), lambda i: (i, 0)), _full((d, 2 * d)), _full((1, 2 * d))],
        out_specs=pl.BlockSpec((tm, d), lambda i: (i, 0)),
        out_shape=jax.ShapeDtypeStruct((n, d), F32),
        compiler_params=_params("parallel"),
        name="conv_glu",
    )(x2, w, b)


def _top2_gates(logits):
    lane = lax.broadcasted_iota(jnp.int32, logits.shape, 1).astype(F32)
    lg = jnp.where(lane < N_EXPERTS, logits, NEG_BIG)
    m1 = jnp.max(lg, axis=-1, keepdims=True)
    i1 = jnp.min(jnp.where(lg == m1, lane, 128.0), axis=-1, keepdims=True)
    lg2 = jnp.where(lane == i1, NEG_BIG, lg)
    m2 = jnp.max(lg2, axis=-1, keepdims=True)
    i2 = jnp.min(jnp.where(lg2 == m2, lane, 128.0), axis=-1, keepdims=True)
    e2 = jnp.exp(m2 - m1)
    w1 = 1.0 / (1.0 + e2)
    w2 = e2 / (1.0 + e2)
    return jnp.where(lane == i1, w1, 0.0) + jnp.where(lane == i2, w2, 0.0)


def _conv_kernel(hc_ref, hp_ref, x_ref, wdw_ref, bdw_ref, lng_ref, lnb_ref, w2_ref, b2_ref,
                 pg_ref, pb_ref, wr_ref, x3_ref, x3b_ref, gates_ref, ext_ref, cv_ref,
                 *, tiles_per_seq):
    tm, d = x_ref.shape
    first = pl.program_id(0) % tiles_per_seq == 0
    ext_ref[0, 0:CONV_HALO, :] = jnp.where(first, 0.0, hp_ref[...])
    ext_ref[0, CONV_HALO:CONV_HALO + tm, :] = hc_ref[...]
    nrows = tm + CONV_HALO
    for c0 in range(0, d, 256):
        base = ext_ref[0, :, c0:c0 + 256]
        for j in range(1, SUBLANES):
            ext_ref[j, :, c0:c0 + 256] = pltpu.roll(base, nrows - j, 0)
    off = CONV_HALO - (CONV_WIDTH - 1)
    rc, cc = 64, 256
    for r0 in range(0, tm, rc):
        for c0 in range(0, d, cc):
            acc = jnp.broadcast_to(bdw_ref[:, c0:c0 + cc], (rc, cc))
            for t in range(CONV_WIDTH):
                base, j = divmod(off + t, SUBLANES)
                rs = r0 + base * SUBLANES
                acc = acc + wdw_ref[t:t + 1, c0:c0 + cc] * ext_ref[j, rs:rs + rc, c0:c0 + cc]
            cv_ref[r0:r0 + rc, c0:c0 + cc] = acc
    hn = _layer_norm(cv_ref[...], lng_ref[...], lnb_ref[...])
    hs = (hn * _sigmoid(hn)).astype(BF16)
    conv = _dot(hs, w2_ref[...]) + b2_ref[...]
    x3 = _layer_norm(ALPHA * x_ref[...] + conv, pg_ref[...], pb_ref[...])
    x3_ref[...] = x3
    x3b_ref[...] = x3.astype(BF16)
    x_hi = x3.astype(BF16)
    x_lo = (x3 - x_hi.astype(F32)).astype(BF16)
    hi_part = _dot(x_hi, wr_ref[...])
    logits = hi_part[:, :128] + hi_part[:, 128:] + _dot(x_lo, wr_ref[:, 0:128])
    gates_ref[...] = _top2_gates(logits)


def _conv(hg, x2, w_dw, b_dw, ln_g, ln_b, w2, b2, pg, pb, w_router, *, seq, tm):
    n, d = x2.shape
    ratio = tm // CONV_HALO
    vec = _full((1, d))
    return pl.pallas_call(
        functools.partial(_conv_kernel, tiles_per_seq=seq // tm),
        grid=(n // tm,),
        in_specs=[pl.BlockSpec((tm, d), lambda i: (i, 0)),
                  pl.BlockSpec((CONV_HALO, d), lambda i: (jnp.maximum(i * ratio - 1, 0), 0)),
                  pl.BlockSpec((tm, d), lambda i: (i, 0)),
                  _full((CONV_HALO, d)), vec, vec, vec, _full((d, d)), vec, vec, vec,
                  _full((d, 256))],
        out_specs=[pl.BlockSpec((tm, d), lambda i: (i, 0)),
                   pl.BlockSpec((tm, d), lambda i: (i, 0)),
                   pl.BlockSpec((tm, 128), lambda i: (i, 0))],
        out_shape=[jax.ShapeDtypeStruct((n, d), F32),
                   jax.ShapeDtypeStruct((n, d), BF16),
                   jax.ShapeDtypeStruct((n, 128), F32)],
        scratch_shapes=[pltpu.VMEM((SUBLANES, tm + CONV_HALO, d), F32), pltpu.VMEM((tm, d), F32)],
        compiler_params=_params("parallel"),
        name="conv_module",
    )(hg, hg, x2, w_dw, b_dw, ln_g, ln_b, w2, b2, pg, pb, w_router)


MOE_SUB = 256
MOE_CUM = 256


def _moe_kernel(xb_ref, gates_ref, tri_ref, wg_ref, wu_ref, wd_ref, x_ref, lng_ref, lnb_ref,
                o_ref, xg_ref, ya_ref, gg_ref, pos_ref, tot_ref, nsb_ref):
    e = pl.program_id(1)
    f = pl.program_id(2)
    last_e = pl.num_programs(1) - 1
    last_f = pl.num_programs(2) - 1
    tm = xb_ref.shape[0]
    lane = lax.broadcasted_iota(jnp.int32, (1, 128), 1)

    @pl.when(jnp.logical_and(e == 0, f == 0))
    def _():
        o_ref[...] = jnp.zeros_like(o_ref)
        carry = jnp.zeros((1, 128), F32)
        for r0 in range(0, tm, MOE_CUM):
            sel = jnp.where(gates_ref[r0:r0 + MOE_CUM, :] > 0.0, 1.0, 0.0)
            pos_ref[r0:r0 + MOE_CUM, :] = _dot(tri_ref[...], sel.astype(BF16)) + carry
            carry = carry + jnp.sum(sel, axis=0, keepdims=True)
        tot_ref[...] = carry

    def expert_columns():
        gate = jnp.sum(jnp.where(lane == e, gates_ref[...], 0.0), axis=1, keepdims=True)
        pos = jnp.sum(jnp.where(lane == e, pos_ref[...], 0.0), axis=1, keepdims=True)
        return gate, jnp.where(gate > 0.0, pos, -1.0)

    def one_hot(pos, sbi):
        slot = (lax.broadcasted_iota(jnp.int32, (1, MOE_SUB), 1) + sbi * MOE_SUB).astype(F32)
        return jnp.where(pos == slot, 1.0, 0.0).astype(BF16)

    @pl.when(f == 0)
    def _():
        count = jnp.sum(jnp.where(lane == e, tot_ref[...], 0.0)).astype(jnp.int32)
        nsb = (count + (MOE_SUB - 1)) // MOE_SUB
        nsb_ref[0] = nsb
        gate, pos = expert_columns()
        g_hi = gate.astype(BF16).astype(F32)
        g_mid = (gate - g_hi).astype(BF16).astype(F32)
        g_lo = gate - g_hi - g_mid
        g_parts = jnp.where(lane == 0, g_hi, jnp.where(lane == 1, g_mid,
                                                      jnp.where(lane == 2, g_lo, 0.0))).astype(BF16)

        def gather(sbi, carry):
            rs = pl.multiple_of(sbi * MOE_SUB, MOE_SUB)
            oh = one_hot(pos, sbi)
            xg_ref[pl.ds(rs, MOE_SUB), :] = _dot_tn(oh, xb_ref[...]).astype(BF16)
            gg = jnp.sum(_dot_tn(oh, g_parts), axis=1, keepdims=True)
            gg_ref[pl.ds(rs, MOE_SUB), :] = jnp.broadcast_to(gg, (MOE_SUB, 128))
            ya_ref[pl.ds(rs, MOE_SUB), :] = jnp.zeros((MOE_SUB, ya_ref.shape[1]), F32)
            return carry
        lax.fori_loop(0, nsb, gather, 0)

    nsb = nsb_ref[0]

    def ffn(sbi, carry):
        rs = pl.multiple_of(sbi * MOE_SUB, MOE_SUB)
        xg = xg_ref[pl.ds(rs, MOE_SUB), :]
        g = _dot(xg, wg_ref[...])
        u = _dot(xg, wu_ref[...])
        h = (g * _sigmoid(g) * u * gg_ref[pl.ds(rs, MOE_SUB), 0:1]).astype(BF16)
        ya_ref[pl.ds(rs, MOE_SUB), :] += _dot(h, wd_ref[...])
        return carry
    lax.fori_loop(0, nsb, ffn, 0)

    @pl.when(f == last_f)
    def _():
        _, pos = expert_columns()

        def scatter(sbi, carry):
            rs = pl.multiple_of(sbi * MOE_SUB, MOE_SUB)
            o_ref[...] += _dot(one_hot(pos, sbi), ya_ref[pl.ds(rs, MOE_SUB), :].astype(BF16))
            return carry
        lax.fori_loop(0, nsb, scatter, 0)

    @pl.when(jnp.logical_and(e == last_e, f == last_f))
    def _():
        o_ref[...] = _layer_norm(ALPHA * x_ref[...] + o_ref[...], lng_ref[...], lnb_ref[...])


def _moe(x3, x3b, gates, wg, wu, wd, ln_g, ln_b, *, tm, tf):
    n, d = x3.shape
    ne, _, ff = wg.shape
    tri = (lax.broadcasted_iota(jnp.int32, (MOE_CUM, MOE_CUM), 1)
           < lax.broadcasted_iota(jnp.int32, (MOE_CUM, MOE_CUM), 0)).astype(BF16)
    once = pl.Buffered(1)
    return pl.pallas_call(
        _moe_kernel,
        grid=(n // tm, ne, ff // tf),
        in_specs=[pl.BlockSpec((tm, d), lambda i, e, f: (i, 0), pipeline_mode=once),
                  pl.BlockSpec((tm, 128), lambda i, e, f: (i, 0)),
                  _full((MOE_CUM, MOE_CUM)),
                  pl.BlockSpec((None, d, tf), lambda i, e, f: (e, 0, f)),
                  pl.BlockSpec((None, d, tf), lambda i, e, f: (e, 0, f)),
                  pl.BlockSpec((None, tf, d), lambda i, e, f: (e, f, 0)),
                  pl.BlockSpec((tm, d), lambda i, e, f: (i, 0), pipeline_mode=once),
                  _full((1, d)), _full((1, d))],
        out_specs=pl.BlockSpec((tm, d), lambda i, e, f: (i, 0), pipeline_mode=once),
        out_shape=jax.ShapeDtypeStruct((n, d), F32),
        scratch_shapes=[pltpu.VMEM((tm, d), BF16), pltpu.VMEM((tm, d), F32),
                        pltpu.VMEM((tm, 128), F32), pltpu.VMEM((tm, 128), F32),
                        pltpu.VMEM((1, 128), F32), pltpu.SMEM((1,), jnp.int32)],
        compiler_params=_params("arbitrary", "arbitrary", "arbitrary"),
        name="moe_swiglu",
    )(x3b, gates, tri, wg, wu, wd, x3, ln_g, ln_b)


def _pad_cols(w, width):
    return jnp.pad(w, ((0, 0), (0, width - w.shape[1])))


def _pad_rows(w, height):
    return jnp.pad(w, ((0, height - w.shape[0]), (0, 0)))


def _forward(x, mix_w_in, rwkv_mu, rwkv_w0, rwkv_w_up, rwkv_a0, rwkv_a_up, rwkv_g_up,
             rwkv_k_k, rwkv_k_a, rwkv_r_k, rwkv_gn_g, rwkv_gn_b, fox_b_f, mix_w_out,
             mix_ln_g, mix_ln_b, ffn_w_gate, ffn_w_up, ffn_w_down, ffn_ln_g, ffn_ln_b,
             conv_w_pw1, conv_b_pw1, conv_w_dw, conv_b_dw, conv_ln_g, conv_ln_b,
             conv_w_pw2, conv_b_pw2, conv_post_ln_g, conv_post_ln_b,
             moe_w_router, moe_w_gate, moe_w_up, moe_w_down, moe_ln_g, moe_ln_b,
             *, tm=512, chunk=64, nb_rwkv=4, tq=512, tf_ffn=1408, tm_moe=2048, tf_moe=896):
    batch, seq, d = x.shape
    n = batch * seq
    gw = GROUP_W
    x2 = x.reshape(n, d)
    row = lambda t: t.reshape(1, -1)

    w_in = mix_w_in[0]
    mu = rwkv_mu[0]
    o_w, o_a, o_g = 3 * gw, 3 * gw + DECAY_LORA, 3 * gw + DECAY_LORA + AAA_LORA
    o_fox = o_g + GATE_LORA

    def lora_layout(t):
        return jnp.concatenate([t[..., :o_w],
                                _pad_cols(t[..., o_w:o_a], LORA_PAD),
                                _pad_cols(t[..., o_a:o_g], LORA_PAD),
                                _pad_cols(t[..., o_g:o_fox], LORA_PAD)], axis=-1)

    wa = lora_layout(w_in).astype(BF16)
    mu_a = lora_layout(row(mu))
    scale = LOG2E / math.sqrt(HEAD_DIM)
    wb = jnp.concatenate([w_in[:, o_fox:o_fox + gw] * scale,
                          w_in[:, o_fox + gw:o_fox + 3 * gw]], axis=1).astype(BF16)
    wf = _pad_cols(w_in[:, o_fox + 3 * gw:], 128).astype(BF16)
    bf = _pad_cols(row(fox_b_f[0]), 128)

    pr, qk, vt, c = _inproj(x2, wa, wb, wf, mu_a, bf, seq=seq, tm=tq)

    wup = _pad_rows(rwkv_w_up[0], LORA_PAD).astype(BF16)
    aup = _pad_rows(rwkv_a_up[0], LORA_PAD).astype(BF16)
    gup = _pad_rows(rwkv_g_up[0], LORA_PAD).astype(BF16)
    k_k, k_a, r_k = row(rwkv_k_k[0]), row(rwkv_k_a[0]), row(rwkv_r_k[0])
    yr = _rwkv(pr, row(rwkv_w0[0]), wup, row(rwkv_a0[0]), aup, k_k, k_a,
               batch=batch, seq=seq, chunk=chunk, nb=nb_rwkv)

    yf = _fox(qk, vt, c, batch=batch, seq=seq, tq=tq)

    w_out = mix_w_out[0].astype(BF16)
    x1 = _mixout(x2, pr, yr, yf, row(rwkv_a0[0]), aup, gup, k_a, r_k,
                 row(rwkv_gn_g[0]), row(rwkv_gn_b[0]), w_out[:gw], w_out[gw:],
                 row(mix_ln_g[0]), row(mix_ln_b[0]), tm=tm)
    x2b = _ffn(x1, ffn_w_gate[0].astype(BF16), ffn_w_up[0].astype(BF16),
               ffn_w_down[0].astype(BF16), row(ffn_ln_g[0]), row(ffn_ln_b[0]), tm=tm, tf=tf_ffn)

    hg = _glu(x2b, conv_w_pw1[0].astype(BF16), row(conv_b_pw1[0]), tm=tm)
    w_router = _pad_cols(moe_w_router[0], 128)
    wr_hi = w_router.astype(BF16)
    w_router = jnp.concatenate([wr_hi, (w_router - wr_hi.astype(F32)).astype(BF16)], axis=1)
    x3, x3b, gates = _conv(hg, x2b, _pad_rows(conv_w_dw[0], CONV_HALO), row(conv_b_dw[0]),
                           row(conv_ln_g[0]), row(conv_ln_b[0]), conv_w_pw2[0].astype(BF16),
                           row(conv_b_pw2[0]), row(conv_post_ln_g[0]), row(conv_post_ln_b[0]),
                           w_router, seq=seq, tm=tm)
    out = _moe(x3, x3b, gates, moe_w_gate[0].astype(BF16), moe_w_up[0].astype(BF16),
               moe_w_down[0].astype(BF16), row(moe_ln_g[0]), row(moe_ln_b[0]),
               tm=tm_moe, tf=tf_moe)
    return out.reshape(batch, seq, d)


def kernel(x, mix_w_in, rwkv_mu, rwkv_w0, rwkv_w_up, rwkv_a0, rwkv_a_up, rwkv_g_up, rwkv_k_k, rwkv_k_a, rwkv_r_k, rwkv_gn_g, rwkv_gn_b, fox_b_f, mix_w_out, mix_ln_g, mix_ln_b, ffn_w_gate, ffn_w_up, ffn_w_down, ffn_ln_g, ffn_ln_b, conv_w_pw1, conv_b_pw1, conv_w_dw, conv_b_dw, conv_ln_g, conv_ln_b, conv_w_pw2, conv_b_pw2, conv_post_ln_g, conv_post_ln_b, moe_w_router, moe_w_gate, moe_w_up, moe_w_down, moe_ln_g, moe_ln_b):
    return _forward(x, mix_w_in, rwkv_mu, rwkv_w0, rwkv_w_up, rwkv_a0, rwkv_a_up, rwkv_g_up,
                    rwkv_k_k, rwkv_k_a, rwkv_r_k, rwkv_gn_g, rwkv_gn_b, fox_b_f, mix_w_out,
                    mix_ln_g, mix_ln_b, ffn_w_gate, ffn_w_up, ffn_w_down, ffn_ln_g, ffn_ln_b,
                    conv_w_pw1, conv_b_pw1, conv_w_dw, conv_b_dw, conv_ln_g, conv_ln_b,
                    conv_w_pw2, conv_b_pw2, conv_post_ln_g, conv_post_ln_b,
                    moe_w_router, moe_w_gate, moe_w_up, moe_w_down, moe_ln_g, moe_ln_b)
```

```python
import functools
import math

import jax
import jax.numpy as jnp
from jax import lax
from jax.experimental import pallas as pl
from jax.experimental.pallas import tpu as pltpu

F32 = jnp.float32
BF16 = jnp.bfloat16
HIGHEST = lax.Precision.HIGHEST

HEAD_DIM = 64
N_HEADS = 8
GROUP_W = N_HEADS * HEAD_DIM
LORA_PAD = 128
DECAY_LORA = 32
AAA_LORA = 32
GATE_LORA = 96
CONV_WIDTH = 31
CONV_HALO = 32
SUBLANES = 8
N_EXPERTS = 8
TOP_K = 2
LN_EPS = 1e-5
GN_EPS = 64e-5
DEPTH = 2
ALPHA = (2.0 * DEPTH) ** 0.25
NEG_BIG = -1e30
LOG2E = math.log2(math.e)
VMEM_LIMIT = 56 * 1024 * 1024


def _dot(a, b, **kw):
    return jnp.dot(a, b, preferred_element_type=F32, **kw)


def _dot_nt(a, b):
    return lax.dot_general(a, b, (((1,), (1,)), ((), ())), preferred_element_type=F32)


def _dot_tn(a, b):
    return lax.dot_general(a, b, (((0,), (0,)), ((), ())), preferred_element_type=F32)


def _dot_exact_lhs(a, v):
    hi = v.astype(BF16)
    rem = v - hi.astype(F32)
    mid = rem.astype(BF16)
    lo = (rem - mid.astype(F32)).astype(BF16)
    w = v.shape[1]
    out = _dot(a, jnp.concatenate([hi, mid, lo], axis=1))
    return out[:, :w] + out[:, w:2 * w] + out[:, 2 * w:]


def _sigmoid(z):
    return 1.0 / (1.0 + jnp.exp(-z))


def _softplus(z):
    return jnp.maximum(z, 0.0) + jnp.log1p(jnp.exp(-jnp.abs(z)))


def _layer_norm(h, g, b):
    mu = jnp.mean(h, axis=-1, keepdims=True)
    d = h - mu
    var = jnp.mean(d * d, axis=-1, keepdims=True)
    return d * lax.rsqrt(var + LN_EPS) * g + b


def _params(*sem):
    return pltpu.CompilerParams(dimension_semantics=sem, vmem_limit_bytes=VMEM_LIMIT)


def _full(shape):
    return pl.BlockSpec(shape, lambda *_: (0,) * len(shape))


def _inproj_kernel(x_ref, wa_ref, wb_ref, wf_ref, mu_ref, bf_ref, tri_ref,
                   pr_ref, qk_ref, vt_ref, c_ref, last_ref, carry_ref, *, tiles_per_seq):
    i = pl.program_id(0)

    @pl.when(i % tiles_per_seq == 0)
    def _():
        last_ref[...] = jnp.zeros_like(last_ref)
        carry_ref[...] = jnp.zeros_like(carry_ref)

    xb = x_ref[...].astype(BF16)
    tm = xb.shape[0]
    row0 = lax.broadcasted_iota(jnp.int32, (tm, 1), 0) == 0
    ca = wa_ref.shape[1]
    for c0 in range(0, ca, GROUP_W):
        cw = min(GROUP_W, ca - c0)
        p = _dot(xb, wa_ref[:, c0:c0 + cw])
        prev = jnp.where(row0, last_ref[:, c0:c0 + cw], pltpu.roll(p, 1, 0))
        last_ref[:, c0:c0 + cw] = p[tm - 1:tm, :]
        pr_ref[:, c0:c0 + cw] = p + mu_ref[:, c0:c0 + cw] * (prev - p)
    for c0 in range(0, 2 * GROUP_W, GROUP_W):
        qk_ref[:, c0:c0 + GROUP_W] = _dot(xb, wb_ref[:, c0:c0 + GROUP_W]).astype(BF16)
    vt_ref[...] = _dot(xb, wb_ref[:, 2 * GROUP_W:3 * GROUP_W]).T.astype(BF16)
    fl = _dot(xb, wf_ref[...]) + bf_ref[...]
    log_f = jnp.minimum(fl, 0.0) - jnp.log1p(jnp.exp(-jnp.abs(fl)))
    c = _dot_exact_lhs(tri_ref[...], log_f) + carry_ref[...]
    c_ref[...] = c
    carry_ref[...] = c[tm - 1:tm, :]


def _inproj(x2, wa, wb, wf, mu, bf, *, seq, tm):
    n, d = x2.shape
    ca, cb = wa.shape[1], wb.shape[1]
    tps = seq // tm
    tri = (lax.broadcasted_iota(jnp.int32, (tm, tm), 1)
           <= lax.broadcasted_iota(jnp.int32, (tm, tm), 0)).astype(BF16)
    return pl.pallas_call(
        functools.partial(_inproj_kernel, tiles_per_seq=tps),
        grid=(n // tm,),
        in_specs=[pl.BlockSpec((tm, d), lambda i: (i, 0)),
                  _full((d, ca)), _full((d, cb)), _full((d, 128)),
                  _full((1, ca)), _full((1, 128)), _full((tm, tm))],
        out_specs=[pl.BlockSpec((tm, ca), lambda i: (i, 0)),
                   pl.BlockSpec((tm, 2 * GROUP_W), lambda i: (i, 0)),
                   pl.BlockSpec((None, None, GROUP_W, tm), lambda i: (i // tps, i % tps, 0, 0)),
                   pl.BlockSpec((tm, 128), lambda i: (i, 0))],
        out_shape=[jax.ShapeDtypeStruct((n, ca), F32),
                   jax.ShapeDtypeStruct((n, 2 * GROUP_W), BF16),
                   jax.ShapeDtypeStruct((n // seq, tps, GROUP_W, tm), BF16),
                   jax.ShapeDtypeStruct((n, 128), F32)],
        scratch_shapes=[pltpu.VMEM((1, ca), F32), pltpu.VMEM((1, 128), F32)],
        compiler_params=_params("arbitrary"),
        name="inproj",
    )(x2, wa, wb, wf, mu, bf, tri)


def _rwkv_kernel(pr_ref, w0_ref, wup_ref, a0_ref, aup_ref, kk_ref, ka_ref,
                 gsum_ref, tri_ref, y_ref, h_ref, *, chunk, nb):
    @pl.when(pl.program_id(1) == 0)
    def _():
        h_ref[...] = jnp.zeros_like(h_ref)

    gw = GROUP_W
    pw = 2 * HEAD_DIM
    npair = N_HEADS // 2
    rows = 2 * chunk
    log_chunk = int(math.log2(chunk))
    head0 = lax.broadcasted_iota(jnp.int32, (1, pw), 1) < HEAD_DIM
    row = lax.broadcasted_iota(jnp.int32, (rows, rows), 0)
    col = lax.broadcasted_iota(jnp.int32, (rows, rows), 1)
    strict = (col & (chunk - 1)) < (row & (chunk - 1))
    incl = (col & (chunk - 1)) <= (row & (chunk - 1))
    eye = (col == row).astype(F32)
    peye = (lax.broadcasted_iota(jnp.int32, (pw, pw), 0)
            == lax.broadcasted_iota(jnp.int32, (pw, pw), 1))

    def stack(x):
        return jnp.concatenate([jnp.where(head0, x, 0.0), jnp.where(head0, 0.0, x)],
                               axis=0).astype(BF16)

    units = [(b, j) for b in range(nb) for j in range(npair)]
    nu = len(units)
    ar, bk, vs, bhs, khs, pcs = [], [], [], [], [], []
    for b in range(nb):
        r = pr_ref[b, :, 0:gw]
        k = pr_ref[b, :, gw:2 * gw]
        v = pr_ref[b, :, 2 * gw:3 * gw]
        wd = pr_ref[b, :, 3 * gw:3 * gw + LORA_PAD]
        ad = pr_ref[b, :, 3 * gw + LORA_PAD:3 * gw + 2 * LORA_PAD]
        w_pre = w0_ref[...] + _dot(jnp.tanh(wd).astype(BF16), wup_ref[...])
        w = -_softplus(-w_pre) - 0.5
        log_decay = -jnp.exp(w)
        a = _sigmoid(a0_ref[...] + _dot(ad.astype(BF16), aup_ref[...]))
        kk = k * kk_ref[...]
        norm = jnp.sqrt(_dot((kk * kk).astype(BF16), gsum_ref[...]))
        kk = kk / jnp.maximum(norm, 1e-12)
        k_mod = k * (1.0 + (a - 1.0) * ka_ref[...])
        b_vec = kk * a
        cum = _dot_exact_lhs(tri_ref[...], log_decay)
        last = cum[chunk - 1:chunk, :]
        p_inv = jnp.exp(-cum)
        p_tail = jnp.exp(last - cum)
        a_t = -kk * jnp.exp(cum - log_decay)
        r_t = r * jnp.exp(cum)
        b_t = b_vec * p_inv
        k_t = k_mod * p_inv
        b_h = b_vec * p_tail
        k_h = k_mod * p_tail
        p_last = jnp.exp(last)
        for j in range(npair):
            sl = slice(j * pw, (j + 1) * pw)
            ar.append(jnp.concatenate([stack(a_t[:, sl]), stack(r_t[:, sl])], axis=0))
            bk.append(jnp.concatenate([stack(b_t[:, sl]), stack(k_t[:, sl])], axis=0))
            vs.append(stack(v[:, sl]))
            bhs.append(stack(b_h[:, sl]))
            khs.append(stack(k_h[:, sl]))
            pcs.append(jnp.sum(jnp.where(peye, p_last[:, sl], 0.0), axis=1, keepdims=True))

    gram = [_dot_nt(ar[u], bk[u]) for u in range(nu)]
    l_ab = [jnp.where(strict, gram[u][:rows, :rows], 0.0) for u in range(nu)]
    l_akv = [_dot(jnp.where(strict, gram[u][:rows, rows:], 0.0).astype(BF16), vs[u])
             for u in range(nu)]
    m_rb = [jnp.where(incl, gram[u][rows:, :rows], 0.0).astype(BF16) for u in range(nu)]
    m_rkv = [_dot(jnp.where(incl, gram[u][rows:, rows:], 0.0).astype(BF16), vs[u])
             for u in range(nu)]
    t_inv = [eye + l_ab[u] for u in range(nu)]
    xb = [l_ab[u].astype(BF16) for u in range(nu)]
    xp = [_dot(xb[u], xb[u]) for u in range(nu)]
    for step in range(log_chunk - 1):
        xb = [xp[u].astype(BF16) for u in range(nu)]
        if step < log_chunk - 2:
            both = [_dot(jnp.concatenate([t_inv[u].astype(BF16), xb[u]], axis=0), xb[u])
                    for u in range(nu)]
            t_inv = [t_inv[u] + both[u][:rows] for u in range(nu)]
            xp = [both[u][rows:] for u in range(nu)]
        else:
            t_inv = [t_inv[u] + _dot(t_inv[u].astype(BF16), xb[u]) for u in range(nu)]
    tw = [_dot(t_inv[u].astype(BF16),
               jnp.concatenate([ar[u][:rows], l_akv[u].astype(BF16)], axis=1)).astype(BF16)
          for u in range(nu)]
    mw = [_dot(m_rb[u], tw[u]) for u in range(nu)]
    bw = [_dot_tn(bhs[u], tw[u]) for u in range(nu)]
    kv = [_dot_tn(khs[u], vs[u]) for u in range(nu)]
    for u, (b, j) in enumerate(units):
        wy = ar[u][rows:].astype(F32) + mw[u][:, :pw]
        yc = mw[u][:, pw:] + m_rkv[u]
        hf = h_ref[u]
        yh = _dot(jnp.concatenate([wy.astype(BF16), bw[u][:, :pw].astype(BF16)], axis=0),
                  hf.astype(BF16))
        h_ref[u] = pcs[u] * hf + yh[rows:] + bw[u][:, pw:] + kv[u]
        ys = yh[:rows] + yc
        y_ref[b, :, j * pw:(j + 1) * pw] = ys[:chunk] + ys[chunk:]


def _rwkv(pr, w0, wup, a0, aup, k_k, k_a, *, batch, seq, chunk, nb):
    n, ca = pr.shape
    nch = seq // chunk
    gidx = lax.broadcasted_iota(jnp.int32, (GROUP_W, GROUP_W), 0) // HEAD_DIM
    gsum = (gidx == gidx.T).astype(BF16)
    tri = (lax.broadcasted_iota(jnp.int32, (chunk, chunk), 1)
           <= lax.broadcasted_iota(jnp.int32, (chunk, chunk), 0)).astype(BF16)
    y = pl.pallas_call(
        functools.partial(_rwkv_kernel, chunk=chunk, nb=nb),
        grid=(batch // nb, nch),
        in_specs=[pl.BlockSpec((nb, chunk, ca), lambda g, c: (g, c, 0)),
                  _full((1, GROUP_W)), _full((LORA_PAD, GROUP_W)),
                  _full((1, GROUP_W)), _full((LORA_PAD, GROUP_W)),
                  _full((1, GROUP_W)), _full((1, GROUP_W)),
                  _full((GROUP_W, GROUP_W)), _full((chunk, chunk))],
        out_specs=pl.BlockSpec((nb, chunk, GROUP_W), lambda g, c: (g, c, 0)),
        out_shape=jax.ShapeDtypeStruct((batch, seq, GROUP_W), F32),
        scratch_shapes=[pltpu.VMEM((nb * (N_HEADS // 2), 2 * HEAD_DIM, 2 * HEAD_DIM), F32)],
        compiler_params=_params("arbitrary", "arbitrary"),
        name="rwkv_scan",
    )(pr.reshape(batch, seq, ca), w0, wup, a0, aup, k_k, k_a, gsum, tri)
    return y.reshape(n, GROUP_W)


def _fox_kernel(q_ref, k_ref, vt_ref, c_ref, o_ref, acc_ref, m_ref, l_ref, kb_ref,
                sa_ref, sb_ref, *, tq):
    j = pl.program_id(1)
    qi = pl.program_id(2)
    pw = 2 * HEAD_DIM
    seq = k_ref.shape[0]
    lane = lax.broadcasted_iota(jnp.int32, (1, pw), 1)

    first = lane < HEAD_DIM
    bias_lane = (HEAD_DIM, 0)

    @pl.when(qi == 0)
    def _():
        def fill(rb, carry):
            rs = pl.multiple_of(rb * tq, tq)
            cblk = c_ref[pl.ds(rs, tq), :]
            for hh in range(2):
                bias = -LOG2E * jnp.sum(jnp.where(lane == 2 * j + hh, cblk, 0.0),
                                        axis=1, keepdims=True)
                b_hi = bias.astype(BF16).astype(F32)
                b_mid = (bias - b_hi).astype(BF16).astype(F32)
                b_lo = bias - b_hi - b_mid
                l0 = bias_lane[hh]
                kb_ref[hh, pl.ds(rs, tq), :] = jnp.where(
                    lane == l0, b_hi, jnp.where(lane == l0 + 1, b_mid,
                                                jnp.where(lane == l0 + 2, b_lo, 0.0))).astype(BF16)
            return carry
        lax.fori_loop(0, seq // tq, fill, 0)

    q = q_ref[...]
    ones3 = [jnp.where(jnp.logical_and(lane >= l0, lane < l0 + 3), 1.0, 0.0).astype(BF16)
             for l0 in bias_lane]
    own = (first, jnp.logical_not(first))
    qh = tuple(jnp.where(own[hh], q, ones3[hh]) for hh in range(2))
    acc_ref[...] = jnp.zeros_like(acc_ref)
    m_ref[...] = jnp.full_like(m_ref, NEG_BIG)
    l_ref[...] = jnp.zeros_like(l_ref)
    causal = (lax.broadcasted_iota(jnp.int32, (tq, tq), 0)
              <= lax.broadcasted_iota(jnp.int32, (tq, tq), 1))
    top = lax.broadcasted_iota(jnp.int32, (pw, 1), 0) < HEAD_DIM

    def scores(kb, s_ref):
        ks = pl.multiple_of(kb * tq, tq)
        kblk = k_ref[pl.ds(ks, tq), :]
        for hh in range(2):
            k_aug = jnp.where(own[hh], kblk, kb_ref[hh, pl.ds(ks, tq), :])
            s_ref[hh] = _dot_nt(k_aug, qh[hh])

    def softmax_pv(kb, s_ref, masked):
        vt = vt_ref[kb].astype(F32)
        vts = (jnp.where(top, vt, 1.0).astype(BF16), jnp.where(top, 1.0, vt).astype(BF16))
        alphas, pvs = [], []
        for hh in range(2):
            z = s_ref[hh]
            if masked:
                z = jnp.where(causal, z, NEG_BIG)
            m_prev = m_ref[hh]
            m_new = jnp.maximum(m_prev, jnp.max(z, axis=0, keepdims=True))
            alpha = jnp.exp2(m_prev - m_new)
            p = jnp.exp2(z - m_new)
            pv = _dot(vts[hh], p.astype(BF16))
            ones_row = (1 - hh) * HEAD_DIM
            l_ref[hh] = alpha * l_ref[hh] + pv[ones_row:ones_row + 1, :]
            m_ref[hh] = m_new
            alphas.append(alpha)
            pvs.append(pv)
        acc_ref[...] = (acc_ref[...] * jnp.where(top, alphas[0], alphas[1])
                        + jnp.where(top, pvs[0], pvs[1]))

    scores(0, sa_ref)

    def body(i, carry):
        scores(2 * i + 1, sb_ref)
        softmax_pv(2 * i, sa_ref, False)
        scores(2 * i + 2, sa_ref)
        softmax_pv(2 * i + 1, sb_ref, False)
        return carry

    lax.fori_loop(0, qi // 2, body, 0)

    @pl.when(qi % 2 == 0)
    def _():
        softmax_pv(qi, sa_ref, True)

    @pl.when(qi % 2 == 1)
    def _():
        scores(qi, sb_ref)
        softmax_pv(qi - 1, sa_ref, False)
        softmax_pv(qi, sb_ref, True)

    out_t = acc_ref[...] / jnp.where(top, l_ref[0], l_ref[1])
    o_ref[...] = out_t.T.astype(BF16)


def _fox(qk, vt, c, *, batch, seq, tq):
    n = qk.shape[0]
    nq = seq // tq
    npair = N_HEADS // 2
    pw = 2 * HEAD_DIM
    assert vt.shape == (batch, nq, GROUP_W, tq)
    return pl.pallas_call(
        functools.partial(_fox_kernel, tq=tq),
        grid=(batch, npair, nq),
        in_specs=[pl.BlockSpec((tq, pw), lambda b, j, i: (b * nq + i, j)),
                  pl.BlockSpec((seq, pw), lambda b, j, i: (b, npair + j)),
                  pl.BlockSpec((None, nq, pw, tq), lambda b, j, i: (b, 0, j, 0)),
                  pl.BlockSpec((seq, 128), lambda b, j, i: (b, 0))],
        out_specs=pl.BlockSpec((tq, pw), lambda b, j, i: (b * nq + i, j)),
        out_shape=jax.ShapeDtypeStruct((n, GROUP_W), BF16),
        scratch_shapes=[pltpu.VMEM((pw, tq), F32),
                        pltpu.VMEM((2, 1, tq), F32), pltpu.VMEM((2, 1, tq), F32),
                        pltpu.VMEM((2, seq, pw), BF16),
                        pltpu.VMEM((2, tq, tq), F32), pltpu.VMEM((2, tq, tq), F32)],
        compiler_params=_params("arbitrary", "arbitrary", "arbitrary"),
        name="fox_attention",
    )(qk, qk, vt, c)


def _mixout_kernel(x_ref, pr_ref, yr_ref, yf_ref, a0_ref, aup_ref, gup_ref, ka_ref,
                   rk_ref, gng_ref, gnb_ref, gsum_ref, wr_ref, wf_ref, lng_ref, lnb_ref,
                   o_ref):
    gw = GROUP_W
    r = pr_ref[:, 0:gw]
    k = pr_ref[:, gw:2 * gw]
    v = pr_ref[:, 2 * gw:3 * gw]
    ad = pr_ref[:, 3 * gw + LORA_PAD:3 * gw + 2 * LORA_PAD]
    gd = pr_ref[:, 3 * gw + 2 * LORA_PAD:3 * gw + 3 * LORA_PAD]
    a = _sigmoid(a0_ref[...] + _dot(ad.astype(BF16), aup_ref[...]))
    k_mod = k * (1.0 + (a - 1.0) * ka_ref[...])
    gate = _dot(_sigmoid(gd).astype(BF16), gup_ref[...])
    gsum = gsum_ref[...]

    def group_sum(t):
        return _dot(t.astype(BF16), gsum)

    y = yr_ref[...]
    y_hi = y.astype(BF16)
    y_lo = (y - y_hi.astype(F32)).astype(BF16)
    mean = (_dot(y_hi, gsum) + _dot(y_lo, gsum)) * (1.0 / HEAD_DIM)
    d = y - mean
    var = group_sum(d * d) * (1.0 / HEAD_DIM)
    yn = d * lax.rsqrt(var + GN_EPS) * gng_ref[...] + gnb_ref[...]
    bonus = group_sum(r * k_mod * rk_ref[...])
    y_rwkv = ((yn + bonus * v) * gate).astype(BF16)
    mixed = _dot(y_rwkv, wr_ref[...]) + _dot(yf_ref[...], wf_ref[...])
    o_ref[...] = _layer_norm(ALPHA * x_ref[...] + mixed, lng_ref[...], lnb_ref[...])


def _mixout(x2, pr, yr, yf, a0, aup, gup, k_a, r_k, gn_g, gn_b, w_r, w_f, ln_g, ln_b, *, tm):
    n, d = x2.shape
    ca = pr.shape[1]
    gidx = lax.broadcasted_iota(jnp.int32, (GROUP_W, GROUP_W), 0) // HEAD_DIM
    gsum = (gidx == gidx.T).astype(BF16)
    vec = _full((1, GROUP_W))
    return pl.pallas_call(
        _mixout_kernel,
        grid=(n // tm,),
        in_specs=[pl.BlockSpec((tm, d), lambda i: (i, 0)),
                  pl.BlockSpec((tm, ca), lambda i: (i, 0)),
                  pl.BlockSpec((tm, GROUP_W), lambda i: (i, 0)),
                  pl.BlockSpec((tm, GROUP_W), lambda i: (i, 0)),
                  vec, _full((LORA_PAD, GROUP_W)), _full((LORA_PAD, GROUP_W)),
                  vec, vec, vec, vec, _full((GROUP_W, GROUP_W)),
                  _full((GROUP_W, d)), _full((GROUP_W, d)),
                  _full((1, d)), _full((1, d))],
        out_specs=pl.BlockSpec((tm, d), lambda i: (i, 0)),
        out_shape=jax.ShapeDtypeStruct((n, d), F32),
        compiler_params=_params("parallel"),
        name="mix_out",
    )(x2, pr, yr, yf, a0, aup, gup, k_a, r_k, gn_g, gn_b, gsum, w_r, w_f, ln_g, ln_b)


def _ffn_kernel(x_ref, wg_ref, wu_ref, wd_ref, lng_ref, lnb_ref, o_ref, acc_ref):
    f = pl.program_id(1)

    @pl.when(f == 0)
    def _():
        acc_ref[...] = jnp.zeros_like(acc_ref)

    xb = x_ref[...].astype(BF16)
    g = _dot(xb, wg_ref[...])
    u = _dot(xb, wu_ref[...])
    h = (g * _sigmoid(g) * u).astype(BF16)
    acc_ref[...] += _dot(h, wd_ref[...])

    @pl.when(f == pl.num_programs(1) - 1)
    def _():
        o_ref[...] = _layer_norm(ALPHA * x_ref[...] + acc_ref[...], lng_ref[...], lnb_ref[...])


def _ffn(x2, wg, wu, wd, ln_g, ln_b, *, tm, tf):
    n, d = x2.shape
    ff = wg.shape[1]
    return pl.pallas_call(
        _ffn_kernel,
        grid=(n // tm, ff // tf),
        in_specs=[pl.BlockSpec((tm, d), lambda i, f: (i, 0)),
                  pl.BlockSpec((d, tf), lambda i, f: (0, f)),
                  pl.BlockSpec((d, tf), lambda i, f: (0, f)),
                  pl.BlockSpec((tf, d), lambda i, f: (f, 0)),
                  _full((1, d)), _full((1, d))],
        out_specs=pl.BlockSpec((tm, d), lambda i, f: (i, 0)),
        out_shape=jax.ShapeDtypeStruct((n, d), F32),
        scratch_shapes=[pltpu.VMEM((tm, d), F32)],
        compiler_params=_params("parallel", "arbitrary"),
        name="ffn_swiglu",
    )(x2, wg, wu, wd, ln_g, ln_b)


def _glu_kernel(x_ref, w_ref, b_ref, o_ref):
    d = o_ref.shape[1]
    xb = x_ref[...].astype(BF16)
    val = _dot(xb, w_ref[:, 0:d]) + b_ref[:, 0:d]
    gat = _dot(xb, w_ref[:, d:2 * d]) + b_ref[:, d:2 * d]
    o_ref[...] = val * _sigmoid(gat)


def _glu(x2, w, b, *, tm):
    n, d = x2.shape
    return pl.pallas_call(
        _glu_kernel,
        grid=(n // tm,),
        in_specs=[pl.BlockSpec((tm, d), lambda i: (i, 0)), _full((d, 2 * d)), _full((1, 2 * d))],
        out_specs=pl.BlockSpec((tm, d), lambda i: (i, 0)),
        out_shape=jax.ShapeDtypeStruct((n, d), F32),
        compiler_params=_params("parallel"),
        name="conv_glu",
    )(x2, w, b)


def _top2(logits):
    lane = lax.broadcasted_iota(jnp.int32, logits.shape, 1).astype(F32)
    lg = jnp.where(lane < N_EXPERTS, logits, NEG_BIG)
    m1 = jnp.max(lg, axis=-1, keepdims=True)
    i1 = jnp.min(jnp.where(lg == m1, lane, 128.0), axis=-1, keepdims=True)
    lg2 = jnp.where(lane == i1, NEG_BIG, lg)
    m2 = jnp.max(lg2, axis=-1, keepdims=True)
    i2 = jnp.min(jnp.where(lg2 == m2, lane, 128.0), axis=-1, keepdims=True)
    e2 = jnp.exp(m2 - m1)
    w1 = 1.0 / (1.0 + e2)
    w2 = e2 / (1.0 + e2)
    member = jnp.logical_or(lane == i1, lane == i2)
    record = jnp.where(lane == 0.0, i1, jnp.where(lane == 1.0, i2,
                                                  jnp.where(lane == 2.0, w1,
                                                            jnp.where(lane == 3.0, w2, 0.0))))
    return member, record


def _conv_kernel(hc_ref, hp_ref, x_ref, wdw_ref, bdw_ref, lng_ref, lnb_ref, w2_ref, b2_ref,
                 pg_ref, pb_ref, wr_ref, tri_ref, x3_ref, route_ref, pos_ref, tot_ref,
                 ext_ref, cv_ref, cnt_ref, *, tiles_per_seq):
    tm, d = x_ref.shape

    @pl.when(pl.program_id(0) == 0)
    def _():
        cnt_ref[...] = jnp.zeros_like(cnt_ref)

    first = pl.program_id(0) % tiles_per_seq == 0
    ext_ref[0, 0:CONV_HALO, :] = jnp.where(first, 0.0, hp_ref[...])
    ext_ref[0, CONV_HALO:CONV_HALO + tm, :] = hc_ref[...]
    nrows = tm + CONV_HALO
    for c0 in range(0, d, 256):
        base = ext_ref[0, :, c0:c0 + 256]
        for j in range(1, SUBLANES):
            ext_ref[j, :, c0:c0 + 256] = pltpu.roll(base, nrows - j, 0)
    off = CONV_HALO - (CONV_WIDTH - 1)
    rc, cc = 64, 256
    for r0 in range(0, tm, rc):
        for c0 in range(0, d, cc):
            acc = jnp.broadcast_to(bdw_ref[:, c0:c0 + cc], (rc, cc))
            for t in range(CONV_WIDTH):
                base, j = divmod(off + t, SUBLANES)
                rs = r0 + base * SUBLANES
                acc = acc + wdw_ref[t:t + 1, c0:c0 + cc] * ext_ref[j, rs:rs + rc, c0:c0 + cc]
            cv_ref[r0:r0 + rc, c0:c0 + cc] = acc
    hn = _layer_norm(cv_ref[...], lng_ref[...], lnb_ref[...])
    hs = (hn * _sigmoid(hn)).astype(BF16)
    conv = _dot(hs, w2_ref[...]) + b2_ref[...]
    x3 = _layer_norm(ALPHA * x_ref[...] + conv, pg_ref[...], pb_ref[...])
    x3_ref[...] = x3
    x_hi = x3.astype(BF16)
    x_lo = (x3 - x_hi.astype(F32)).astype(BF16)
    hi_part = _dot(x_hi, wr_ref[...])
    logits = hi_part[:, :128] + hi_part[:, 128:] + _dot(x_lo, wr_ref[:, 0:128])
    member, record = _top2(logits)
    route_ref[...] = record
    sel = jnp.where(member, 1.0, 0.0)
    pos_ref[...] = _dot(tri_ref[...], sel.astype(BF16)) + cnt_ref[...]
    cnt_ref[...] += jnp.sum(sel, axis=0, keepdims=True)
    tot_ref[...] = jnp.broadcast_to(cnt_ref[...], tot_ref.shape)


def _conv(hg, x2, w_dw, b_dw, ln_g, ln_b, w2, b2, pg, pb, w_router, *, seq, tm):
    n, d = x2.shape
    ratio = tm // CONV_HALO
    vec = _full((1, d))
    tri = (lax.broadcasted_iota(jnp.int32, (tm, tm), 1)
           < lax.broadcasted_iota(jnp.int32, (tm, tm), 0)).astype(BF16)
    return pl.pallas_call(
        functools.partial(_conv_kernel, tiles_per_seq=seq // tm),
        grid=(n // tm,),
        in_specs=[pl.BlockSpec((tm, d), lambda i: (i, 0)),
                  pl.BlockSpec((CONV_HALO, d), lambda i: (jnp.maximum(i * ratio - 1, 0), 0)),
                  pl.BlockSpec((tm, d), lambda i: (i, 0)),
                  _full((CONV_HALO, d)), vec, vec, vec, _full((d, d)), vec, vec, vec,
                  _full((d, 256)), _full((tm, tm))],
        out_specs=[pl.BlockSpec((tm, d), lambda i: (i, 0)),
                   pl.BlockSpec((tm, 128), lambda i: (i, 0)),
                   pl.BlockSpec((tm, 128), lambda i: (i, 0)),
                   _full((SUBLANES, 128))],
        out_shape=[jax.ShapeDtypeStruct((n, d), F32),
                   jax.ShapeDtypeStruct((n, 128), F32),
                   jax.ShapeDtypeStruct((n, 128), F32),
                   jax.ShapeDtypeStruct((SUBLANES, 128), F32)],
        scratch_shapes=[pltpu.VMEM((SUBLANES, tm + CONV_HALO, d), F32), pltpu.VMEM((tm, d), F32),
                        pltpu.VMEM((1, 128), F32)],
        compiler_params=_params("arbitrary"),
        name="conv_module",
    )(hg, hg, x2, w_dw, b_dw, ln_g, ln_b, w2, b2, pg, pb, w_router, tri)


MOE_TILE = 512
GATHER_ROWS = 2048


def _row_gather_kernel(idx_ref, table_ref, out_ref, sem):
    base = pl.program_id(0) * GATHER_ROWS

    def issue(r, carry):
        pltpu.make_async_copy(table_ref.at[pl.ds(idx_ref[base + r], 1), :],
                              out_ref.at[pl.ds(base + r, 1), :], sem).start()
        return carry
    lax.fori_loop(0, GATHER_ROWS, issue, 0, unroll=8)
    pltpu.make_async_copy(table_ref.at[pl.ds(0, GATHER_ROWS), :],
                          out_ref.at[pl.ds(base, GATHER_ROWS), :], sem).wait()


def _row_gather(table, idx):
    rows = idx.shape[0]
    return pl.pallas_call(
        _row_gather_kernel,
        grid_spec=pltpu.PrefetchScalarGridSpec(
            num_scalar_prefetch=1, grid=(rows // GATHER_ROWS,),
            in_specs=[pl.BlockSpec(memory_space=pl.ANY)],
            out_specs=pl.BlockSpec(memory_space=pl.ANY),
            scratch_shapes=[pltpu.SemaphoreType.DMA]),
        out_shape=jax.ShapeDtypeStruct((rows, table.shape[1]), table.dtype),
        compiler_params=pltpu.CompilerParams(dimension_semantics=("arbitrary",),
                                             has_side_effects=True,
                                             disable_bounds_checks=True),
        name="moe_row_gather",
    )(idx, table)


def _expert_ffn_kernel(expert_ref, used_ref, x_ref, wg_ref, wu_ref, wd_ref, o_ref):
    i = pl.program_id(0)
    f = pl.program_id(1)

    @pl.when(f == 0)
    def _():
        o_ref[...] = jnp.zeros_like(o_ref)

    @pl.when(used_ref[i] > 0)
    def _():
        xb = x_ref[...].astype(BF16)
        g = _dot(xb, wg_ref[...])
        u = _dot(xb, wu_ref[...])
        h = (g * _sigmoid(g) * u).astype(BF16)
        o_ref[...] += _dot(h, wd_ref[...])


def _expert_ffn(xs, tile_expert, tile_used, wg, wu, wd, *, tf):
    slots, d = xs.shape
    ff = wg.shape[2]
    return pl.pallas_call(
        _expert_ffn_kernel,
        grid_spec=pltpu.PrefetchScalarGridSpec(
            num_scalar_prefetch=2, grid=(slots // MOE_TILE, ff // tf),
            in_specs=[pl.BlockSpec((MOE_TILE, d), lambda i, f, te, tu: (i, 0)),
                      pl.BlockSpec((None, d, tf), lambda i, f, te, tu: (te[i], 0, f)),
                      pl.BlockSpec((None, d, tf), lambda i, f, te, tu: (te[i], 0, f)),
                      pl.BlockSpec((None, tf, d), lambda i, f, te, tu: (te[i], f, 0))],
            out_specs=pl.BlockSpec((MOE_TILE, d), lambda i, f, te, tu: (i, 0))),
        out_shape=jax.ShapeDtypeStruct((slots, d), F32),
        compiler_params=_params("arbitrary", "arbitrary"),
        name="moe_expert_ffn",
    )(tile_expert, tile_used, xs, wg, wu, wd)


def _combine_kernel(x_ref, y1_ref, y2_ref, route_ref, lng_ref, lnb_ref, o_ref):
    route = route_ref[...]
    moe = route[:, 2:3] * y1_ref[...] + route[:, 3:4] * y2_ref[...]
    o_ref[...] = _layer_norm(ALPHA * x_ref[...] + moe, lng_ref[...], lnb_ref[...])


def _combine(x3, yt, route, ln_g, ln_b, *, tm):
    n, d = x3.shape
    nt = n // tm
    return pl.pallas_call(
        _combine_kernel,
        grid=(nt,),
        in_specs=[pl.BlockSpec((tm, d), lambda i: (i, 0)),
                  pl.BlockSpec((tm, d), lambda i: (i, 0)),
                  pl.BlockSpec((tm, d), lambda i: (i + nt, 0)),
                  pl.BlockSpec((tm, 128), lambda i: (i, 0)),
                  _full((1, d)), _full((1, d))],
        out_specs=pl.BlockSpec((tm, d), lambda i: (i, 0)),
        out_shape=jax.ShapeDtypeStruct((n, d), F32),
        compiler_params=_params("parallel"),
        name="moe_combine",
    )(x3, yt, yt, route, ln_g, ln_b)


def _moe(x3, route, pos, tot, wg, wu, wd, ln_g, ln_b, *, tm, tf):
    n, d = x3.shape
    ne = wg.shape[0]
    slots = TOP_K * n + ne * MOE_TILE
    count = tot[0, :ne].astype(jnp.int32)
    cap = (count + (MOE_TILE - 1)) // MOE_TILE * MOE_TILE
    ends = jnp.cumsum(cap)
    off = ends - cap
    e1 = route[:, 0].astype(jnp.int32)
    e2 = route[:, 1].astype(jnp.int32)
    rank = pos[:, :ne].astype(jnp.int32)
    slot1 = off[e1] + jnp.take_along_axis(rank, e1[:, None], axis=1)[:, 0]
    slot2 = off[e2] + jnp.take_along_axis(rank, e2[:, None], axis=1)[:, 0]
    token = jnp.arange(n, dtype=jnp.int32)
    token_of_slot = jnp.zeros((slots,), jnp.int32).at[slot1].set(token).at[slot2].set(token)
    tile_start = jnp.arange(slots // MOE_TILE, dtype=jnp.int32) * MOE_TILE
    tile_expert = jnp.minimum(jnp.searchsorted(ends, tile_start, side="right"),
                              ne - 1).astype(jnp.int32)
    tile_used = (tile_start < ends[-1]).astype(jnp.int32)

    xs = _row_gather(x3, token_of_slot)
    ys = _expert_ffn(xs, tile_expert, tile_used, wg, wu, wd, tf=tf)
    yt = _row_gather(ys, jnp.concatenate([slot1, slot2]))
    return _combine(x3, yt, route, ln_g, ln_b, tm=tm)


def _pad_cols(w, width):
    return jnp.pad(w, ((0, 0), (0, width - w.shape[1])))


def _pad_rows(w, height):
    return jnp.pad(w, ((0, height - w.shape[0]), (0, 0)))


def _forward(x, mix_w_in, rwkv_mu, rwkv_w0, rwkv_w_up, rwkv_a0, rwkv_a_up, rwkv_g_up,
             rwkv_k_k, rwkv_k_a, rwkv_r_k, rwkv_gn_g, rwkv_gn_b, fox_b_f, mix_w_out,
             mix_ln_g, mix_ln_b, ffn_w_gate, ffn_w_up, ffn_w_down, ffn_ln_g, ffn_ln_b,
             conv_w_pw1, conv_b_pw1, conv_w_dw, conv_b_dw, conv_ln_g, conv_ln_b,
             conv_w_pw2, conv_b_pw2, conv_post_ln_g, conv_post_ln_b,
             moe_w_router, moe_w_gate, moe_w_up, moe_w_down, moe_ln_g, moe_ln_b,
             *, tm=512, chunk=64, nb_rwkv=4, tq=512, tf_ffn=1408, tf_moe=1792):
    batch, seq, d = x.shape
    n = batch * seq
    gw = GROUP_W
    x2 = x.reshape(n, d)
    row = lambda t: t.reshape(1, -1)

    w_in = mix_w_in[0]
    mu = rwkv_mu[0]
    o_w, o_a, o_g = 3 * gw, 3 * gw + DECAY_LORA, 3 * gw + DECAY_LORA + AAA_LORA
    o_fox = o_g + GATE_LORA

    def lora_layout(t):
        return jnp.concatenate([t[..., :o_w],
                                _pad_cols(t[..., o_w:o_a], LORA_PAD),
                                _pad_cols(t[..., o_a:o_g], LORA_PAD),
                                _pad_cols(t[..., o_g:o_fox], LORA_PAD)], axis=-1)

    wa = lora_layout(w_in).astype(BF16)
    mu_a = lora_layout(row(mu))
    scale = LOG2E / math.sqrt(HEAD_DIM)
    wb = jnp.concatenate([w_in[:, o_fox:o_fox + gw] * scale,
                          w_in[:, o_fox + gw:o_fox + 3 * gw]], axis=1).astype(BF16)
    wf = _pad_cols(w_in[:, o_fox + 3 * gw:], 128).astype(BF16)
    bf = _pad_cols(row(fox_b_f[0]), 128)

    pr, qk, vt, c = _inproj(x2, wa, wb, wf, mu_a, bf, seq=seq, tm=tq)

    wup = _pad_rows(rwkv_w_up[0], LORA_PAD).astype(BF16)
    aup = _pad_rows(rwkv_a_up[0], LORA_PAD).astype(BF16)
    gup = _pad_rows(rwkv_g_up[0], LORA_PAD).astype(BF16)
    k_k, k_a, r_k = row(rwkv_k_k[0]), row(rwkv_k_a[0]), row(rwkv_r_k[0])
    yr = _rwkv(pr, row(rwkv_w0[0]), wup, row(rwkv_a0[0]), aup, k_k, k_a,
               batch=batch, seq=seq, chunk=chunk, nb=nb_rwkv)

    yf = _fox(qk, vt, c, batch=batch, seq=seq, tq=tq)

    w_out = mix_w_out[0].astype(BF16)
    x1 = _mixout(x2, pr, yr, yf, row(rwkv_a0[0]), aup, gup, k_a, r_k,
                 row(rwkv_gn_g[0]), row(rwkv_gn_b[0]), w_out[:gw], w_out[gw:],
                 row(mix_ln_g[0]), row(mix_ln_b[0]), tm=tm)
    x2b = _ffn(x1, ffn_w_gate[0].astype(BF16), ffn_w_up[0].astype(BF16),
               ffn_w_down[0].astype(BF16), row(ffn_ln_g[0]), row(ffn_ln_b[0]), tm=tm, tf=tf_ffn)

    hg = _glu(x2b, conv_w_pw1[0].astype(BF16), row(conv_b_pw1[0]), tm=tm)
    w_router = _pad_cols(moe_w_router[0], 128)
    wr_hi = w_router.astype(BF16)
    w_router = jnp.concatenate([wr_hi, (w_router - wr_hi.astype(F32)).astype(BF16)], axis=1)
    x3, route, pos, tot = _conv(hg, x2b, _pad_rows(conv_w_dw[0], CONV_HALO), row(conv_b_dw[0]),
                                row(conv_ln_g[0]), row(conv_ln_b[0]),
                                conv_w_pw2[0].astype(BF16), row(conv_b_pw2[0]),
                                row(conv_post_ln_g[0]), row(conv_post_ln_b[0]),
                                w_router, seq=seq, tm=tm)
    out = _moe(x3, route, pos, tot, moe_w_gate[0].astype(BF16), moe_w_up[0].astype(BF16),
               moe_w_down[0].astype(BF16), row(moe_ln_g[0]), row(moe_ln_b[0]),
               tm=tm, tf=tf_moe)
    return out.reshape(batch, seq, d)


def kernel(x, mix_w_in, rwkv_mu, rwkv_w0, rwkv_w_up, rwkv_a0, rwkv_a_up, rwkv_g_up, rwkv_k_k, rwkv_k_a, rwkv_r_k, rwkv_gn_g, rwkv_gn_b, fox_b_f, mix_w_out, mix_ln_g, mix_ln_b, ffn_w_gate, ffn_w_up, ffn_w_down, ffn_ln_g, ffn_ln_b, conv_w_pw1, conv_b_pw1, conv_w_dw, conv_b_dw, conv_ln_g, conv_ln_b, conv_w_pw2, conv_b_pw2, conv_post_ln_g, conv_post_ln_b, moe_w_router, moe_w_gate, moe_w_up, moe_w_down, moe_ln_g, moe_ln_b):
    return _forward(x, mix_w_in, rwkv_mu, rwkv_w0, rwkv_w_up, rwkv_a0, rwkv_a_up, rwkv_g_up,
                    rwkv_k_k, rwkv_k_a, rwkv_r_k, rwkv_gn_g, rwkv_gn_b, fox_b_f, mix_w_out,
                    mix_ln_g, mix_ln_b, ffn_w_gate, ffn_w_up, ffn_w_down, ffn_ln_g, ffn_ln_b,
                    conv_w_pw1, conv_b_pw1, conv_w_dw, conv_b_dw, conv_ln_g, conv_ln_b,
                    conv_w_pw2, conv_b_pw2, conv_post_ln_g, conv_post_ln_b,
                    moe_w_router, moe_w_gate, moe_w_up, moe_w_down, moe_ln_g, moe_ln_b)
```

```python
import functools
import math

import jax
import jax.numpy as jnp
from jax import lax
from jax.experimental import pallas as pl
from jax.experimental.pallas import tpu as pltpu
from jax.experimental.pallas import tpu_sc as plsc

F32 = jnp.float32
BF16 = jnp.bfloat16
HIGHEST = lax.Precision.HIGHEST

HEAD_DIM = 64
N_HEADS = 8
GROUP_W = N_HEADS * HEAD_DIM
LORA_PAD = 128
DECAY_LORA = 32
AAA_LORA = 32
GATE_LORA = 96
CONV_WIDTH = 31
CONV_HALO = 32
SUBLANES = 8
N_EXPERTS = 8
TOP_K = 2
LN_EPS = 1e-5
GN_EPS = 64e-5
DEPTH = 2
ALPHA = (2.0 * DEPTH) ** 0.25
NEG_BIG = -1e30
LOG2E = math.log2(math.e)
VMEM_LIMIT = 56 * 1024 * 1024


def _dot(a, b, **kw):
    return jnp.dot(a, b, preferred_element_type=F32, **kw)


def _dot_nt(a, b):
    return lax.dot_general(a, b, (((1,), (1,)), ((), ())), preferred_element_type=F32)


def _dot_tn(a, b):
    return lax.dot_general(a, b, (((0,), (0,)), ((), ())), preferred_element_type=F32)


def _dot_exact_lhs(a, v):
    hi = v.astype(BF16)
    rem = v - hi.astype(F32)
    mid = rem.astype(BF16)
    lo = (rem - mid.astype(F32)).astype(BF16)
    w = v.shape[1]
    out = _dot(a, jnp.concatenate([hi, mid, lo], axis=1))
    return out[:, :w] + out[:, w:2 * w] + out[:, 2 * w:]


def _sigmoid(z):
    return 1.0 / (1.0 + jnp.exp(-z))


def _softplus(z):
    return jnp.maximum(z, 0.0) + jnp.log1p(jnp.exp(-jnp.abs(z)))


def _layer_norm(h, g, b):
    mu = jnp.mean(h, axis=-1, keepdims=True)
    d = h - mu
    var = jnp.mean(d * d, axis=-1, keepdims=True)
    return d * lax.rsqrt(var + LN_EPS) * g + b


def _params(*sem):
    return pltpu.CompilerParams(dimension_semantics=sem, vmem_limit_bytes=VMEM_LIMIT)


def _full(shape):
    return pl.BlockSpec(shape, lambda *_: (0,) * len(shape))


def _inproj_kernel(x_ref, wa_ref, wb_ref, wf_ref, mu_ref, bf_ref, tri_ref,
                   pr_ref, qk_ref, vt_ref, c_ref, last_ref, carry_ref, *, tiles_per_seq):
    i = pl.program_id(0)

    @pl.when(i % tiles_per_seq == 0)
    def _():
        last_ref[...] = jnp.zeros_like(last_ref)
        carry_ref[...] = jnp.zeros_like(carry_ref)

    xb = x_ref[...].astype(BF16)
    tm = xb.shape[0]
    row0 = lax.broadcasted_iota(jnp.int32, (tm, 1), 0) == 0
    ca = wa_ref.shape[1]
    for c0 in range(0, ca, GROUP_W):
        cw = min(GROUP_W, ca - c0)
        p = _dot(xb, wa_ref[:, c0:c0 + cw])
        prev = jnp.where(row0, last_ref[:, c0:c0 + cw], pltpu.roll(p, 1, 0))
        last_ref[:, c0:c0 + cw] = p[tm - 1:tm, :]
        pr_ref[:, c0:c0 + cw] = p + mu_ref[:, c0:c0 + cw] * (prev - p)
    for c0 in range(0, 2 * GROUP_W, GROUP_W):
        qk_ref[:, c0:c0 + GROUP_W] = _dot(xb, wb_ref[:, c0:c0 + GROUP_W]).astype(BF16)
    vt_ref[...] = _dot(xb, wb_ref[:, 2 * GROUP_W:3 * GROUP_W]).T.astype(BF16)
    fl = _dot(xb, wf_ref[...]) + bf_ref[...]
    log_f = jnp.minimum(fl, 0.0) - jnp.log1p(jnp.exp(-jnp.abs(fl)))
    c = _dot_exact_lhs(tri_ref[...], log_f) + carry_ref[...]
    c_ref[...] = c
    carry_ref[...] = c[tm - 1:tm, :]


def _inproj(x2, wa, wb, wf, mu, bf, *, seq, tm):
    n, d = x2.shape
    ca, cb = wa.shape[1], wb.shape[1]
    tps = seq // tm
    tri = (lax.broadcasted_iota(jnp.int32, (tm, tm), 1)
           <= lax.broadcasted_iota(jnp.int32, (tm, tm), 0)).astype(BF16)
    return pl.pallas_call(
        functools.partial(_inproj_kernel, tiles_per_seq=tps),
        grid=(n // tm,),
        in_specs=[pl.BlockSpec((tm, d), lambda i: (i, 0)),
                  _full((d, ca)), _full((d, cb)), _full((d, 128)),
                  _full((1, ca)), _full((1, 128)), _full((tm, tm))],
        out_specs=[pl.BlockSpec((tm, ca), lambda i: (i, 0)),
                   pl.BlockSpec((tm, 2 * GROUP_W), lambda i: (i, 0)),
                   pl.BlockSpec((None, None, GROUP_W, tm), lambda i: (i // tps, i % tps, 0, 0)),
                   pl.BlockSpec((tm, 128), lambda i: (i, 0))],
        out_shape=[jax.ShapeDtypeStruct((n, ca), F32),
                   jax.ShapeDtypeStruct((n, 2 * GROUP_W), BF16),
                   jax.ShapeDtypeStruct((n // seq, tps, GROUP_W, tm), BF16),
                   jax.ShapeDtypeStruct((n, 128), F32)],
        scratch_shapes=[pltpu.VMEM((1, ca), F32), pltpu.VMEM((1, 128), F32)],
        compiler_params=_params("arbitrary"),
        name="inproj",
    )(x2, wa, wb, wf, mu, bf, tri)


def _rwkv_kernel(pr_ref, w0_ref, wup_ref, a0_ref, aup_ref, kk_ref, ka_ref,
                 gsum_ref, tri_ref, y_ref, h_ref, *, chunk, nb):
    @pl.when(pl.program_id(1) == 0)
    def _():
        h_ref[...] = jnp.zeros_like(h_ref)

    gw = GROUP_W
    pw = 2 * HEAD_DIM
    npair = N_HEADS // 2
    rows = 2 * chunk
    log_chunk = int(math.log2(chunk))
    head0 = lax.broadcasted_iota(jnp.int32, (1, pw), 1) < HEAD_DIM
    row = lax.broadcasted_iota(jnp.int32, (rows, rows), 0)
    col = lax.broadcasted_iota(jnp.int32, (rows, rows), 1)
    strict = (col & (chunk - 1)) < (row & (chunk - 1))
    incl = (col & (chunk - 1)) <= (row & (chunk - 1))
    eye = (col == row).astype(F32)
    peye = (lax.broadcasted_iota(jnp.int32, (pw, pw), 0)
            == lax.broadcasted_iota(jnp.int32, (pw, pw), 1))

    def stack(x):
        return jnp.concatenate([jnp.where(head0, x, 0.0), jnp.where(head0, 0.0, x)],
                               axis=0).astype(BF16)

    units = [(b, j) for b in range(nb) for j in range(npair)]
    nu = len(units)
    ar, bk, vs, bhs, khs, pcs = [], [], [], [], [], []
    for b in range(nb):
        r = pr_ref[b, :, 0:gw]
        k = pr_ref[b, :, gw:2 * gw]
        v = pr_ref[b, :, 2 * gw:3 * gw]
        wd = pr_ref[b, :, 3 * gw:3 * gw + LORA_PAD]
        ad = pr_ref[b, :, 3 * gw + LORA_PAD:3 * gw + 2 * LORA_PAD]
        w_pre = w0_ref[...] + _dot(jnp.tanh(wd).astype(BF16), wup_ref[...])
        w = -_softplus(-w_pre) - 0.5
        log_decay = -jnp.exp(w)
        a = _sigmoid(a0_ref[...] + _dot(ad.astype(BF16), aup_ref[...]))
        kk = k * kk_ref[...]
        norm = jnp.sqrt(_dot((kk * kk).astype(BF16), gsum_ref[...]))
        kk = kk / jnp.maximum(norm, 1e-12)
        k_mod = k * (1.0 + (a - 1.0) * ka_ref[...])
        b_vec = kk * a
        cum = _dot_exact_lhs(tri_ref[...], log_decay)
        last = cum[chunk - 1:chunk, :]
        p_inv = jnp.exp(-cum)
        p_tail = jnp.exp(last - cum)
        a_t = -kk * jnp.exp(cum - log_decay)
        r_t = r * jnp.exp(cum)
        b_t = b_vec * p_inv
        k_t = k_mod * p_inv
        b_h = b_vec * p_tail
        k_h = k_mod * p_tail
        p_last = jnp.exp(last)
        for j in range(npair):
            sl = slice(j * pw, (j + 1) * pw)
            ar.append(jnp.concatenate([stack(a_t[:, sl]), stack(r_t[:, sl])], axis=0))
            bk.append(jnp.concatenate([stack(b_t[:, sl]), stack(k_t[:, sl])], axis=0))
            vs.append(stack(v[:, sl]))
            bhs.append(stack(b_h[:, sl]))
            khs.append(stack(k_h[:, sl]))
            pcs.append(jnp.sum(jnp.where(peye, p_last[:, sl], 0.0), axis=1, keepdims=True))

    gram = [_dot_nt(ar[u], bk[u]) for u in range(nu)]
    l_ab = [jnp.where(strict, gram[u][:rows, :rows], 0.0) for u in range(nu)]
    l_akv = [_dot(jnp.where(strict, gram[u][:rows, rows:], 0.0).astype(BF16), vs[u])
             for u in range(nu)]
    m_rb = [jnp.where(incl, gram[u][rows:, :rows], 0.0).astype(BF16) for u in range(nu)]
    m_rkv = [_dot(jnp.where(incl, gram[u][rows:, rows:], 0.0).astype(BF16), vs[u])
             for u in range(nu)]
    t_inv = [eye + l_ab[u] for u in range(nu)]
    xb = [l_ab[u].astype(BF16) for u in range(nu)]
    xp = [_dot(xb[u], xb[u]) for u in range(nu)]
    for step in range(log_chunk - 1):
        xb = [xp[u].astype(BF16) for u in range(nu)]
        if step < log_chunk - 2:
            both = [_dot(jnp.concatenate([t_inv[u].astype(BF16), xb[u]], axis=0), xb[u])
                    for u in range(nu)]
            t_inv = [t_inv[u] + both[u][:rows] for u in range(nu)]
            xp = [both[u][rows:] for u in range(nu)]
        else:
            t_inv = [t_inv[u] + _dot(t_inv[u].astype(BF16), xb[u]) for u in range(nu)]
    tw = [_dot(t_inv[u].astype(BF16),
               jnp.concatenate([ar[u][:rows], l_akv[u].astype(BF16)], axis=1)).astype(BF16)
          for u in range(nu)]
    mw = [_dot(m_rb[u], tw[u]) for u in range(nu)]
    bw = [_dot_tn(bhs[u], tw[u]) for u in range(nu)]
    kv = [_dot_tn(khs[u], vs[u]) for u in range(nu)]
    for u, (b, j) in enumerate(units):
        wy = ar[u][rows:].astype(F32) + mw[u][:, :pw]
        yc = mw[u][:, pw:] + m_rkv[u]
        hf = h_ref[u]
        yh = _dot(jnp.concatenate([wy.astype(BF16), bw[u][:, :pw].astype(BF16)], axis=0),
                  hf.astype(BF16))
        h_ref[u] = pcs[u] * hf + yh[rows:] + bw[u][:, pw:] + kv[u]
        ys = yh[:rows] + yc
        y_ref[b, :, j * pw:(j + 1) * pw] = ys[:chunk] + ys[chunk:]


def _rwkv(pr, w0, wup, a0, aup, k_k, k_a, *, batch, seq, chunk, nb):
    n, ca = pr.shape
    nch = seq // chunk
    gidx = lax.broadcasted_iota(jnp.int32, (GROUP_W, GROUP_W), 0) // HEAD_DIM
    gsum = (gidx == gidx.T).astype(BF16)
    tri = (lax.broadcasted_iota(jnp.int32, (chunk, chunk), 1)
           <= lax.broadcasted_iota(jnp.int32, (chunk, chunk), 0)).astype(BF16)
    y = pl.pallas_call(
        functools.partial(_rwkv_kernel, chunk=chunk, nb=nb),
        grid=(batch // nb, nch),
        in_specs=[pl.BlockSpec((nb, chunk, ca), lambda g, c: (g, c, 0)),
                  _full((1, GROUP_W)), _full((LORA_PAD, GROUP_W)),
                  _full((1, GROUP_W)), _full((LORA_PAD, GROUP_W)),
                  _full((1, GROUP_W)), _full((1, GROUP_W)),
                  _full((GROUP_W, GROUP_W)), _full((chunk, chunk))],
        out_specs=pl.BlockSpec((nb, chunk, GROUP_W), lambda g, c: (g, c, 0)),
        out_shape=jax.ShapeDtypeStruct((batch, seq, GROUP_W), F32),
        scratch_shapes=[pltpu.VMEM((nb * (N_HEADS // 2), 2 * HEAD_DIM, 2 * HEAD_DIM), F32)],
        compiler_params=_params("arbitrary", "arbitrary"),
        name="rwkv_scan",
    )(pr.reshape(batch, seq, ca), w0, wup, a0, aup, k_k, k_a, gsum, tri)
    return y.reshape(n, GROUP_W)


def _fox_kernel(q_ref, k_ref, vt_ref, c_ref, o_ref, acc_ref, m_ref, l_ref, kb_ref,
                sa_ref, sb_ref, *, tq):
    j = pl.program_id(1)
    qi = pl.program_id(2)
    pw = 2 * HEAD_DIM
    seq = k_ref.shape[0]
    lane = lax.broadcasted_iota(jnp.int32, (1, pw), 1)

    first = lane < HEAD_DIM
    bias_lane = (HEAD_DIM, 0)

    @pl.when(qi == 0)
    def _():
        def fill(rb, carry):
            rs = pl.multiple_of(rb * tq, tq)
            cblk = c_ref[pl.ds(rs, tq), :]
            for hh in range(2):
                bias = -LOG2E * jnp.sum(jnp.where(lane == 2 * j + hh, cblk, 0.0),
                                        axis=1, keepdims=True)
                b_hi = bias.astype(BF16).astype(F32)
                b_mid = (bias - b_hi).astype(BF16).astype(F32)
                b_lo = bias - b_hi - b_mid
                l0 = bias_lane[hh]
                kb_ref[hh, pl.ds(rs, tq), :] = jnp.where(
                    lane == l0, b_hi, jnp.where(lane == l0 + 1, b_mid,
                                                jnp.where(lane == l0 + 2, b_lo, 0.0))).astype(BF16)
            return carry
        lax.fori_loop(0, seq // tq, fill, 0)

    q = q_ref[...]
    ones3 = [jnp.where(jnp.logical_and(lane >= l0, lane < l0 + 3), 1.0, 0.0).astype(BF16)
             for l0 in bias_lane]
    own = (first, jnp.logical_not(first))
    qh = tuple(jnp.where(own[hh], q, ones3[hh]) for hh in range(2))
    acc_ref[...] = jnp.zeros_like(acc_ref)
    m_ref[...] = jnp.full_like(m_ref, NEG_BIG)
    l_ref[...] = jnp.zeros_like(l_ref)
    causal = (lax.broadcasted_iota(jnp.int32, (tq, tq), 0)
              <= lax.broadcasted_iota(jnp.int32, (tq, tq), 1))
    top = lax.broadcasted_iota(jnp.int32, (pw, 1), 0) < HEAD_DIM

    def scores(kb, s_ref):
        ks = pl.multiple_of(kb * tq, tq)
        kblk = k_ref[pl.ds(ks, tq), :]
        for hh in range(2):
            k_aug = jnp.where(own[hh], kblk, kb_ref[hh, pl.ds(ks, tq), :])
            s_ref[hh] = _dot_nt(k_aug, qh[hh])

    def softmax_pv(kb, s_ref, masked):
        vt = vt_ref[kb].astype(F32)
        vts = (jnp.where(top, vt, 1.0).astype(BF16), jnp.where(top, 1.0, vt).astype(BF16))
        alphas, pvs = [], []
        for hh in range(2):
            z = s_ref[hh]
            if masked:
                z = jnp.where(causal, z, NEG_BIG)
            m_prev = m_ref[hh]
            m_new = jnp.maximum(m_prev, jnp.max(z, axis=0, keepdims=True))
            alpha = jnp.exp2(m_prev - m_new)
            p = jnp.exp2(z - m_new)
            pv = _dot(vts[hh], p.astype(BF16))
            ones_row = (1 - hh) * HEAD_DIM
            l_ref[hh] = alpha * l_ref[hh] + pv[ones_row:ones_row + 1, :]
            m_ref[hh] = m_new
            alphas.append(alpha)
            pvs.append(pv)
        acc_ref[...] = (acc_ref[...] * jnp.where(top, alphas[0], alphas[1])
                        + jnp.where(top, pvs[0], pvs[1]))

    scores(0, sa_ref)

    def body(i, carry):
        scores(2 * i + 1, sb_ref)
        softmax_pv(2 * i, sa_ref, False)
        scores(2 * i + 2, sa_ref)
        softmax_pv(2 * i + 1, sb_ref, False)
        return carry

    lax.fori_loop(0, qi // 2, body, 0)

    @pl.when(qi % 2 == 0)
    def _():
        softmax_pv(qi, sa_ref, True)

    @pl.when(qi % 2 == 1)
    def _():
        scores(qi, sb_ref)
        softmax_pv(qi - 1, sa_ref, False)
        softmax_pv(qi, sb_ref, True)

    out_t = acc_ref[...] / jnp.where(top, l_ref[0], l_ref[1])
    o_ref[...] = out_t.T.astype(BF16)


def _fox(qk, vt, c, *, batch, seq, tq):
    n = qk.shape[0]
    nq = seq // tq
    npair = N_HEADS // 2
    pw = 2 * HEAD_DIM
    assert vt.shape == (batch, nq, GROUP_W, tq)
    return pl.pallas_call(
        functools.partial(_fox_kernel, tq=tq),
        grid=(batch, npair, nq),
        in_specs=[pl.BlockSpec((tq, pw), lambda b, j, i: (b * nq + i, j)),
                  pl.BlockSpec((seq, pw), lambda b, j, i: (b, npair + j)),
                  pl.BlockSpec((None, nq, pw, tq), lambda b, j, i: (b, 0, j, 0)),
                  pl.BlockSpec((seq, 128), lambda b, j, i: (b, 0))],
        out_specs=pl.BlockSpec((tq, pw), lambda b, j, i: (b * nq + i, j)),
        out_shape=jax.ShapeDtypeStruct((n, GROUP_W), BF16),
        scratch_shapes=[pltpu.VMEM((pw, tq), F32),
                        pltpu.VMEM((2, 1, tq), F32), pltpu.VMEM((2, 1, tq), F32),
                        pltpu.VMEM((2, seq, pw), BF16),
                        pltpu.VMEM((2, tq, tq), F32), pltpu.VMEM((2, tq, tq), F32)],
        compiler_params=_params("arbitrary", "arbitrary", "arbitrary"),
        name="fox_attention",
    )(qk, qk, vt, c)


def _mixout_kernel(x_ref, pr_ref, yr_ref, yf_ref, a0_ref, aup_ref, gup_ref, ka_ref,
                   rk_ref, gng_ref, gnb_ref, gsum_ref, wr_ref, wf_ref, lng_ref, lnb_ref,
                   o_ref):
    gw = GROUP_W
    r = pr_ref[:, 0:gw]
    k = pr_ref[:, gw:2 * gw]
    v = pr_ref[:, 2 * gw:3 * gw]
    ad = pr_ref[:, 3 * gw + LORA_PAD:3 * gw + 2 * LORA_PAD]
    gd = pr_ref[:, 3 * gw + 2 * LORA_PAD:3 * gw + 3 * LORA_PAD]
    a = _sigmoid(a0_ref[...] + _dot(ad.astype(BF16), aup_ref[...]))
    k_mod = k * (1.0 + (a - 1.0) * ka_ref[...])
    gate = _dot(_sigmoid(gd).astype(BF16), gup_ref[...])
    gsum = gsum_ref[...]

    def group_sum(t):
        return _dot(t.astype(BF16), gsum)

    y = yr_ref[...]
    y_hi = y.astype(BF16)
    y_lo = (y - y_hi.astype(F32)).astype(BF16)
    mean = (_dot(y_hi, gsum) + _dot(y_lo, gsum)) * (1.0 / HEAD_DIM)
    d = y - mean
    var = group_sum(d * d) * (1.0 / HEAD_DIM)
    yn = d * lax.rsqrt(var + GN_EPS) * gng_ref[...] + gnb_ref[...]
    bonus = group_sum(r * k_mod * rk_ref[...])
    y_rwkv = ((yn + bonus * v) * gate).astype(BF16)
    mixed = _dot(y_rwkv, wr_ref[...]) + _dot(yf_ref[...], wf_ref[...])
    o_ref[...] = _layer_norm(ALPHA * x_ref[...] + mixed, lng_ref[...], lnb_ref[...])


def _mixout(x2, pr, yr, yf, a0, aup, gup, k_a, r_k, gn_g, gn_b, w_r, w_f, ln_g, ln_b, *, tm):
    n, d = x2.shape
    ca = pr.shape[1]
    gidx = lax.broadcasted_iota(jnp.int32, (GROUP_W, GROUP_W), 0) // HEAD_DIM
    gsum = (gidx == gidx.T).astype(BF16)
    vec = _full((1, GROUP_W))
    return pl.pallas_call(
        _mixout_kernel,
        grid=(n // tm,),
        in_specs=[pl.BlockSpec((tm, d), lambda i: (i, 0)),
                  pl.BlockSpec((tm, ca), lambda i: (i, 0)),
                  pl.BlockSpec((tm, GROUP_W), lambda i: (i, 0)),
                  pl.BlockSpec((tm, GROUP_W), lambda i: (i, 0)),
                  vec, _full((LORA_PAD, GROUP_W)), _full((LORA_PAD, GROUP_W)),
                  vec, vec, vec, vec, _full((GROUP_W, GROUP_W)),
                  _full((GROUP_W, d)), _full((GROUP_W, d)),
                  _full((1, d)), _full((1, d))],
        out_specs=pl.BlockSpec((tm, d), lambda i: (i, 0)),
        out_shape=jax.ShapeDtypeStruct((n, d), F32),
        compiler_params=_params("parallel"),
        name="mix_out",
    )(x2, pr, yr, yf, a0, aup, gup, k_a, r_k, gn_g, gn_b, gsum, w_r, w_f, ln_g, ln_b)


def _ffn_kernel(x_ref, wg_ref, wu_ref, wd_ref, lng_ref, lnb_ref, o_ref, acc_ref):
    f = pl.program_id(1)

    @pl.when(f == 0)
    def _():
        acc_ref[...] = jnp.zeros_like(acc_ref)

    xb = x_ref[...].astype(BF16)
    g = _dot(xb, wg_ref[...])
    u = _dot(xb, wu_ref[...])
    h = (g * _sigmoid(g) * u).astype(BF16)
    acc_ref[...] += _dot(h, wd_ref[...])

    @pl.when(f == pl.num_programs(1) - 1)
    def _():
        o_ref[...] = _layer_norm(ALPHA * x_ref[...] + acc_ref[...], lng_ref[...], lnb_ref[...])


def _ffn(x2, wg, wu, wd, ln_g, ln_b, *, tm, tf):
    n, d = x2.shape
    ff = wg.shape[1]
    return pl.pallas_call(
        _ffn_kernel,
        grid=(n // tm, ff // tf),
        in_specs=[pl.BlockSpec((tm, d), lambda i, f: (i, 0)),
                  pl.BlockSpec((d, tf), lambda i, f: (0, f)),
                  pl.BlockSpec((d, tf), lambda i, f: (0, f)),
                  pl.BlockSpec((tf, d), lambda i, f: (f, 0)),
                  _full((1, d)), _full((1, d))],
        out_specs=pl.BlockSpec((tm, d), lambda i, f: (i, 0)),
        out_shape=jax.ShapeDtypeStruct((n, d), F32),
        scratch_shapes=[pltpu.VMEM((tm, d), F32)],
        compiler_params=_params("parallel", "arbitrary"),
        name="ffn_swiglu",
    )(x2, wg, wu, wd, ln_g, ln_b)


def _glu_kernel(x_ref, w_ref, b_ref, o_ref):
    d = o_ref.shape[1]
    xb = x_ref[...].astype(BF16)
    val = _dot(xb, w_ref[:, 0:d]) + b_ref[:, 0:d]
    gat = _dot(xb, w_ref[:, d:2 * d]) + b_ref[:, d:2 * d]
    o_ref[...] = val * _sigmoid(gat)


def _glu(x2, w, b, *, tm):
    n, d = x2.shape
    return pl.pallas_call(
        _glu_kernel,
        grid=(n // tm,),
        in_specs=[pl.BlockSpec((tm, d), lambda i: (i, 0)), _full((d, 2 * d)), _full((1, 2 * d))],
        out_specs=pl.BlockSpec((tm, d), lambda i: (i, 0)),
        out_shape=jax.ShapeDtypeStruct((n, d), F32),
        compiler_params=_params("parallel"),
        name="conv_glu",
    )(x2, w, b)


def _top2(logits):
    lane = lax.broadcasted_iota(jnp.int32, logits.shape, 1).astype(F32)
    lg = jnp.where(lane < N_EXPERTS, logits, NEG_BIG)
    m1 = jnp.max(lg, axis=-1, keepdims=True)
    i1 = jnp.min(jnp.where(lg == m1, lane, 128.0), axis=-1, keepdims=True)
    lg2 = jnp.where(lane == i1, NEG_BIG, lg)
    m2 = jnp.max(lg2, axis=-1, keepdims=True)
    i2 = jnp.min(jnp.where(lg2 == m2, lane, 128.0), axis=-1, keepdims=True)
    e2 = jnp.exp(m2 - m1)
    w1 = 1.0 / (1.0 + e2)
    w2 = e2 / (1.0 + e2)
    member = jnp.logical_or(lane == i1, lane == i2)
    record = jnp.where(lane == 0.0, i1, jnp.where(lane == 1.0, i2,
                                                  jnp.where(lane == 2.0, w1,
                                                            jnp.where(lane == 3.0, w2, 0.0))))
    return member, record


def _conv_kernel(hc_ref, hp_ref, x_ref, wdw_ref, bdw_ref, lng_ref, lnb_ref, w2_ref, b2_ref,
                 pg_ref, pb_ref, wr_ref, tri_ref, x3_ref, route_ref, pos_ref, tot_ref,
                 ext_ref, cv_ref, cnt_ref, *, tiles_per_seq):
    tm, d = x_ref.shape

    @pl.when(pl.program_id(0) == 0)
    def _():
        cnt_ref[...] = jnp.zeros_like(cnt_ref)

    first = pl.program_id(0) % tiles_per_seq == 0
    ext_ref[0, 0:CONV_HALO, :] = jnp.where(first, 0.0, hp_ref[...])
    ext_ref[0, CONV_HALO:CONV_HALO + tm, :] = hc_ref[...]
    nrows = tm + CONV_HALO
    for c0 in range(0, d, 256):
        base = ext_ref[0, :, c0:c0 + 256]
        for j in range(1, SUBLANES):
            ext_ref[j, :, c0:c0 + 256] = pltpu.roll(base, nrows - j, 0)
    off = CONV_HALO - (CONV_WIDTH - 1)
    rc, cc = 64, 256
    for r0 in range(0, tm, rc):
        for c0 in range(0, d, cc):
            acc = jnp.broadcast_to(bdw_ref[:, c0:c0 + cc], (rc, cc))
            for t in range(CONV_WIDTH):
                base, j = divmod(off + t, SUBLANES)
                rs = r0 + base * SUBLANES
                acc = acc + wdw_ref[t:t + 1, c0:c0 + cc] * ext_ref[j, rs:rs + rc, c0:c0 + cc]
            cv_ref[r0:r0 + rc, c0:c0 + cc] = acc
    hn = _layer_norm(cv_ref[...], lng_ref[...], lnb_ref[...])
    hs = (hn * _sigmoid(hn)).astype(BF16)
    conv = _dot(hs, w2_ref[...]) + b2_ref[...]
    x3 = _layer_norm(ALPHA * x_ref[...] + conv, pg_ref[...], pb_ref[...])
    x3_ref[...] = x3
    x_hi = x3.astype(BF16)
    x_lo = (x3 - x_hi.astype(F32)).astype(BF16)
    hi_part = _dot(x_hi, wr_ref[...])
    logits = hi_part[:, :128] + hi_part[:, 128:] + _dot(x_lo, wr_ref[:, 0:128])
    member, record = _top2(logits)
    route_ref[...] = record
    sel = jnp.where(member, 1.0, 0.0)
    pos_ref[...] = _dot(tri_ref[...], sel.astype(BF16)) + cnt_ref[...]
    cnt_ref[...] += jnp.sum(sel, axis=0, keepdims=True)
    tot_ref[...] = jnp.broadcast_to(cnt_ref[...], tot_ref.shape)


def _conv(hg, x2, w_dw, b_dw, ln_g, ln_b, w2, b2, pg, pb, w_router, *, seq, tm):
    n, d = x2.shape
    ratio = tm // CONV_HALO
    vec = _full((1, d))
    tri = (lax.broadcasted_iota(jnp.int32, (tm, tm), 1)
           < lax.broadcasted_iota(jnp.int32, (tm, tm), 0)).astype(BF16)
    return pl.pallas_call(
        functools.partial(_conv_kernel, tiles_per_seq=seq // tm),
        grid=(n // tm,),
        in_specs=[pl.BlockSpec((tm, d), lambda i: (i, 0)),
                  pl.BlockSpec((CONV_HALO, d), lambda i: (jnp.maximum(i * ratio - 1, 0), 0)),
                  pl.BlockSpec((tm, d), lambda i: (i, 0)),
                  _full((CONV_HALO, d)), vec, vec, vec, _full((d, d)), vec, vec, vec,
                  _full((d, 256)), _full((tm, tm))],
        out_specs=[pl.BlockSpec((tm, d), lambda i: (i, 0)),
                   pl.BlockSpec((tm, 128), lambda i: (i, 0)),
                   pl.BlockSpec((tm, 128), lambda i: (i, 0)),
                   _full((SUBLANES, 128))],
        out_shape=[jax.ShapeDtypeStruct((n, d), F32),
                   jax.ShapeDtypeStruct((n, 128), F32),
                   jax.ShapeDtypeStruct((n, 128), F32),
                   jax.ShapeDtypeStruct((SUBLANES, 128), F32)],
        scratch_shapes=[pltpu.VMEM((SUBLANES, tm + CONV_HALO, d), F32), pltpu.VMEM((tm, d), F32),
                        pltpu.VMEM((1, 128), F32)],
        compiler_params=_params("arbitrary"),
        name="conv_module",
    )(hg, hg, x2, w_dw, b_dw, ln_g, ln_b, w2, b2, pg, pb, w_router, tri)


MOE_TILE = 512
GATHER_ROWS = 2048
SC_CORES = 2
SC_SUBCORES = 16
SC_CHUNK = 64


def _row_gather_kernel(idx_ref, table_ref, out_ref, sem):
    base = pl.program_id(0) * GATHER_ROWS

    def issue(r, carry):
        pltpu.make_async_copy(table_ref.at[pl.ds(idx_ref[base + r], 1), :],
                              out_ref.at[pl.ds(base + r, 1), :], sem).start()
        return carry
    lax.fori_loop(0, GATHER_ROWS, issue, 0, unroll=8)
    pltpu.make_async_copy(table_ref.at[pl.ds(0, GATHER_ROWS), :],
                          out_ref.at[pl.ds(base, GATHER_ROWS), :], sem).wait()


def _row_gather(table, idx):
    rows = idx.shape[0]
    return pl.pallas_call(
        _row_gather_kernel,
        grid_spec=pltpu.PrefetchScalarGridSpec(
            num_scalar_prefetch=1, grid=(rows // GATHER_ROWS,),
            in_specs=[pl.BlockSpec(memory_space=pl.ANY)],
            out_specs=pl.BlockSpec(memory_space=pl.ANY),
            scratch_shapes=[pltpu.SemaphoreType.DMA]),
        out_shape=jax.ShapeDtypeStruct((rows, table.shape[1]), table.dtype),
        compiler_params=pltpu.CompilerParams(dimension_semantics=("arbitrary",),
                                             has_side_effects=True,
                                             disable_bounds_checks=True),
        name="moe_row_gather",
    )(idx, table)


def _sc_row_gather(table, idx):
    rows = idx.shape[0]
    d = table.shape[1]
    per_worker = rows // (SC_CORES * SC_SUBCORES)
    mesh = plsc.VectorSubcoreMesh(core_axis_name="c", subcore_axis_name="s",
                                  num_cores=SC_CORES, num_subcores=SC_SUBCORES)

    @functools.partial(
        pl.kernel, mesh=mesh,
        out_type=jax.ShapeDtypeStruct((rows, d), table.dtype),
        scratch_types=[pltpu.VMEM((SC_CHUNK,), jnp.int32),
                       pltpu.VMEM((SC_CHUNK, d), table.dtype),
                       pltpu.SemaphoreType.DMA],
        name="moe_sc_row_gather")
    def gather(table_hbm, idx_hbm, out_hbm, idx_v, rows_v, sem):
        worker = lax.axis_index("s") * SC_CORES + lax.axis_index("c")
        base = worker * per_worker

        @pl.loop(0, per_worker // SC_CHUNK)
        def _(ci):
            off = base + ci * SC_CHUNK
            pltpu.sync_copy(idx_hbm.at[pl.ds(off, SC_CHUNK)], idx_v)
            pltpu.async_copy(table_hbm.at[idx_v], rows_v, sem).wait()
            pltpu.sync_copy(rows_v, out_hbm.at[pl.ds(off, SC_CHUNK)])

    return gather(table, idx)


def _expert_ffn_kernel(expert_ref, used_ref, x_ref, wg_ref, wu_ref, wd_ref, o_ref):
    i = pl.program_id(0)
    f = pl.program_id(1)

    @pl.when(f == 0)
    def _():
        o_ref[...] = jnp.zeros_like(o_ref)

    @pl.when(used_ref[i] > 0)
    def _():
        xb = x_ref[...].astype(BF16)
        g = _dot(xb, wg_ref[...])
        u = _dot(xb, wu_ref[...])
        h = (g * _sigmoid(g) * u).astype(BF16)
        o_ref[...] += _dot(h, wd_ref[...])


def _expert_ffn(xs, tile_expert, tile_used, wg, wu, wd, *, tf):
    slots, d = xs.shape
    ff = wg.shape[2]
    return pl.pallas_call(
        _expert_ffn_kernel,
        grid_spec=pltpu.PrefetchScalarGridSpec(
            num_scalar_prefetch=2, grid=(slots // MOE_TILE, ff // tf),
            in_specs=[pl.BlockSpec((MOE_TILE, d), lambda i, f, te, tu: (i, 0)),
                      pl.BlockSpec((None, d, tf), lambda i, f, te, tu: (te[i], 0, f)),
                      pl.BlockSpec((None, d, tf), lambda i, f, te, tu: (te[i], 0, f)),
                      pl.BlockSpec((None, tf, d), lambda i, f, te, tu: (te[i], f, 0))],
            out_specs=pl.BlockSpec((MOE_TILE, d), lambda i, f, te, tu: (i, 0))),
        out_shape=jax.ShapeDtypeStruct((slots, d), F32),
        compiler_params=_params("arbitrary", "arbitrary"),
        name="moe_expert_ffn",
    )(tile_expert, tile_used, xs, wg, wu, wd)


def _combine_kernel(x_ref, y1_ref, y2_ref, route_ref, lng_ref, lnb_ref, o_ref):
    route = route_ref[...]
    moe = route[:, 2:3] * y1_ref[...] + route[:, 3:4] * y2_ref[...]
    o_ref[...] = _layer_norm(ALPHA * x_ref[...] + moe, lng_ref[...], lnb_ref[...])


def _combine(x3, yt, route, ln_g, ln_b, *, tm):
    n, d = x3.shape
    nt = n // tm
    return pl.pallas_call(
        _combine_kernel,
        grid=(nt,),
        in_specs=[pl.BlockSpec((tm, d), lambda i: (i, 0)),
                  pl.BlockSpec((tm, d), lambda i: (i, 0)),
                  pl.BlockSpec((tm, d), lambda i: (i + nt, 0)),
                  pl.BlockSpec((tm, 128), lambda i: (i, 0)),
                  _full((1, d)), _full((1, d))],
        out_specs=pl.BlockSpec((tm, d), lambda i: (i, 0)),
        out_shape=jax.ShapeDtypeStruct((n, d), F32),
        compiler_params=_params("parallel"),
        name="moe_combine",
    )(x3, yt, yt, route, ln_g, ln_b)


def _moe(x3, route, pos, tot, wg, wu, wd, ln_g, ln_b, *, tm, tf):
    n, d = x3.shape
    ne = wg.shape[0]
    slots = TOP_K * n + ne * MOE_TILE
    count = tot[0, :ne].astype(jnp.int32)
    cap = (count + (MOE_TILE - 1)) // MOE_TILE * MOE_TILE
    ends = jnp.cumsum(cap)
    off = ends - cap
    e1 = route[:, 0].astype(jnp.int32)
    e2 = route[:, 1].astype(jnp.int32)
    rank = pos[:, :ne].astype(jnp.int32)
    slot1 = off[e1] + jnp.take_along_axis(rank, e1[:, None], axis=1)[:, 0]
    slot2 = off[e2] + jnp.take_along_axis(rank, e2[:, None], axis=1)[:, 0]
    token = jnp.arange(n, dtype=jnp.int32)
    token_of_slot = jnp.zeros((slots,), jnp.int32).at[slot1].set(token).at[slot2].set(token)
    tile_start = jnp.arange(slots // MOE_TILE, dtype=jnp.int32) * MOE_TILE
    tile_expert = jnp.minimum(jnp.searchsorted(ends, tile_start, side="right"),
                              ne - 1).astype(jnp.int32)
    tile_used = (tile_start < ends[-1]).astype(jnp.int32)

    xs = _sc_row_gather(x3, token_of_slot)
    ys = _expert_ffn(xs, tile_expert, tile_used, wg, wu, wd, tf=tf)
    yt = _sc_row_gather(ys, jnp.concatenate([slot1, slot2]))
    return _combine(x3, yt, route, ln_g, ln_b, tm=tm)


def _pad_cols(w, width):
    return jnp.pad(w, ((0, 0), (0, width - w.shape[1])))


def _pad_rows(w, height):
    return jnp.pad(w, ((0, height - w.shape[0]), (0, 0)))


def _forward(x, mix_w_in, rwkv_mu, rwkv_w0, rwkv_w_up, rwkv_a0, rwkv_a_up, rwkv_g_up,
             rwkv_k_k, rwkv_k_a, rwkv_r_k, rwkv_gn_g, rwkv_gn_b, fox_b_f, mix_w_out,
             mix_ln_g, mix_ln_b, ffn_w_gate, ffn_w_up, ffn_w_down, ffn_ln_g, ffn_ln_b,
             conv_w_pw1, conv_b_pw1, conv_w_dw, conv_b_dw, conv_ln_g, conv_ln_b,
             conv_w_pw2, conv_b_pw2, conv_post_ln_g, conv_post_ln_b,
             moe_w_router, moe_w_gate, moe_w_up, moe_w_down, moe_ln_g, moe_ln_b,
             *, tm=512, chunk=64, nb_rwkv=4, tq=512, tf_ffn=1408, tf_moe=1792):
    batch, seq, d = x.shape
    n = batch * seq
    gw = GROUP_W
    x2 = x.reshape(n, d)
    row = lambda t: t.reshape(1, -1)

    w_in = mix_w_in[0]
    mu = rwkv_mu[0]
    o_w, o_a, o_g = 3 * gw, 3 * gw + DECAY_LORA, 3 * gw + DECAY_LORA + AAA_LORA
    o_fox = o_g + GATE_LORA

    def lora_layout(t):
        return jnp.concatenate([t[..., :o_w],
                                _pad_cols(t[..., o_w:o_a], LORA_PAD),
                                _pad_cols(t[..., o_a:o_g], LORA_PAD),
                                _pad_cols(t[..., o_g:o_fox], LORA_PAD)], axis=-1)

    wa = lora_layout(w_in).astype(BF16)
    mu_a = lora_layout(row(mu))
    scale = LOG2E / math.sqrt(HEAD_DIM)
    wb = jnp.concatenate([w_in[:, o_fox:o_fox + gw] * scale,
                          w_in[:, o_fox + gw:o_fox + 3 * gw]], axis=1).astype(BF16)
    wf = _pad_cols(w_in[:, o_fox + 3 * gw:], 128).astype(BF16)
    bf = _pad_cols(row(fox_b_f[0]), 128)

    pr, qk, vt, c = _inproj(x2, wa, wb, wf, mu_a, bf, seq=seq, tm=tq)

    wup = _pad_rows(rwkv_w_up[0], LORA_PAD).astype(BF16)
    aup = _pad_rows(rwkv_a_up[0], LORA_PAD).astype(BF16)
    gup = _pad_rows(rwkv_g_up[0], LORA_PAD).astype(BF16)
    k_k, k_a, r_k = row(rwkv_k_k[0]), row(rwkv_k_a[0]), row(rwkv_r_k[0])
    yr = _rwkv(pr, row(rwkv_w0[0]), wup, row(rwkv_a0[0]), aup, k_k, k_a,
               batch=batch, seq=seq, chunk=chunk, nb=nb_rwkv)

    yf = _fox(qk, vt, c, batch=batch, seq=seq, tq=tq)

    w_out = mix_w_out[0].astype(BF16)
    x1 = _mixout(x2, pr, yr, yf, row(rwkv_a0[0]), aup, gup, k_a, r_k,
                 row(rwkv_gn_g[0]), row(rwkv_gn_b[0]), w_out[:gw], w_out[gw:],
                 row(mix_ln_g[0]), row(mix_ln_b[0]), tm=tm)
    x2b = _ffn(x1, ffn_w_gate[0].astype(BF16), ffn_w_up[0].astype(BF16),
               ffn_w_down[0].astype(BF16), row(ffn_ln_g[0]), row(ffn_ln_b[0]), tm=tm, tf=tf_ffn)

    hg = _glu(x2b, conv_w_pw1[0].astype(BF16), row(conv_b_pw1[0]), tm=tm)
    w_router = _pad_cols(moe_w_router[0], 128)
    wr_hi = w_router.astype(BF16)
    w_router = jnp.concatenate([wr_hi, (w_router - wr_hi.astype(F32)).astype(BF16)], axis=1)
    x3, route, pos, tot = _conv(hg, x2b, _pad_rows(conv_w_dw[0], CONV_HALO), row(conv_b_dw[0]),
                                row(conv_ln_g[0]), row(conv_ln_b[0]),
                                conv_w_pw2[0].astype(BF16), row(conv_b_pw2[0]),
                                row(conv_post_ln_g[0]), row(conv_post_ln_b[0]),
                                w_router, seq=seq, tm=tm)
    out = _moe(x3, route, pos, tot, moe_w_gate[0].astype(BF16), moe_w_up[0].astype(BF16),
               moe_w_down[0].astype(BF16), row(moe_ln_g[0]), row(moe_ln_b[0]),
               tm=tm, tf=tf_moe)
    return out.reshape(batch, seq, d)


def kernel(x, mix_w_in, rwkv_mu, rwkv_w0, rwkv_w_up, rwkv_a0, rwkv_a_up, rwkv_g_up, rwkv_k_k, rwkv_k_a, rwkv_r_k, rwkv_gn_g, rwkv_gn_b, fox_b_f, mix_w_out, mix_ln_g, mix_ln_b, ffn_w_gate, ffn_w_up, ffn_w_down, ffn_ln_g, ffn_ln_b, conv_w_pw1, conv_b_pw1, conv_w_dw, conv_b_dw, conv_ln_g, conv_ln_b, conv_w_pw2, conv_b_pw2, conv_post_ln_g, conv_post_ln_b, moe_w_router, moe_w_gate, moe_w_up, moe_w_down, moe_ln_g, moe_ln_b):
    return _forward(x, mix_w_in, rwkv_mu, rwkv_w0, rwkv_w_up, rwkv_a0, rwkv_a_up, rwkv_g_up,
                    rwkv_k_k, rwkv_k_a, rwkv_r_k, rwkv_gn_g, rwkv_gn_b, fox_b_f, mix_w_out,
                    mix_ln_g, mix_ln_b, ffn_w_gate, ffn_w_up, ffn_w_down, ffn_ln_g, ffn_ln_b,
                    conv_w_pw1, conv_b_pw1, conv_w_dw, conv_b_dw, conv_ln_g, conv_ln_b,
                    conv_w_pw2, conv_b_pw2, conv_post_ln_g, conv_post_ln_b,
                    moe_w_router, moe_w_gate, moe_w_up, moe_w_down, moe_ln_g, moe_ln_b)
```

```python
import functools
import math

import jax
import jax.numpy as jnp
from jax import lax
from jax.experimental import pallas as pl
from jax.experimental.pallas import tpu as pltpu
from jax.experimental.pallas import tpu_sc as plsc

F32 = jnp.float32
BF16 = jnp.bfloat16
HIGHEST = lax.Precision.HIGHEST

HEAD_DIM = 64
N_HEADS = 8
GROUP_W = N_HEADS * HEAD_DIM
LORA_PAD = 128
DECAY_LORA = 32
AAA_LORA = 32
GATE_LORA = 96
CONV_WIDTH = 31
CONV_HALO = 32
SUBLANES = 8
N_EXPERTS = 8
TOP_K = 2
LN_EPS = 1e-5
GN_EPS = 64e-5
DEPTH = 2
ALPHA = (2.0 * DEPTH) ** 0.25
NEG_BIG = -1e30
LOG2E = math.log2(math.e)
VMEM_LIMIT = 56 * 1024 * 1024


def _dot(a, b, **kw):
    return jnp.dot(a, b, preferred_element_type=F32, **kw)


def _dot_nt(a, b):
    return lax.dot_general(a, b, (((1,), (1,)), ((), ())), preferred_element_type=F32)


def _dot_tn(a, b):
    return lax.dot_general(a, b, (((0,), (0,)), ((), ())), preferred_element_type=F32)


def _dot_exact_lhs(a, v):
    hi = v.astype(BF16)
    rem = v - hi.astype(F32)
    mid = rem.astype(BF16)
    lo = (rem - mid.astype(F32)).astype(BF16)
    w = v.shape[1]
    out = _dot(a, jnp.concatenate([hi, mid, lo], axis=1))
    return out[:, :w] + out[:, w:2 * w] + out[:, 2 * w:]


def _sigmoid(z):
    return 1.0 / (1.0 + jnp.exp(-z))


def _softplus(z):
    return jnp.maximum(z, 0.0) + jnp.log1p(jnp.exp(-jnp.abs(z)))


def _layer_norm(h, g, b):
    mu = jnp.mean(h, axis=-1, keepdims=True)
    d = h - mu
    var = jnp.mean(d * d, axis=-1, keepdims=True)
    return d * lax.rsqrt(var + LN_EPS) * g + b


def _params(*sem):
    return pltpu.CompilerParams(dimension_semantics=sem, vmem_limit_bytes=VMEM_LIMIT)


def _full(shape):
    return pl.BlockSpec(shape, lambda *_: (0,) * len(shape))


def _inproj_kernel(x_ref, wa_ref, wb_ref, wf_ref, mu_ref, bf_ref, tri_ref,
                   pr_ref, qk_ref, vt_ref, c_ref, last_ref, carry_ref, *, tiles_per_seq):
    i = pl.program_id(0)

    @pl.when(i % tiles_per_seq == 0)
    def _():
        last_ref[...] = jnp.zeros_like(last_ref)
        carry_ref[...] = jnp.zeros_like(carry_ref)

    xb = x_ref[...].astype(BF16)
    tm = xb.shape[0]
    row0 = lax.broadcasted_iota(jnp.int32, (tm, 1), 0) == 0
    ca = wa_ref.shape[1]
    for c0 in range(0, ca, GROUP_W):
        cw = min(GROUP_W, ca - c0)
        p = _dot(xb, wa_ref[:, c0:c0 + cw])
        prev = jnp.where(row0, last_ref[:, c0:c0 + cw], pltpu.roll(p, 1, 0))
        last_ref[:, c0:c0 + cw] = p[tm - 1:tm, :]
        pr_ref[:, c0:c0 + cw] = p + mu_ref[:, c0:c0 + cw] * (prev - p)
    for c0 in range(0, 2 * GROUP_W, GROUP_W):
        qk_ref[:, c0:c0 + GROUP_W] = _dot(xb, wb_ref[:, c0:c0 + GROUP_W]).astype(BF16)
    vt_ref[...] = _dot(xb, wb_ref[:, 2 * GROUP_W:3 * GROUP_W]).T.astype(BF16)
    fl = _dot(xb, wf_ref[...]) + bf_ref[...]
    log_f = jnp.minimum(fl, 0.0) - jnp.log1p(jnp.exp(-jnp.abs(fl)))
    c = _dot_exact_lhs(tri_ref[...], log_f) + carry_ref[...]
    c_ref[...] = c
    carry_ref[...] = c[tm - 1:tm, :]


def _inproj(x2, wa, wb, wf, mu, bf, *, seq, tm):
    n, d = x2.shape
    ca, cb = wa.shape[1], wb.shape[1]
    tps = seq // tm
    tri = (lax.broadcasted_iota(jnp.int32, (tm, tm), 1)
           <= lax.broadcasted_iota(jnp.int32, (tm, tm), 0)).astype(BF16)
    return pl.pallas_call(
        functools.partial(_inproj_kernel, tiles_per_seq=tps),
        grid=(n // tm,),
        in_specs=[pl.BlockSpec((tm, d), lambda i: (i, 0)),
                  _full((d, ca)), _full((d, cb)), _full((d, 128)),
                  _full((1, ca)), _full((1, 128)), _full((tm, tm))],
        out_specs=[pl.BlockSpec((tm, ca), lambda i: (i, 0)),
                   pl.BlockSpec((tm, 2 * GROUP_W), lambda i: (i, 0)),
                   pl.BlockSpec((None, None, GROUP_W, tm), lambda i: (i // tps, i % tps, 0, 0)),
                   pl.BlockSpec((tm, 128), lambda i: (i, 0))],
        out_shape=[jax.ShapeDtypeStruct((n, ca), F32),
                   jax.ShapeDtypeStruct((n, 2 * GROUP_W), BF16),
                   jax.ShapeDtypeStruct((n // seq, tps, GROUP_W, tm), BF16),
                   jax.ShapeDtypeStruct((n, 128), F32)],
        scratch_shapes=[pltpu.VMEM((1, ca), F32), pltpu.VMEM((1, 128), F32)],
        compiler_params=_params("arbitrary"),
        name="inproj",
    )(x2, wa, wb, wf, mu, bf, tri)


def _rwkv_kernel(pr_ref, w0_ref, wup_ref, a0_ref, aup_ref, kk_ref, ka_ref,
                 gsum_ref, tri_ref, y_ref, h_ref, *, chunk, nb):
    @pl.when(pl.program_id(1) == 0)
    def _():
        h_ref[...] = jnp.zeros_like(h_ref)

    gw = GROUP_W
    pw = 2 * HEAD_DIM
    npair = N_HEADS // 2
    rows = 2 * chunk
    log_chunk = int(math.log2(chunk))
    head0 = lax.broadcasted_iota(jnp.int32, (1, pw), 1) < HEAD_DIM
    row = lax.broadcasted_iota(jnp.int32, (rows, rows), 0)
    col = lax.broadcasted_iota(jnp.int32, (rows, rows), 1)
    strict = (col & (chunk - 1)) < (row & (chunk - 1))
    incl = (col & (chunk - 1)) <= (row & (chunk - 1))
    eye = (col == row).astype(F32)
    peye = (lax.broadcasted_iota(jnp.int32, (pw, pw), 0)
            == lax.broadcasted_iota(jnp.int32, (pw, pw), 1))

    def stack(x):
        return jnp.concatenate([jnp.where(head0, x, 0.0), jnp.where(head0, 0.0, x)],
                               axis=0).astype(BF16)

    units = [(b, j) for b in range(nb) for j in range(npair)]
    nu = len(units)
    ar, bk, vs, bhs, khs, pcs = [], [], [], [], [], []
    for b in range(nb):
        r = pr_ref[b, :, 0:gw]
        k = pr_ref[b, :, gw:2 * gw]
        v = pr_ref[b, :, 2 * gw:3 * gw]
        wd = pr_ref[b, :, 3 * gw:3 * gw + LORA_PAD]
        ad = pr_ref[b, :, 3 * gw + LORA_PAD:3 * gw + 2 * LORA_PAD]
        w_pre = w0_ref[...] + _dot(jnp.tanh(wd).astype(BF16), wup_ref[...])
        w = -_softplus(-w_pre) - 0.5
        log_decay = -jnp.exp(w)
        a = _sigmoid(a0_ref[...] + _dot(ad.astype(BF16), aup_ref[...]))
        kk = k * kk_ref[...]
        norm = jnp.sqrt(_dot((kk * kk).astype(BF16), gsum_ref[...]))
        kk = kk / jnp.maximum(norm, 1e-12)
        k_mod = k * (1.0 + (a - 1.0) * ka_ref[...])
        b_vec = kk * a
        cum = _dot_exact_lhs(tri_ref[...], log_decay)
        last = cum[chunk - 1:chunk, :]
        p_inv = jnp.exp(-cum)
        p_tail = jnp.exp(last - cum)
        a_t = -kk * jnp.exp(cum - log_decay)
        r_t = r * jnp.exp(cum)
        b_t = b_vec * p_inv
        k_t = k_mod * p_inv
        b_h = b_vec * p_tail
        k_h = k_mod * p_tail
        p_last = jnp.exp(last)
        for j in range(npair):
            sl = slice(j * pw, (j + 1) * pw)
            ar.append(jnp.concatenate([stack(a_t[:, sl]), stack(r_t[:, sl])], axis=0))
            bk.append(jnp.concatenate([stack(b_t[:, sl]), stack(k_t[:, sl])], axis=0))
            vs.append(stack(v[:, sl]))
            bhs.append(stack(b_h[:, sl]))
            khs.append(stack(k_h[:, sl]))
            pcs.append(jnp.sum(jnp.where(peye, p_last[:, sl], 0.0), axis=1, keepdims=True))

    gram = [_dot_nt(ar[u], bk[u]) for u in range(nu)]
    l_ab = [jnp.where(strict, gram[u][:rows, :rows], 0.0) for u in range(nu)]
    l_akv = [_dot(jnp.where(strict, gram[u][:rows, rows:], 0.0).astype(BF16), vs[u])
             for u in range(nu)]
    m_rb = [jnp.where(incl, gram[u][rows:, :rows], 0.0).astype(BF16) for u in range(nu)]
    m_rkv = [_dot(jnp.where(incl, gram[u][rows:, rows:], 0.0).astype(BF16), vs[u])
             for u in range(nu)]
    t_inv = [eye + l_ab[u] for u in range(nu)]
    xb = [l_ab[u].astype(BF16) for u in range(nu)]
    xp = [_dot(xb[u], xb[u]) for u in range(nu)]
    for step in range(log_chunk - 1):
        xb = [xp[u].astype(BF16) for u in range(nu)]
        if step < log_chunk - 2:
            both = [_dot(jnp.concatenate([t_inv[u].astype(BF16), xb[u]], axis=0), xb[u])
                    for u in range(nu)]
            t_inv = [t_inv[u] + both[u][:rows] for u in range(nu)]
            xp = [both[u][rows:] for u in range(nu)]
        else:
            t_inv = [t_inv[u] + _dot(t_inv[u].astype(BF16), xb[u]) for u in range(nu)]
    tw = [_dot(t_inv[u].astype(BF16),
               jnp.concatenate([ar[u][:rows], l_akv[u].astype(BF16)], axis=1)).astype(BF16)
          for u in range(nu)]
    mw = [_dot(m_rb[u], tw[u]) for u in range(nu)]
    bw = [_dot_tn(bhs[u], tw[u]) for u in range(nu)]
    kv = [_dot_tn(khs[u], vs[u]) for u in range(nu)]
    for u, (b, j) in enumerate(units):
        wy = ar[u][rows:].astype(F32) + mw[u][:, :pw]
        yc = mw[u][:, pw:] + m_rkv[u]
        hf = h_ref[u]
        yh = _dot(jnp.concatenate([wy.astype(BF16), bw[u][:, :pw].astype(BF16)], axis=0),
                  hf.astype(BF16))
        h_ref[u] = pcs[u] * hf + yh[rows:] + bw[u][:, pw:] + kv[u]
        ys = yh[:rows] + yc
        y_ref[b, :, j * pw:(j + 1) * pw] = ys[:chunk] + ys[chunk:]


def _rwkv(pr, w0, wup, a0, aup, k_k, k_a, *, batch, seq, chunk, nb):
    n, ca = pr.shape
    nch = seq // chunk
    gidx = lax.broadcasted_iota(jnp.int32, (GROUP_W, GROUP_W), 0) // HEAD_DIM
    gsum = (gidx == gidx.T).astype(BF16)
    tri = (lax.broadcasted_iota(jnp.int32, (chunk, chunk), 1)
           <= lax.broadcasted_iota(jnp.int32, (chunk, chunk), 0)).astype(BF16)
    y = pl.pallas_call(
        functools.partial(_rwkv_kernel, chunk=chunk, nb=nb),
        grid=(batch // nb, nch),
        in_specs=[pl.BlockSpec((nb, chunk, ca), lambda g, c: (g, c, 0)),
                  _full((1, GROUP_W)), _full((LORA_PAD, GROUP_W)),
                  _full((1, GROUP_W)), _full((LORA_PAD, GROUP_W)),
                  _full((1, GROUP_W)), _full((1, GROUP_W)),
                  _full((GROUP_W, GROUP_W)), _full((chunk, chunk))],
        out_specs=pl.BlockSpec((nb, chunk, GROUP_W), lambda g, c: (g, c, 0)),
        out_shape=jax.ShapeDtypeStruct((batch, seq, GROUP_W), F32),
        scratch_shapes=[pltpu.VMEM((nb * (N_HEADS // 2), 2 * HEAD_DIM, 2 * HEAD_DIM), F32)],
        compiler_params=_params("arbitrary", "arbitrary"),
        name="rwkv_scan",
    )(pr.reshape(batch, seq, ca), w0, wup, a0, aup, k_k, k_a, gsum, tri)
    return y.reshape(n, GROUP_W)


def _fox_kernel(q_ref, k_ref, vt_ref, c_ref, o_ref, acc_ref, m_ref, l_ref, kb_ref,
                sa_ref, sb_ref, *, tq):
    j = pl.program_id(1)
    qi = pl.program_id(2)
    pw = 2 * HEAD_DIM
    seq = k_ref.shape[0]
    lane = lax.broadcasted_iota(jnp.int32, (1, pw), 1)

    first = lane < HEAD_DIM
    bias_lane = (HEAD_DIM, 0)

    @pl.when(qi == 0)
    def _():
        def fill(rb, carry):
            rs = pl.multiple_of(rb * tq, tq)
            cblk = c_ref[pl.ds(rs, tq), :]
            for hh in range(2):
                bias = -LOG2E * jnp.sum(jnp.where(lane == 2 * j + hh, cblk, 0.0),
                                        axis=1, keepdims=True)
                b_hi = bias.astype(BF16).astype(F32)
                b_mid = (bias - b_hi).astype(BF16).astype(F32)
                b_lo = bias - b_hi - b_mid
                l0 = bias_lane[hh]
                kb_ref[hh, pl.ds(rs, tq), :] = jnp.where(
                    lane == l0, b_hi, jnp.where(lane == l0 + 1, b_mid,
                                                jnp.where(lane == l0 + 2, b_lo, 0.0))).astype(BF16)
            return carry
        lax.fori_loop(0, seq // tq, fill, 0)

    q = q_ref[...]
    ones3 = [jnp.where(jnp.logical_and(lane >= l0, lane < l0 + 3), 1.0, 0.0).astype(BF16)
             for l0 in bias_lane]
    own = (first, jnp.logical_not(first))
    qh = tuple(jnp.where(own[hh], q, ones3[hh]) for hh in range(2))
    acc_ref[...] = jnp.zeros_like(acc_ref)
    m_ref[...] = jnp.full_like(m_ref, NEG_BIG)
    l_ref[...] = jnp.zeros_like(l_ref)
    causal = (lax.broadcasted_iota(jnp.int32, (tq, tq), 0)
              <= lax.broadcasted_iota(jnp.int32, (tq, tq), 1))
    top = lax.broadcasted_iota(jnp.int32, (pw, 1), 0) < HEAD_DIM

    def scores(kb, s_ref):
        ks = pl.multiple_of(kb * tq, tq)
        kblk = k_ref[pl.ds(ks, tq), :]
        for hh in range(2):
            k_aug = jnp.where(own[hh], kblk, kb_ref[hh, pl.ds(ks, tq), :])
            s_ref[hh] = _dot_nt(k_aug, qh[hh])

    def softmax_pv(kb, s_ref, masked):
        vt = vt_ref[kb].astype(F32)
        vts = (jnp.where(top, vt, 1.0).astype(BF16), jnp.where(top, 1.0, vt).astype(BF16))
        alphas, pvs = [], []
        for hh in range(2):
            z = s_ref[hh]
            if masked:
                z = jnp.where(causal, z, NEG_BIG)
            m_prev = m_ref[hh]
            m_new = jnp.maximum(m_prev, jnp.max(z, axis=0, keepdims=True))
            alpha = jnp.exp2(m_prev - m_new)
            p = jnp.exp2(z - m_new)
            pv = _dot(vts[hh], p.astype(BF16))
            ones_row = (1 - hh) * HEAD_DIM
            l_ref[hh] = alpha * l_ref[hh] + pv[ones_row:ones_row + 1, :]
            m_ref[hh] = m_new
            alphas.append(alpha)
            pvs.append(pv)
        acc_ref[...] = (acc_ref[...] * jnp.where(top, alphas[0], alphas[1])
                        + jnp.where(top, pvs[0], pvs[1]))

    scores(0, sa_ref)

    def body(i, carry):
        scores(2 * i + 1, sb_ref)
        softmax_pv(2 * i, sa_ref, False)
        scores(2 * i + 2, sa_ref)
        softmax_pv(2 * i + 1, sb_ref, False)
        return carry

    lax.fori_loop(0, qi // 2, body, 0)

    @pl.when(qi % 2 == 0)
    def _():
        softmax_pv(qi, sa_ref, True)

    @pl.when(qi % 2 == 1)
    def _():
        scores(qi, sb_ref)
        softmax_pv(qi - 1, sa_ref, False)
        softmax_pv(qi, sb_ref, True)

    out_t = acc_ref[...] / jnp.where(top, l_ref[0], l_ref[1])
    o_ref[...] = out_t.T.astype(BF16)


def _fox(qk, vt, c, *, batch, seq, tq):
    n = qk.shape[0]
    nq = seq // tq
    npair = N_HEADS // 2
    pw = 2 * HEAD_DIM
    assert vt.shape == (batch, nq, GROUP_W, tq)
    return pl.pallas_call(
        functools.partial(_fox_kernel, tq=tq),
        grid=(batch, npair, nq),
        in_specs=[pl.BlockSpec((tq, pw), lambda b, j, i: (b * nq + i, j)),
                  pl.BlockSpec((seq, pw), lambda b, j, i: (b, npair + j)),
                  pl.BlockSpec((None, nq, pw, tq), lambda b, j, i: (b, 0, j, 0)),
                  pl.BlockSpec((seq, 128), lambda b, j, i: (b, 0))],
        out_specs=pl.BlockSpec((tq, pw), lambda b, j, i: (b * nq + i, j)),
        out_shape=jax.ShapeDtypeStruct((n, GROUP_W), BF16),
        scratch_shapes=[pltpu.VMEM((pw, tq), F32),
                        pltpu.VMEM((2, 1, tq), F32), pltpu.VMEM((2, 1, tq), F32),
                        pltpu.VMEM((2, seq, pw), BF16),
                        pltpu.VMEM((2, tq, tq), F32), pltpu.VMEM((2, tq, tq), F32)],
        compiler_params=_params("arbitrary", "arbitrary", "arbitrary"),
        name="fox_attention",
    )(qk, qk, vt, c)


def _mixout_kernel(x_ref, pr_ref, yr_ref, yf_ref, a0_ref, aup_ref, gup_ref, ka_ref,
                   rk_ref, gng_ref, gnb_ref, gsum_ref, wr_ref, wf_ref, lng_ref, lnb_ref,
                   o_ref):
    gw = GROUP_W
    r = pr_ref[:, 0:gw]
    k = pr_ref[:, gw:2 * gw]
    v = pr_ref[:, 2 * gw:3 * gw]
    ad = pr_ref[:, 3 * gw + LORA_PAD:3 * gw + 2 * LORA_PAD]
    gd = pr_ref[:, 3 * gw + 2 * LORA_PAD:3 * gw + 3 * LORA_PAD]
    a = _sigmoid(a0_ref[...] + _dot(ad.astype(BF16), aup_ref[...]))
    k_mod = k * (1.0 + (a - 1.0) * ka_ref[...])
    gate = _dot(_sigmoid(gd).astype(BF16), gup_ref[...])
    gsum = gsum_ref[...]

    def group_sum(t):
        return _dot(t.astype(BF16), gsum)

    y = yr_ref[...]
    y_hi = y.astype(BF16)
    y_lo = (y - y_hi.astype(F32)).astype(BF16)
    mean = (_dot(y_hi, gsum) + _dot(y_lo, gsum)) * (1.0 / HEAD_DIM)
    d = y - mean
    var = group_sum(d * d) * (1.0 / HEAD_DIM)
    yn = d * lax.rsqrt(var + GN_EPS) * gng_ref[...] + gnb_ref[...]
    bonus = group_sum(r * k_mod * rk_ref[...])
    y_rwkv = ((yn + bonus * v) * gate).astype(BF16)
    mixed = _dot(y_rwkv, wr_ref[...]) + _dot(yf_ref[...], wf_ref[...])
    o_ref[...] = _layer_norm(ALPHA * x_ref[...] + mixed, lng_ref[...], lnb_ref[...])


def _mixout(x2, pr, yr, yf, a0, aup, gup, k_a, r_k, gn_g, gn_b, w_r, w_f, ln_g, ln_b, *, tm):
    n, d = x2.shape
    ca = pr.shape[1]
    gidx = lax.broadcasted_iota(jnp.int32, (GROUP_W, GROUP_W), 0) // HEAD_DIM
    gsum = (gidx == gidx.T).astype(BF16)
    vec = _full((1, GROUP_W))
    return pl.pallas_call(
        _mixout_kernel,
        grid=(n // tm,),
        in_specs=[pl.BlockSpec((tm, d), lambda i: (i, 0)),
                  pl.BlockSpec((tm, ca), lambda i: (i, 0)),
                  pl.BlockSpec((tm, GROUP_W), lambda i: (i, 0)),
                  pl.BlockSpec((tm, GROUP_W), lambda i: (i, 0)),
                  vec, _full((LORA_PAD, GROUP_W)), _full((LORA_PAD, GROUP_W)),
                  vec, vec, vec, vec, _full((GROUP_W, GROUP_W)),
                  _full((GROUP_W, d)), _full((GROUP_W, d)),
                  _full((1, d)), _full((1, d))],
        out_specs=pl.BlockSpec((tm, d), lambda i: (i, 0)),
        out_shape=jax.ShapeDtypeStruct((n, d), F32),
        compiler_params=_params("parallel"),
        name="mix_out",
    )(x2, pr, yr, yf, a0, aup, gup, k_a, r_k, gn_g, gn_b, gsum, w_r, w_f, ln_g, ln_b)


def _ffn_kernel(x_ref, wg_ref, wu_ref, wd_ref, lng_ref, lnb_ref, o_ref, acc_ref):
    f = pl.program_id(1)

    @pl.when(f == 0)
    def _():
        acc_ref[...] = jnp.zeros_like(acc_ref)

    xb = x_ref[...].astype(BF16)
    g = _dot(xb, wg_ref[...])
    u = _dot(xb, wu_ref[...])
    h = (g * _sigmoid(g) * u).astype(BF16)
    acc_ref[...] += _dot(h, wd_ref[...])

    @pl.when(f == pl.num_programs(1) - 1)
    def _():
        o_ref[...] = _layer_norm(ALPHA * x_ref[...] + acc_ref[...], lng_ref[...], lnb_ref[...])


def _ffn(x2, wg, wu, wd, ln_g, ln_b, *, tm, tf):
    n, d = x2.shape
    ff = wg.shape[1]
    return pl.pallas_call(
        _ffn_kernel,
        grid=(n // tm, ff // tf),
        in_specs=[pl.BlockSpec((tm, d), lambda i, f: (i, 0)),
                  pl.BlockSpec((d, tf), lambda i, f: (0, f)),
                  pl.BlockSpec((d, tf), lambda i, f: (0, f)),
                  pl.BlockSpec((tf, d), lambda i, f: (f, 0)),
                  _full((1, d)), _full((1, d))],
        out_specs=pl.BlockSpec((tm, d), lambda i, f: (i, 0)),
        out_shape=jax.ShapeDtypeStruct((n, d), F32),
        scratch_shapes=[pltpu.VMEM((tm, d), F32)],
        compiler_params=_params("parallel", "arbitrary"),
        name="ffn_swiglu",
    )(x2, wg, wu, wd, ln_g, ln_b)


def _glu_kernel(x_ref, w_ref, b_ref, o_ref):
    d = o_ref.shape[1]
    xb = x_ref[...].astype(BF16)
    val = _dot(xb, w_ref[:, 0:d]) + b_ref[:, 0:d]
    gat = _dot(xb, w_ref[:, d:2 * d]) + b_ref[:, d:2 * d]
    o_ref[...] = val * _sigmoid(gat)


def _glu(x2, w, b, *, tm):
    n, d = x2.shape
    return pl.pallas_call(
        _glu_kernel,
        grid=(n // tm,),
        in_specs=[pl.BlockSpec((tm, d), lambda i: (i, 0)), _full((d, 2 * d)), _full((1, 2 * d))],
        out_specs=pl.BlockSpec((tm, d), lambda i: (i, 0)),
        out_shape=jax.ShapeDtypeStruct((n, d), F32),
        compiler_params=_params("parallel"),
        name="conv_glu",
    )(x2, w, b)


def _top2(logits):
    lane = lax.broadcasted_iota(jnp.int32, logits.shape, 1).astype(F32)
    lg = jnp.where(lane < N_EXPERTS, logits, NEG_BIG)
    m1 = jnp.max(lg, axis=-1, keepdims=True)
    i1 = jnp.min(jnp.where(lg == m1, lane, 128.0), axis=-1, keepdims=True)
    lg2 = jnp.where(lane == i1, NEG_BIG, lg)
    m2 = jnp.max(lg2, axis=-1, keepdims=True)
    i2 = jnp.min(jnp.where(lg2 == m2, lane, 128.0), axis=-1, keepdims=True)
    e2 = jnp.exp(m2 - m1)
    w1 = 1.0 / (1.0 + e2)
    w2 = e2 / (1.0 + e2)
    member = jnp.logical_or(lane == i1, lane == i2)
    record = jnp.where(lane == 0.0, i1, jnp.where(lane == 1.0, i2,
                                                  jnp.where(lane == 2.0, w1,
                                                            jnp.where(lane == 3.0, w2, 0.0))))
    return member, record


def _conv_kernel(hc_ref, hp_ref, x_ref, wdw_ref, bdw_ref, lng_ref, lnb_ref, w2_ref, b2_ref,
                 pg_ref, pb_ref, wr_ref, tri_ref, x3_ref, route_ref, pos_ref, tot_ref,
                 ext_ref, cv_ref, cnt_ref, *, tiles_per_seq):
    tm, d = x_ref.shape

    @pl.when(pl.program_id(0) == 0)
    def _():
        cnt_ref[...] = jnp.zeros_like(cnt_ref)

    first = pl.program_id(0) % tiles_per_seq == 0
    ext_ref[0, 0:CONV_HALO, :] = jnp.where(first, 0.0, hp_ref[...])
    ext_ref[0, CONV_HALO:CONV_HALO + tm, :] = hc_ref[...]
    nrows = tm + CONV_HALO
    for c0 in range(0, d, 256):
        base = ext_ref[0, :, c0:c0 + 256]
        for j in range(1, SUBLANES):
            ext_ref[j, :, c0:c0 + 256] = pltpu.roll(base, nrows - j, 0)
    off = CONV_HALO - (CONV_WIDTH - 1)
    rc, cc = 64, 256
    for r0 in range(0, tm, rc):
        for c0 in range(0, d, cc):
            acc = jnp.broadcast_to(bdw_ref[:, c0:c0 + cc], (rc, cc))
            for t in range(CONV_WIDTH):
                base, j = divmod(off + t, SUBLANES)
                rs = r0 + base * SUBLANES
                acc = acc + wdw_ref[t:t + 1, c0:c0 + cc] * ext_ref[j, rs:rs + rc, c0:c0 + cc]
            cv_ref[r0:r0 + rc, c0:c0 + cc] = acc
    hn = _layer_norm(cv_ref[...], lng_ref[...], lnb_ref[...])
    hs = (hn * _sigmoid(hn)).astype(BF16)
    conv = _dot(hs, w2_ref[...]) + b2_ref[...]
    x3 = _layer_norm(ALPHA * x_ref[...] + conv, pg_ref[...], pb_ref[...])
    x3_ref[...] = x3
    x_hi = x3.astype(BF16)
    x_lo = (x3 - x_hi.astype(F32)).astype(BF16)
    hi_part = _dot(x_hi, wr_ref[...])
    logits = hi_part[:, :128] + hi_part[:, 128:] + _dot(x_lo, wr_ref[:, 0:128])
    member, record = _top2(logits)
    route_ref[...] = record
    sel = jnp.where(member, 1.0, 0.0)
    pos_ref[...] = _dot(tri_ref[...], sel.astype(BF16)) + cnt_ref[...]
    cnt_ref[...] += jnp.sum(sel, axis=0, keepdims=True)
    tot_ref[...] = jnp.broadcast_to(cnt_ref[...], tot_ref.shape)


def _conv(hg, x2, w_dw, b_dw, ln_g, ln_b, w2, b2, pg, pb, w_router, *, seq, tm):
    n, d = x2.shape
    ratio = tm // CONV_HALO
    vec = _full((1, d))
    tri = (lax.broadcasted_iota(jnp.int32, (tm, tm), 1)
           < lax.broadcasted_iota(jnp.int32, (tm, tm), 0)).astype(BF16)
    return pl.pallas_call(
        functools.partial(_conv_kernel, tiles_per_seq=seq // tm),
        grid=(n // tm,),
        in_specs=[pl.BlockSpec((tm, d), lambda i: (i, 0)),
                  pl.BlockSpec((CONV_HALO, d), lambda i: (jnp.maximum(i * ratio - 1, 0), 0)),
                  pl.BlockSpec((tm, d), lambda i: (i, 0)),
                  _full((CONV_HALO, d)), vec, vec, vec, _full((d, d)), vec, vec, vec,
                  _full((d, 256)), _full((tm, tm))],
        out_specs=[pl.BlockSpec((tm, d), lambda i: (i, 0)),
                   pl.BlockSpec((tm, 128), lambda i: (i, 0)),
                   pl.BlockSpec((tm, 128), lambda i: (i, 0)),
                   _full((SUBLANES, 128))],
        out_shape=[jax.ShapeDtypeStruct((n, d), F32),
                   jax.ShapeDtypeStruct((n, 128), F32),
                   jax.ShapeDtypeStruct((n, 128), F32),
                   jax.ShapeDtypeStruct((SUBLANES, 128), F32)],
        scratch_shapes=[pltpu.VMEM((SUBLANES, tm + CONV_HALO, d), F32), pltpu.VMEM((tm, d), F32),
                        pltpu.VMEM((1, 128), F32)],
        compiler_params=_params("arbitrary"),
        name="conv_module",
    )(hg, hg, x2, w_dw, b_dw, ln_g, ln_b, w2, b2, pg, pb, w_router, tri)


MOE_TILE = 512
SC_CORES = 2
SC_SUBCORES = 16
SC_CHUNK = 64


def _sc_row_scatter(x, slot_a, slot_b, slots):
    n, d = x.shape
    per_worker = n // (SC_CORES * SC_SUBCORES)
    mesh = plsc.VectorSubcoreMesh(core_axis_name="c", subcore_axis_name="s",
                                  num_cores=SC_CORES, num_subcores=SC_SUBCORES)

    @functools.partial(
        pl.kernel, mesh=mesh,
        out_type=jax.ShapeDtypeStruct((slots, d), x.dtype),
        scratch_types=[pltpu.VMEM((SC_CHUNK,), jnp.int32),
                       pltpu.VMEM((SC_CHUNK,), jnp.int32),
                       pltpu.VMEM((SC_CHUNK, d), x.dtype),
                       pltpu.SemaphoreType.DMA],
        name="moe_sc_row_scatter")
    def scatter(x_hbm, a_hbm, b_hbm, out_hbm, a_v, b_v, rows_v, sem):
        worker = lax.axis_index("s") * SC_CORES + lax.axis_index("c")
        base = worker * per_worker

        @pl.loop(0, per_worker // SC_CHUNK)
        def _(ci):
            off = base + ci * SC_CHUNK
            pltpu.sync_copy(a_hbm.at[pl.ds(off, SC_CHUNK)], a_v)
            pltpu.sync_copy(b_hbm.at[pl.ds(off, SC_CHUNK)], b_v)
            pltpu.sync_copy(x_hbm.at[pl.ds(off, SC_CHUNK)], rows_v)
            pltpu.async_copy(rows_v, out_hbm.at[a_v], sem).wait()
            pltpu.async_copy(rows_v, out_hbm.at[b_v], sem).wait()

    return scatter(x, slot_a, slot_b)


def _sc_row_gather(table, idx):
    rows = idx.shape[0]
    d = table.shape[1]
    per_worker = rows // (SC_CORES * SC_SUBCORES)
    mesh = plsc.VectorSubcoreMesh(core_axis_name="c", subcore_axis_name="s",
                                  num_cores=SC_CORES, num_subcores=SC_SUBCORES)

    @functools.partial(
        pl.kernel, mesh=mesh,
        out_type=jax.ShapeDtypeStruct((rows, d), table.dtype),
        scratch_types=[pltpu.VMEM((SC_CHUNK,), jnp.int32),
                       pltpu.VMEM((SC_CHUNK, d), table.dtype),
                       pltpu.SemaphoreType.DMA],
        name="moe_sc_row_gather")
    def gather(table_hbm, idx_hbm, out_hbm, idx_v, rows_v, sem):
        worker = lax.axis_index("s") * SC_CORES + lax.axis_index("c")
        base = worker * per_worker

        @pl.loop(0, per_worker // SC_CHUNK)
        def _(ci):
            off = base + ci * SC_CHUNK
            pltpu.sync_copy(idx_hbm.at[pl.ds(off, SC_CHUNK)], idx_v)
            pltpu.async_copy(table_hbm.at[idx_v], rows_v, sem).wait()
            pltpu.sync_copy(rows_v, out_hbm.at[pl.ds(off, SC_CHUNK)])

    return gather(table, idx)


def _expert_ffn_kernel(expert_ref, used_ref, x_ref, wg_ref, wu_ref, wd_ref, o_ref):
    i = pl.program_id(0)
    f = pl.program_id(1)

    @pl.when(f == 0)
    def _():
        o_ref[...] = jnp.zeros_like(o_ref)

    @pl.when(used_ref[i] > 0)
    def _():
        xb = x_ref[...].astype(BF16)
        g = _dot(xb, wg_ref[...])
        u = _dot(xb, wu_ref[...])
        h = (g * _sigmoid(g) * u).astype(BF16)
        o_ref[...] += _dot(h, wd_ref[...])


def _expert_ffn(xs, tile_expert, tile_used, wg, wu, wd, *, tf):
    slots, d = xs.shape
    ff = wg.shape[2]
    return pl.pallas_call(
        _expert_ffn_kernel,
        grid_spec=pltpu.PrefetchScalarGridSpec(
            num_scalar_prefetch=2, grid=(slots // MOE_TILE, ff // tf),
            in_specs=[pl.BlockSpec((MOE_TILE, d), lambda i, f, te, tu: (i, 0)),
                      pl.BlockSpec((None, d, tf), lambda i, f, te, tu: (te[i], 0, f)),
                      pl.BlockSpec((None, d, tf), lambda i, f, te, tu: (te[i], 0, f)),
                      pl.BlockSpec((None, tf, d), lambda i, f, te, tu: (te[i], f, 0))],
            out_specs=pl.BlockSpec((MOE_TILE, d), lambda i, f, te, tu: (i, 0))),
        out_shape=jax.ShapeDtypeStruct((slots, d), F32),
        compiler_params=_params("arbitrary", "arbitrary"),
        name="moe_expert_ffn",
    )(tile_expert, tile_used, xs, wg, wu, wd)


def _combine_kernel(x_ref, y1_ref, y2_ref, route_ref, lng_ref, lnb_ref, o_ref):
    route = route_ref[...]
    moe = route[:, 2:3] * y1_ref[...] + route[:, 3:4] * y2_ref[...]
    o_ref[...] = _layer_norm(ALPHA * x_ref[...] + moe, lng_ref[...], lnb_ref[...])


def _combine(x3, yt, route, ln_g, ln_b, *, tm):
    n, d = x3.shape
    nt = n // tm
    return pl.pallas_call(
        _combine_kernel,
        grid=(nt,),
        in_specs=[pl.BlockSpec((tm, d), lambda i: (i, 0)),
                  pl.BlockSpec((tm, d), lambda i: (i, 0)),
                  pl.BlockSpec((tm, d), lambda i: (i + nt, 0)),
                  pl.BlockSpec((tm, 128), lambda i: (i, 0)),
                  _full((1, d)), _full((1, d))],
        out_specs=pl.BlockSpec((tm, d), lambda i: (i, 0)),
        out_shape=jax.ShapeDtypeStruct((n, d), F32),
        compiler_params=_params("parallel"),
        name="moe_combine",
    )(x3, yt, yt, route, ln_g, ln_b)


def _moe(x3, route, pos, tot, wg, wu, wd, ln_g, ln_b, *, tm, tf):
    n, d = x3.shape
    ne = wg.shape[0]
    slots = TOP_K * n + ne * MOE_TILE
    count = tot[0, :ne].astype(jnp.int32)
    cap = (count + (MOE_TILE - 1)) // MOE_TILE * MOE_TILE
    ends = jnp.cumsum(cap)
    off = ends - cap
    e1 = route[:, 0].astype(jnp.int32)
    e2 = route[:, 1].astype(jnp.int32)
    rank = pos[:, :ne].astype(jnp.int32)
    slot1 = off[e1] + jnp.take_along_axis(rank, e1[:, None], axis=1)[:, 0]
    slot2 = off[e2] + jnp.take_along_axis(rank, e2[:, None], axis=1)[:, 0]
    tile_start = jnp.arange(slots // MOE_TILE, dtype=jnp.int32) * MOE_TILE
    tile_expert = jnp.minimum(jnp.searchsorted(ends, tile_start, side="right"),
                              ne - 1).astype(jnp.int32)
    tile_used = (tile_start < ends[-1]).astype(jnp.int32)

    xs = _sc_row_scatter(x3, slot1, slot2, slots)
    ys = _expert_ffn(xs, tile_expert, tile_used, wg, wu, wd, tf=tf)
    yt = _sc_row_gather(ys, jnp.concatenate([slot1, slot2]))
    return _combine(x3, yt, route, ln_g, ln_b, tm=tm)


def _pad_cols(w, width):
    return jnp.pad(w, ((0, 0), (0, width - w.shape[1])))


def _pad_rows(w, height):
    return jnp.pad(w, ((0, height - w.shape[0]), (0, 0)))


def _forward(x, mix_w_in, rwkv_mu, rwkv_w0, rwkv_w_up, rwkv_a0, rwkv_a_up, rwkv_g_up,
             rwkv_k_k, rwkv_k_a, rwkv_r_k, rwkv_gn_g, rwkv_gn_b, fox_b_f, mix_w_out,
             mix_ln_g, mix_ln_b, ffn_w_gate, ffn_w_up, ffn_w_down, ffn_ln_g, ffn_ln_b,
             conv_w_pw1, conv_b_pw1, conv_w_dw, conv_b_dw, conv_ln_g, conv_ln_b,
             conv_w_pw2, conv_b_pw2, conv_post_ln_g, conv_post_ln_b,
             moe_w_router, moe_w_gate, moe_w_up, moe_w_down, moe_ln_g, moe_ln_b,
             *, tm=512, chunk=64, nb_rwkv=4, tq=512, tf_ffn=1408, tf_moe=1792):
    batch, seq, d = x.shape
    n = batch * seq
    gw = GROUP_W
    x2 = x.reshape(n, d)
    row = lambda t: t.reshape(1, -1)

    w_in = mix_w_in[0]
    mu = rwkv_mu[0]
    o_w, o_a, o_g = 3 * gw, 3 * gw + DECAY_LORA, 3 * gw + DECAY_LORA + AAA_LORA
    o_fox = o_g + GATE_LORA

    def lora_layout(t):
        return jnp.concatenate([t[..., :o_w],
                                _pad_cols(t[..., o_w:o_a], LORA_PAD),
                                _pad_cols(t[..., o_a:o_g], LORA_PAD),
                                _pad_cols(t[..., o_g:o_fox], LORA_PAD)], axis=-1)

    wa = lora_layout(w_in).astype(BF16)
    mu_a = lora_layout(row(mu))
    scale = LOG2E / math.sqrt(HEAD_DIM)
    wb = jnp.concatenate([w_in[:, o_fox:o_fox + gw] * scale,
                          w_in[:, o_fox + gw:o_fox + 3 * gw]], axis=1).astype(BF16)
    wf = _pad_cols(w_in[:, o_fox + 3 * gw:], 128).astype(BF16)
    bf = _pad_cols(row(fox_b_f[0]), 128)

    pr, qk, vt, c = _inproj(x2, wa, wb, wf, mu_a, bf, seq=seq, tm=tq)

    wup = _pad_rows(rwkv_w_up[0], LORA_PAD).astype(BF16)
    aup = _pad_rows(rwkv_a_up[0], LORA_PAD).astype(BF16)
    gup = _pad_rows(rwkv_g_up[0], LORA_PAD).astype(BF16)
    k_k, k_a, r_k = row(rwkv_k_k[0]), row(rwkv_k_a[0]), row(rwkv_r_k[0])
    yr = _rwkv(pr, row(rwkv_w0[0]), wup, row(rwkv_a0[0]), aup, k_k, k_a,
               batch=batch, seq=seq, chunk=chunk, nb=nb_rwkv)

    yf = _fox(qk, vt, c, batch=batch, seq=seq, tq=tq)

    w_out = mix_w_out[0].astype(BF16)
    x1 = _mixout(x2, pr, yr, yf, row(rwkv_a0[0]), aup, gup, k_a, r_k,
                 row(rwkv_gn_g[0]), row(rwkv_gn_b[0]), w_out[:gw], w_out[gw:],
                 row(mix_ln_g[0]), row(mix_ln_b[0]), tm=tm)
    x2b = _ffn(x1, ffn_w_gate[0].astype(BF16), ffn_w_up[0].astype(BF16),
               ffn_w_down[0].astype(BF16), row(ffn_ln_g[0]), row(ffn_ln_b[0]), tm=tm, tf=tf_ffn)

    hg = _glu(x2b, conv_w_pw1[0].astype(BF16), row(conv_b_pw1[0]), tm=tm)
    w_router = _pad_cols(moe_w_router[0], 128)
    wr_hi = w_router.astype(BF16)
    w_router = jnp.concatenate([wr_hi, (w_router - wr_hi.astype(F32)).astype(BF16)], axis=1)
    x3, route, pos, tot = _conv(hg, x2b, _pad_rows(conv_w_dw[0], CONV_HALO), row(conv_b_dw[0]),
                                row(conv_ln_g[0]), row(conv_ln_b[0]),
                                conv_w_pw2[0].astype(BF16), row(conv_b_pw2[0]),
                                row(conv_post_ln_g[0]), row(conv_post_ln_b[0]),
                                w_router, seq=seq, tm=tm)
    out = _moe(x3, route, pos, tot, moe_w_gate[0].astype(BF16), moe_w_up[0].astype(BF16),
               moe_w_down[0].astype(BF16), row(moe_ln_g[0]), row(moe_ln_b[0]),
               tm=tm, tf=tf_moe)
    return out.reshape(batch, seq, d)


def kernel(x, mix_w_in, rwkv_mu, rwkv_w0, rwkv_w_up, rwkv_a0, rwkv_a_up, rwkv_g_up, rwkv_k_k, rwkv_k_a, rwkv_r_k, rwkv_gn_g, rwkv_gn_b, fox_b_f, mix_w_out, mix_ln_g, mix_ln_b, ffn_w_gate, ffn_w_up, ffn_w_down, ffn_ln_g, ffn_ln_b, conv_w_pw1, conv_b_pw1, conv_w_dw, conv_b_dw, conv_ln_g, conv_ln_b, conv_w_pw2, conv_b_pw2, conv_post_ln_g, conv_post_ln_b, moe_w_router, moe_w_gate, moe_w_up, moe_w_down, moe_ln_g, moe_ln_b):
    return _forward(x, mix_w_in, rwkv_mu, rwkv_w0, rwkv_w_up, rwkv_a0, rwkv_a_up, rwkv_g_up,
                    rwkv_k_k, rwkv_k_a, rwkv_r_k, rwkv_gn_g, rwkv_gn_b, fox_b_f, mix_w_out,
                    mix_ln_g, mix_ln_b, ffn_w_gate, ffn_w_up, ffn_w_down, ffn_ln_g, ffn_ln_b,
                    conv_w_pw1, conv_b_pw1, conv_w_dw, conv_b_dw, conv_ln_g, conv_ln_b,
                    conv_w_pw2, conv_b_pw2, conv_post_ln_g, conv_post_ln_b,
                    moe_w_router, moe_w_gate, moe_w_up, moe_w_down, moe_ln_g, moe_ln_b)
```

```python
import functools
import math

import jax
import jax.numpy as jnp
from jax import lax
from jax.experimental import pallas as pl
from jax.experimental.pallas import tpu as pltpu
from jax.experimental.pallas import tpu_sc as plsc

F32 = jnp.float32
BF16 = jnp.bfloat16
HIGHEST = lax.Precision.HIGHEST

HEAD_DIM = 64
N_HEADS = 8
GROUP_W = N_HEADS * HEAD_DIM
LORA_PAD = 128
DECAY_LORA = 32
AAA_LORA = 32
GATE_LORA = 96
CONV_WIDTH = 31
CONV_HALO = 32
SUBLANES = 8
N_EXPERTS = 8
TOP_K = 2
LN_EPS = 1e-5
GN_EPS = 64e-5
DEPTH = 2
ALPHA = (2.0 * DEPTH) ** 0.25
NEG_BIG = -1e30
LOG2E = math.log2(math.e)
VMEM_LIMIT = 56 * 1024 * 1024


def _dot(a, b, **kw):
    return jnp.dot(a, b, preferred_element_type=F32, **kw)


def _dot_nt(a, b):
    return lax.dot_general(a, b, (((1,), (1,)), ((), ())), preferred_element_type=F32)


def _dot_tn(a, b):
    return lax.dot_general(a, b, (((0,), (0,)), ((), ())), preferred_element_type=F32)


def _dot_exact_lhs(a, v):
    hi = v.astype(BF16)
    rem = v - hi.astype(F32)
    mid = rem.astype(BF16)
    lo = (rem - mid.astype(F32)).astype(BF16)
    w = v.shape[1]
    out = _dot(a, jnp.concatenate([hi, mid, lo], axis=1))
    return out[:, :w] + out[:, w:2 * w] + out[:, 2 * w:]


def _sigmoid(z):
    return 1.0 / (1.0 + jnp.exp(-z))


def _softplus(z):
    return jnp.maximum(z, 0.0) + jnp.log1p(jnp.exp(-jnp.abs(z)))


def _layer_norm(h, g, b):
    mu = jnp.mean(h, axis=-1, keepdims=True)
    d = h - mu
    var = jnp.mean(d * d, axis=-1, keepdims=True)
    return d * lax.rsqrt(var + LN_EPS) * g + b


def _params(*sem):
    return pltpu.CompilerParams(dimension_semantics=sem, vmem_limit_bytes=VMEM_LIMIT)


def _full(shape):
    return pl.BlockSpec(shape, lambda *_: (0,) * len(shape))


def _inproj_kernel(x_ref, wa_ref, wb_ref, wf_ref, mu_ref, bf_ref, tri_ref,
                   pr_ref, qk_ref, vt_ref, c_ref, last_ref, carry_ref, *, tiles_per_seq):
    i = pl.program_id(0)

    @pl.when(i % tiles_per_seq == 0)
    def _():
        last_ref[...] = jnp.zeros_like(last_ref)
        carry_ref[...] = jnp.zeros_like(carry_ref)

    xb = x_ref[...].astype(BF16)
    tm = xb.shape[0]
    row0 = lax.broadcasted_iota(jnp.int32, (tm, 1), 0) == 0
    ca = wa_ref.shape[1]
    for c0 in range(0, ca, GROUP_W):
        cw = min(GROUP_W, ca - c0)
        p = _dot(xb, wa_ref[:, c0:c0 + cw])
        prev = jnp.where(row0, last_ref[:, c0:c0 + cw], pltpu.roll(p, 1, 0))
        last_ref[:, c0:c0 + cw] = p[tm - 1:tm, :]
        pr_ref[:, c0:c0 + cw] = p + mu_ref[:, c0:c0 + cw] * (prev - p)
    for c0 in range(0, 2 * GROUP_W, GROUP_W):
        qk_ref[:, c0:c0 + GROUP_W] = _dot(xb, wb_ref[:, c0:c0 + GROUP_W]).astype(BF16)
    vt_ref[...] = _dot(xb, wb_ref[:, 2 * GROUP_W:3 * GROUP_W]).T.astype(BF16)
    fl = _dot(xb, wf_ref[...]) + bf_ref[...]
    log_f = jnp.minimum(fl, 0.0) - jnp.log1p(jnp.exp(-jnp.abs(fl)))
    c = _dot_exact_lhs(tri_ref[...], log_f) + carry_ref[...]
    c_ref[...] = c
    carry_ref[...] = c[tm - 1:tm, :]


def _inproj(x2, wa, wb, wf, mu, bf, *, seq, tm):
    n, d = x2.shape
    ca, cb = wa.shape[1], wb.shape[1]
    tps = seq // tm
    tri = (lax.broadcasted_iota(jnp.int32, (tm, tm), 1)
           <= lax.broadcasted_iota(jnp.int32, (tm, tm), 0)).astype(BF16)
    return pl.pallas_call(
        functools.partial(_inproj_kernel, tiles_per_seq=tps),
        grid=(n // tm,),
        in_specs=[pl.BlockSpec((tm, d), lambda i: (i, 0)),
                  _full((d, ca)), _full((d, cb)), _full((d, 128)),
                  _full((1, ca)), _full((1, 128)), _full((tm, tm))],
        out_specs=[pl.BlockSpec((tm, ca), lambda i: (i, 0)),
                   pl.BlockSpec((tm, 2 * GROUP_W), lambda i: (i, 0)),
                   pl.BlockSpec((None, None, GROUP_W, tm), lambda i: (i // tps, i % tps, 0, 0)),
                   pl.BlockSpec((tm, 128), lambda i: (i, 0))],
        out_shape=[jax.ShapeDtypeStruct((n, ca), F32),
                   jax.ShapeDtypeStruct((n, 2 * GROUP_W), BF16),
                   jax.ShapeDtypeStruct((n // seq, tps, GROUP_W, tm), BF16),
                   jax.ShapeDtypeStruct((n, 128), F32)],
        scratch_shapes=[pltpu.VMEM((1, ca), F32), pltpu.VMEM((1, 128), F32)],
        compiler_params=_params("arbitrary"),
        name="inproj",
    )(x2, wa, wb, wf, mu, bf, tri)


def _rwkv_kernel(pr_ref, w0_ref, wup_ref, a0_ref, aup_ref, kk_ref, ka_ref,
                 gsum_ref, tri_ref, y_ref, h_ref, *, chunk, nb):
    @pl.when(pl.program_id(1) == 0)
    def _():
        h_ref[...] = jnp.zeros_like(h_ref)

    gw = GROUP_W
    pw = 2 * HEAD_DIM
    npair = N_HEADS // 2
    rows = 2 * chunk
    log_chunk = int(math.log2(chunk))
    head0 = lax.broadcasted_iota(jnp.int32, (1, pw), 1) < HEAD_DIM
    row = lax.broadcasted_iota(jnp.int32, (rows, rows), 0)
    col = lax.broadcasted_iota(jnp.int32, (rows, rows), 1)
    strict = (col & (chunk - 1)) < (row & (chunk - 1))
    incl = (col & (chunk - 1)) <= (row & (chunk - 1))
    eye = (col == row).astype(F32)
    peye = (lax.broadcasted_iota(jnp.int32, (pw, pw), 0)
            == lax.broadcasted_iota(jnp.int32, (pw, pw), 1))

    def stack(x):
        return jnp.concatenate([jnp.where(head0, x, 0.0), jnp.where(head0, 0.0, x)],
                               axis=0).astype(BF16)

    units = [(b, j) for b in range(nb) for j in range(npair)]
    nu = len(units)
    ar, bk, vs, bhs, khs, pcs = [], [], [], [], [], []
    for b in range(nb):
        r = pr_ref[b, :, 0:gw]
        k = pr_ref[b, :, gw:2 * gw]
        v = pr_ref[b, :, 2 * gw:3 * gw]
        wd = pr_ref[b, :, 3 * gw:3 * gw + LORA_PAD]
        ad = pr_ref[b, :, 3 * gw + LORA_PAD:3 * gw + 2 * LORA_PAD]
        w_pre = w0_ref[...] + _dot(jnp.tanh(wd).astype(BF16), wup_ref[...])
        w = -_softplus(-w_pre) - 0.5
        log_decay = -jnp.exp(w)
        a = _sigmoid(a0_ref[...] + _dot(ad.astype(BF16), aup_ref[...]))
        kk = k * kk_ref[...]
        norm = jnp.sqrt(_dot((kk * kk).astype(BF16), gsum_ref[...]))
        kk = kk / jnp.maximum(norm, 1e-12)
        k_mod = k * (1.0 + (a - 1.0) * ka_ref[...])
        b_vec = kk * a
        cum = _dot_exact_lhs(tri_ref[...], log_decay)
        last = cum[chunk - 1:chunk, :]
        p_inv = jnp.exp(-cum)
        p_tail = jnp.exp(last - cum)
        a_t = -kk * jnp.exp(cum - log_decay)
        r_t = r * jnp.exp(cum)
        b_t = b_vec * p_inv
        k_t = k_mod * p_inv
        b_h = b_vec * p_tail
        k_h = k_mod * p_tail
        p_last = jnp.exp(last)
        for j in range(npair):
            sl = slice(j * pw, (j + 1) * pw)
            ar.append(jnp.concatenate([stack(a_t[:, sl]), stack(r_t[:, sl])], axis=0))
            bk.append(jnp.concatenate([stack(b_t[:, sl]), stack(k_t[:, sl])], axis=0))
            vs.append(stack(v[:, sl]))
            bhs.append(stack(b_h[:, sl]))
            khs.append(stack(k_h[:, sl]))
            pcs.append(jnp.sum(jnp.where(peye, p_last[:, sl], 0.0), axis=1, keepdims=True))

    gram = [_dot_nt(ar[u], bk[u]) for u in range(nu)]
    l_ab = [jnp.where(strict, gram[u][:rows, :rows], 0.0) for u in range(nu)]
    l_akv = [_dot(jnp.where(strict, gram[u][:rows, rows:], 0.0).astype(BF16), vs[u])
             for u in range(nu)]
    m_rb = [jnp.where(incl, gram[u][rows:, :rows], 0.0).astype(BF16) for u in range(nu)]
    m_rkv = [_dot(jnp.where(incl, gram[u][rows:, rows:], 0.0).astype(BF16), vs[u])
             for u in range(nu)]
    t_inv = [eye + l_ab[u] for u in range(nu)]
    xb = [l_ab[u].astype(BF16) for u in range(nu)]
    xp = [_dot(xb[u], xb[u]) for u in range(nu)]
    for step in range(log_chunk - 1):
        xb = [xp[u].astype(BF16) for u in range(nu)]
        if step < log_chunk - 2:
            both = [_dot(jnp.concatenate([t_inv[u].astype(BF16), xb[u]], axis=0), xb[u])
                    for u in range(nu)]
            t_inv = [t_inv[u] + both[u][:rows] for u in range(nu)]
            xp = [both[u][rows:] for u in range(nu)]
        else:
            t_inv = [t_inv[u] + _dot(t_inv[u].astype(BF16), xb[u]) for u in range(nu)]
    tw = [_dot(t_inv[u].astype(BF16),
               jnp.concatenate([ar[u][:rows], l_akv[u].astype(BF16)], axis=1)).astype(BF16)
          for u in range(nu)]
    mw = [_dot(m_rb[u], tw[u]) for u in range(nu)]
    bw = [_dot_tn(bhs[u], tw[u]) for u in range(nu)]
    kv = [_dot_tn(khs[u], vs[u]) for u in range(nu)]
    for u, (b, j) in enumerate(units):
        wy = ar[u][rows:].astype(F32) + mw[u][:, :pw]
        yc = mw[u][:, pw:] + m_rkv[u]
        hf = h_ref[u]
        yh = _dot(jnp.concatenate([wy.astype(BF16), bw[u][:, :pw].astype(BF16)], axis=0),
                  hf.astype(BF16))
        h_ref[u] = pcs[u] * hf + yh[rows:] + bw[u][:, pw:] + kv[u]
        ys = yh[:rows] + yc
        y_ref[b, :, j * pw:(j + 1) * pw] = ys[:chunk] + ys[chunk:]


def _rwkv(pr, w0, wup, a0, aup, k_k, k_a, *, batch, seq, chunk, nb):
    n, ca = pr.shape
    nch = seq // chunk
    gidx = lax.broadcasted_iota(jnp.int32, (GROUP_W, GROUP_W), 0) // HEAD_DIM
    gsum = (gidx == gidx.T).astype(BF16)
    tri = (lax.broadcasted_iota(jnp.int32, (chunk, chunk), 1)
           <= lax.broadcasted_iota(jnp.int32, (chunk, chunk), 0)).astype(BF16)
    y = pl.pallas_call(
        functools.partial(_rwkv_kernel, chunk=chunk, nb=nb),
        grid=(batch // nb, nch),
        in_specs=[pl.BlockSpec((nb, chunk, ca), lambda g, c: (g, c, 0)),
                  _full((1, GROUP_W)), _full((LORA_PAD, GROUP_W)),
                  _full((1, GROUP_W)), _full((LORA_PAD, GROUP_W)),
                  _full((1, GROUP_W)), _full((1, GROUP_W)),
                  _full((GROUP_W, GROUP_W)), _full((chunk, chunk))],
        out_specs=pl.BlockSpec((nb, chunk, GROUP_W), lambda g, c: (g, c, 0)),
        out_shape=jax.ShapeDtypeStruct((batch, seq, GROUP_W), F32),
        scratch_shapes=[pltpu.VMEM((nb * (N_HEADS // 2), 2 * HEAD_DIM, 2 * HEAD_DIM), F32)],
        compiler_params=_params("arbitrary", "arbitrary"),
        name="rwkv_scan",
    )(pr.reshape(batch, seq, ca), w0, wup, a0, aup, k_k, k_a, gsum, tri)
    return y.reshape(n, GROUP_W)


def _fox_kernel(q_ref, k_ref, vt_ref, c_ref, o_ref, acc_ref, m_ref, l_ref, kb_ref,
                sa_ref, sb_ref, *, tq):
    j = pl.program_id(1)
    qi = pl.program_id(2)
    pw = 2 * HEAD_DIM
    seq = k_ref.shape[0]
    lane = lax.broadcasted_iota(jnp.int32, (1, pw), 1)

    first = lane < HEAD_DIM
    bias_lane = (HEAD_DIM, 0)

    @pl.when(qi == 0)
    def _():
        def fill(rb, carry):
            rs = pl.multiple_of(rb * tq, tq)
            cblk = c_ref[pl.ds(rs, tq), :]
            for hh in range(2):
                bias = -LOG2E * jnp.sum(jnp.where(lane == 2 * j + hh, cblk, 0.0),
                                        axis=1, keepdims=True)
                b_hi = bias.astype(BF16).astype(F32)
                b_mid = (bias - b_hi).astype(BF16).astype(F32)
                b_lo = bias - b_hi - b_mid
                l0 = bias_lane[hh]
                kb_ref[hh, pl.ds(rs, tq), :] = jnp.where(
                    lane == l0, b_hi, jnp.where(lane == l0 + 1, b_mid,
                                                jnp.where(lane == l0 + 2, b_lo, 0.0))).astype(BF16)
            return carry
        lax.fori_loop(0, seq // tq, fill, 0)

    q = q_ref[...]
    ones3 = [jnp.where(jnp.logical_and(lane >= l0, lane < l0 + 3), 1.0, 0.0).astype(BF16)
             for l0 in bias_lane]
    own = (first, jnp.logical_not(first))
    qh = tuple(jnp.where(own[hh], q, ones3[hh]) for hh in range(2))
    acc_ref[...] = jnp.zeros_like(acc_ref)
    m_ref[...] = jnp.full_like(m_ref, NEG_BIG)
    l_ref[...] = jnp.zeros_like(l_ref)
    causal = (lax.broadcasted_iota(jnp.int32, (tq, tq), 0)
              <= lax.broadcasted_iota(jnp.int32, (tq, tq), 1))
    top = lax.broadcasted_iota(jnp.int32, (pw, 1), 0) < HEAD_DIM

    def scores(kb, s_ref):
        ks = pl.multiple_of(kb * tq, tq)
        kblk = k_ref[pl.ds(ks, tq), :]
        for hh in range(2):
            k_aug = jnp.where(own[hh], kblk, kb_ref[hh, pl.ds(ks, tq), :])
            s_ref[hh] = _dot_nt(k_aug, qh[hh])

    def softmax_pv(kb, s_ref, masked):
        vt = vt_ref[kb].astype(F32)
        vts = (jnp.where(top, vt, 1.0).astype(BF16), jnp.where(top, 1.0, vt).astype(BF16))
        alphas, pvs = [], []
        for hh in range(2):
            z = s_ref[hh]
            if masked:
                z = jnp.where(causal, z, NEG_BIG)
            m_prev = m_ref[hh]
            m_new = jnp.maximum(m_prev, jnp.max(z, axis=0, keepdims=True))
            alpha = jnp.exp2(m_prev - m_new)
            p = jnp.exp2(z - m_new)
            pv = _dot(vts[hh], p.astype(BF16))
            ones_row = (1 - hh) * HEAD_DIM
            l_ref[hh] = alpha * l_ref[hh] + pv[ones_row:ones_row + 1, :]
            m_ref[hh] = m_new
            alphas.append(alpha)
            pvs.append(pv)
        acc_ref[...] = (acc_ref[...] * jnp.where(top, alphas[0], alphas[1])
                        + jnp.where(top, pvs[0], pvs[1]))

    scores(0, sa_ref)

    def body(i, carry):
        scores(2 * i + 1, sb_ref)
        softmax_pv(2 * i, sa_ref, False)
        scores(2 * i + 2, sa_ref)
        softmax_pv(2 * i + 1, sb_ref, False)
        return carry

    lax.fori_loop(0, qi // 2, body, 0)

    @pl.when(qi % 2 == 0)
    def _():
        softmax_pv(qi, sa_ref, True)

    @pl.when(qi % 2 == 1)
    def _():
        scores(qi, sb_ref)
        softmax_pv(qi - 1, sa_ref, False)
        softmax_pv(qi, sb_ref, True)

    out_t = acc_ref[...] / jnp.where(top, l_ref[0], l_ref[1])
    o_ref[...] = out_t.T.astype(BF16)


def _fox(qk, vt, c, *, batch, seq, tq):
    n = qk.shape[0]
    nq = seq // tq
    npair = N_HEADS // 2
    pw = 2 * HEAD_DIM
    assert vt.shape == (batch, nq, GROUP_W, tq)
    return pl.pallas_call(
        functools.partial(_fox_kernel, tq=tq),
        grid=(batch, npair, nq),
        in_specs=[pl.BlockSpec((tq, pw), lambda b, j, i: (b * nq + i, j)),
                  pl.BlockSpec((seq, pw), lambda b, j, i: (b, npair + j)),
                  pl.BlockSpec((None, nq, pw, tq), lambda b, j, i: (b, 0, j, 0)),
                  pl.BlockSpec((seq, 128), lambda b, j, i: (b, 0))],
        out_specs=pl.BlockSpec((tq, pw), lambda b, j, i: (b * nq + i, j)),
        out_shape=jax.ShapeDtypeStruct((n, GROUP_W), BF16),
        scratch_shapes=[pltpu.VMEM((pw, tq), F32),
                        pltpu.VMEM((2, 1, tq), F32), pltpu.VMEM((2, 1, tq), F32),
                        pltpu.VMEM((2, seq, pw), BF16),
                        pltpu.VMEM((2, tq, tq), F32), pltpu.VMEM((2, tq, tq), F32)],
        compiler_params=_params("arbitrary", "arbitrary", "arbitrary"),
        name="fox_attention",
    )(qk, qk, vt, c)


def _mixout_kernel(x_ref, pr_ref, yr_ref, yf_ref, a0_ref, aup_ref, gup_ref, ka_ref,
                   rk_ref, gng_ref, gnb_ref, gsum_ref, wr_ref, wf_ref, lng_ref, lnb_ref,
                   o_ref):
    gw = GROUP_W
    r = pr_ref[:, 0:gw]
    k = pr_ref[:, gw:2 * gw]
    v = pr_ref[:, 2 * gw:3 * gw]
    ad = pr_ref[:, 3 * gw + LORA_PAD:3 * gw + 2 * LORA_PAD]
    gd = pr_ref[:, 3 * gw + 2 * LORA_PAD:3 * gw + 3 * LORA_PAD]
    a = _sigmoid(a0_ref[...] + _dot(ad.astype(BF16), aup_ref[...]))
    k_mod = k * (1.0 + (a - 1.0) * ka_ref[...])
    gate = _dot(_sigmoid(gd).astype(BF16), gup_ref[...])
    gsum = gsum_ref[...]

    def group_sum(t):
        return _dot(t.astype(BF16), gsum)

    y = yr_ref[...]
    y_hi = y.astype(BF16)
    y_lo = (y - y_hi.astype(F32)).astype(BF16)
    mean = (_dot(y_hi, gsum) + _dot(y_lo, gsum)) * (1.0 / HEAD_DIM)
    d = y - mean
    var = group_sum(d * d) * (1.0 / HEAD_DIM)
    yn = d * lax.rsqrt(var + GN_EPS) * gng_ref[...] + gnb_ref[...]
    bonus = group_sum(r * k_mod * rk_ref[...])
    y_rwkv = ((yn + bonus * v) * gate).astype(BF16)
    mixed = _dot(y_rwkv, wr_ref[...]) + _dot(yf_ref[...], wf_ref[...])
    o_ref[...] = _layer_norm(ALPHA * x_ref[...] + mixed, lng_ref[...], lnb_ref[...])


def _mixout(x2, pr, yr, yf, a0, aup, gup, k_a, r_k, gn_g, gn_b, w_r, w_f, ln_g, ln_b, *, tm):
    n, d = x2.shape
    ca = pr.shape[1]
    gidx = lax.broadcasted_iota(jnp.int32, (GROUP_W, GROUP_W), 0) // HEAD_DIM
    gsum = (gidx == gidx.T).astype(BF16)
    vec = _full((1, GROUP_W))
    return pl.pallas_call(
        _mixout_kernel,
        grid=(n // tm,),
        in_specs=[pl.BlockSpec((tm, d), lambda i: (i, 0)),
                  pl.BlockSpec((tm, ca), lambda i: (i, 0)),
                  pl.BlockSpec((tm, GROUP_W), lambda i: (i, 0)),
                  pl.BlockSpec((tm, GROUP_W), lambda i: (i, 0)),
                  vec, _full((LORA_PAD, GROUP_W)), _full((LORA_PAD, GROUP_W)),
                  vec, vec, vec, vec, _full((GROUP_W, GROUP_W)),
                  _full((GROUP_W, d)), _full((GROUP_W, d)),
                  _full((1, d)), _full((1, d))],
        out_specs=pl.BlockSpec((tm, d), lambda i: (i, 0)),
        out_shape=jax.ShapeDtypeStruct((n, d), F32),
        compiler_params=_params("parallel"),
        name="mix_out",
    )(x2, pr, yr, yf, a0, aup, gup, k_a, r_k, gn_g, gn_b, gsum, w_r, w_f, ln_g, ln_b)


def _ffn_kernel(x_ref, wg_ref, wu_ref, wd_ref, lng_ref, lnb_ref, o_ref, *, tf):
    xb = x_ref[...].astype(BF16)
    ff = wg_ref.shape[1]
    acc = None
    for f0 in range(0, ff, tf):
        f1 = min(f0 + tf, ff)
        g = _dot(xb, wg_ref[:, f0:f1])
        u = _dot(xb, wu_ref[:, f0:f1])
        h = (g * _sigmoid(g) * u).astype(BF16)
        part = _dot(h, wd_ref[f0:f1, :])
        acc = part if acc is None else acc + part
    o_ref[...] = _layer_norm(ALPHA * x_ref[...] + acc, lng_ref[...], lnb_ref[...])


def _ffn(x2, wg, wu, wd, ln_g, ln_b, *, tm, tf):
    n, d = x2.shape
    ff = wg.shape[1]
    once = pl.Buffered(1)

    def resident(shape):
        return pl.BlockSpec(shape, lambda i: (0, 0), pipeline_mode=once)

    return pl.pallas_call(
        functools.partial(_ffn_kernel, tf=tf),
        grid=(n // tm,),
        in_specs=[pl.BlockSpec((tm, d), lambda i: (i, 0)),
                  resident((d, ff)), resident((d, ff)), resident((ff, d)),
                  _full((1, d)), _full((1, d))],
        out_specs=pl.BlockSpec((tm, d), lambda i: (i, 0)),
        out_shape=jax.ShapeDtypeStruct((n, d), F32),
        compiler_params=_params("parallel"),
        name="ffn_swiglu",
    )(x2, wg, wu, wd, ln_g, ln_b)


def _glu_kernel(x_ref, w_ref, b_ref, o_ref):
    d = o_ref.shape[1]
    xb = x_ref[...].astype(BF16)
    val = _dot(xb, w_ref[:, 0:d]) + b_ref[:, 0:d]
    gat = _dot(xb, w_ref[:, d:2 * d]) + b_ref[:, d:2 * d]
    o_ref[...] = val * _sigmoid(gat)


def _glu(x2, w, b, *, tm):
    n, d = x2.shape
    return pl.pallas_call(
        _glu_kernel,
        grid=(n // tm,),
        in_specs=[pl.BlockSpec((tm, d), lambda i: (i, 0)), _full((d, 2 * d)), _full((1, 2 * d))],
        out_specs=pl.BlockSpec((tm, d), lambda i: (i, 0)),
        out_shape=jax.ShapeDtypeStruct((n, d), F32),
        compiler_params=_params("parallel"),
        name="conv_glu",
    )(x2, w, b)


def _top2(logits):
    lane = lax.broadcasted_iota(jnp.int32, logits.shape, 1).astype(F32)
    lg = jnp.where(lane < N_EXPERTS, logits, NEG_BIG)
    m1 = jnp.max(lg, axis=-1, keepdims=True)
    i1 = jnp.min(jnp.where(lg == m1, lane, 128.0), axis=-1, keepdims=True)
    lg2 = jnp.where(lane == i1, NEG_BIG, lg)
    m2 = jnp.max(lg2, axis=-1, keepdims=True)
    i2 = jnp.min(jnp.where(lg2 == m2, lane, 128.0), axis=-1, keepdims=True)
    e2 = jnp.exp(m2 - m1)
    w1 = 1.0 / (1.0 + e2)
    w2 = e2 / (1.0 + e2)
    return lane, i1, i2, w1, w2


def _conv_kernel(hc_ref, hp_ref, x_ref, wdw_ref, bdw_ref, lng_ref, lnb_ref, w2_ref, b2_ref,
                 pg_ref, pb_ref, wr_ref, tri_ref, x3_ref, route_ref, tot_ref,
                 ext_ref, cv_ref, cnt_ref, *, tiles_per_seq):
    tm, d = x_ref.shape

    @pl.when(pl.program_id(0) == 0)
    def _():
        cnt_ref[...] = jnp.zeros_like(cnt_ref)

    first = pl.program_id(0) % tiles_per_seq == 0
    ext_ref[0, 0:CONV_HALO, :] = jnp.where(first, 0.0, hp_ref[...])
    ext_ref[0, CONV_HALO:CONV_HALO + tm, :] = hc_ref[...]
    nrows = tm + CONV_HALO
    for c0 in range(0, d, 256):
        base = ext_ref[0, :, c0:c0 + 256]
        for j in range(1, SUBLANES):
            ext_ref[j, :, c0:c0 + 256] = pltpu.roll(base, nrows - j, 0)
    off = CONV_HALO - (CONV_WIDTH - 1)
    rc, cc = 64, 256
    for r0 in range(0, tm, rc):
        for c0 in range(0, d, cc):
            acc = jnp.broadcast_to(bdw_ref[:, c0:c0 + cc], (rc, cc))
            for t in range(CONV_WIDTH):
                base, j = divmod(off + t, SUBLANES)
                rs = r0 + base * SUBLANES
                acc = acc + wdw_ref[t:t + 1, c0:c0 + cc] * ext_ref[j, rs:rs + rc, c0:c0 + cc]
            cv_ref[r0:r0 + rc, c0:c0 + cc] = acc
    hn = _layer_norm(cv_ref[...], lng_ref[...], lnb_ref[...])
    hs = (hn * _sigmoid(hn)).astype(BF16)
    conv = _dot(hs, w2_ref[...]) + b2_ref[...]
    x3 = _layer_norm(ALPHA * x_ref[...] + conv, pg_ref[...], pb_ref[...])
    x3_ref[...] = x3
    x_hi = x3.astype(BF16)
    x_lo = (x3 - x_hi.astype(F32)).astype(BF16)
    hi_part = _dot(x_hi, wr_ref[...])
    logits = hi_part[:, :128] + hi_part[:, 128:] + _dot(x_lo, wr_ref[:, 0:128])
    lane, i1, i2, w1, w2 = _top2(logits)
    first, second = lane == i1, lane == i2
    sel = jnp.where(jnp.logical_or(first, second), 1.0, 0.0)
    pos = _dot(tri_ref[...], sel.astype(BF16)) + cnt_ref[...]
    cnt_ref[...] += jnp.sum(sel, axis=0, keepdims=True)
    tot_ref[...] = jnp.broadcast_to(cnt_ref[...], tot_ref.shape)
    rank1 = jnp.sum(jnp.where(first, pos, 0.0), axis=-1, keepdims=True)
    rank2 = jnp.sum(jnp.where(second, pos, 0.0), axis=-1, keepdims=True)
    fields = (i1, i2, w1, w2, rank1, rank2)
    record = jnp.zeros_like(logits)
    for k, field in enumerate(fields):
        record = jnp.where(lane == float(k), field, record)
    route_ref[...] = record


def _conv(hg, x2, w_dw, b_dw, ln_g, ln_b, w2, b2, pg, pb, w_router, *, seq, tm):
    n, d = x2.shape
    ratio = tm // CONV_HALO
    vec = _full((1, d))
    tri = (lax.broadcasted_iota(jnp.int32, (tm, tm), 1)
           < lax.broadcasted_iota(jnp.int32, (tm, tm), 0)).astype(BF16)
    return pl.pallas_call(
        functools.partial(_conv_kernel, tiles_per_seq=seq // tm),
        grid=(n // tm,),
        in_specs=[pl.BlockSpec((tm, d), lambda i: (i, 0)),
                  pl.BlockSpec((CONV_HALO, d), lambda i: (jnp.maximum(i * ratio - 1, 0), 0)),
                  pl.BlockSpec((tm, d), lambda i: (i, 0)),
                  _full((CONV_HALO, d)), vec, vec, vec, _full((d, d)), vec, vec, vec,
                  _full((d, 256)), _full((tm, tm))],
        out_specs=[pl.BlockSpec((tm, d), lambda i: (i, 0)),
                   pl.BlockSpec((tm, 128), lambda i: (i, 0)),
                   _full((SUBLANES, 128))],
        out_shape=[jax.ShapeDtypeStruct((n, d), F32),
                   jax.ShapeDtypeStruct((n, 128), F32),
                   jax.ShapeDtypeStruct((SUBLANES, 128), F32)],
        scratch_shapes=[pltpu.VMEM((SUBLANES, tm + CONV_HALO, d), F32), pltpu.VMEM((tm, d), F32),
                        pltpu.VMEM((1, 128), F32)],
        compiler_params=_params("arbitrary"),
        name="conv_module",
    )(hg, hg, x2, w_dw, b_dw, ln_g, ln_b, w2, b2, pg, pb, w_router, tri)


MOE_TILE = 512
SC_CORES = 2
SC_SUBCORES = 16
SC_CHUNK = 64


def _sc_row_scatter(x, slot_a, slot_b, slots):
    n, d = x.shape
    per_worker = n // (SC_CORES * SC_SUBCORES)
    mesh = plsc.VectorSubcoreMesh(core_axis_name="c", subcore_axis_name="s",
                                  num_cores=SC_CORES, num_subcores=SC_SUBCORES)

    @functools.partial(
        pl.kernel, mesh=mesh,
        out_type=jax.ShapeDtypeStruct((slots, d), x.dtype),
        scratch_types=[pltpu.VMEM((SC_CHUNK,), jnp.int32),
                       pltpu.VMEM((SC_CHUNK,), jnp.int32),
                       pltpu.VMEM((SC_CHUNK, d), x.dtype),
                       pltpu.SemaphoreType.DMA],
        name="moe_sc_row_scatter")
    def scatter(x_hbm, a_hbm, b_hbm, out_hbm, a_v, b_v, rows_v, sem):
        worker = lax.axis_index("s") * SC_CORES + lax.axis_index("c")
        base = worker * per_worker

        @pl.loop(0, per_worker // SC_CHUNK)
        def _(ci):
            off = base + ci * SC_CHUNK
            pltpu.sync_copy(a_hbm.at[pl.ds(off, SC_CHUNK)], a_v)
            pltpu.sync_copy(b_hbm.at[pl.ds(off, SC_CHUNK)], b_v)
            pltpu.sync_copy(x_hbm.at[pl.ds(off, SC_CHUNK)], rows_v)
            pltpu.async_copy(rows_v, out_hbm.at[a_v], sem).wait()
            pltpu.async_copy(rows_v, out_hbm.at[b_v], sem).wait()

    return scatter(x, slot_a, slot_b)


def _sc_row_gather(table, idx):
    rows = idx.shape[0]
    d = table.shape[1]
    per_worker = rows // (SC_CORES * SC_SUBCORES)
    mesh = plsc.VectorSubcoreMesh(core_axis_name="c", subcore_axis_name="s",
                                  num_cores=SC_CORES, num_subcores=SC_SUBCORES)

    @functools.partial(
        pl.kernel, mesh=mesh,
        out_type=jax.ShapeDtypeStruct((rows, d), table.dtype),
        scratch_types=[pltpu.VMEM((SC_CHUNK,), jnp.int32),
                       pltpu.VMEM((SC_CHUNK, d), table.dtype),
                       pltpu.SemaphoreType.DMA],
        name="moe_sc_row_gather")
    def gather(table_hbm, idx_hbm, out_hbm, idx_v, rows_v, sem):
        worker = lax.axis_index("s") * SC_CORES + lax.axis_index("c")
        base = worker * per_worker

        @pl.loop(0, per_worker // SC_CHUNK)
        def _(ci):
            off = base + ci * SC_CHUNK
            pltpu.sync_copy(idx_hbm.at[pl.ds(off, SC_CHUNK)], idx_v)
            pltpu.async_copy(table_hbm.at[idx_v], rows_v, sem).wait()
            pltpu.sync_copy(rows_v, out_hbm.at[pl.ds(off, SC_CHUNK)])

    return gather(table, idx)


def _expert_ffn_kernel(expert_ref, used_ref, x_ref, wg_ref, wu_ref, wd_ref, o_ref):
    i = pl.program_id(0)
    f = pl.program_id(1)

    @pl.when(f == 0)
    def _():
        o_ref[...] = jnp.zeros_like(o_ref)

    @pl.when(used_ref[i] > 0)
    def _():
        xb = x_ref[...].astype(BF16)
        g = _dot(xb, wg_ref[...])
        u = _dot(xb, wu_ref[...])
        h = (g * _sigmoid(g) * u).astype(BF16)
        o_ref[...] += _dot(h, wd_ref[...])


def _expert_ffn(xs, tile_expert, tile_used, wg, wu, wd, *, tf):
    slots, d = xs.shape
    ff = wg.shape[2]
    return pl.pallas_call(
        _expert_ffn_kernel,
        grid_spec=pltpu.PrefetchScalarGridSpec(
            num_scalar_prefetch=2, grid=(slots // MOE_TILE, ff // tf),
            in_specs=[pl.BlockSpec((MOE_TILE, d), lambda i, f, te, tu: (i, 0)),
                      pl.BlockSpec((None, d, tf), lambda i, f, te, tu: (te[i], 0, f)),
                      pl.BlockSpec((None, d, tf), lambda i, f, te, tu: (te[i], 0, f)),
                      pl.BlockSpec((None, tf, d), lambda i, f, te, tu: (te[i], f, 0))],
            out_specs=pl.BlockSpec((MOE_TILE, d), lambda i, f, te, tu: (i, 0))),
        out_shape=jax.ShapeDtypeStruct((slots, d), F32),
        compiler_params=_params("arbitrary", "arbitrary"),
        name="moe_expert_ffn",
    )(tile_expert, tile_used, xs, wg, wu, wd)


def _combine_kernel(x_ref, y1_ref, y2_ref, route_ref, lng_ref, lnb_ref, o_ref):
    route = route_ref[...]
    moe = route[:, 2:3] * y1_ref[...] + route[:, 3:4] * y2_ref[...]
    o_ref[...] = _layer_norm(ALPHA * x_ref[...] + moe, lng_ref[...], lnb_ref[...])


def _combine(x3, yt, route, ln_g, ln_b, *, tm):
    n, d = x3.shape
    nt = n // tm
    return pl.pallas_call(
        _combine_kernel,
        grid=(nt,),
        in_specs=[pl.BlockSpec((tm, d), lambda i: (i, 0)),
                  pl.BlockSpec((tm, d), lambda i: (i, 0)),
                  pl.BlockSpec((tm, d), lambda i: (i + nt, 0)),
                  pl.BlockSpec((tm, 128), lambda i: (i, 0)),
                  _full((1, d)), _full((1, d))],
        out_specs=pl.BlockSpec((tm, d), lambda i: (i, 0)),
        out_shape=jax.ShapeDtypeStruct((n, d), F32),
        compiler_params=_params("parallel"),
        name="moe_combine",
    )(x3, yt, yt, route, ln_g, ln_b)


def _moe(x3, route, tot, wg, wu, wd, ln_g, ln_b, *, tm, tf):
    n, d = x3.shape
    ne = wg.shape[0]
    slots = TOP_K * n + ne * MOE_TILE
    count = tot[0, :ne].astype(jnp.int32)
    cap = (count + (MOE_TILE - 1)) // MOE_TILE * MOE_TILE
    ends = jnp.cumsum(cap)
    off = ends - cap
    e1 = route[:, 0].astype(jnp.int32)
    e2 = route[:, 1].astype(jnp.int32)
    slot1 = off[e1] + route[:, 4].astype(jnp.int32)
    slot2 = off[e2] + route[:, 5].astype(jnp.int32)
    tile_start = jnp.arange(slots // MOE_TILE, dtype=jnp.int32) * MOE_TILE
    tile_expert = jnp.minimum(jnp.searchsorted(ends, tile_start, side="right"),
                              ne - 1).astype(jnp.int32)
    tile_used = (tile_start < ends[-1]).astype(jnp.int32)

    xs = _sc_row_scatter(x3, slot1, slot2, slots)
    ys = _expert_ffn(xs, tile_expert, tile_used, wg, wu, wd, tf=tf)
    yt = _sc_row_gather(ys, jnp.concatenate([slot1, slot2]))
    return _combine(x3, yt, route, ln_g, ln_b, tm=tm)


def _pad_cols(w, width):
    return jnp.pad(w, ((0, 0), (0, width - w.shape[1])))


def _pad_rows(w, height):
    return jnp.pad(w, ((0, height - w.shape[0]), (0, 0)))


def _forward(x, mix_w_in, rwkv_mu, rwkv_w0, rwkv_w_up, rwkv_a0, rwkv_a_up, rwkv_g_up,
             rwkv_k_k, rwkv_k_a, rwkv_r_k, rwkv_gn_g, rwkv_gn_b, fox_b_f, mix_w_out,
             mix_ln_g, mix_ln_b, ffn_w_gate, ffn_w_up, ffn_w_down, ffn_ln_g, ffn_ln_b,
             conv_w_pw1, conv_b_pw1, conv_w_dw, conv_b_dw, conv_ln_g, conv_ln_b,
             conv_w_pw2, conv_b_pw2, conv_post_ln_g, conv_post_ln_b,
             moe_w_router, moe_w_gate, moe_w_up, moe_w_down, moe_ln_g, moe_ln_b,
             *, tm=512, chunk=64, nb_rwkv=4, tq=512, tf_ffn=1536, tf_moe=1792):
    batch, seq, d = x.shape
    n = batch * seq
    gw = GROUP_W
    x2 = x.reshape(n, d)
    row = lambda t: t.reshape(1, -1)

    w_in = mix_w_in[0]
    mu = rwkv_mu[0]
    o_w, o_a, o_g = 3 * gw, 3 * gw + DECAY_LORA, 3 * gw + DECAY_LORA + AAA_LORA
    o_fox = o_g + GATE_LORA

    def lora_layout(t):
        return jnp.concatenate([t[..., :o_w],
                                _pad_cols(t[..., o_w:o_a], LORA_PAD),
                                _pad_cols(t[..., o_a:o_g], LORA_PAD),
                                _pad_cols(t[..., o_g:o_fox], LORA_PAD)], axis=-1)

    wa = lora_layout(w_in).astype(BF16)
    mu_a = lora_layout(row(mu))
    scale = LOG2E / math.sqrt(HEAD_DIM)
    wb = jnp.concatenate([w_in[:, o_fox:o_fox + gw] * scale,
                          w_in[:, o_fox + gw:o_fox + 3 * gw]], axis=1).astype(BF16)
    wf = _pad_cols(w_in[:, o_fox + 3 * gw:], 128).astype(BF16)
    bf = _pad_cols(row(fox_b_f[0]), 128)

    pr, qk, vt, c = _inproj(x2, wa, wb, wf, mu_a, bf, seq=seq, tm=tq)

    wup = _pad_rows(rwkv_w_up[0], LORA_PAD).astype(BF16)
    aup = _pad_rows(rwkv_a_up[0], LORA_PAD).astype(BF16)
    gup = _pad_rows(rwkv_g_up[0], LORA_PAD).astype(BF16)
    k_k, k_a, r_k = row(rwkv_k_k[0]), row(rwkv_k_a[0]), row(rwkv_r_k[0])
    yr = _rwkv(pr, row(rwkv_w0[0]), wup, row(rwkv_a0[0]), aup, k_k, k_a,
               batch=batch, seq=seq, chunk=chunk, nb=nb_rwkv)

    yf = _fox(qk, vt, c, batch=batch, seq=seq, tq=tq)

    w_out = mix_w_out[0].astype(BF16)
    x1 = _mixout(x2, pr, yr, yf, row(rwkv_a0[0]), aup, gup, k_a, r_k,
                 row(rwkv_gn_g[0]), row(rwkv_gn_b[0]), w_out[:gw], w_out[gw:],
                 row(mix_ln_g[0]), row(mix_ln_b[0]), tm=tm)
    x2b = _ffn(x1, ffn_w_gate[0].astype(BF16), ffn_w_up[0].astype(BF16),
               ffn_w_down[0].astype(BF16), row(ffn_ln_g[0]), row(ffn_ln_b[0]), tm=tm, tf=tf_ffn)

    hg = _glu(x2b, conv_w_pw1[0].astype(BF16), row(conv_b_pw1[0]), tm=tm)
    w_router = _pad_cols(moe_w_router[0], 128)
    wr_hi = w_router.astype(BF16)
    w_router = jnp.concatenate([wr_hi, (w_router - wr_hi.astype(F32)).astype(BF16)], axis=1)
    x3, route, tot = _conv(hg, x2b, _pad_rows(conv_w_dw[0], CONV_HALO), row(conv_b_dw[0]),
                                row(conv_ln_g[0]), row(conv_ln_b[0]),
                                conv_w_pw2[0].astype(BF16), row(conv_b_pw2[0]),
                                row(conv_post_ln_g[0]), row(conv_post_ln_b[0]),
                                w_router, seq=seq, tm=tm)
    out = _moe(x3, route, tot,moe_w_gate[0].astype(BF16), moe_w_up[0].astype(BF16),
               moe_w_down[0].astype(BF16), row(moe_ln_g[0]), row(moe_ln_b[0]),
               tm=tm, tf=tf_moe)
    return out.reshape(batch, seq, d)


def kernel(x, mix_w_in, rwkv_mu, rwkv_w0, rwkv_w_up, rwkv_a0, rwkv_a_up, rwkv_g_up, rwkv_k_k, rwkv_k_a, rwkv_r_k, rwkv_gn_g, rwkv_gn_b, fox_b_f, mix_w_out, mix_ln_g, mix_ln_b, ffn_w_gate, ffn_w_up, ffn_w_down, ffn_ln_g, ffn_ln_b, conv_w_pw1, conv_b_pw1, conv_w_dw, conv_b_dw, conv_ln_g, conv_ln_b, conv_w_pw2, conv_b_pw2, conv_post_ln_g, conv_post_ln_b, moe_w_router, moe_w_gate, moe_w_up, moe_w_down, moe_ln_g, moe_ln_b):
    return _forward(x, mix_w_in, rwkv_mu, rwkv_w0, rwkv_w_up, rwkv_a0, rwkv_a_up, rwkv_g_up,
                    rwkv_k_k, rwkv_k_a, rwkv_r_k, rwkv_gn_g, rwkv_gn_b, fox_b_f, mix_w_out,
                    mix_ln_g, mix_ln_b, ffn_w_gate, ffn_w_up, ffn_w_down, ffn_ln_g, ffn_ln_b,
                    conv_w_pw1, conv_b_pw1, conv_w_dw, conv_b_dw, conv_ln_g, conv_ln_b,
                    conv_w_pw2, conv_b_pw2, conv_post_ln_g, conv_post_ln_b,
                    moe_w_router, moe_w_gate, moe_w_up, moe_w_down, moe_ln_g, moe_ln_b)
```

```python
import functools
import math

import jax
import jax.numpy as jnp
from jax import lax
from jax.experimental import pallas as pl
from jax.experimental.pallas import tpu as pltpu
from jax.experimental.pallas import tpu_sc as plsc

F32 = jnp.float32
BF16 = jnp.bfloat16
HIGHEST = lax.Precision.HIGHEST

HEAD_DIM = 64
N_HEADS = 8
GROUP_W = N_HEADS * HEAD_DIM
LORA_PAD = 128
DECAY_LORA = 32
AAA_LORA = 32
GATE_LORA = 96
CONV_WIDTH = 31
CONV_HALO = 32
SUBLANES = 8
N_EXPERTS = 8
TOP_K = 2
LN_EPS = 1e-5
GN_EPS = 64e-5
DEPTH = 2
ALPHA = (2.0 * DEPTH) ** 0.25
NEG_BIG = -1e30
LOG2E = math.log2(math.e)
VMEM_LIMIT = 56 * 1024 * 1024


def _dot(a, b, **kw):
    return jnp.dot(a, b, preferred_element_type=F32, **kw)


def _dot_nt(a, b):
    return lax.dot_general(a, b, (((1,), (1,)), ((), ())), preferred_element_type=F32)


def _dot_tn(a, b):
    return lax.dot_general(a, b, (((0,), (0,)), ((), ())), preferred_element_type=F32)


def _dot_exact_lhs(a, v):
    hi = v.astype(BF16)
    rem = v - hi.astype(F32)
    mid = rem.astype(BF16)
    lo = (rem - mid.astype(F32)).astype(BF16)
    w = v.shape[1]
    out = _dot(a, jnp.concatenate([hi, mid, lo], axis=1))
    return out[:, :w] + out[:, w:2 * w] + out[:, 2 * w:]


def _sigmoid(z):
    return 1.0 / (1.0 + jnp.exp(-z))


def _softplus(z):
    return jnp.maximum(z, 0.0) + jnp.log1p(jnp.exp(-jnp.abs(z)))


def _layer_norm(h, g, b):
    mu = jnp.mean(h, axis=-1, keepdims=True)
    d = h - mu
    var = jnp.mean(d * d, axis=-1, keepdims=True)
    return d * lax.rsqrt(var + LN_EPS) * g + b


def _params(*sem):
    return pltpu.CompilerParams(dimension_semantics=sem, vmem_limit_bytes=VMEM_LIMIT)


def _full(shape):
    return pl.BlockSpec(shape, lambda *_: (0,) * len(shape))


def _inproj_kernel(x_ref, wa_ref, wb_ref, wf_ref, mu_ref, bf_ref, tri_ref,
                   pr_ref, qk_ref, vt_ref, c_ref, last_ref, carry_ref, *, tiles_per_seq):
    i = pl.program_id(0)

    @pl.when(i % tiles_per_seq == 0)
    def _():
        last_ref[...] = jnp.zeros_like(last_ref)
        carry_ref[...] = jnp.zeros_like(carry_ref)

    xb = x_ref[...].astype(BF16)
    tm = xb.shape[0]
    row0 = lax.broadcasted_iota(jnp.int32, (tm, 1), 0) == 0
    ca = wa_ref.shape[1]
    for c0 in range(0, ca, GROUP_W):
        cw = min(GROUP_W, ca - c0)
        p = _dot(xb, wa_ref[:, c0:c0 + cw])
        prev = jnp.where(row0, last_ref[:, c0:c0 + cw], pltpu.roll(p, 1, 0))
        last_ref[:, c0:c0 + cw] = p[tm - 1:tm, :]
        pr_ref[:, c0:c0 + cw] = p + mu_ref[:, c0:c0 + cw] * (prev - p)
    for c0 in range(0, 2 * GROUP_W, GROUP_W):
        qk_ref[:, c0:c0 + GROUP_W] = _dot(xb, wb_ref[:, c0:c0 + GROUP_W]).astype(BF16)
    vt_ref[...] = _dot(xb, wb_ref[:, 2 * GROUP_W:3 * GROUP_W]).T.astype(BF16)
    fl = _dot(xb, wf_ref[...]) + bf_ref[...]
    log_f = jnp.minimum(fl, 0.0) - jnp.log1p(jnp.exp(-jnp.abs(fl)))
    c = _dot_exact_lhs(tri_ref[...], log_f) + carry_ref[...]
    c_ref[...] = c
    carry_ref[...] = c[tm - 1:tm, :]


def _inproj(x2, wa, wb, wf, mu, bf, *, seq, tm):
    n, d = x2.shape
    ca, cb = wa.shape[1], wb.shape[1]
    tps = seq // tm
    tri = (lax.broadcasted_iota(jnp.int32, (tm, tm), 1)
           <= lax.broadcasted_iota(jnp.int32, (tm, tm), 0)).astype(BF16)
    return pl.pallas_call(
        functools.partial(_inproj_kernel, tiles_per_seq=tps),
        grid=(n // tm,),
        in_specs=[pl.BlockSpec((tm, d), lambda i: (i, 0)),
                  _full((d, ca)), _full((d, cb)), _full((d, 128)),
                  _full((1, ca)), _full((1, 128)), _full((tm, tm))],
        out_specs=[pl.BlockSpec((tm, ca), lambda i: (i, 0)),
                   pl.BlockSpec((tm, 2 * GROUP_W), lambda i: (i, 0)),
                   pl.BlockSpec((None, None, GROUP_W, tm), lambda i: (i // tps, i % tps, 0, 0)),
                   pl.BlockSpec((tm, 128), lambda i: (i, 0))],
        out_shape=[jax.ShapeDtypeStruct((n, ca), F32),
                   jax.ShapeDtypeStruct((n, 2 * GROUP_W), BF16),
                   jax.ShapeDtypeStruct((n // seq, tps, GROUP_W, tm), BF16),
                   jax.ShapeDtypeStruct((n, 128), F32)],
        scratch_shapes=[pltpu.VMEM((1, ca), F32), pltpu.VMEM((1, 128), F32)],
        compiler_params=_params("arbitrary"),
        name="inproj",
    )(x2, wa, wb, wf, mu, bf, tri)


def _rwkv_kernel(pr_ref, w0_ref, wup_ref, a0_ref, aup_ref, kk_ref, ka_ref,
                 gsum_ref, tri_ref, y_ref, h_ref, *, chunk, nb):
    @pl.when(pl.program_id(1) == 0)
    def _():
        h_ref[...] = jnp.zeros_like(h_ref)

    gw = GROUP_W
    pw = 2 * HEAD_DIM
    npair = N_HEADS // 2
    rows = 2 * chunk
    log_chunk = int(math.log2(chunk))
    head0 = lax.broadcasted_iota(jnp.int32, (1, pw), 1) < HEAD_DIM
    row = lax.broadcasted_iota(jnp.int32, (rows, rows), 0)
    col = lax.broadcasted_iota(jnp.int32, (rows, rows), 1)
    strict = (col & (chunk - 1)) < (row & (chunk - 1))
    incl = (col & (chunk - 1)) <= (row & (chunk - 1))
    eye = (col == row).astype(F32)
    peye = (lax.broadcasted_iota(jnp.int32, (pw, pw), 0)
            == lax.broadcasted_iota(jnp.int32, (pw, pw), 1))

    def stack(x):
        return jnp.concatenate([jnp.where(head0, x, 0.0), jnp.where(head0, 0.0, x)],
                               axis=0).astype(BF16)

    units = [(b, j) for b in range(nb) for j in range(npair)]
    nu = len(units)
    ar, bk, vs, bhs, khs, pcs = [], [], [], [], [], []
    for b in range(nb):
        r = pr_ref[b, :, 0:gw]
        k = pr_ref[b, :, gw:2 * gw]
        v = pr_ref[b, :, 2 * gw:3 * gw]
        wd = pr_ref[b, :, 3 * gw:3 * gw + LORA_PAD]
        ad = pr_ref[b, :, 3 * gw + LORA_PAD:3 * gw + 2 * LORA_PAD]
        w_pre = w0_ref[...] + _dot(jnp.tanh(wd).astype(BF16), wup_ref[...])
        w = -_softplus(-w_pre) - 0.5
        log_decay = -jnp.exp(w)
        a = _sigmoid(a0_ref[...] + _dot(ad.astype(BF16), aup_ref[...]))
        kk = k * kk_ref[...]
        norm = jnp.sqrt(_dot((kk * kk).astype(BF16), gsum_ref[...]))
        kk = kk / jnp.maximum(norm, 1e-12)
        k_mod = k * (1.0 + (a - 1.0) * ka_ref[...])
        b_vec = kk * a
        cum = _dot_exact_lhs(tri_ref[...], log_decay)
        last = cum[chunk - 1:chunk, :]
        p_inv = jnp.exp(-cum)
        p_tail = jnp.exp(last - cum)
        a_t = -kk * jnp.exp(cum - log_decay)
        r_t = r * jnp.exp(cum)
        b_t = b_vec * p_inv
        k_t = k_mod * p_inv
        b_h = b_vec * p_tail
        k_h = k_mod * p_tail
        p_last = jnp.exp(last)
        for j in range(npair):
            sl = slice(j * pw, (j + 1) * pw)
            ar.append(jnp.concatenate([stack(a_t[:, sl]), stack(r_t[:, sl])], axis=0))
            bk.append(jnp.concatenate([stack(b_t[:, sl]), stack(k_t[:, sl])], axis=0))
            vs.append(stack(v[:, sl]))
            bhs.append(stack(b_h[:, sl]))
            khs.append(stack(k_h[:, sl]))
            pcs.append(jnp.sum(jnp.where(peye, p_last[:, sl], 0.0), axis=1, keepdims=True))

    gram = [_dot_nt(ar[u], bk[u]) for u in range(nu)]
    l_ab = [jnp.where(strict, gram[u][:rows, :rows], 0.0) for u in range(nu)]
    l_akv = [_dot(jnp.where(strict, gram[u][:rows, rows:], 0.0).astype(BF16), vs[u])
             for u in range(nu)]
    m_rb = [jnp.where(incl, gram[u][rows:, :rows], 0.0).astype(BF16) for u in range(nu)]
    m_rkv = [_dot(jnp.where(incl, gram[u][rows:, rows:], 0.0).astype(BF16), vs[u])
             for u in range(nu)]
    t_inv = [eye + l_ab[u] for u in range(nu)]
    xb = [l_ab[u].astype(BF16) for u in range(nu)]
    xp = [_dot(xb[u], xb[u]) for u in range(nu)]
    for step in range(log_chunk - 1):
        xb = [xp[u].astype(BF16) for u in range(nu)]
        if step < log_chunk - 2:
            both = [_dot(jnp.concatenate([t_inv[u].astype(BF16), xb[u]], axis=0), xb[u])
                    for u in range(nu)]
            t_inv = [t_inv[u] + both[u][:rows] for u in range(nu)]
            xp = [both[u][rows:] for u in range(nu)]
        else:
            t_inv = [t_inv[u] + _dot(t_inv[u].astype(BF16), xb[u]) for u in range(nu)]
    tw = [_dot(t_inv[u].astype(BF16),
               jnp.concatenate([ar[u][:rows], l_akv[u].astype(BF16)], axis=1)).astype(BF16)
          for u in range(nu)]
    mw = [_dot(m_rb[u], tw[u]) for u in range(nu)]
    bw = [_dot_tn(bhs[u], tw[u]) for u in range(nu)]
    kv = [_dot_tn(khs[u], vs[u]) for u in range(nu)]
    for u, (b, j) in enumerate(units):
        wy = ar[u][rows:].astype(F32) + mw[u][:, :pw]
        yc = mw[u][:, pw:] + m_rkv[u]
        hf = h_ref[u]
        yh = _dot(jnp.concatenate([wy.astype(BF16), bw[u][:, :pw].astype(BF16)], axis=0),
                  hf.astype(BF16))
        h_ref[u] = pcs[u] * hf + yh[rows:] + bw[u][:, pw:] + kv[u]
        ys = yh[:rows] + yc
        y_ref[b, :, j * pw:(j + 1) * pw] = ys[:chunk] + ys[chunk:]


def _rwkv(pr, w0, wup, a0, aup, k_k, k_a, *, batch, seq, chunk, nb):
    n, ca = pr.shape
    nch = seq // chunk
    gidx = lax.broadcasted_iota(jnp.int32, (GROUP_W, GROUP_W), 0) // HEAD_DIM
    gsum = (gidx == gidx.T).astype(BF16)
    tri = (lax.broadcasted_iota(jnp.int32, (chunk, chunk), 1)
           <= lax.broadcasted_iota(jnp.int32, (chunk, chunk), 0)).astype(BF16)
    y = pl.pallas_call(
        functools.partial(_rwkv_kernel, chunk=chunk, nb=nb),
        grid=(batch // nb, nch),
        in_specs=[pl.BlockSpec((nb, chunk, ca), lambda g, c: (g, c, 0)),
                  _full((1, GROUP_W)), _full((LORA_PAD, GROUP_W)),
                  _full((1, GROUP_W)), _full((LORA_PAD, GROUP_W)),
                  _full((1, GROUP_W)), _full((1, GROUP_W)),
                  _full((GROUP_W, GROUP_W)), _full((chunk, chunk))],
        out_specs=pl.BlockSpec((nb, chunk, GROUP_W), lambda g, c: (g, c, 0)),
        out_shape=jax.ShapeDtypeStruct((batch, seq, GROUP_W), F32),
        scratch_shapes=[pltpu.VMEM((nb * (N_HEADS // 2), 2 * HEAD_DIM, 2 * HEAD_DIM), F32)],
        compiler_params=_params("arbitrary", "arbitrary"),
        name="rwkv_scan",
    )(pr.reshape(batch, seq, ca), w0, wup, a0, aup, k_k, k_a, gsum, tri)
    return y.reshape(n, GROUP_W)


def _fox_kernel(q_ref, k_ref, vt_ref, c_ref, o_ref, acc_ref, m_ref, l_ref, kb_ref,
                sa_ref, sb_ref, *, tq, tk):
    j = pl.program_id(1)
    qi = pl.program_id(2)
    pw = 2 * HEAD_DIM
    seq = k_ref.shape[0]
    lane = lax.broadcasted_iota(jnp.int32, (1, pw), 1)

    first = lane < HEAD_DIM
    bias_lane = (HEAD_DIM, 0)

    @pl.when(qi == 0)
    def _():
        def fill(rb, carry):
            rs = pl.multiple_of(rb * tk, tk)
            cblk = c_ref[pl.ds(rs, tk), :]
            for hh in range(2):
                bias = -LOG2E * jnp.sum(jnp.where(lane == 2 * j + hh, cblk, 0.0),
                                        axis=1, keepdims=True)
                b_hi = bias.astype(BF16).astype(F32)
                b_mid = (bias - b_hi).astype(BF16).astype(F32)
                b_lo = bias - b_hi - b_mid
                l0 = bias_lane[hh]
                kb_ref[hh, pl.ds(rs, tk), :] = jnp.where(
                    lane == l0, b_hi, jnp.where(lane == l0 + 1, b_mid,
                                                jnp.where(lane == l0 + 2, b_lo, 0.0))).astype(BF16)
            return carry
        lax.fori_loop(0, seq // tk, fill, 0)

    q = q_ref[...]
    ones3 = [jnp.where(jnp.logical_and(lane >= l0, lane < l0 + 3), 1.0, 0.0).astype(BF16)
             for l0 in bias_lane]
    own = (first, jnp.logical_not(first))
    qh = tuple(jnp.where(own[hh], q, ones3[hh]) for hh in range(2))
    acc_ref[...] = jnp.zeros_like(acc_ref)
    m_ref[...] = jnp.full_like(m_ref, NEG_BIG)
    l_ref[...] = jnp.zeros_like(l_ref)
    key_minus_query = (lax.broadcasted_iota(jnp.int32, (tk, tq), 0)
                       - lax.broadcasted_iota(jnp.int32, (tk, tq), 1))
    top = lax.broadcasted_iota(jnp.int32, (pw, 1), 0) < HEAD_DIM

    def scores(kb, s_ref):
        ks = pl.multiple_of(kb * tk, tk)
        kblk = k_ref[pl.ds(ks, tk), :]
        for hh in range(2):
            k_aug = jnp.where(own[hh], kblk, kb_ref[hh, pl.ds(ks, tk), :])
            s_ref[hh] = _dot_nt(k_aug, qh[hh])

    def softmax_pv(kb, s_ref, masked):
        causal = key_minus_query <= qi * tq - kb * tk
        vt = vt_ref[kb].astype(F32)
        vts = (jnp.where(top, vt, 1.0).astype(BF16), jnp.where(top, 1.0, vt).astype(BF16))
        alphas, pvs = [], []
        for hh in range(2):
            z = s_ref[hh]
            if masked:
                z = jnp.where(causal, z, NEG_BIG)
            m_prev = m_ref[hh]
            m_new = jnp.maximum(m_prev, jnp.max(z, axis=0, keepdims=True))
            alpha = jnp.exp2(m_prev - m_new)
            p = jnp.exp2(z - m_new)
            pv = _dot(vts[hh], p.astype(BF16))
            ones_row = (1 - hh) * HEAD_DIM
            l_ref[hh] = alpha * l_ref[hh] + pv[ones_row:ones_row + 1, :]
            m_ref[hh] = m_new
            alphas.append(alpha)
            pvs.append(pv)
        acc_ref[...] = (acc_ref[...] * jnp.where(top, alphas[0], alphas[1])
                        + jnp.where(top, pvs[0], pvs[1]))

    scores(0, sa_ref)

    def body(i, carry):
        scores(2 * i + 1, sb_ref)
        softmax_pv(2 * i, sa_ref, False)
        scores(2 * i + 2, sa_ref)
        softmax_pv(2 * i + 1, sb_ref, False)
        return carry

    lax.fori_loop(0, qi, body, 0)
    scores(2 * qi + 1, sb_ref)
    softmax_pv(2 * qi, sa_ref, True)
    softmax_pv(2 * qi + 1, sb_ref, True)

    out_t = acc_ref[...] / jnp.where(top, l_ref[0], l_ref[1])
    o_ref[...] = out_t.T.astype(BF16)


def _fox(qk, vt, c, *, batch, seq, tq, tk):
    n = qk.shape[0]
    nq = seq // tq
    nk = seq // tk
    npair = N_HEADS // 2
    pw = 2 * HEAD_DIM
    assert tq == 2 * tk and vt.shape == (batch, nk, GROUP_W, tk)
    return pl.pallas_call(
        functools.partial(_fox_kernel, tq=tq, tk=tk),
        grid=(batch, npair, nq),
        in_specs=[pl.BlockSpec((tq, pw), lambda b, j, i: (b * nq + i, j)),
                  pl.BlockSpec((seq, pw), lambda b, j, i: (b, npair + j)),
                  pl.BlockSpec((None, nk, pw, tk), lambda b, j, i: (b, 0, j, 0)),
                  pl.BlockSpec((seq, 128), lambda b, j, i: (b, 0))],
        out_specs=pl.BlockSpec((tq, pw), lambda b, j, i: (b * nq + i, j)),
        out_shape=jax.ShapeDtypeStruct((n, GROUP_W), BF16),
        scratch_shapes=[pltpu.VMEM((pw, tq), F32),
                        pltpu.VMEM((2, 1, tq), F32), pltpu.VMEM((2, 1, tq), F32),
                        pltpu.VMEM((2, seq, pw), BF16),
                        pltpu.VMEM((2, tk, tq), F32), pltpu.VMEM((2, tk, tq), F32)],
        compiler_params=_params("arbitrary", "arbitrary", "arbitrary"),
        name="fox_attention",
    )(qk, qk, vt, c)


def _mixout_kernel(x_ref, pr_ref, yr_ref, yf_ref, a0_ref, aup_ref, gup_ref, ka_ref,
                   rk_ref, gng_ref, gnb_ref, gsum_ref, wr_ref, wf_ref, lng_ref, lnb_ref,
                   o_ref):
    gw = GROUP_W
    r = pr_ref[:, 0:gw]
    k = pr_ref[:, gw:2 * gw]
    v = pr_ref[:, 2 * gw:3 * gw]
    ad = pr_ref[:, 3 * gw + LORA_PAD:3 * gw + 2 * LORA_PAD]
    gd = pr_ref[:, 3 * gw + 2 * LORA_PAD:3 * gw + 3 * LORA_PAD]
    a = _sigmoid(a0_ref[...] + _dot(ad.astype(BF16), aup_ref[...]))
    k_mod = k * (1.0 + (a - 1.0) * ka_ref[...])
    gate = _dot(_sigmoid(gd).astype(BF16), gup_ref[...])
    gsum = gsum_ref[...]

    def group_sum(t):
        return _dot(t.astype(BF16), gsum)

    y = yr_ref[...]
    y_hi = y.astype(BF16)
    y_lo = (y - y_hi.astype(F32)).astype(BF16)
    mean = (_dot(y_hi, gsum) + _dot(y_lo, gsum)) * (1.0 / HEAD_DIM)
    d = y - mean
    var = group_sum(d * d) * (1.0 / HEAD_DIM)
    yn = d * lax.rsqrt(var + GN_EPS) * gng_ref[...] + gnb_ref[...]
    bonus = group_sum(r * k_mod * rk_ref[...])
    y_rwkv = ((yn + bonus * v) * gate).astype(BF16)
    mixed = _dot(y_rwkv, wr_ref[...]) + _dot(yf_ref[...], wf_ref[...])
    o_ref[...] = _layer_norm(ALPHA * x_ref[...] + mixed, lng_ref[...], lnb_ref[...])


def _mixout(x2, pr, yr, yf, a0, aup, gup, k_a, r_k, gn_g, gn_b, w_r, w_f, ln_g, ln_b, *, tm):
    n, d = x2.shape
    ca = pr.shape[1]
    gidx = lax.broadcasted_iota(jnp.int32, (GROUP_W, GROUP_W), 0) // HEAD_DIM
    gsum = (gidx == gidx.T).astype(BF16)
    vec = _full((1, GROUP_W))
    return pl.pallas_call(
        _mixout_kernel,
        grid=(n // tm,),
        in_specs=[pl.BlockSpec((tm, d), lambda i: (i, 0)),
                  pl.BlockSpec((tm, ca), lambda i: (i, 0)),
                  pl.BlockSpec((tm, GROUP_W), lambda i: (i, 0)),
                  pl.BlockSpec((tm, GROUP_W), lambda i: (i, 0)),
                  vec, _full((LORA_PAD, GROUP_W)), _full((LORA_PAD, GROUP_W)),
                  vec, vec, vec, vec, _full((GROUP_W, GROUP_W)),
                  _full((GROUP_W, d)), _full((GROUP_W, d)),
                  _full((1, d)), _full((1, d))],
        out_specs=pl.BlockSpec((tm, d), lambda i: (i, 0)),
        out_shape=jax.ShapeDtypeStruct((n, d), F32),
        compiler_params=_params("parallel"),
        name="mix_out",
    )(x2, pr, yr, yf, a0, aup, gup, k_a, r_k, gn_g, gn_b, gsum, w_r, w_f, ln_g, ln_b)


def _ffn_kernel(x_ref, wg_ref, wu_ref, wd_ref, lng_ref, lnb_ref, o_ref, *, tf):
    xb = x_ref[...].astype(BF16)
    ff = wg_ref.shape[1]
    acc = None
    for f0 in range(0, ff, tf):
        f1 = min(f0 + tf, ff)
        g = _dot(xb, wg_ref[:, f0:f1])
        u = _dot(xb, wu_ref[:, f0:f1])
        h = (g * _sigmoid(g) * u).astype(BF16)
        part = _dot(h, wd_ref[f0:f1, :])
        acc = part if acc is None else acc + part
    o_ref[...] = _layer_norm(ALPHA * x_ref[...] + acc, lng_ref[...], lnb_ref[...])


def _ffn(x2, wg, wu, wd, ln_g, ln_b, *, tm, tf):
    n, d = x2.shape
    ff = wg.shape[1]
    once = pl.Buffered(1)

    def resident(shape):
        return pl.BlockSpec(shape, lambda i: (0, 0), pipeline_mode=once)

    return pl.pallas_call(
        functools.partial(_ffn_kernel, tf=tf),
        grid=(n // tm,),
        in_specs=[pl.BlockSpec((tm, d), lambda i: (i, 0)),
                  resident((d, ff)), resident((d, ff)), resident((ff, d)),
                  _full((1, d)), _full((1, d))],
        out_specs=pl.BlockSpec((tm, d), lambda i: (i, 0)),
        out_shape=jax.ShapeDtypeStruct((n, d), F32),
        compiler_params=_params("parallel"),
        name="ffn_swiglu",
    )(x2, wg, wu, wd, ln_g, ln_b)


def _glu_kernel(x_ref, w_ref, b_ref, o_ref):
    d = o_ref.shape[1]
    xb = x_ref[...].astype(BF16)
    val = _dot(xb, w_ref[:, 0:d]) + b_ref[:, 0:d]
    gat = _dot(xb, w_ref[:, d:2 * d]) + b_ref[:, d:2 * d]
    o_ref[...] = val * _sigmoid(gat)


def _glu(x2, w, b, *, tm):
    n, d = x2.shape
    return pl.pallas_call(
        _glu_kernel,
        grid=(n // tm,),
        in_specs=[pl.BlockSpec((tm, d), lambda i: (i, 0)), _full((d, 2 * d)), _full((1, 2 * d))],
        out_specs=pl.BlockSpec((tm, d), lambda i: (i, 0)),
        out_shape=jax.ShapeDtypeStruct((n, d), F32),
        compiler_params=_params("parallel"),
        name="conv_glu",
    )(x2, w, b)


def _top2(logits):
    lane = lax.broadcasted_iota(jnp.int32, logits.shape, 1).astype(F32)
    lg = jnp.where(lane < N_EXPERTS, logits, NEG_BIG)
    m1 = jnp.max(lg, axis=-1, keepdims=True)
    i1 = jnp.min(jnp.where(lg == m1, lane, 128.0), axis=-1, keepdims=True)
    lg2 = jnp.where(lane == i1, NEG_BIG, lg)
    m2 = jnp.max(lg2, axis=-1, keepdims=True)
    i2 = jnp.min(jnp.where(lg2 == m2, lane, 128.0), axis=-1, keepdims=True)
    e2 = jnp.exp(m2 - m1)
    w1 = 1.0 / (1.0 + e2)
    w2 = e2 / (1.0 + e2)
    return lane, i1, i2, w1, w2


def _conv_kernel(hc_ref, hp_ref, x_ref, wdw_ref, bdw_ref, lng_ref, lnb_ref, w2_ref, b2_ref,
                 pg_ref, pb_ref, wr_ref, tri_ref, x3_ref, route_ref, tot_ref,
                 ext_ref, cv_ref, cnt_ref, *, tiles_per_seq):
    tm, d = x_ref.shape

    @pl.when(pl.program_id(0) == 0)
    def _():
        cnt_ref[...] = jnp.zeros_like(cnt_ref)

    first = pl.program_id(0) % tiles_per_seq == 0
    ext_ref[0, 0:CONV_HALO, :] = jnp.where(first, 0.0, hp_ref[...])
    ext_ref[0, CONV_HALO:CONV_HALO + tm, :] = hc_ref[...]
    nrows = tm + CONV_HALO
    for c0 in range(0, d, 256):
        base = ext_ref[0, :, c0:c0 + 256]
        for j in range(1, SUBLANES):
            ext_ref[j, :, c0:c0 + 256] = pltpu.roll(base, nrows - j, 0)
    off = CONV_HALO - (CONV_WIDTH - 1)
    rc, cc = 64, 256
    for r0 in range(0, tm, rc):
        for c0 in range(0, d, cc):
            acc = jnp.broadcast_to(bdw_ref[:, c0:c0 + cc], (rc, cc))
            for t in range(CONV_WIDTH):
                base, j = divmod(off + t, SUBLANES)
                rs = r0 + base * SUBLANES
                acc = acc + wdw_ref[t:t + 1, c0:c0 + cc] * ext_ref[j, rs:rs + rc, c0:c0 + cc]
            cv_ref[r0:r0 + rc, c0:c0 + cc] = acc
    hn = _layer_norm(cv_ref[...], lng_ref[...], lnb_ref[...])
    hs = (hn * _sigmoid(hn)).astype(BF16)
    conv = _dot(hs, w2_ref[...]) + b2_ref[...]
    x3 = _layer_norm(ALPHA * x_ref[...] + conv, pg_ref[...], pb_ref[...])
    x3_ref[...] = x3
    x_hi = x3.astype(BF16)
    x_lo = (x3 - x_hi.astype(F32)).astype(BF16)
    hi_part = _dot(x_hi, wr_ref[...])
    logits = hi_part[:, :128] + hi_part[:, 128:] + _dot(x_lo, wr_ref[:, 0:128])
    lane, i1, i2, w1, w2 = _top2(logits)
    first, second = lane == i1, lane == i2
    sel = jnp.where(jnp.logical_or(first, second), 1.0, 0.0)
    pos = _dot(tri_ref[...], sel.astype(BF16)) + cnt_ref[...]
    cnt_ref[...] += jnp.sum(sel, axis=0, keepdims=True)
    tot_ref[...] = jnp.broadcast_to(cnt_ref[...], tot_ref.shape)
    rank1 = jnp.sum(jnp.where(first, pos, 0.0), axis=-1, keepdims=True)
    rank2 = jnp.sum(jnp.where(second, pos, 0.0), axis=-1, keepdims=True)
    fields = (i1, i2, w1, w2, rank1, rank2)
    record = jnp.zeros_like(logits)
    for k, field in enumerate(fields):
        record = jnp.where(lane == float(k), field, record)
    route_ref[...] = record


def _conv(hg, x2, w_dw, b_dw, ln_g, ln_b, w2, b2, pg, pb, w_router, *, seq, tm):
    n, d = x2.shape
    ratio = tm // CONV_HALO
    vec = _full((1, d))
    tri = (lax.broadcasted_iota(jnp.int32, (tm, tm), 1)
           < lax.broadcasted_iota(jnp.int32, (tm, tm), 0)).astype(BF16)
    return pl.pallas_call(
        functools.partial(_conv_kernel, tiles_per_seq=seq // tm),
        grid=(n // tm,),
        in_specs=[pl.BlockSpec((tm, d), lambda i: (i, 0)),
                  pl.BlockSpec((CONV_HALO, d), lambda i: (jnp.maximum(i * ratio - 1, 0), 0)),
                  pl.BlockSpec((tm, d), lambda i: (i, 0)),
                  _full((CONV_HALO, d)), vec, vec, vec, _full((d, d)), vec, vec, vec,
                  _full((d, 256)), _full((tm, tm))],
        out_specs=[pl.BlockSpec((tm, d), lambda i: (i, 0)),
                   pl.BlockSpec((tm, 128), lambda i: (i, 0)),
                   _full((SUBLANES, 128))],
        out_shape=[jax.ShapeDtypeStruct((n, d), F32),
                   jax.ShapeDtypeStruct((n, 128), F32),
                   jax.ShapeDtypeStruct((SUBLANES, 128), F32)],
        scratch_shapes=[pltpu.VMEM((SUBLANES, tm + CONV_HALO, d), F32), pltpu.VMEM((tm, d), F32),
                        pltpu.VMEM((1, 128), F32)],
        compiler_params=_params("arbitrary"),
        name="conv_module",
    )(hg, hg, x2, w_dw, b_dw, ln_g, ln_b, w2, b2, pg, pb, w_router, tri)


MOE_TILE = 512
SC_CORES = 2
SC_SUBCORES = 16
SC_CHUNK = 64


def _sc_row_scatter(x, slot_a, slot_b, slots):
    n, d = x.shape
    per_worker = n // (SC_CORES * SC_SUBCORES)
    mesh = plsc.VectorSubcoreMesh(core_axis_name="c", subcore_axis_name="s",
                                  num_cores=SC_CORES, num_subcores=SC_SUBCORES)

    @functools.partial(
        pl.kernel, mesh=mesh,
        out_type=jax.ShapeDtypeStruct((slots, d), x.dtype),
        scratch_types=[pltpu.VMEM((SC_CHUNK,), jnp.int32),
                       pltpu.VMEM((SC_CHUNK,), jnp.int32),
                       pltpu.VMEM((SC_CHUNK, d), x.dtype),
                       pltpu.SemaphoreType.DMA],
        name="moe_sc_row_scatter")
    def scatter(x_hbm, a_hbm, b_hbm, out_hbm, a_v, b_v, rows_v, sem):
        worker = lax.axis_index("s") * SC_CORES + lax.axis_index("c")
        base = worker * per_worker

        @pl.loop(0, per_worker // SC_CHUNK)
        def _(ci):
            off = base + ci * SC_CHUNK
            pltpu.sync_copy(a_hbm.at[pl.ds(off, SC_CHUNK)], a_v)
            pltpu.sync_copy(b_hbm.at[pl.ds(off, SC_CHUNK)], b_v)
            pltpu.sync_copy(x_hbm.at[pl.ds(off, SC_CHUNK)], rows_v)
            pltpu.async_copy(rows_v, out_hbm.at[a_v], sem).wait()
            pltpu.async_copy(rows_v, out_hbm.at[b_v], sem).wait()

    return scatter(x, slot_a, slot_b)


def _sc_row_gather(table, idx):
    rows = idx.shape[0]
    d = table.shape[1]
    per_worker = rows // (SC_CORES * SC_SUBCORES)
    mesh = plsc.VectorSubcoreMesh(core_axis_name="c", subcore_axis_name="s",
                                  num_cores=SC_CORES, num_subcores=SC_SUBCORES)

    @functools.partial(
        pl.kernel, mesh=mesh,
        out_type=jax.ShapeDtypeStruct((rows, d), table.dtype),
        scratch_types=[pltpu.VMEM((SC_CHUNK,), jnp.int32),
                       pltpu.VMEM((SC_CHUNK, d), table.dtype),
                       pltpu.SemaphoreType.DMA],
        name="moe_sc_row_gather")
    def gather(table_hbm, idx_hbm, out_hbm, idx_v, rows_v, sem):
        worker = lax.axis_index("s") * SC_CORES + lax.axis_index("c")
        base = worker * per_worker

        @pl.loop(0, per_worker // SC_CHUNK)
        def _(ci):
            off = base + ci * SC_CHUNK
            pltpu.sync_copy(idx_hbm.at[pl.ds(off, SC_CHUNK)], idx_v)
            pltpu.async_copy(table_hbm.at[idx_v], rows_v, sem).wait()
            pltpu.sync_copy(rows_v, out_hbm.at[pl.ds(off, SC_CHUNK)])

    return gather(table, idx)


def _expert_ffn_kernel(expert_ref, used_ref, x_ref, wg_ref, wu_ref, wd_ref, o_ref):
    i = pl.program_id(0)
    f = pl.program_id(1)

    @pl.when(f == 0)
    def _():
        o_ref[...] = jnp.zeros_like(o_ref)

    @pl.when(used_ref[i] > 0)
    def _():
        xb = x_ref[...].astype(BF16)
        g = _dot(xb, wg_ref[...])
        u = _dot(xb, wu_ref[...])
        h = (g * _sigmoid(g) * u).astype(BF16)
        o_ref[...] += _dot(h, wd_ref[...])


def _expert_ffn(xs, tile_expert, tile_used, wg, wu, wd, *, tf):
    slots, d = xs.shape
    ff = wg.shape[2]
    return pl.pallas_call(
        _expert_ffn_kernel,
        grid_spec=pltpu.PrefetchScalarGridSpec(
            num_scalar_prefetch=2, grid=(slots // MOE_TILE, ff // tf),
            in_specs=[pl.BlockSpec((MOE_TILE, d), lambda i, f, te, tu: (i, 0)),
                      pl.BlockSpec((None, d, tf), lambda i, f, te, tu: (te[i], 0, f)),
                      pl.BlockSpec((None, d, tf), lambda i, f, te, tu: (te[i], 0, f)),
                      pl.BlockSpec((None, tf, d), lambda i, f, te, tu: (te[i], f, 0))],
            out_specs=pl.BlockSpec((MOE_TILE, d), lambda i, f, te, tu: (i, 0))),
        out_shape=jax.ShapeDtypeStruct((slots, d), F32),
        compiler_params=_params("arbitrary", "arbitrary"),
        name="moe_expert_ffn",
    )(tile_expert, tile_used, xs, wg, wu, wd)


def _combine_kernel(x_ref, y1_ref, y2_ref, route_ref, lng_ref, lnb_ref, o_ref):
    route = route_ref[...]
    moe = route[:, 2:3] * y1_ref[...] + route[:, 3:4] * y2_ref[...]
    o_ref[...] = _layer_norm(ALPHA * x_ref[...] + moe, lng_ref[...], lnb_ref[...])


def _combine(x3, yt, route, ln_g, ln_b, *, tm):
    n, d = x3.shape
    nt = n // tm
    return pl.pallas_call(
        _combine_kernel,
        grid=(nt,),
        in_specs=[pl.BlockSpec((tm, d), lambda i: (i, 0)),
                  pl.BlockSpec((tm, d), lambda i: (i, 0)),
                  pl.BlockSpec((tm, d), lambda i: (i + nt, 0)),
                  pl.BlockSpec((tm, 128), lambda i: (i, 0)),
                  _full((1, d)), _full((1, d))],
        out_specs=pl.BlockSpec((tm, d), lambda i: (i, 0)),
        out_shape=jax.ShapeDtypeStruct((n, d), F32),
        compiler_params=_params("parallel"),
        name="moe_combine",
    )(x3, yt, yt, route, ln_g, ln_b)


def _moe(x3, route, tot, wg, wu, wd, ln_g, ln_b, *, tm, tf):
    n, d = x3.shape
    ne = wg.shape[0]
    slots = TOP_K * n + ne * MOE_TILE
    count = tot[0, :ne].astype(jnp.int32)
    cap = (count + (MOE_TILE - 1)) // MOE_TILE * MOE_TILE
    ends = jnp.cumsum(cap)
    off = ends - cap
    e1 = route[:, 0].astype(jnp.int32)
    e2 = route[:, 1].astype(jnp.int32)
    slot1 = off[e1] + route[:, 4].astype(jnp.int32)
    slot2 = off[e2] + route[:, 5].astype(jnp.int32)
    tile_start = jnp.arange(slots // MOE_TILE, dtype=jnp.int32) * MOE_TILE
    tile_expert = jnp.minimum(jnp.searchsorted(ends, tile_start, side="right"),
                              ne - 1).astype(jnp.int32)
    tile_used = (tile_start < ends[-1]).astype(jnp.int32)

    xs = _sc_row_scatter(x3, slot1, slot2, slots)
    ys = _expert_ffn(xs, tile_expert, tile_used, wg, wu, wd, tf=tf)
    yt = _sc_row_gather(ys, jnp.concatenate([slot1, slot2]))
    return _combine(x3, yt, route, ln_g, ln_b, tm=tm)


def _pad_cols(w, width):
    return jnp.pad(w, ((0, 0), (0, width - w.shape[1])))


def _pad_rows(w, height):
    return jnp.pad(w, ((0, height - w.shape[0]), (0, 0)))


def _forward(x, mix_w_in, rwkv_mu, rwkv_w0, rwkv_w_up, rwkv_a0, rwkv_a_up, rwkv_g_up,
             rwkv_k_k, rwkv_k_a, rwkv_r_k, rwkv_gn_g, rwkv_gn_b, fox_b_f, mix_w_out,
             mix_ln_g, mix_ln_b, ffn_w_gate, ffn_w_up, ffn_w_down, ffn_ln_g, ffn_ln_b,
             conv_w_pw1, conv_b_pw1, conv_w_dw, conv_b_dw, conv_ln_g, conv_ln_b,
             conv_w_pw2, conv_b_pw2, conv_post_ln_g, conv_post_ln_b,
             moe_w_router, moe_w_gate, moe_w_up, moe_w_down, moe_ln_g, moe_ln_b,
             *, tm=512, chunk=64, nb_rwkv=4, tk_fox=512, tf_ffn=1536, tf_moe=1792):
    batch, seq, d = x.shape
    n = batch * seq
    gw = GROUP_W
    x2 = x.reshape(n, d)
    row = lambda t: t.reshape(1, -1)

    w_in = mix_w_in[0]
    mu = rwkv_mu[0]
    o_w, o_a, o_g = 3 * gw, 3 * gw + DECAY_LORA, 3 * gw + DECAY_LORA + AAA_LORA
    o_fox = o_g + GATE_LORA

    def lora_layout(t):
        return jnp.concatenate([t[..., :o_w],
                                _pad_cols(t[..., o_w:o_a], LORA_PAD),
                                _pad_cols(t[..., o_a:o_g], LORA_PAD),
                                _pad_cols(t[..., o_g:o_fox], LORA_PAD)], axis=-1)

    wa = lora_layout(w_in).astype(BF16)
    mu_a = lora_layout(row(mu))
    scale = LOG2E / math.sqrt(HEAD_DIM)
    wb = jnp.concatenate([w_in[:, o_fox:o_fox + gw] * scale,
                          w_in[:, o_fox + gw:o_fox + 3 * gw]], axis=1).astype(BF16)
    wf = _pad_cols(w_in[:, o_fox + 3 * gw:], 128).astype(BF16)
    bf = _pad_cols(row(fox_b_f[0]), 128)

    pr, qk, vt, c = _inproj(x2, wa, wb, wf, mu_a, bf, seq=seq, tm=tk_fox)

    wup = _pad_rows(rwkv_w_up[0], LORA_PAD).astype(BF16)
    aup = _pad_rows(rwkv_a_up[0], LORA_PAD).astype(BF16)
    gup = _pad_rows(rwkv_g_up[0], LORA_PAD).astype(BF16)
    k_k, k_a, r_k = row(rwkv_k_k[0]), row(rwkv_k_a[0]), row(rwkv_r_k[0])
    yr = _rwkv(pr, row(rwkv_w0[0]), wup, row(rwkv_a0[0]), aup, k_k, k_a,
               batch=batch, seq=seq, chunk=chunk, nb=nb_rwkv)

    yf = _fox(qk, vt, c, batch=batch, seq=seq, tq=2 * tk_fox, tk=tk_fox)

    w_out = mix_w_out[0].astype(BF16)
    x1 = _mixout(x2, pr, yr, yf, row(rwkv_a0[0]), aup, gup, k_a, r_k,
                 row(rwkv_gn_g[0]), row(rwkv_gn_b[0]), w_out[:gw], w_out[gw:],
                 row(mix_ln_g[0]), row(mix_ln_b[0]), tm=tm)
    x2b = _ffn(x1, ffn_w_gate[0].astype(BF16), ffn_w_up[0].astype(BF16),
               ffn_w_down[0].astype(BF16), row(ffn_ln_g[0]), row(ffn_ln_b[0]), tm=tm, tf=tf_ffn)

    hg = _glu(x2b, conv_w_pw1[0].astype(BF16), row(conv_b_pw1[0]), tm=tm)
    w_router = _pad_cols(moe_w_router[0], 128)
    wr_hi = w_router.astype(BF16)
    w_router = jnp.concatenate([wr_hi, (w_router - wr_hi.astype(F32)).astype(BF16)], axis=1)
    x3, route, tot = _conv(hg, x2b, _pad_rows(conv_w_dw[0], CONV_HALO), row(conv_b_dw[0]),
                                row(conv_ln_g[0]), row(conv_ln_b[0]),
                                conv_w_pw2[0].astype(BF16), row(conv_b_pw2[0]),
                                row(conv_post_ln_g[0]), row(conv_post_ln_b[0]),
                                w_router, seq=seq, tm=tm)
    out = _moe(x3, route, tot,moe_w_gate[0].astype(BF16), moe_w_up[0].astype(BF16),
               moe_w_down[0].astype(BF16), row(moe_ln_g[0]), row(moe_ln_b[0]),
               tm=tm, tf=tf_moe)
    return out.reshape(batch, seq, d)


def kernel(x, mix_w_in, rwkv_mu, rwkv_w0, rwkv_w_up, rwkv_a0, rwkv_a_up, rwkv_g_up, rwkv_k_k, rwkv_k_a, rwkv_r_k, rwkv_gn_g, rwkv_gn_b, fox_b_f, mix_w_out, mix_ln_g, mix_ln_b, ffn_w_gate, ffn_w_up, ffn_w_down, ffn_ln_g, ffn_ln_b, conv_w_pw1, conv_b_pw1, conv_w_dw, conv_b_dw, conv_ln_g, conv_ln_b, conv_w_pw2, conv_b_pw2, conv_post_ln_g, conv_post_ln_b, moe_w_router, moe_w_gate, moe_w_up, moe_w_down, moe_ln_g, moe_ln_b):
    return _forward(x, mix_w_in, rwkv_mu, rwkv_w0, rwkv_w_up, rwkv_a0, rwkv_a_up, rwkv_g_up,
                    rwkv_k_k, rwkv_k_a, rwkv_r_k, rwkv_gn_g, rwkv_gn_b, fox_b_f, mix_w_out,
                    mix_ln_g, mix_ln_b, ffn_w_gate, ffn_w_up, ffn_w_down, ffn_ln_g, ffn_ln_b,
                    conv_w_pw1, conv_b_pw1, conv_w_dw, conv_b_dw, conv_ln_g, conv_ln_b,
                    conv_w_pw2, conv_b_pw2, conv_post_ln_g, conv_post_ln_b,
                    moe_w_router, moe_w_gate, moe_w_up, moe_w_down, moe_ln_g, moe_ln_b)
```

```python
import functools
import math

import jax
import jax.numpy as jnp
from jax import lax
from jax.experimental import pallas as pl
from jax.experimental.pallas import tpu as pltpu
from jax.experimental.pallas import tpu_sc as plsc

F32 = jnp.float32
BF16 = jnp.bfloat16
HIGHEST = lax.Precision.HIGHEST

HEAD_DIM = 64
N_HEADS = 8
GROUP_W = N_HEADS * HEAD_DIM
LORA_PAD = 128
DECAY_LORA = 32
AAA_LORA = 32
GATE_LORA = 96
CONV_WIDTH = 31
CONV_HALO = 32
SUBLANES = 8
N_EXPERTS = 8
TOP_K = 2
LN_EPS = 1e-5
GN_EPS = 64e-5
DEPTH = 2
ALPHA = (2.0 * DEPTH) ** 0.25
NEG_BIG = -1e30
LOG2E = math.log2(math.e)
VMEM_LIMIT = 56 * 1024 * 1024


def _dot(a, b, **kw):
    return jnp.dot(a, b, preferred_element_type=F32, **kw)


def _dot_nt(a, b):
    return lax.dot_general(a, b, (((1,), (1,)), ((), ())), preferred_element_type=F32)


def _dot_tn(a, b):
    return lax.dot_general(a, b, (((0,), (0,)), ((), ())), preferred_element_type=F32)


def _dot_exact_lhs(a, v):
    hi = v.astype(BF16)
    rem = v - hi.astype(F32)
    mid = rem.astype(BF16)
    lo = (rem - mid.astype(F32)).astype(BF16)
    w = v.shape[1]
    out = _dot(a, jnp.concatenate([hi, mid, lo], axis=1))
    return out[:, :w] + out[:, w:2 * w] + out[:, 2 * w:]


def _sigmoid(z):
    return 1.0 / (1.0 + jnp.exp(-z))


def _softplus(z):
    return jnp.maximum(z, 0.0) + jnp.log1p(jnp.exp(-jnp.abs(z)))


def _layer_norm(h, g, b):
    mu = jnp.mean(h, axis=-1, keepdims=True)
    d = h - mu
    var = jnp.mean(d * d, axis=-1, keepdims=True)
    return d * lax.rsqrt(var + LN_EPS) * g + b


def _params(*sem):
    return pltpu.CompilerParams(dimension_semantics=sem, vmem_limit_bytes=VMEM_LIMIT)


def _full(shape):
    return pl.BlockSpec(shape, lambda *_: (0,) * len(shape))


def _inproj_kernel(x_ref, wa_ref, wb_ref, wf_ref, mu_ref, bf_ref, tri_ref,
                   pr_ref, qk_ref, vt_ref, c_ref, last_ref, carry_ref, *, tiles_per_seq):
    i = pl.program_id(0)

    @pl.when(i % tiles_per_seq == 0)
    def _():
        last_ref[...] = jnp.zeros_like(last_ref)
        carry_ref[...] = jnp.zeros_like(carry_ref)

    xb = x_ref[...].astype(BF16)
    tm = xb.shape[0]
    row0 = lax.broadcasted_iota(jnp.int32, (tm, 1), 0) == 0
    ca = wa_ref.shape[1]
    for c0 in range(0, ca, GROUP_W):
        cw = min(GROUP_W, ca - c0)
        p = _dot(xb, wa_ref[:, c0:c0 + cw])
        prev = jnp.where(row0, last_ref[:, c0:c0 + cw], pltpu.roll(p, 1, 0))
        last_ref[:, c0:c0 + cw] = p[tm - 1:tm, :]
        pr_ref[:, c0:c0 + cw] = p + mu_ref[:, c0:c0 + cw] * (prev - p)
    for c0 in range(0, 2 * GROUP_W, GROUP_W):
        qk_ref[:, c0:c0 + GROUP_W] = _dot(xb, wb_ref[:, c0:c0 + GROUP_W]).astype(BF16)
    vt_ref[...] = _dot(xb, wb_ref[:, 2 * GROUP_W:3 * GROUP_W]).T.astype(BF16)
    fl = _dot(xb, wf_ref[...]) + bf_ref[...]
    log_f = jnp.minimum(fl, 0.0) - jnp.log1p(jnp.exp(-jnp.abs(fl)))
    c = _dot_exact_lhs(tri_ref[...], log_f) + carry_ref[...]
    c_ref[...] = c
    carry_ref[...] = c[tm - 1:tm, :]


def _inproj(x2, wa, wb, wf, mu, bf, *, seq, tm):
    n, d = x2.shape
    ca, cb = wa.shape[1], wb.shape[1]
    tps = seq // tm
    tri = (lax.broadcasted_iota(jnp.int32, (tm, tm), 1)
           <= lax.broadcasted_iota(jnp.int32, (tm, tm), 0)).astype(BF16)
    return pl.pallas_call(
        functools.partial(_inproj_kernel, tiles_per_seq=tps),
        grid=(n // tm,),
        in_specs=[pl.BlockSpec((tm, d), lambda i: (i, 0)),
                  _full((d, ca)), _full((d, cb)), _full((d, 128)),
                  _full((1, ca)), _full((1, 128)), _full((tm, tm))],
        out_specs=[pl.BlockSpec((tm, ca), lambda i: (i, 0)),
                   pl.BlockSpec((tm, 2 * GROUP_W), lambda i: (i, 0)),
                   pl.BlockSpec((None, None, GROUP_W, tm), lambda i: (i // tps, i % tps, 0, 0)),
                   pl.BlockSpec((tm, 128), lambda i: (i, 0))],
        out_shape=[jax.ShapeDtypeStruct((n, ca), F32),
                   jax.ShapeDtypeStruct((n, 2 * GROUP_W), BF16),
                   jax.ShapeDtypeStruct((n // seq, tps, GROUP_W, tm), BF16),
                   jax.ShapeDtypeStruct((n, 128), F32)],
        scratch_shapes=[pltpu.VMEM((1, ca), F32), pltpu.VMEM((1, 128), F32)],
        compiler_params=_params("arbitrary"),
        name="inproj",
    )(x2, wa, wb, wf, mu, bf, tri)


def _rwkv_kernel(pr_ref, w0_ref, wup_ref, a0_ref, aup_ref, kk_ref, ka_ref,
                 gsum_ref, tri_ref, y_ref, h_ref, *, chunk, nb):
    @pl.when(pl.program_id(1) == 0)
    def _():
        h_ref[...] = jnp.zeros_like(h_ref)

    gw = GROUP_W
    pw = 2 * HEAD_DIM
    npair = N_HEADS // 2
    rows = 2 * chunk
    log_chunk = int(math.log2(chunk))
    head0 = lax.broadcasted_iota(jnp.int32, (1, pw), 1) < HEAD_DIM
    row = lax.broadcasted_iota(jnp.int32, (rows, rows), 0)
    col = lax.broadcasted_iota(jnp.int32, (rows, rows), 1)
    strict = (col & (chunk - 1)) < (row & (chunk - 1))
    incl = (col & (chunk - 1)) <= (row & (chunk - 1))
    eye = (col == row).astype(F32)
    peye = (lax.broadcasted_iota(jnp.int32, (pw, pw), 0)
            == lax.broadcasted_iota(jnp.int32, (pw, pw), 1))

    def stack(x):
        return jnp.concatenate([jnp.where(head0, x, 0.0), jnp.where(head0, 0.0, x)],
                               axis=0).astype(BF16)

    units = [(b, j) for b in range(nb) for j in range(npair)]
    nu = len(units)
    x = pr_ref[...].reshape(nb * chunk, pr_ref.shape[2])
    r = x[:, 0:gw]
    k = x[:, gw:2 * gw]
    v = x[:, 2 * gw:3 * gw]
    wd = x[:, 3 * gw:3 * gw + LORA_PAD]
    ad = x[:, 3 * gw + LORA_PAD:3 * gw + 2 * LORA_PAD]
    w_pre = w0_ref[...] + _dot(jnp.tanh(wd).astype(BF16), wup_ref[...])
    w = -_softplus(-w_pre) - 0.5
    log_decay = -jnp.exp(w)
    a = _sigmoid(a0_ref[...] + _dot(ad.astype(BF16), aup_ref[...]))
    kk = k * kk_ref[...]
    norm = jnp.sqrt(_dot((kk * kk).astype(BF16), gsum_ref[...]))
    kk = kk / jnp.maximum(norm, 1e-12)
    k_mod = k * (1.0 + (a - 1.0) * ka_ref[...])
    b_vec = kk * a
    cum = _dot_exact_lhs(tri_ref[...], log_decay)
    lasts = [cum[(b + 1) * chunk - 1:(b + 1) * chunk, :] for b in range(nb)]
    last = jnp.concatenate([jnp.broadcast_to(t, (chunk, gw)) for t in lasts], axis=0)
    p_inv = jnp.exp(-cum)
    p_tail = jnp.exp(last - cum)
    a_t = -kk * jnp.exp(cum - log_decay)
    r_t = r * jnp.exp(cum)
    b_t = b_vec * p_inv
    k_t = k_mod * p_inv
    b_h = b_vec * p_tail
    k_h = k_mod * p_tail
    ar, bk, vs, bhs, khs, pcs = [], [], [], [], [], []
    for b in range(nb):
        rb = slice(b * chunk, (b + 1) * chunk)
        p_last = jnp.exp(lasts[b])
        for j in range(npair):
            sl = slice(j * pw, (j + 1) * pw)
            ar.append(jnp.concatenate([stack(a_t[rb, sl]), stack(r_t[rb, sl])], axis=0))
            bk.append(jnp.concatenate([stack(b_t[rb, sl]), stack(k_t[rb, sl])], axis=0))
            vs.append(stack(v[rb, sl]))
            bhs.append(stack(b_h[rb, sl]))
            khs.append(stack(k_h[rb, sl]))
            pcs.append(jnp.sum(jnp.where(peye, p_last[:, sl], 0.0), axis=1, keepdims=True))

    gram = [_dot_nt(ar[u], bk[u]) for u in range(nu)]
    l_ab = [jnp.where(strict, gram[u][:rows, :rows], 0.0) for u in range(nu)]
    l_akv = [_dot(jnp.where(strict, gram[u][:rows, rows:], 0.0).astype(BF16), vs[u])
             for u in range(nu)]
    m_rb = [jnp.where(incl, gram[u][rows:, :rows], 0.0).astype(BF16) for u in range(nu)]
    m_rkv = [_dot(jnp.where(incl, gram[u][rows:, rows:], 0.0).astype(BF16), vs[u])
             for u in range(nu)]
    t_inv = [eye + l_ab[u] for u in range(nu)]
    xb = [l_ab[u].astype(BF16) for u in range(nu)]
    xp = [_dot(xb[u], xb[u]) for u in range(nu)]
    for step in range(log_chunk - 1):
        xb = [xp[u].astype(BF16) for u in range(nu)]
        if step < log_chunk - 2:
            both = [_dot(jnp.concatenate([t_inv[u].astype(BF16), xb[u]], axis=0), xb[u])
                    for u in range(nu)]
            t_inv = [t_inv[u] + both[u][:rows] for u in range(nu)]
            xp = [both[u][rows:] for u in range(nu)]
        else:
            t_inv = [t_inv[u] + _dot(t_inv[u].astype(BF16), xb[u]) for u in range(nu)]
    tw = [_dot(t_inv[u].astype(BF16),
               jnp.concatenate([ar[u][:rows], l_akv[u].astype(BF16)], axis=1)).astype(BF16)
          for u in range(nu)]
    mw = [_dot(m_rb[u], tw[u]) for u in range(nu)]
    bw = [_dot_tn(bhs[u], tw[u]) for u in range(nu)]
    kv = [_dot_tn(khs[u], vs[u]) for u in range(nu)]
    for u, (b, j) in enumerate(units):
        wy = ar[u][rows:].astype(F32) + mw[u][:, :pw]
        yc = mw[u][:, pw:] + m_rkv[u]
        hf = h_ref[u]
        yh = _dot(jnp.concatenate([wy.astype(BF16), bw[u][:, :pw].astype(BF16)], axis=0),
                  hf.astype(BF16))
        h_ref[u] = pcs[u] * hf + yh[rows:] + bw[u][:, pw:] + kv[u]
        ys = yh[:rows] + yc
        y_ref[b, :, j * pw:(j + 1) * pw] = ys[:chunk] + ys[chunk:]


def _rwkv(pr, w0, wup, a0, aup, k_k, k_a, *, batch, seq, chunk, nb):
    n, ca = pr.shape
    nch = seq // chunk
    gidx = lax.broadcasted_iota(jnp.int32, (GROUP_W, GROUP_W), 0) // HEAD_DIM
    gsum = (gidx == gidx.T).astype(BF16)
    rr = lax.broadcasted_iota(jnp.int32, (nb * chunk, nb * chunk), 0)
    cc = lax.broadcasted_iota(jnp.int32, (nb * chunk, nb * chunk), 1)
    tri = jnp.logical_and(cc <= rr, cc // chunk == rr // chunk).astype(BF16)
    y = pl.pallas_call(
        functools.partial(_rwkv_kernel, chunk=chunk, nb=nb),
        grid=(batch // nb, nch),
        in_specs=[pl.BlockSpec((nb, chunk, ca), lambda g, c: (g, c, 0)),
                  _full((1, GROUP_W)), _full((LORA_PAD, GROUP_W)),
                  _full((1, GROUP_W)), _full((LORA_PAD, GROUP_W)),
                  _full((1, GROUP_W)), _full((1, GROUP_W)),
                  _full((GROUP_W, GROUP_W)), _full((nb * chunk, nb * chunk))],
        out_specs=pl.BlockSpec((nb, chunk, GROUP_W), lambda g, c: (g, c, 0)),
        out_shape=jax.ShapeDtypeStruct((batch, seq, GROUP_W), F32),
        scratch_shapes=[pltpu.VMEM((nb * (N_HEADS // 2), 2 * HEAD_DIM, 2 * HEAD_DIM), F32)],
        compiler_params=_params("arbitrary", "arbitrary"),
        name="rwkv_scan",
    )(pr.reshape(batch, seq, ca), w0, wup, a0, aup, k_k, k_a, gsum, tri)
    return y.reshape(n, GROUP_W)


def _fox_kernel(q_ref, k_ref, vt_ref, c_ref, o_ref, acc_ref, m_ref, l_ref, kb_ref,
                sa_ref, sb_ref, *, tq, tk):
    j = pl.program_id(1)
    qi = pl.program_id(2)
    pw = 2 * HEAD_DIM
    seq = k_ref.shape[0]
    lane = lax.broadcasted_iota(jnp.int32, (1, pw), 1)

    first = lane < HEAD_DIM
    bias_lane = (HEAD_DIM, 0)

    @pl.when(qi == 0)
    def _():
        def fill(rb, carry):
            rs = pl.multiple_of(rb * tk, tk)
            cblk = c_ref[pl.ds(rs, tk), :]
            for hh in range(2):
                bias = -LOG2E * jnp.sum(jnp.where(lane == 2 * j + hh, cblk, 0.0),
                                        axis=1, keepdims=True)
                b_hi = bias.astype(BF16).astype(F32)
                b_mid = (bias - b_hi).astype(BF16).astype(F32)
                b_lo = bias - b_hi - b_mid
                l0 = bias_lane[hh]
                kb_ref[hh, pl.ds(rs, tk), :] = jnp.where(
                    lane == l0, b_hi, jnp.where(lane == l0 + 1, b_mid,
                                                jnp.where(lane == l0 + 2, b_lo, 0.0))).astype(BF16)
            return carry
        lax.fori_loop(0, seq // tk, fill, 0)

    q = q_ref[...]
    ones3 = [jnp.where(jnp.logical_and(lane >= l0, lane < l0 + 3), 1.0, 0.0).astype(BF16)
             for l0 in bias_lane]
    own = (first, jnp.logical_not(first))
    qh = tuple(jnp.where(own[hh], q, ones3[hh]) for hh in range(2))
    acc_ref[...] = jnp.zeros_like(acc_ref)
    m_ref[...] = jnp.full_like(m_ref, NEG_BIG)
    l_ref[...] = jnp.zeros_like(l_ref)
    key_minus_query = (lax.broadcasted_iota(jnp.int32, (tk, tq), 0)
                       - lax.broadcasted_iota(jnp.int32, (tk, tq), 1))
    top = lax.broadcasted_iota(jnp.int32, (pw, 1), 0) < HEAD_DIM

    def scores(kb, s_ref):
        ks = pl.multiple_of(kb * tk, tk)
        kblk = k_ref[pl.ds(ks, tk), :]
        for hh in range(2):
            k_aug = jnp.where(own[hh], kblk, kb_ref[hh, pl.ds(ks, tk), :])
            s_ref[hh] = _dot_nt(k_aug, qh[hh])

    def softmax_pv(kb, s_ref, masked):
        causal = key_minus_query <= qi * tq - kb * tk
        vt = vt_ref[kb].astype(F32)
        vts = (jnp.where(top, vt, 1.0).astype(BF16), jnp.where(top, 1.0, vt).astype(BF16))
        alphas, pvs = [], []
        for hh in range(2):
            z = s_ref[hh]
            if masked:
                z = jnp.where(causal, z, NEG_BIG)
            m_prev = m_ref[hh]
            m_new = jnp.maximum(m_prev, jnp.max(z, axis=0, keepdims=True))
            alpha = jnp.exp2(m_prev - m_new)
            p = jnp.exp2(z - m_new)
            pv = _dot(vts[hh], p.astype(BF16))
            ones_row = (1 - hh) * HEAD_DIM
            l_ref[hh] = alpha * l_ref[hh] + pv[ones_row:ones_row + 1, :]
            m_ref[hh] = m_new
            alphas.append(alpha)
            pvs.append(pv)
        acc_ref[...] = (acc_ref[...] * jnp.where(top, alphas[0], alphas[1])
                        + jnp.where(top, pvs[0], pvs[1]))

    scores(0, sa_ref)

    def body(i, carry):
        scores(2 * i + 1, sb_ref)
        softmax_pv(2 * i, sa_ref, False)
        scores(2 * i + 2, sa_ref)
        softmax_pv(2 * i + 1, sb_ref, False)
        return carry

    lax.fori_loop(0, qi, body, 0)
    scores(2 * qi + 1, sb_ref)
    softmax_pv(2 * qi, sa_ref, True)
    softmax_pv(2 * qi + 1, sb_ref, True)

    out_t = acc_ref[...] / jnp.where(top, l_ref[0], l_ref[1])
    o_ref[...] = out_t.T.astype(BF16)


def _fox(qk, vt, c, *, batch, seq, tq, tk):
    n = qk.shape[0]
    nq = seq // tq
    nk = seq // tk
    npair = N_HEADS // 2
    pw = 2 * HEAD_DIM
    assert tq == 2 * tk and vt.shape == (batch, nk, GROUP_W, tk)
    return pl.pallas_call(
        functools.partial(_fox_kernel, tq=tq, tk=tk),
        grid=(batch, npair, nq),
        in_specs=[pl.BlockSpec((tq, pw), lambda b, j, i: (b * nq + i, j)),
                  pl.BlockSpec((seq, pw), lambda b, j, i: (b, npair + j)),
                  pl.BlockSpec((None, nk, pw, tk), lambda b, j, i: (b, 0, j, 0)),
                  pl.BlockSpec((seq, 128), lambda b, j, i: (b, 0))],
        out_specs=pl.BlockSpec((tq, pw), lambda b, j, i: (b * nq + i, j)),
        out_shape=jax.ShapeDtypeStruct((n, GROUP_W), BF16),
        scratch_shapes=[pltpu.VMEM((pw, tq), F32),
                        pltpu.VMEM((2, 1, tq), F32), pltpu.VMEM((2, 1, tq), F32),
                        pltpu.VMEM((2, seq, pw), BF16),
                        pltpu.VMEM((2, tk, tq), F32), pltpu.VMEM((2, tk, tq), F32)],
        compiler_params=_params("arbitrary", "arbitrary", "arbitrary"),
        name="fox_attention",
    )(qk, qk, vt, c)


def _mixout_kernel(x_ref, pr_ref, yr_ref, yf_ref, a0_ref, aup_ref, gup_ref, ka_ref,
                   rk_ref, gng_ref, gnb_ref, gsum_ref, wr_ref, wf_ref, lng_ref, lnb_ref,
                   o_ref):
    gw = GROUP_W
    r = pr_ref[:, 0:gw]
    k = pr_ref[:, gw:2 * gw]
    v = pr_ref[:, 2 * gw:3 * gw]
    ad = pr_ref[:, 3 * gw + LORA_PAD:3 * gw + 2 * LORA_PAD]
    gd = pr_ref[:, 3 * gw + 2 * LORA_PAD:3 * gw + 3 * LORA_PAD]
    a = _sigmoid(a0_ref[...] + _dot(ad.astype(BF16), aup_ref[...]))
    k_mod = k * (1.0 + (a - 1.0) * ka_ref[...])
    gate = _dot(_sigmoid(gd).astype(BF16), gup_ref[...])
    gsum = gsum_ref[...]

    def group_sum(t):
        return _dot(t.astype(BF16), gsum)

    y = yr_ref[...]
    y_hi = y.astype(BF16)
    y_lo = (y - y_hi.astype(F32)).astype(BF16)
    mean = (_dot(y_hi, gsum) + _dot(y_lo, gsum)) * (1.0 / HEAD_DIM)
    d = y - mean
    var = group_sum(d * d) * (1.0 / HEAD_DIM)
    yn = d * lax.rsqrt(var + GN_EPS) * gng_ref[...] + gnb_ref[...]
    bonus = group_sum(r * k_mod * rk_ref[...])
    y_rwkv = ((yn + bonus * v) * gate).astype(BF16)
    mixed = _dot(y_rwkv, wr_ref[...]) + _dot(yf_ref[...], wf_ref[...])
    o_ref[...] = _layer_norm(ALPHA * x_ref[...] + mixed, lng_ref[...], lnb_ref[...])


def _mixout(x2, pr, yr, yf, a0, aup, gup, k_a, r_k, gn_g, gn_b, w_r, w_f, ln_g, ln_b, *, tm):
    n, d = x2.shape
    ca = pr.shape[1]
    gidx = lax.broadcasted_iota(jnp.int32, (GROUP_W, GROUP_W), 0) // HEAD_DIM
    gsum = (gidx == gidx.T).astype(BF16)
    vec = _full((1, GROUP_W))
    return pl.pallas_call(
        _mixout_kernel,
        grid=(n // tm,),
        in_specs=[pl.BlockSpec((tm, d), lambda i: (i, 0)),
                  pl.BlockSpec((tm, ca), lambda i: (i, 0)),
                  pl.BlockSpec((tm, GROUP_W), lambda i: (i, 0)),
                  pl.BlockSpec((tm, GROUP_W), lambda i: (i, 0)),
                  vec, _full((LORA_PAD, GROUP_W)), _full((LORA_PAD, GROUP_W)),
                  vec, vec, vec, vec, _full((GROUP_W, GROUP_W)),
                  _full((GROUP_W, d)), _full((GROUP_W, d)),
                  _full((1, d)), _full((1, d))],
        out_specs=pl.BlockSpec((tm, d), lambda i: (i, 0)),
        out_shape=jax.ShapeDtypeStruct((n, d), F32),
        compiler_params=_params("parallel"),
        name="mix_out",
    )(x2, pr, yr, yf, a0, aup, gup, k_a, r_k, gn_g, gn_b, gsum, w_r, w_f, ln_g, ln_b)


def _ffn_kernel(x_ref, wg_ref, wu_ref, wd_ref, lng_ref, lnb_ref, o_ref, *, tf):
    xb = x_ref[...].astype(BF16)
    ff = wg_ref.shape[1]
    acc = None
    for f0 in range(0, ff, tf):
        f1 = min(f0 + tf, ff)
        g = _dot(xb, wg_ref[:, f0:f1])
        u = _dot(xb, wu_ref[:, f0:f1])
        h = (g * _sigmoid(g) * u).astype(BF16)
        part = _dot(h, wd_ref[f0:f1, :])
        acc = part if acc is None else acc + part
    o_ref[...] = _layer_norm(ALPHA * x_ref[...] + acc, lng_ref[...], lnb_ref[...])


def _ffn(x2, wg, wu, wd, ln_g, ln_b, *, tm, tf):
    n, d = x2.shape
    ff = wg.shape[1]
    once = pl.Buffered(1)

    def resident(shape):
        return pl.BlockSpec(shape, lambda i: (0, 0), pipeline_mode=once)

    return pl.pallas_call(
        functools.partial(_ffn_kernel, tf=tf),
        grid=(n // tm,),
        in_specs=[pl.BlockSpec((tm, d), lambda i: (i, 0)),
                  resident((d, ff)), resident((d, ff)), resident((ff, d)),
                  _full((1, d)), _full((1, d))],
        out_specs=pl.BlockSpec((tm, d), lambda i: (i, 0)),
        out_shape=jax.ShapeDtypeStruct((n, d), F32),
        compiler_params=_params("parallel"),
        name="ffn_swiglu",
    )(x2, wg, wu, wd, ln_g, ln_b)


def _glu_kernel(x_ref, w_ref, b_ref, o_ref):
    d = o_ref.shape[1]
    xb = x_ref[...].astype(BF16)
    val = _dot(xb, w_ref[:, 0:d]) + b_ref[:, 0:d]
    gat = _dot(xb, w_ref[:, d:2 * d]) + b_ref[:, d:2 * d]
    o_ref[...] = val * _sigmoid(gat)


def _glu(x2, w, b, *, tm):
    n, d = x2.shape
    return pl.pallas_call(
        _glu_kernel,
        grid=(n // tm,),
        in_specs=[pl.BlockSpec((tm, d), lambda i: (i, 0)), _full((d, 2 * d)), _full((1, 2 * d))],
        out_specs=pl.BlockSpec((tm, d), lambda i: (i, 0)),
        out_shape=jax.ShapeDtypeStruct((n, d), F32),
        compiler_params=_params("parallel"),
        name="conv_glu",
    )(x2, w, b)


def _top2(logits):
    lane = lax.broadcasted_iota(jnp.int32, logits.shape, 1).astype(F32)
    lg = jnp.where(lane < N_EXPERTS, logits, NEG_BIG)
    m1 = jnp.max(lg, axis=-1, keepdims=True)
    i1 = jnp.min(jnp.where(lg == m1, lane, 128.0), axis=-1, keepdims=True)
    lg2 = jnp.where(lane == i1, NEG_BIG, lg)
    m2 = jnp.max(lg2, axis=-1, keepdims=True)
    i2 = jnp.min(jnp.where(lg2 == m2, lane, 128.0), axis=-1, keepdims=True)
    e2 = jnp.exp(m2 - m1)
    w1 = 1.0 / (1.0 + e2)
    w2 = e2 / (1.0 + e2)
    return lane, i1, i2, w1, w2


def _conv_kernel(hc_ref, hp_ref, x_ref, wdw_ref, bdw_ref, lng_ref, lnb_ref, w2_ref, b2_ref,
                 pg_ref, pb_ref, wr_ref, tri_ref, x3_ref, route_ref, tot_ref,
                 ext_ref, cv_ref, cnt_ref, *, tiles_per_seq):
    tm, d = x_ref.shape

    @pl.when(pl.program_id(0) == 0)
    def _():
        cnt_ref[...] = jnp.zeros_like(cnt_ref)

    first = pl.program_id(0) % tiles_per_seq == 0
    ext_ref[0, 0:CONV_HALO, :] = jnp.where(first, 0.0, hp_ref[...])
    ext_ref[0, CONV_HALO:CONV_HALO + tm, :] = hc_ref[...]
    nrows = tm + CONV_HALO
    for c0 in range(0, d, 256):
        base = ext_ref[0, :, c0:c0 + 256]
        for j in range(1, SUBLANES):
            ext_ref[j, :, c0:c0 + 256] = pltpu.roll(base, nrows - j, 0)
    off = CONV_HALO - (CONV_WIDTH - 1)
    rc, cc = 64, 256
    for r0 in range(0, tm, rc):
        for c0 in range(0, d, cc):
            acc = jnp.broadcast_to(bdw_ref[:, c0:c0 + cc], (rc, cc))
            for t in range(CONV_WIDTH):
                base, j = divmod(off + t, SUBLANES)
                rs = r0 + base * SUBLANES
                acc = acc + wdw_ref[t:t + 1, c0:c0 + cc] * ext_ref[j, rs:rs + rc, c0:c0 + cc]
            cv_ref[r0:r0 + rc, c0:c0 + cc] = acc
    hn = _layer_norm(cv_ref[...], lng_ref[...], lnb_ref[...])
    hs = (hn * _sigmoid(hn)).astype(BF16)
    conv = _dot(hs, w2_ref[...]) + b2_ref[...]
    x3 = _layer_norm(ALPHA * x_ref[...] + conv, pg_ref[...], pb_ref[...])
    x3_ref[...] = x3
    x_hi = x3.astype(BF16)
    x_lo = (x3 - x_hi.astype(F32)).astype(BF16)
    hi_part = _dot(x_hi, wr_ref[...])
    logits = hi_part[:, :128] + hi_part[:, 128:] + _dot(x_lo, wr_ref[:, 0:128])
    lane, i1, i2, w1, w2 = _top2(logits)
    first, second = lane == i1, lane == i2
    sel = jnp.where(jnp.logical_or(first, second), 1.0, 0.0)
    pos = _dot(tri_ref[...], sel.astype(BF16)) + cnt_ref[...]
    cnt_ref[...] += jnp.sum(sel, axis=0, keepdims=True)
    tot_ref[...] = jnp.broadcast_to(cnt_ref[...], tot_ref.shape)
    rank1 = jnp.sum(jnp.where(first, pos, 0.0), axis=-1, keepdims=True)
    rank2 = jnp.sum(jnp.where(second, pos, 0.0), axis=-1, keepdims=True)
    fields = (i1, i2, w1, w2, rank1, rank2)
    record = jnp.zeros_like(logits)
    for k, field in enumerate(fields):
        record = jnp.where(lane == float(k), field, record)
    route_ref[...] = record


def _conv(hg, x2, w_dw, b_dw, ln_g, ln_b, w2, b2, pg, pb, w_router, *, seq, tm):
    n, d = x2.shape
    ratio = tm // CONV_HALO
    vec = _full((1, d))
    tri = (lax.broadcasted_iota(jnp.int32, (tm, tm), 1)
           < lax.broadcasted_iota(jnp.int32, (tm, tm), 0)).astype(BF16)
    return pl.pallas_call(
        functools.partial(_conv_kernel, tiles_per_seq=seq // tm),
        grid=(n // tm,),
        in_specs=[pl.BlockSpec((tm, d), lambda i: (i, 0)),
                  pl.BlockSpec((CONV_HALO, d), lambda i: (jnp.maximum(i * ratio - 1, 0), 0)),
                  pl.BlockSpec((tm, d), lambda i: (i, 0)),
                  _full((CONV_HALO, d)), vec, vec, vec, _full((d, d)), vec, vec, vec,
                  _full((d, 256)), _full((tm, tm))],
        out_specs=[pl.BlockSpec((tm, d), lambda i: (i, 0)),
                   pl.BlockSpec((tm, 128), lambda i: (i, 0)),
                   _full((SUBLANES, 128))],
        out_shape=[jax.ShapeDtypeStruct((n, d), F32),
                   jax.ShapeDtypeStruct((n, 128), F32),
                   jax.ShapeDtypeStruct((SUBLANES, 128), F32)],
        scratch_shapes=[pltpu.VMEM((SUBLANES, tm + CONV_HALO, d), F32), pltpu.VMEM((tm, d), F32),
                        pltpu.VMEM((1, 128), F32)],
        compiler_params=_params("arbitrary"),
        name="conv_module",
    )(hg, hg, x2, w_dw, b_dw, ln_g, ln_b, w2, b2, pg, pb, w_router, tri)


MOE_TILE = 512
SC_CORES = 2
SC_SUBCORES = 16
SC_CHUNK = 64


def _sc_row_scatter(x, slot_a, slot_b, slots):
    n, d = x.shape
    per_worker = n // (SC_CORES * SC_SUBCORES)
    mesh = plsc.VectorSubcoreMesh(core_axis_name="c", subcore_axis_name="s",
                                  num_cores=SC_CORES, num_subcores=SC_SUBCORES)

    @functools.partial(
        pl.kernel, mesh=mesh,
        out_type=jax.ShapeDtypeStruct((slots, d), x.dtype),
        scratch_types=[pltpu.VMEM((SC_CHUNK,), jnp.int32),
                       pltpu.VMEM((SC_CHUNK,), jnp.int32),
                       pltpu.VMEM((SC_CHUNK, d), x.dtype),
                       pltpu.SemaphoreType.DMA],
        name="moe_sc_row_scatter")
    def scatter(x_hbm, a_hbm, b_hbm, out_hbm, a_v, b_v, rows_v, sem):
        worker = lax.axis_index("s") * SC_CORES + lax.axis_index("c")
        base = worker * per_worker

        @pl.loop(0, per_worker // SC_CHUNK)
        def _(ci):
            off = base + ci * SC_CHUNK
            pltpu.sync_copy(a_hbm.at[pl.ds(off, SC_CHUNK)], a_v)
            pltpu.sync_copy(b_hbm.at[pl.ds(off, SC_CHUNK)], b_v)
            pltpu.sync_copy(x_hbm.at[pl.ds(off, SC_CHUNK)], rows_v)
            pltpu.async_copy(rows_v, out_hbm.at[a_v], sem).wait()
            pltpu.async_copy(rows_v, out_hbm.at[b_v], sem).wait()

    return scatter(x, slot_a, slot_b)


def _sc_row_gather(table, idx):
    rows = idx.shape[0]
    d = table.shape[1]
    per_worker = rows // (SC_CORES * SC_SUBCORES)
    mesh = plsc.VectorSubcoreMesh(core_axis_name="c", subcore_axis_name="s",
                                  num_cores=SC_CORES, num_subcores=SC_SUBCORES)

    @functools.partial(
        pl.kernel, mesh=mesh,
        out_type=jax.ShapeDtypeStruct((rows, d), table.dtype),
        scratch_types=[pltpu.VMEM((SC_CHUNK,), jnp.int32),
                       pltpu.VMEM((SC_CHUNK, d), table.dtype),
                       pltpu.SemaphoreType.DMA],
        name="moe_sc_row_gather")
    def gather(table_hbm, idx_hbm, out_hbm, idx_v, rows_v, sem):
        worker = lax.axis_index("s") * SC_CORES + lax.axis_index("c")
        base = worker * per_worker

        @pl.loop(0, per_worker // SC_CHUNK)
        def _(ci):
            off = base + ci * SC_CHUNK
            pltpu.sync_copy(idx_hbm.at[pl.ds(off, SC_CHUNK)], idx_v)
            pltpu.async_copy(table_hbm.at[idx_v], rows_v, sem).wait()
            pltpu.sync_copy(rows_v, out_hbm.at[pl.ds(off, SC_CHUNK)])

    return gather(table, idx)


def _expert_ffn_kernel(expert_ref, used_ref, x_ref, wg_ref, wu_ref, wd_ref, o_ref):
    i = pl.program_id(0)
    f = pl.program_id(1)

    @pl.when(f == 0)
    def _():
        o_ref[...] = jnp.zeros_like(o_ref)

    @pl.when(used_ref[i] > 0)
    def _():
        xb = x_ref[...].astype(BF16)
        g = _dot(xb, wg_ref[...])
        u = _dot(xb, wu_ref[...])
        h = (g * _sigmoid(g) * u).astype(BF16)
        o_ref[...] += _dot(h, wd_ref[...])


def _expert_ffn(xs, tile_expert, tile_used, wg, wu, wd, *, tf):
    slots, d = xs.shape
    ff = wg.shape[2]
    return pl.pallas_call(
        _expert_ffn_kernel,
        grid_spec=pltpu.PrefetchScalarGridSpec(
            num_scalar_prefetch=2, grid=(slots // MOE_TILE, ff // tf),
            in_specs=[pl.BlockSpec((MOE_TILE, d), lambda i, f, te, tu: (i, 0)),
                      pl.BlockSpec((None, d, tf), lambda i, f, te, tu: (te[i], 0, f)),
                      pl.BlockSpec((None, d, tf), lambda i, f, te, tu: (te[i], 0, f)),
                      pl.BlockSpec((None, tf, d), lambda i, f, te, tu: (te[i], f, 0))],
            out_specs=pl.BlockSpec((MOE_TILE, d), lambda i, f, te, tu: (i, 0))),
        out_shape=jax.ShapeDtypeStruct((slots, d), F32),
        compiler_params=_params("arbitrary", "arbitrary"),
        name="moe_expert_ffn",
    )(tile_expert, tile_used, xs, wg, wu, wd)


def _combine_kernel(x_ref, y1_ref, y2_ref, route_ref, lng_ref, lnb_ref, *rest):
    o_ref = rest[-1]
    route = route_ref[...]
    moe = route[:, 2:3] * y1_ref[...] + route[:, 3:4] * y2_ref[...]
    o_ref[...] = _layer_norm(ALPHA * x_ref[...] + moe, lng_ref[...], lnb_ref[...])


def _combine(x3, yt, route, ln_g, ln_b, *, tm, first_tile, earlier=None):
    n, d = x3.shape
    nt = yt.shape[0] // (TOP_K * tm)
    in_specs = [pl.BlockSpec((tm, d), lambda i: (i + first_tile, 0)),
                pl.BlockSpec((tm, d), lambda i: (i, 0)),
                pl.BlockSpec((tm, d), lambda i: (i + nt, 0)),
                pl.BlockSpec((tm, 128), lambda i: (i + first_tile, 0)),
                _full((1, d)), _full((1, d))]
    args = [x3, yt, yt, route, ln_g, ln_b]
    aliases = {}
    if earlier is not None:
        in_specs.append(pl.BlockSpec(memory_space=pl.ANY))
        args.append(earlier)
        aliases = {len(args) - 1: 0}
    return pl.pallas_call(
        _combine_kernel,
        grid=(nt,),
        in_specs=in_specs,
        out_specs=pl.BlockSpec((tm, d), lambda i: (i + first_tile, 0)),
        out_shape=jax.ShapeDtypeStruct((n, d), F32),
        input_output_aliases=aliases,
        compiler_params=_params("parallel"),
        name="moe_combine",
    )(*args)


def _moe(x3, route, tot, wg, wu, wd, ln_g, ln_b, *, tm, tf):
    n, d = x3.shape
    ne = wg.shape[0]
    slots = TOP_K * n + ne * MOE_TILE
    count = tot[0, :ne].astype(jnp.int32)
    cap = (count + (MOE_TILE - 1)) // MOE_TILE * MOE_TILE
    ends = jnp.cumsum(cap)
    off = ends - cap
    e1 = route[:, 0].astype(jnp.int32)
    e2 = route[:, 1].astype(jnp.int32)
    slot1 = off[e1] + route[:, 4].astype(jnp.int32)
    slot2 = off[e2] + route[:, 5].astype(jnp.int32)
    tile_start = jnp.arange(slots // MOE_TILE, dtype=jnp.int32) * MOE_TILE
    tile_expert = jnp.minimum(jnp.searchsorted(ends, tile_start, side="right"),
                              ne - 1).astype(jnp.int32)
    tile_used = (tile_start < ends[-1]).astype(jnp.int32)

    xs = _sc_row_scatter(x3, slot1, slot2, slots)
    ys = _expert_ffn(xs, tile_expert, tile_used, wg, wu, wd, tf=tf)
    half = n // 2
    out = None
    for lo in (0, half):
        idx = jnp.concatenate([slot1[lo:lo + half], slot2[lo:lo + half]])
        yt = _sc_row_gather(ys, idx)
        out = _combine(x3, yt, route, ln_g, ln_b, tm=tm, first_tile=lo // tm, earlier=out)
    return out


def _pad_cols(w, width):
    return jnp.pad(w, ((0, 0), (0, width - w.shape[1])))


def _pad_rows(w, height):
    return jnp.pad(w, ((0, height - w.shape[0]), (0, 0)))


def _forward(x, mix_w_in, rwkv_mu, rwkv_w0, rwkv_w_up, rwkv_a0, rwkv_a_up, rwkv_g_up,
             rwkv_k_k, rwkv_k_a, rwkv_r_k, rwkv_gn_g, rwkv_gn_b, fox_b_f, mix_w_out,
             mix_ln_g, mix_ln_b, ffn_w_gate, ffn_w_up, ffn_w_down, ffn_ln_g, ffn_ln_b,
             conv_w_pw1, conv_b_pw1, conv_w_dw, conv_b_dw, conv_ln_g, conv_ln_b,
             conv_w_pw2, conv_b_pw2, conv_post_ln_g, conv_post_ln_b,
             moe_w_router, moe_w_gate, moe_w_up, moe_w_down, moe_ln_g, moe_ln_b,
             *, tm=512, chunk=64, nb_rwkv=4, tk_fox=512, tf_ffn=1536, tf_moe=1792):
    batch, seq, d = x.shape
    n = batch * seq
    gw = GROUP_W
    x2 = x.reshape(n, d)
    row = lambda t: t.reshape(1, -1)

    w_in = mix_w_in[0]
    mu = rwkv_mu[0]
    o_w, o_a, o_g = 3 * gw, 3 * gw + DECAY_LORA, 3 * gw + DECAY_LORA + AAA_LORA
    o_fox = o_g + GATE_LORA

    def lora_layout(t):
        return jnp.concatenate([t[..., :o_w],
                                _pad_cols(t[..., o_w:o_a], LORA_PAD),
                                _pad_cols(t[..., o_a:o_g], LORA_PAD),
                                _pad_cols(t[..., o_g:o_fox], LORA_PAD)], axis=-1)

    wa = lora_layout(w_in).astype(BF16)
    mu_a = lora_layout(row(mu))
    scale = LOG2E / math.sqrt(HEAD_DIM)
    wb = jnp.concatenate([w_in[:, o_fox:o_fox + gw] * scale,
                          w_in[:, o_fox + gw:o_fox + 3 * gw]], axis=1).astype(BF16)
    wf = _pad_cols(w_in[:, o_fox + 3 * gw:], 128).astype(BF16)
    bf = _pad_cols(row(fox_b_f[0]), 128)

    pr, qk, vt, c = _inproj(x2, wa, wb, wf, mu_a, bf, seq=seq, tm=tk_fox)

    wup = _pad_rows(rwkv_w_up[0], LORA_PAD).astype(BF16)
    aup = _pad_rows(rwkv_a_up[0], LORA_PAD).astype(BF16)
    gup = _pad_rows(rwkv_g_up[0], LORA_PAD).astype(BF16)
    k_k, k_a, r_k = row(rwkv_k_k[0]), row(rwkv_k_a[0]), row(rwkv_r_k[0])
    yr = _rwkv(pr, row(rwkv_w0[0]), wup, row(rwkv_a0[0]), aup, k_k, k_a,
               batch=batch, seq=seq, chunk=chunk, nb=nb_rwkv)

    yf = _fox(qk, vt, c, batch=batch, seq=seq, tq=2 * tk_fox, tk=tk_fox)

    w_out = mix_w_out[0].astype(BF16)
    x1 = _mixout(x2, pr, yr, yf, row(rwkv_a0[0]), aup, gup, k_a, r_k,
                 row(rwkv_gn_g[0]), row(rwkv_gn_b[0]), w_out[:gw], w_out[gw:],
                 row(mix_ln_g[0]), row(mix_ln_b[0]), tm=tm)
    x2b = _ffn(x1, ffn_w_gate[0].astype(BF16), ffn_w_up[0].astype(BF16),
               ffn_w_down[0].astype(BF16), row(ffn_ln_g[0]), row(ffn_ln_b[0]), tm=tm, tf=tf_ffn)

    hg = _glu(x2b, conv_w_pw1[0].astype(BF16), row(conv_b_pw1[0]), tm=tm)
    w_router = _pad_cols(moe_w_router[0], 128)
    wr_hi = w_router.astype(BF16)
    w_router = jnp.concatenate([wr_hi, (w_router - wr_hi.astype(F32)).astype(BF16)], axis=1)
    x3, route, tot = _conv(hg, x2b, _pad_rows(conv_w_dw[0], CONV_HALO), row(conv_b_dw[0]),
                                row(conv_ln_g[0]), row(conv_ln_b[0]),
                                conv_w_pw2[0].astype(BF16), row(conv_b_pw2[0]),
                                row(conv_post_ln_g[0]), row(conv_post_ln_b[0]),
                                w_router, seq=seq, tm=tm)
    out = _moe(x3, route, tot,moe_w_gate[0].astype(BF16), moe_w_up[0].astype(BF16),
               moe_w_down[0].astype(BF16), row(moe_ln_g[0]), row(moe_ln_b[0]),
               tm=tm, tf=tf_moe)
    return out.reshape(batch, seq, d)


def kernel(x, mix_w_in, rwkv_mu, rwkv_w0, rwkv_w_up, rwkv_a0, rwkv_a_up, rwkv_g_up, rwkv_k_k, rwkv_k_a, rwkv_r_k, rwkv_gn_g, rwkv_gn_b, fox_b_f, mix_w_out, mix_ln_g, mix_ln_b, ffn_w_gate, ffn_w_up, ffn_w_down, ffn_ln_g, ffn_ln_b, conv_w_pw1, conv_b_pw1, conv_w_dw, conv_b_dw, conv_ln_g, conv_ln_b, conv_w_pw2, conv_b_pw2, conv_post_ln_g, conv_post_ln_b, moe_w_router, moe_w_gate, moe_w_up, moe_w_down, moe_ln_g, moe_ln_b):
    return _forward(x, mix_w_in, rwkv_mu, rwkv_w0, rwkv_w_up, rwkv_a0, rwkv_a_up, rwkv_g_up,
                    rwkv_k_k, rwkv_k_a, rwkv_r_k, rwkv_gn_g, rwkv_gn_b, fox_b_f, mix_w_out,
                    mix_ln_g, mix_ln_b, ffn_w_gate, ffn_w_up, ffn_w_down, ffn_ln_g, ffn_ln_b,
                    conv_w_pw1, conv_b_pw1, conv_w_dw, conv_b_dw, conv_ln_g, conv_ln_b,
                    conv_w_pw2, conv_b_pw2, conv_post_ln_g, conv_post_ln_b,
                    moe_w_router, moe_w_gate, moe_w_up, moe_w_down, moe_ln_g, moe_ln_b)
```

```python
import functools
import math

import jax
import jax.numpy as jnp
from jax import lax
from jax.experimental import pallas as pl
from jax.experimental.pallas import tpu as pltpu
from jax.experimental.pallas import tpu_sc as plsc

F32 = jnp.float32
BF16 = jnp.bfloat16
LANES = 128

HEAD_DIM = 64
N_HEADS = 8
GROUP_W = N_HEADS * HEAD_DIM
LORA_PAD = LANES
DECAY_LORA = 32
AAA_LORA = 32
GATE_LORA = 96
CONV_WIDTH = 31
CONV_HALO = 32
SUBLANES = 8
N_EXPERTS = 8
TOP_K = 2
LN_EPS = 1e-5
GN_EPS = 64e-5
DEPTH = 2
ALPHA = (2.0 * DEPTH) ** 0.25
NEG_BIG = -1e30
LOG2E = math.log2(math.e)
VMEM_LIMIT = 56 * 1024 * 1024


def _dot(a, b, **kw):
    return jnp.dot(a, b, preferred_element_type=F32, **kw)


def _dot_nt(a, b):
    return lax.dot_general(a, b, (((1,), (1,)), ((), ())), preferred_element_type=F32)


def _dot_tn(a, b):
    return lax.dot_general(a, b, (((0,), (0,)), ((), ())), preferred_element_type=F32)


def _dot_exact_lhs(a, v):
    hi = v.astype(BF16)
    rem = v - hi.astype(F32)
    mid = rem.astype(BF16)
    lo = (rem - mid.astype(F32)).astype(BF16)
    w = v.shape[1]
    out = _dot(a, jnp.concatenate([hi, mid, lo], axis=1))
    return out[:, :w] + out[:, w:2 * w] + out[:, 2 * w:]


def _sigmoid(z):
    return 1.0 / (1.0 + jnp.exp(-z))


def _softplus(z):
    return jnp.maximum(z, 0.0) + jnp.log1p(jnp.exp(-jnp.abs(z)))


def _layer_norm(h, g, b):
    mu = jnp.mean(h, axis=-1, keepdims=True)
    d = h - mu
    var = jnp.mean(d * d, axis=-1, keepdims=True)
    return d * lax.rsqrt(var + LN_EPS) * g + b


def _params(*sem):
    return pltpu.CompilerParams(dimension_semantics=sem, vmem_limit_bytes=VMEM_LIMIT)


def _full(shape):
    return pl.BlockSpec(shape, lambda *_: (0,) * len(shape))


def _inproj_kernel(x_ref, wa_ref, wb_ref, wf_ref, mu_ref, bf_ref, tri_ref,
                   pr_ref, qk_ref, vt_ref, c_ref, last_ref, carry_ref, *, tiles_per_seq):
    i = pl.program_id(0)

    @pl.when(i % tiles_per_seq == 0)
    def _():
        last_ref[...] = jnp.zeros_like(last_ref)
        carry_ref[...] = jnp.zeros_like(carry_ref)

    xb = x_ref[...].astype(BF16)
    tm = xb.shape[0]
    row0 = lax.broadcasted_iota(jnp.int32, (tm, 1), 0) == 0
    ca = wa_ref.shape[1]
    for c0 in range(0, ca, GROUP_W):
        cw = min(GROUP_W, ca - c0)
        p = _dot(xb, wa_ref[:, c0:c0 + cw])
        prev = jnp.where(row0, last_ref[:, c0:c0 + cw], pltpu.roll(p, 1, 0))
        last_ref[:, c0:c0 + cw] = p[tm - 1:tm, :]
        pr_ref[:, c0:c0 + cw] = p + mu_ref[:, c0:c0 + cw] * (prev - p)
    for c0 in range(0, 2 * GROUP_W, GROUP_W):
        qk_ref[:, c0:c0 + GROUP_W] = _dot(xb, wb_ref[:, c0:c0 + GROUP_W]).astype(BF16)
    vt_ref[...] = _dot(xb, wb_ref[:, 2 * GROUP_W:3 * GROUP_W]).T.astype(BF16)
    fl = _dot(xb, wf_ref[...]) + bf_ref[...]
    log_f = jnp.minimum(fl, 0.0) - jnp.log1p(jnp.exp(-jnp.abs(fl)))
    c = _dot_exact_lhs(tri_ref[...], log_f) + carry_ref[...]
    c_ref[...] = c
    carry_ref[...] = c[tm - 1:tm, :]


def _inproj(x2, wa, wb, wf, mu, bf, *, seq, tm):
    n, d = x2.shape
    ca, cb = wa.shape[1], wb.shape[1]
    tps = seq // tm
    tri = (lax.broadcasted_iota(jnp.int32, (tm, tm), 1)
           <= lax.broadcasted_iota(jnp.int32, (tm, tm), 0)).astype(BF16)
    return pl.pallas_call(
        functools.partial(_inproj_kernel, tiles_per_seq=tps),
        grid=(n // tm,),
        in_specs=[pl.BlockSpec((tm, d), lambda i: (i, 0)),
                  _full((d, ca)), _full((d, cb)), _full((d, LANES)),
                  _full((1, ca)), _full((1, LANES)), _full((tm, tm))],
        out_specs=[pl.BlockSpec((tm, ca), lambda i: (i, 0)),
                   pl.BlockSpec((tm, 2 * GROUP_W), lambda i: (i, 0)),
                   pl.BlockSpec((None, None, GROUP_W, tm), lambda i: (i // tps, i % tps, 0, 0)),
                   pl.BlockSpec((tm, LANES), lambda i: (i, 0))],
        out_shape=[jax.ShapeDtypeStruct((n, ca), F32),
                   jax.ShapeDtypeStruct((n, 2 * GROUP_W), BF16),
                   jax.ShapeDtypeStruct((n // seq, tps, GROUP_W, tm), BF16),
                   jax.ShapeDtypeStruct((n, LANES), F32)],
        scratch_shapes=[pltpu.VMEM((1, ca), F32), pltpu.VMEM((1, LANES), F32)],
        compiler_params=_params("arbitrary"),
        name="inproj",
    )(x2, wa, wb, wf, mu, bf, tri)


def _rwkv_kernel(pr_ref, w0_ref, wup_ref, a0_ref, aup_ref, kk_ref, ka_ref,
                 gsum_ref, tri_ref, y_ref, h_ref, *, chunk, nb):
    @pl.when(pl.program_id(1) == 0)
    def _():
        h_ref[...] = jnp.zeros_like(h_ref)

    gw = GROUP_W
    pw = 2 * HEAD_DIM
    npair = N_HEADS // 2
    rows = 2 * chunk
    log_chunk = int(math.log2(chunk))
    head0 = lax.broadcasted_iota(jnp.int32, (1, pw), 1) < HEAD_DIM
    row = lax.broadcasted_iota(jnp.int32, (rows, rows), 0)
    col = lax.broadcasted_iota(jnp.int32, (rows, rows), 1)
    strict = (col & (chunk - 1)) < (row & (chunk - 1))
    incl = (col & (chunk - 1)) <= (row & (chunk - 1))
    eye = (col == row).astype(F32)
    peye = (lax.broadcasted_iota(jnp.int32, (pw, pw), 0)
            == lax.broadcasted_iota(jnp.int32, (pw, pw), 1))

    def stack(x):
        return jnp.concatenate([jnp.where(head0, x, 0.0), jnp.where(head0, 0.0, x)],
                               axis=0).astype(BF16)

    units = [(b, j) for b in range(nb) for j in range(npair)]
    nu = len(units)
    x = pr_ref[...].reshape(nb * chunk, pr_ref.shape[2])
    r = x[:, 0:gw]
    k = x[:, gw:2 * gw]
    v = x[:, 2 * gw:3 * gw]
    wd = x[:, 3 * gw:3 * gw + LORA_PAD]
    ad = x[:, 3 * gw + LORA_PAD:3 * gw + 2 * LORA_PAD]
    w_pre = w0_ref[...] + _dot(jnp.tanh(wd).astype(BF16), wup_ref[...])
    w = -_softplus(-w_pre) - 0.5
    log_decay = -jnp.exp(w)
    a = _sigmoid(a0_ref[...] + _dot(ad.astype(BF16), aup_ref[...]))
    kk = k * kk_ref[...]
    norm = jnp.sqrt(_dot((kk * kk).astype(BF16), gsum_ref[...]))
    kk = kk / jnp.maximum(norm, 1e-12)
    k_mod = k * (1.0 + (a - 1.0) * ka_ref[...])
    b_vec = kk * a
    cum = _dot_exact_lhs(tri_ref[...], log_decay)
    lasts = [cum[(b + 1) * chunk - 1:(b + 1) * chunk, :] for b in range(nb)]
    last = jnp.concatenate([jnp.broadcast_to(t, (chunk, gw)) for t in lasts], axis=0)
    p_inv = jnp.exp(-cum)
    p_tail = jnp.exp(last - cum)
    a_t = -kk * jnp.exp(cum - log_decay)
    r_t = r * jnp.exp(cum)
    b_t = b_vec * p_inv
    k_t = k_mod * p_inv
    b_h = b_vec * p_tail
    k_h = k_mod * p_tail
    ar, bk, vs, bhs, khs, pcs = [], [], [], [], [], []
    for b in range(nb):
        rb = slice(b * chunk, (b + 1) * chunk)
        p_last = jnp.exp(lasts[b])
        for j in range(npair):
            sl = slice(j * pw, (j + 1) * pw)
            ar.append(jnp.concatenate([stack(a_t[rb, sl]), stack(r_t[rb, sl])], axis=0))
            bk.append(jnp.concatenate([stack(b_t[rb, sl]), stack(k_t[rb, sl])], axis=0))
            vs.append(stack(v[rb, sl]))
            bhs.append(stack(b_h[rb, sl]))
            khs.append(stack(k_h[rb, sl]))
            pcs.append(jnp.sum(jnp.where(peye, p_last[:, sl], 0.0), axis=1, keepdims=True))

    gram = [_dot_nt(ar[u], bk[u]) for u in range(nu)]
    l_ab = [jnp.where(strict, gram[u][:rows, :rows], 0.0) for u in range(nu)]
    l_akv = [_dot(jnp.where(strict, gram[u][:rows, rows:], 0.0).astype(BF16), vs[u])
             for u in range(nu)]
    m_rb = [jnp.where(incl, gram[u][rows:, :rows], 0.0).astype(BF16) for u in range(nu)]
    m_rkv = [_dot(jnp.where(incl, gram[u][rows:, rows:], 0.0).astype(BF16), vs[u])
             for u in range(nu)]
    t_inv = [eye + l_ab[u] for u in range(nu)]
    xb = [l_ab[u].astype(BF16) for u in range(nu)]
    xp = [_dot(xb[u], xb[u]) for u in range(nu)]
    for step in range(log_chunk - 1):
        xb = [xp[u].astype(BF16) for u in range(nu)]
        if step < log_chunk - 2:
            both = [_dot(jnp.concatenate([t_inv[u].astype(BF16), xb[u]], axis=0), xb[u])
                    for u in range(nu)]
            t_inv = [t_inv[u] + both[u][:rows] for u in range(nu)]
            xp = [both[u][rows:] for u in range(nu)]
        else:
            t_inv = [t_inv[u] + _dot(t_inv[u].astype(BF16), xb[u]) for u in range(nu)]
    tw = [_dot(t_inv[u].astype(BF16),
               jnp.concatenate([ar[u][:rows], l_akv[u].astype(BF16)], axis=1)).astype(BF16)
          for u in range(nu)]
    mw = [_dot(m_rb[u], tw[u]) for u in range(nu)]
    bw = [_dot_tn(bhs[u], tw[u]) for u in range(nu)]
    kv = [_dot_tn(khs[u], vs[u]) for u in range(nu)]
    for u, (b, j) in enumerate(units):
        wy = ar[u][rows:].astype(F32) + mw[u][:, :pw]
        yc = mw[u][:, pw:] + m_rkv[u]
        hf = h_ref[u]
        yh = _dot(jnp.concatenate([wy.astype(BF16), bw[u][:, :pw].astype(BF16)], axis=0),
                  hf.astype(BF16))
        h_ref[u] = pcs[u] * hf + yh[rows:] + bw[u][:, pw:] + kv[u]
        ys = yh[:rows] + yc
        y_ref[b, :, j * pw:(j + 1) * pw] = ys[:chunk] + ys[chunk:]


def _rwkv(pr, w0, wup, a0, aup, k_k, k_a, *, batch, seq, chunk, nb):
    n, ca = pr.shape
    nch = seq // chunk
    gidx = lax.broadcasted_iota(jnp.int32, (GROUP_W, GROUP_W), 0) // HEAD_DIM
    gsum = (gidx == gidx.T).astype(BF16)
    rr = lax.broadcasted_iota(jnp.int32, (nb * chunk, nb * chunk), 0)
    cc = lax.broadcasted_iota(jnp.int32, (nb * chunk, nb * chunk), 1)
    tri = jnp.logical_and(cc <= rr, cc // chunk == rr // chunk).astype(BF16)
    y = pl.pallas_call(
        functools.partial(_rwkv_kernel, chunk=chunk, nb=nb),
        grid=(batch // nb, nch),
        in_specs=[pl.BlockSpec((nb, chunk, ca), lambda g, c: (g, c, 0)),
                  _full((1, GROUP_W)), _full((LORA_PAD, GROUP_W)),
                  _full((1, GROUP_W)), _full((LORA_PAD, GROUP_W)),
                  _full((1, GROUP_W)), _full((1, GROUP_W)),
                  _full((GROUP_W, GROUP_W)), _full((nb * chunk, nb * chunk))],
        out_specs=pl.BlockSpec((nb, chunk, GROUP_W), lambda g, c: (g, c, 0)),
        out_shape=jax.ShapeDtypeStruct((batch, seq, GROUP_W), F32),
        scratch_shapes=[pltpu.VMEM((nb * (N_HEADS // 2), 2 * HEAD_DIM, 2 * HEAD_DIM), F32)],
        compiler_params=_params("arbitrary", "arbitrary"),
        name="rwkv_scan",
    )(pr.reshape(batch, seq, ca), w0, wup, a0, aup, k_k, k_a, gsum, tri)
    return y.reshape(n, GROUP_W)


def _fox_kernel(q_ref, k_ref, vt_ref, c_ref, o_ref, acc_ref, m_ref, l_ref, kb_ref,
                sa_ref, sb_ref, *, tq, tk):
    j = pl.program_id(1)
    qi = pl.program_id(2)
    pw = 2 * HEAD_DIM
    seq = k_ref.shape[0]
    lane = lax.broadcasted_iota(jnp.int32, (1, pw), 1)

    first = lane < HEAD_DIM
    bias_lane = (HEAD_DIM, 0)

    @pl.when(qi == 0)
    def _():
        def fill(rb, carry):
            rs = pl.multiple_of(rb * tk, tk)
            cblk = c_ref[pl.ds(rs, tk), :]
            for hh in range(2):
                bias = -LOG2E * jnp.sum(jnp.where(lane == 2 * j + hh, cblk, 0.0),
                                        axis=1, keepdims=True)
                b_hi = bias.astype(BF16).astype(F32)
                b_mid = (bias - b_hi).astype(BF16).astype(F32)
                b_lo = bias - b_hi - b_mid
                l0 = bias_lane[hh]
                kb_ref[hh, pl.ds(rs, tk), :] = jnp.where(
                    lane == l0, b_hi, jnp.where(lane == l0 + 1, b_mid,
                                                jnp.where(lane == l0 + 2, b_lo, 0.0))).astype(BF16)
            return carry
        lax.fori_loop(0, seq // tk, fill, 0)

    q = q_ref[...]
    ones3 = [jnp.where(jnp.logical_and(lane >= l0, lane < l0 + 3), 1.0, 0.0).astype(BF16)
             for l0 in bias_lane]
    own = (first, jnp.logical_not(first))
    qh = tuple(jnp.where(own[hh], q, ones3[hh]) for hh in range(2))
    acc_ref[...] = jnp.zeros_like(acc_ref)
    m_ref[...] = jnp.full_like(m_ref, NEG_BIG)
    l_ref[...] = jnp.zeros_like(l_ref)
    key_minus_query = (lax.broadcasted_iota(jnp.int32, (tk, tq), 0)
                       - lax.broadcasted_iota(jnp.int32, (tk, tq), 1))
    top = lax.broadcasted_iota(jnp.int32, (pw, 1), 0) < HEAD_DIM

    def scores(kb, s_ref):
        ks = pl.multiple_of(kb * tk, tk)
        kblk = k_ref[pl.ds(ks, tk), :]
        for hh in range(2):
            k_aug = jnp.where(own[hh], kblk, kb_ref[hh, pl.ds(ks, tk), :])
            s_ref[hh] = _dot_nt(k_aug, qh[hh])

    def softmax_pv(kb, s_ref, masked):
        causal = key_minus_query <= qi * tq - kb * tk
        vt = vt_ref[kb].astype(F32)
        vts = (jnp.where(top, vt, 1.0).astype(BF16), jnp.where(top, 1.0, vt).astype(BF16))
        alphas, pvs = [], []
        for hh in range(2):
            z = s_ref[hh]
            if masked:
                z = jnp.where(causal, z, NEG_BIG)
            m_prev = m_ref[hh]
            m_new = jnp.maximum(m_prev, jnp.max(z, axis=0, keepdims=True))
            alpha = jnp.exp2(m_prev - m_new)
            p = jnp.exp2(z - m_new)
            pv = _dot(vts[hh], p.astype(BF16))
            ones_row = (1 - hh) * HEAD_DIM
            l_ref[hh] = alpha * l_ref[hh] + pv[ones_row:ones_row + 1, :]
            m_ref[hh] = m_new
            alphas.append(alpha)
            pvs.append(pv)
        acc_ref[...] = (acc_ref[...] * jnp.where(top, alphas[0], alphas[1])
                        + jnp.where(top, pvs[0], pvs[1]))

    scores(0, sa_ref)

    def body(i, carry):
        scores(2 * i + 1, sb_ref)
        softmax_pv(2 * i, sa_ref, False)
        scores(2 * i + 2, sa_ref)
        softmax_pv(2 * i + 1, sb_ref, False)
        return carry

    lax.fori_loop(0, qi, body, 0)
    scores(2 * qi + 1, sb_ref)
    softmax_pv(2 * qi, sa_ref, True)
    softmax_pv(2 * qi + 1, sb_ref, True)

    out_t = acc_ref[...] / jnp.where(top, l_ref[0], l_ref[1])
    o_ref[...] = out_t.T.astype(BF16)


def _fox(qk, vt, c, *, batch, seq, tq, tk):
    n = qk.shape[0]
    nq = seq // tq
    nk = seq // tk
    npair = N_HEADS // 2
    pw = 2 * HEAD_DIM
    assert tq == 2 * tk and vt.shape == (batch, nk, GROUP_W, tk)
    return pl.pallas_call(
        functools.partial(_fox_kernel, tq=tq, tk=tk),
        grid=(batch, npair, nq),
        in_specs=[pl.BlockSpec((tq, pw), lambda b, j, i: (b * nq + i, j)),
                  pl.BlockSpec((seq, pw), lambda b, j, i: (b, npair + j)),
                  pl.BlockSpec((None, nk, pw, tk), lambda b, j, i: (b, 0, j, 0)),
                  pl.BlockSpec((seq, LANES), lambda b, j, i: (b, 0))],
        out_specs=pl.BlockSpec((tq, pw), lambda b, j, i: (b * nq + i, j)),
        out_shape=jax.ShapeDtypeStruct((n, GROUP_W), BF16),
        scratch_shapes=[pltpu.VMEM((pw, tq), F32),
                        pltpu.VMEM((2, 1, tq), F32), pltpu.VMEM((2, 1, tq), F32),
                        pltpu.VMEM((2, seq, pw), BF16),
                        pltpu.VMEM((2, tk, tq), F32), pltpu.VMEM((2, tk, tq), F32)],
        compiler_params=_params("arbitrary", "arbitrary", "arbitrary"),
        name="fox_attention",
    )(qk, qk, vt, c)


def _mixout_kernel(x_ref, pr_ref, yr_ref, yf_ref, a0_ref, aup_ref, gup_ref, ka_ref,
                   rk_ref, gng_ref, gnb_ref, gsum_ref, wr_ref, wf_ref, lng_ref, lnb_ref,
                   o_ref):
    gw = GROUP_W
    r = pr_ref[:, 0:gw]
    k = pr_ref[:, gw:2 * gw]
    v = pr_ref[:, 2 * gw:3 * gw]
    ad = pr_ref[:, 3 * gw + LORA_PAD:3 * gw + 2 * LORA_PAD]
    gd = pr_ref[:, 3 * gw + 2 * LORA_PAD:3 * gw + 3 * LORA_PAD]
    a = _sigmoid(a0_ref[...] + _dot(ad.astype(BF16), aup_ref[...]))
    k_mod = k * (1.0 + (a - 1.0) * ka_ref[...])
    gate = _dot(_sigmoid(gd).astype(BF16), gup_ref[...])
    gsum = gsum_ref[...]

    def group_sum(t):
        return _dot(t.astype(BF16), gsum)

    y = yr_ref[...]
    y_hi = y.astype(BF16)
    y_lo = (y - y_hi.astype(F32)).astype(BF16)
    mean = (_dot(y_hi, gsum) + _dot(y_lo, gsum)) * (1.0 / HEAD_DIM)
    d = y - mean
    var = group_sum(d * d) * (1.0 / HEAD_DIM)
    yn = d * lax.rsqrt(var + GN_EPS) * gng_ref[...] + gnb_ref[...]
    bonus = group_sum(r * k_mod * rk_ref[...])
    y_rwkv = ((yn + bonus * v) * gate).astype(BF16)
    mixed = _dot(y_rwkv, wr_ref[...]) + _dot(yf_ref[...], wf_ref[...])
    o_ref[...] = _layer_norm(ALPHA * x_ref[...] + mixed, lng_ref[...], lnb_ref[...])


def _mixout(x2, pr, yr, yf, a0, aup, gup, k_a, r_k, gn_g, gn_b, w_r, w_f, ln_g, ln_b, *, tm):
    n, d = x2.shape
    ca = pr.shape[1]
    gidx = lax.broadcasted_iota(jnp.int32, (GROUP_W, GROUP_W), 0) // HEAD_DIM
    gsum = (gidx == gidx.T).astype(BF16)
    vec = _full((1, GROUP_W))
    return pl.pallas_call(
        _mixout_kernel,
        grid=(n // tm,),
        in_specs=[pl.BlockSpec((tm, d), lambda i: (i, 0)),
                  pl.BlockSpec((tm, ca), lambda i: (i, 0)),
                  pl.BlockSpec((tm, GROUP_W), lambda i: (i, 0)),
                  pl.BlockSpec((tm, GROUP_W), lambda i: (i, 0)),
                  vec, _full((LORA_PAD, GROUP_W)), _full((LORA_PAD, GROUP_W)),
                  vec, vec, vec, vec, _full((GROUP_W, GROUP_W)),
                  _full((GROUP_W, d)), _full((GROUP_W, d)),
                  _full((1, d)), _full((1, d))],
        out_specs=pl.BlockSpec((tm, d), lambda i: (i, 0)),
        out_shape=jax.ShapeDtypeStruct((n, d), F32),
        compiler_params=_params("parallel"),
        name="mix_out",
    )(x2, pr, yr, yf, a0, aup, gup, k_a, r_k, gn_g, gn_b, gsum, w_r, w_f, ln_g, ln_b)


def _ffn_kernel(x_ref, wg_ref, wu_ref, wd_ref, lng_ref, lnb_ref, o_ref, *, tf):
    xb = x_ref[...].astype(BF16)
    ff = wg_ref.shape[1]
    acc = None
    for f0 in range(0, ff, tf):
        f1 = min(f0 + tf, ff)
        g = _dot(xb, wg_ref[:, f0:f1])
        u = _dot(xb, wu_ref[:, f0:f1])
        h = (g * _sigmoid(g) * u).astype(BF16)
        part = _dot(h, wd_ref[f0:f1, :])
        acc = part if acc is None else acc + part
    o_ref[...] = _layer_norm(ALPHA * x_ref[...] + acc, lng_ref[...], lnb_ref[...])


def _ffn(x2, wg, wu, wd, ln_g, ln_b, *, tm, tf):
    n, d = x2.shape
    ff = wg.shape[1]
    once = pl.Buffered(1)

    def resident(shape):
        return pl.BlockSpec(shape, lambda i: (0, 0), pipeline_mode=once)

    return pl.pallas_call(
        functools.partial(_ffn_kernel, tf=tf),
        grid=(n // tm,),
        in_specs=[pl.BlockSpec((tm, d), lambda i: (i, 0)),
                  resident((d, ff)), resident((d, ff)), resident((ff, d)),
                  _full((1, d)), _full((1, d))],
        out_specs=pl.BlockSpec((tm, d), lambda i: (i, 0)),
        out_shape=jax.ShapeDtypeStruct((n, d), F32),
        compiler_params=_params("parallel"),
        name="ffn_swiglu",
    )(x2, wg, wu, wd, ln_g, ln_b)


def _glu_kernel(x_ref, w_ref, b_ref, o_ref):
    d = o_ref.shape[1]
    xb = x_ref[...].astype(BF16)
    val = _dot(xb, w_ref[:, 0:d]) + b_ref[:, 0:d]
    gat = _dot(xb, w_ref[:, d:2 * d]) + b_ref[:, d:2 * d]
    o_ref[...] = val * _sigmoid(gat)


def _glu(x2, w, b, *, tm):
    n, d = x2.shape
    return pl.pallas_call(
        _glu_kernel,
        grid=(n // tm,),
        in_specs=[pl.BlockSpec((tm, d), lambda i: (i, 0)), _full((d, 2 * d)), _full((1, 2 * d))],
        out_specs=pl.BlockSpec((tm, d), lambda i: (i, 0)),
        out_shape=jax.ShapeDtypeStruct((n, d), F32),
        compiler_params=_params("parallel"),
        name="conv_glu",
    )(x2, w, b)


def _top2(logits):
    lane = lax.broadcasted_iota(jnp.int32, logits.shape, 1).astype(F32)
    lg = jnp.where(lane < N_EXPERTS, logits, NEG_BIG)
    m1 = jnp.max(lg, axis=-1, keepdims=True)
    i1 = jnp.min(jnp.where(lg == m1, lane, float(LANES)), axis=-1, keepdims=True)
    lg2 = jnp.where(lane == i1, NEG_BIG, lg)
    m2 = jnp.max(lg2, axis=-1, keepdims=True)
    i2 = jnp.min(jnp.where(lg2 == m2, lane, float(LANES)), axis=-1, keepdims=True)
    e2 = jnp.exp(m2 - m1)
    w1 = 1.0 / (1.0 + e2)
    w2 = e2 / (1.0 + e2)
    return lane, i1, i2, w1, w2


def _conv_kernel(hc_ref, hp_ref, x_ref, wdw_ref, bdw_ref, lng_ref, lnb_ref, w2_ref, b2_ref,
                 pg_ref, pb_ref, wr_ref, tri_ref, x3_ref, route_ref, tot_ref,
                 ext_ref, cv_ref, cnt_ref, *, tiles_per_seq):
    tm, d = x_ref.shape

    @pl.when(pl.program_id(0) == 0)
    def _():
        cnt_ref[...] = jnp.zeros_like(cnt_ref)

    first = pl.program_id(0) % tiles_per_seq == 0
    ext_ref[0, 0:CONV_HALO, :] = jnp.where(first, 0.0, hp_ref[...])
    ext_ref[0, CONV_HALO:CONV_HALO + tm, :] = hc_ref[...]
    nrows = tm + CONV_HALO
    for c0 in range(0, d, 256):
        base = ext_ref[0, :, c0:c0 + 256]
        for j in range(1, SUBLANES):
            ext_ref[j, :, c0:c0 + 256] = pltpu.roll(base, nrows - j, 0)
    off = CONV_HALO - (CONV_WIDTH - 1)
    rc, cc = 64, 256
    for r0 in range(0, tm, rc):
        for c0 in range(0, d, cc):
            acc = jnp.broadcast_to(bdw_ref[:, c0:c0 + cc], (rc, cc))
            for t in range(CONV_WIDTH):
                base, j = divmod(off + t, SUBLANES)
                rs = r0 + base * SUBLANES
                acc = acc + wdw_ref[t:t + 1, c0:c0 + cc] * ext_ref[j, rs:rs + rc, c0:c0 + cc]
            cv_ref[r0:r0 + rc, c0:c0 + cc] = acc
    hn = _layer_norm(cv_ref[...], lng_ref[...], lnb_ref[...])
    hs = (hn * _sigmoid(hn)).astype(BF16)
    conv = _dot(hs, w2_ref[...]) + b2_ref[...]
    x3 = _layer_norm(ALPHA * x_ref[...] + conv, pg_ref[...], pb_ref[...])
    x3_ref[...] = x3
    x_hi = x3.astype(BF16)
    x_lo = (x3 - x_hi.astype(F32)).astype(BF16)
    hi_part = _dot(x_hi, wr_ref[...])
    logits = hi_part[:, :LANES] + hi_part[:, LANES:] + _dot(x_lo, wr_ref[:, 0:LANES])
    lane, i1, i2, w1, w2 = _top2(logits)
    first, second = lane == i1, lane == i2
    sel = jnp.where(jnp.logical_or(first, second), 1.0, 0.0)
    pos = _dot(tri_ref[...], sel.astype(BF16)) + cnt_ref[...]
    cnt_ref[...] += jnp.sum(sel, axis=0, keepdims=True)
    tot_ref[...] = jnp.broadcast_to(cnt_ref[...], tot_ref.shape)
    rank1 = jnp.sum(jnp.where(first, pos, 0.0), axis=-1, keepdims=True)
    rank2 = jnp.sum(jnp.where(second, pos, 0.0), axis=-1, keepdims=True)
    fields = (i1, i2, w1, w2, rank1, rank2)
    record = jnp.zeros_like(logits)
    for k, field in enumerate(fields):
        record = jnp.where(lane == float(k), field, record)
    route_ref[...] = record


def _conv(hg, x2, w_dw, b_dw, ln_g, ln_b, w2, b2, pg, pb, w_router, *, seq, tm):
    n, d = x2.shape
    ratio = tm // CONV_HALO
    vec = _full((1, d))
    tri = (lax.broadcasted_iota(jnp.int32, (tm, tm), 1)
           < lax.broadcasted_iota(jnp.int32, (tm, tm), 0)).astype(BF16)
    return pl.pallas_call(
        functools.partial(_conv_kernel, tiles_per_seq=seq // tm),
        grid=(n // tm,),
        in_specs=[pl.BlockSpec((tm, d), lambda i: (i, 0)),
                  pl.BlockSpec((CONV_HALO, d), lambda i: (jnp.maximum(i * ratio - 1, 0), 0)),
                  pl.BlockSpec((tm, d), lambda i: (i, 0)),
                  _full((CONV_HALO, d)), vec, vec, vec, _full((d, d)), vec, vec, vec,
                  _full((d, 2 * LANES)), _full((tm, tm))],
        out_specs=[pl.BlockSpec((tm, d), lambda i: (i, 0)),
                   pl.BlockSpec((tm, LANES), lambda i: (i, 0)),
                   _full((SUBLANES, LANES))],
        out_shape=[jax.ShapeDtypeStruct((n, d), F32),
                   jax.ShapeDtypeStruct((n, LANES), F32),
                   jax.ShapeDtypeStruct((SUBLANES, LANES), F32)],
        scratch_shapes=[pltpu.VMEM((SUBLANES, tm + CONV_HALO, d), F32), pltpu.VMEM((tm, d), F32),
                        pltpu.VMEM((1, LANES), F32)],
        compiler_params=_params("arbitrary"),
        name="conv_module",
    )(hg, hg, x2, w_dw, b_dw, ln_g, ln_b, w2, b2, pg, pb, w_router, tri)


MOE_TILE = 512
SC_CORES = 2
SC_SUBCORES = 16
SC_CHUNK = 64


def _sc_row_scatter(x, slot_a, slot_b, slots):
    n, d = x.shape
    per_worker = n // (SC_CORES * SC_SUBCORES)
    mesh = plsc.VectorSubcoreMesh(core_axis_name="c", subcore_axis_name="s",
                                  num_cores=SC_CORES, num_subcores=SC_SUBCORES)

    @functools.partial(
        pl.kernel, mesh=mesh,
        out_type=jax.ShapeDtypeStruct((slots, d), x.dtype),
        scratch_types=[pltpu.VMEM((SC_CHUNK,), jnp.int32),
                       pltpu.VMEM((SC_CHUNK,), jnp.int32),
                       pltpu.VMEM((SC_CHUNK, d), x.dtype),
                       pltpu.SemaphoreType.DMA],
        name="moe_sc_row_scatter")
    def scatter(x_hbm, a_hbm, b_hbm, out_hbm, a_v, b_v, rows_v, sem):
        worker = lax.axis_index("s") * SC_CORES + lax.axis_index("c")
        base = worker * per_worker

        @pl.loop(0, per_worker // SC_CHUNK)
        def _(ci):
            off = base + ci * SC_CHUNK
            pltpu.sync_copy(a_hbm.at[pl.ds(off, SC_CHUNK)], a_v)
            pltpu.sync_copy(b_hbm.at[pl.ds(off, SC_CHUNK)], b_v)
            pltpu.sync_copy(x_hbm.at[pl.ds(off, SC_CHUNK)], rows_v)
            pltpu.async_copy(rows_v, out_hbm.at[a_v], sem).wait()
            pltpu.async_copy(rows_v, out_hbm.at[b_v], sem).wait()

    return scatter(x, slot_a, slot_b)


def _sc_row_gather(table, idx):
    rows = idx.shape[0]
    d = table.shape[1]
    per_worker = rows // (SC_CORES * SC_SUBCORES)
    mesh = plsc.VectorSubcoreMesh(core_axis_name="c", subcore_axis_name="s",
                                  num_cores=SC_CORES, num_subcores=SC_SUBCORES)

    @functools.partial(
        pl.kernel, mesh=mesh,
        out_type=jax.ShapeDtypeStruct((rows, d), table.dtype),
        scratch_types=[pltpu.VMEM((SC_CHUNK,), jnp.int32),
                       pltpu.VMEM((SC_CHUNK, d), table.dtype),
                       pltpu.SemaphoreType.DMA],
        name="moe_sc_row_gather")
    def gather(table_hbm, idx_hbm, out_hbm, idx_v, rows_v, sem):
        worker = lax.axis_index("s") * SC_CORES + lax.axis_index("c")
        base = worker * per_worker

        @pl.loop(0, per_worker // SC_CHUNK)
        def _(ci):
            off = base + ci * SC_CHUNK
            pltpu.sync_copy(idx_hbm.at[pl.ds(off, SC_CHUNK)], idx_v)
            pltpu.async_copy(table_hbm.at[idx_v], rows_v, sem).wait()
            pltpu.sync_copy(rows_v, out_hbm.at[pl.ds(off, SC_CHUNK)])

    return gather(table, idx)


def _expert_ffn_kernel(expert_ref, used_ref, x_ref, wg_ref, wu_ref, wd_ref, o_ref):
    i = pl.program_id(0)
    f = pl.program_id(1)

    @pl.when(f == 0)
    def _():
        o_ref[...] = jnp.zeros_like(o_ref)

    @pl.when(used_ref[i] > 0)
    def _():
        xb = x_ref[...].astype(BF16)
        g = _dot(xb, wg_ref[...])
        u = _dot(xb, wu_ref[...])
        h = (g * _sigmoid(g) * u).astype(BF16)
        o_ref[...] += _dot(h, wd_ref[...])


def _expert_ffn(xs, tile_expert, tile_used, wg, wu, wd, *, tf):
    slots, d = xs.shape
    ff = wg.shape[2]
    return pl.pallas_call(
        _expert_ffn_kernel,
        grid_spec=pltpu.PrefetchScalarGridSpec(
            num_scalar_prefetch=2, grid=(slots // MOE_TILE, ff // tf),
            in_specs=[pl.BlockSpec((MOE_TILE, d), lambda i, f, te, tu: (i, 0)),
                      pl.BlockSpec((None, d, tf), lambda i, f, te, tu: (te[i], 0, f)),
                      pl.BlockSpec((None, d, tf), lambda i, f, te, tu: (te[i], 0, f)),
                      pl.BlockSpec((None, tf, d), lambda i, f, te, tu: (te[i], f, 0))],
            out_specs=pl.BlockSpec((MOE_TILE, d), lambda i, f, te, tu: (i, 0))),
        out_shape=jax.ShapeDtypeStruct((slots, d), F32),
        compiler_params=_params("arbitrary", "arbitrary"),
        name="moe_expert_ffn",
    )(tile_expert, tile_used, xs, wg, wu, wd)


def _combine_kernel(x_ref, y1_ref, y2_ref, route_ref, lng_ref, lnb_ref, o_ref):
    route = route_ref[...]
    moe = route[:, 2:3] * y1_ref[...] + route[:, 3:4] * y2_ref[...]
    o_ref[...] = _layer_norm(ALPHA * x_ref[...] + moe, lng_ref[...], lnb_ref[...])


def _combine(x3, yt, route, ln_g, ln_b, *, tm):
    n, d = x3.shape
    nt = n // tm
    return pl.pallas_call(
        _combine_kernel,
        grid=(nt,),
        in_specs=[pl.BlockSpec((tm, d), lambda i: (i, 0)),
                  pl.BlockSpec((tm, d), lambda i: (i, 0)),
                  pl.BlockSpec((tm, d), lambda i: (i + nt, 0)),
                  pl.BlockSpec((tm, LANES), lambda i: (i, 0)),
                  _full((1, d)), _full((1, d))],
        out_specs=pl.BlockSpec((tm, d), lambda i: (i, 0)),
        out_shape=jax.ShapeDtypeStruct((n, d), F32),
        compiler_params=_params("parallel"),
        name="moe_combine",
    )(x3, yt, yt, route, ln_g, ln_b)


def _moe(x3, route, tot, wg, wu, wd, ln_g, ln_b, *, tm, tf):
    n, d = x3.shape
    ne = wg.shape[0]
    slots = TOP_K * n + ne * MOE_TILE
    count = tot[0, :ne].astype(jnp.int32)
    cap = (count + (MOE_TILE - 1)) // MOE_TILE * MOE_TILE
    ends = jnp.cumsum(cap)
    off = ends - cap
    e1 = route[:, 0].astype(jnp.int32)
    e2 = route[:, 1].astype(jnp.int32)
    slot1 = off[e1] + route[:, 4].astype(jnp.int32)
    slot2 = off[e2] + route[:, 5].astype(jnp.int32)
    tile_start = jnp.arange(slots // MOE_TILE, dtype=jnp.int32) * MOE_TILE
    tile_expert = jnp.minimum(jnp.sum(tile_start[:, None] >= ends[None, :], axis=1),
                              ne - 1).astype(jnp.int32)
    tile_used = (tile_start < ends[-1]).astype(jnp.int32)

    xs = _sc_row_scatter(x3, slot1, slot2, slots)
    ys = _expert_ffn(xs, tile_expert, tile_used, wg, wu, wd, tf=tf)
    yt = _sc_row_gather(ys, jnp.concatenate([slot1, slot2]))
    return _combine(x3, yt, route, ln_g, ln_b, tm=tm)


def _pad_cols(w, width):
    return jnp.pad(w, ((0, 0), (0, width - w.shape[1])))


def _pad_rows(w, height):
    return jnp.pad(w, ((0, height - w.shape[0]), (0, 0)))


def _forward(x, mix_w_in, rwkv_mu, rwkv_w0, rwkv_w_up, rwkv_a0, rwkv_a_up, rwkv_g_up,
             rwkv_k_k, rwkv_k_a, rwkv_r_k, rwkv_gn_g, rwkv_gn_b, fox_b_f, mix_w_out,
             mix_ln_g, mix_ln_b, ffn_w_gate, ffn_w_up, ffn_w_down, ffn_ln_g, ffn_ln_b,
             conv_w_pw1, conv_b_pw1, conv_w_dw, conv_b_dw, conv_ln_g, conv_ln_b,
             conv_w_pw2, conv_b_pw2, conv_post_ln_g, conv_post_ln_b,
             moe_w_router, moe_w_gate, moe_w_up, moe_w_down, moe_ln_g, moe_ln_b,
             *, tm=512, chunk=64, nb_rwkv=4, tk_fox=512, tf_ffn=1536, tf_moe=1792):
    batch, seq, d = x.shape
    n = batch * seq
    gw = GROUP_W
    x2 = x.reshape(n, d)
    row = lambda t: t.reshape(1, -1)

    w_in = mix_w_in[0]
    mu = rwkv_mu[0]
    o_w, o_a, o_g = 3 * gw, 3 * gw + DECAY_LORA, 3 * gw + DECAY_LORA + AAA_LORA
    o_fox = o_g + GATE_LORA

    def lora_layout(t):
        return jnp.concatenate([t[..., :o_w],
                                _pad_cols(t[..., o_w:o_a], LORA_PAD),
                                _pad_cols(t[..., o_a:o_g], LORA_PAD),
                                _pad_cols(t[..., o_g:o_fox], LORA_PAD)], axis=-1)

    wa = lora_layout(w_in).astype(BF16)
    mu_a = lora_layout(row(mu))
    scale = LOG2E / math.sqrt(HEAD_DIM)
    wb = jnp.concatenate([w_in[:, o_fox:o_fox + gw] * scale,
                          w_in[:, o_fox + gw:o_fox + 3 * gw]], axis=1).astype(BF16)
    wf = _pad_cols(w_in[:, o_fox + 3 * gw:], LANES).astype(BF16)
    bf = _pad_cols(row(fox_b_f[0]), LANES)

    pr, qk, vt, c = _inproj(x2, wa, wb, wf, mu_a, bf, seq=seq, tm=tk_fox)

    wup = _pad_rows(rwkv_w_up[0], LORA_PAD).astype(BF16)
    aup = _pad_rows(rwkv_a_up[0], LORA_PAD).astype(BF16)
    gup = _pad_rows(rwkv_g_up[0], LORA_PAD).astype(BF16)
    k_k, k_a, r_k = row(rwkv_k_k[0]), row(rwkv_k_a[0]), row(rwkv_r_k[0])
    yr = _rwkv(pr, row(rwkv_w0[0]), wup, row(rwkv_a0[0]), aup, k_k, k_a,
               batch=batch, seq=seq, chunk=chunk, nb=nb_rwkv)

    yf = _fox(qk, vt, c, batch=batch, seq=seq, tq=2 * tk_fox, tk=tk_fox)

    w_out = mix_w_out[0].astype(BF16)
    x1 = _mixout(x2, pr, yr, yf, row(rwkv_a0[0]), aup, gup, k_a, r_k,
                 row(rwkv_gn_g[0]), row(rwkv_gn_b[0]), w_out[:gw], w_out[gw:],
                 row(mix_ln_g[0]), row(mix_ln_b[0]), tm=tm)
    x2b = _ffn(x1, ffn_w_gate[0].astype(BF16), ffn_w_up[0].astype(BF16),
               ffn_w_down[0].astype(BF16), row(ffn_ln_g[0]), row(ffn_ln_b[0]), tm=tm, tf=tf_ffn)

    hg = _glu(x2b, conv_w_pw1[0].astype(BF16), row(conv_b_pw1[0]), tm=tm)
    w_router = _pad_cols(moe_w_router[0], LANES)
    wr_hi = w_router.astype(BF16)
    w_router = jnp.concatenate([wr_hi, (w_router - wr_hi.astype(F32)).astype(BF16)], axis=1)
    x3, route, tot = _conv(hg, x2b, _pad_rows(conv_w_dw[0], CONV_HALO), row(conv_b_dw[0]),
                                row(conv_ln_g[0]), row(conv_ln_b[0]),
                                conv_w_pw2[0].astype(BF16), row(conv_b_pw2[0]),
                                row(conv_post_ln_g[0]), row(conv_post_ln_b[0]),
                                w_router, seq=seq, tm=tm)
    out = _moe(x3, route, tot,moe_w_gate[0].astype(BF16), moe_w_up[0].astype(BF16),
               moe_w_down[0].astype(BF16), row(moe_ln_g[0]), row(moe_ln_b[0]),
               tm=tm, tf=tf_moe)
    return out.reshape(batch, seq, d)


def kernel(x, mix_w_in, rwkv_mu, rwkv_w0, rwkv_w_up, rwkv_a0, rwkv_a_up, rwkv_g_up, rwkv_k_k, rwkv_k_a, rwkv_r_k, rwkv_gn_g, rwkv_gn_b, fox_b_f, mix_w_out, mix_ln_g, mix_ln_b, ffn_w_gate, ffn_w_up, ffn_w_down, ffn_ln_g, ffn_ln_b, conv_w_pw1, conv_b_pw1, conv_w_dw, conv_b_dw, conv_ln_g, conv_ln_b, conv_w_pw2, conv_b_pw2, conv_post_ln_g, conv_post_ln_b, moe_w_router, moe_w_gate, moe_w_up, moe_w_down, moe_ln_g, moe_ln_b):
    return _forward(x, mix_w_in, rwkv_mu, rwkv_w0, rwkv_w_up, rwkv_a0, rwkv_a_up, rwkv_g_up,
                    rwkv_k_k, rwkv_k_a, rwkv_r_k, rwkv_gn_g, rwkv_gn_b, fox_b_f, mix_w_out,
                    mix_ln_g, mix_ln_b, ffn_w_gate, ffn_w_up, ffn_w_down, ffn_ln_g, ffn_ln_b,
                    conv_w_pw1, conv_b_pw1, conv_w_dw, conv_b_dw, conv_ln_g, conv_ln_b,
                    conv_w_pw2, conv_b_pw2, conv_post_ln_g, conv_post_ln_b,
                    moe_w_router, moe_w_gate, moe_w_up, moe_w_down, moe_ln_g, moe_ln_b)
```

```python
import functools
import math

import jax
import jax.numpy as jnp
from jax import lax
from jax.experimental import pallas as pl
from jax.experimental.pallas import tpu as pltpu
from jax.experimental.pallas import tpu_sc as plsc

F32 = jnp.float32
BF16 = jnp.bfloat16
LANES = 128

HEAD_DIM = 64
N_HEADS = 8
GROUP_W = N_HEADS * HEAD_DIM
LORA_PAD = LANES
DECAY_LORA = 32
AAA_LORA = 32
GATE_LORA = 96
CONV_WIDTH = 31
CONV_HALO = 32
SUBLANES = 8
N_EXPERTS = 8
TOP_K = 2
LN_EPS = 1e-5
GN_EPS = 64e-5
DEPTH = 2
ALPHA = (2.0 * DEPTH) ** 0.25
NEG_BIG = -1e30
LOG2E = math.log2(math.e)
VMEM_LIMIT = 56 * 1024 * 1024


def _dot(a, b, **kw):
    return jnp.dot(a, b, preferred_element_type=F32, **kw)


def _dot_nt(a, b):
    return lax.dot_general(a, b, (((1,), (1,)), ((), ())), preferred_element_type=F32)


def _dot_tn(a, b):
    return lax.dot_general(a, b, (((0,), (0,)), ((), ())), preferred_element_type=F32)


def _dot_exact_lhs(a, v):
    hi = v.astype(BF16)
    rem = v - hi.astype(F32)
    mid = rem.astype(BF16)
    lo = (rem - mid.astype(F32)).astype(BF16)
    w = v.shape[1]
    out = _dot(a, jnp.concatenate([hi, mid, lo], axis=1))
    return out[:, :w] + out[:, w:2 * w] + out[:, 2 * w:]


def _sigmoid(z):
    return 1.0 / (1.0 + jnp.exp(-z))


def _softplus(z):
    return jnp.maximum(z, 0.0) + jnp.log1p(jnp.exp(-jnp.abs(z)))


def _layer_norm(h, g, b):
    mu = jnp.mean(h, axis=-1, keepdims=True)
    d = h - mu
    var = jnp.mean(d * d, axis=-1, keepdims=True)
    return d * lax.rsqrt(var + LN_EPS) * g + b


def _params(*sem):
    return pltpu.CompilerParams(dimension_semantics=sem, vmem_limit_bytes=VMEM_LIMIT)


def _full(shape):
    return pl.BlockSpec(shape, lambda *_: (0,) * len(shape))


def _inproj_kernel(x_ref, wa_ref, wb_ref, wf_ref, mu_ref, bf_ref, tri_ref,
                   pr_ref, qk_ref, vt_ref, c_ref, last_ref, carry_ref, *, tiles_per_seq):
    i = pl.program_id(0)

    @pl.when(i % tiles_per_seq == 0)
    def _():
        last_ref[...] = jnp.zeros_like(last_ref)
        carry_ref[...] = jnp.zeros_like(carry_ref)

    xb = x_ref[...].astype(BF16)
    tm = xb.shape[0]
    row0 = lax.broadcasted_iota(jnp.int32, (tm, 1), 0) == 0
    ca = wa_ref.shape[1]
    for c0 in range(0, ca, GROUP_W):
        cw = min(GROUP_W, ca - c0)
        p = _dot(xb, wa_ref[:, c0:c0 + cw])
        prev = jnp.where(row0, last_ref[:, c0:c0 + cw], pltpu.roll(p, 1, 0))
        last_ref[:, c0:c0 + cw] = p[tm - 1:tm, :]
        pr_ref[:, c0:c0 + cw] = p + mu_ref[:, c0:c0 + cw] * (prev - p)
    for c0 in range(0, 2 * GROUP_W, GROUP_W):
        qk_ref[:, c0:c0 + GROUP_W] = _dot(xb, wb_ref[:, c0:c0 + GROUP_W]).astype(BF16)
    vt_ref[...] = _dot(xb, wb_ref[:, 2 * GROUP_W:3 * GROUP_W]).T.astype(BF16)
    fl = _dot(xb, wf_ref[...]) + bf_ref[...]
    log_f = jnp.minimum(fl, 0.0) - jnp.log1p(jnp.exp(-jnp.abs(fl)))
    c = _dot_exact_lhs(tri_ref[...], log_f) + carry_ref[...]
    c_ref[...] = c
    carry_ref[...] = c[tm - 1:tm, :]


def _inproj(x2, wa, wb, wf, mu, bf, *, seq, tm):
    n, d = x2.shape
    ca, cb = wa.shape[1], wb.shape[1]
    tps = seq // tm
    tri = (lax.broadcasted_iota(jnp.int32, (tm, tm), 1)
           <= lax.broadcasted_iota(jnp.int32, (tm, tm), 0)).astype(BF16)
    return pl.pallas_call(
        functools.partial(_inproj_kernel, tiles_per_seq=tps),
        grid=(n // tm,),
        in_specs=[pl.BlockSpec((tm, d), lambda i: (i, 0)),
                  _full((d, ca)), _full((d, cb)), _full((d, LANES)),
                  _full((1, ca)), _full((1, LANES)), _full((tm, tm))],
        out_specs=[pl.BlockSpec((tm, ca), lambda i: (i, 0)),
                   pl.BlockSpec((tm, 2 * GROUP_W), lambda i: (i, 0)),
                   pl.BlockSpec((None, None, GROUP_W, tm), lambda i: (i // tps, i % tps, 0, 0)),
                   pl.BlockSpec((tm, LANES), lambda i: (i, 0))],
        out_shape=[jax.ShapeDtypeStruct((n, ca), F32),
                   jax.ShapeDtypeStruct((n, 2 * GROUP_W), BF16),
                   jax.ShapeDtypeStruct((n // seq, tps, GROUP_W, tm), BF16),
                   jax.ShapeDtypeStruct((n, LANES), F32)],
        scratch_shapes=[pltpu.VMEM((1, ca), F32), pltpu.VMEM((1, LANES), F32)],
        compiler_params=_params("arbitrary"),
        name="inproj",
    )(x2, wa, wb, wf, mu, bf, tri)


def _rwkv_kernel(pr_ref, w0_ref, wup_ref, a0_ref, aup_ref, kk_ref, ka_ref,
                 gsum_ref, tri_ref, y_ref, h_ref, *, chunk, nb):
    @pl.when(pl.program_id(1) == 0)
    def _():
        h_ref[...] = jnp.zeros_like(h_ref)

    gw = GROUP_W
    pw = 2 * HEAD_DIM
    npair = N_HEADS // 2
    rows = 2 * chunk
    log_chunk = int(math.log2(chunk))
    head0 = lax.broadcasted_iota(jnp.int32, (1, pw), 1) < HEAD_DIM
    row = lax.broadcasted_iota(jnp.int32, (rows, rows), 0)
    col = lax.broadcasted_iota(jnp.int32, (rows, rows), 1)
    strict = (col & (chunk - 1)) < (row & (chunk - 1))
    incl = (col & (chunk - 1)) <= (row & (chunk - 1))
    eye = (col == row).astype(F32)
    peye = (lax.broadcasted_iota(jnp.int32, (pw, pw), 0)
            == lax.broadcasted_iota(jnp.int32, (pw, pw), 1))

    def stack(x):
        return jnp.concatenate([jnp.where(head0, x, 0.0), jnp.where(head0, 0.0, x)],
                               axis=0).astype(BF16)

    units = [(b, j) for b in range(nb) for j in range(npair)]
    nu = len(units)
    x = pr_ref[...].reshape(nb * chunk, pr_ref.shape[2])
    r = x[:, 0:gw]
    k = x[:, gw:2 * gw]
    v = x[:, 2 * gw:3 * gw]
    wd = x[:, 3 * gw:3 * gw + LORA_PAD]
    ad = x[:, 3 * gw + LORA_PAD:3 * gw + 2 * LORA_PAD]
    w_pre = w0_ref[...] + _dot(jnp.tanh(wd).astype(BF16), wup_ref[...])
    w = -_softplus(-w_pre) - 0.5
    log_decay = -jnp.exp(w)
    a = _sigmoid(a0_ref[...] + _dot(ad.astype(BF16), aup_ref[...]))
    kk = k * kk_ref[...]
    norm = jnp.sqrt(_dot((kk * kk).astype(BF16), gsum_ref[...]))
    kk = kk / jnp.maximum(norm, 1e-12)
    k_mod = k * (1.0 + (a - 1.0) * ka_ref[...])
    b_vec = kk * a
    cum = _dot_exact_lhs(tri_ref[...], log_decay)
    lasts = [cum[(b + 1) * chunk - 1:(b + 1) * chunk, :] for b in range(nb)]
    last = jnp.concatenate([jnp.broadcast_to(t, (chunk, gw)) for t in lasts], axis=0)
    p_inv = jnp.exp(-cum)
    p_tail = jnp.exp(last - cum)
    a_t = -kk * jnp.exp(cum - log_decay)
    r_t = r * jnp.exp(cum)
    b_t = b_vec * p_inv
    k_t = k_mod * p_inv
    b_h = b_vec * p_tail
    k_h = k_mod * p_tail
    ar, bk, vs, bhs, khs, pcs = [], [], [], [], [], []
    for b in range(nb):
        rb = slice(b * chunk, (b + 1) * chunk)
        p_last = jnp.exp(lasts[b])
        for j in range(npair):
            sl = slice(j * pw, (j + 1) * pw)
            ar.append(jnp.concatenate([stack(a_t[rb, sl]), stack(r_t[rb, sl])], axis=0))
            bk.append(jnp.concatenate([stack(b_t[rb, sl]), stack(k_t[rb, sl])], axis=0))
            vs.append(stack(v[rb, sl]))
            bhs.append(stack(b_h[rb, sl]))
            khs.append(stack(k_h[rb, sl]))
            pcs.append(jnp.sum(jnp.where(peye, p_last[:, sl], 0.0), axis=1, keepdims=True))

    gram = [_dot_nt(ar[u], bk[u]) for u in range(nu)]
    l_ab = [jnp.where(strict, gram[u][:rows, :rows], 0.0) for u in range(nu)]
    l_akv = [_dot(jnp.where(strict, gram[u][:rows, rows:], 0.0).astype(BF16), vs[u])
             for u in range(nu)]
    m_rb = [jnp.where(incl, gram[u][rows:, :rows], 0.0).astype(BF16) for u in range(nu)]
    m_rkv = [_dot(jnp.where(incl, gram[u][rows:, rows:], 0.0).astype(BF16), vs[u])
             for u in range(nu)]
    t_inv = [eye + l_ab[u] for u in range(nu)]
    xb = [l_ab[u].astype(BF16) for u in range(nu)]
    xp = [_dot(xb[u], xb[u]) for u in range(nu)]
    for step in range(log_chunk - 1):
        xb = [xp[u].astype(BF16) for u in range(nu)]
        if step < log_chunk - 2:
            both = [_dot(jnp.concatenate([t_inv[u].astype(BF16), xb[u]], axis=0), xb[u])
                    for u in range(nu)]
            t_inv = [t_inv[u] + both[u][:rows] for u in range(nu)]
            xp = [both[u][rows:] for u in range(nu)]
        else:
            t_inv = [t_inv[u] + _dot(t_inv[u].astype(BF16), xb[u]) for u in range(nu)]
    tw = [_dot(t_inv[u].astype(BF16),
               jnp.concatenate([ar[u][:rows], l_akv[u].astype(BF16)], axis=1)).astype(BF16)
          for u in range(nu)]
    mw = [_dot(m_rb[u], tw[u]) for u in range(nu)]
    bw = [_dot_tn(bhs[u], tw[u]) for u in range(nu)]
    kv = [_dot_tn(khs[u], vs[u]) for u in range(nu)]
    for u, (b, j) in enumerate(units):
        wy = ar[u][rows:].astype(F32) + mw[u][:, :pw]
        yc = mw[u][:, pw:] + m_rkv[u]
        hf = h_ref[u]
        yh = _dot(jnp.concatenate([wy.astype(BF16), bw[u][:, :pw].astype(BF16)], axis=0),
                  hf.astype(BF16))
        h_ref[u] = pcs[u] * hf + yh[rows:] + bw[u][:, pw:] + kv[u]
        ys = yh[:rows] + yc
        y_ref[b, :, j * pw:(j + 1) * pw] = ys[:chunk] + ys[chunk:]


def _rwkv(pr, w0, wup, a0, aup, k_k, k_a, *, batch, seq, chunk, nb):
    n, ca = pr.shape
    nch = seq // chunk
    gidx = lax.broadcasted_iota(jnp.int32, (GROUP_W, GROUP_W), 0) // HEAD_DIM
    gsum = (gidx == gidx.T).astype(BF16)
    rr = lax.broadcasted_iota(jnp.int32, (nb * chunk, nb * chunk), 0)
    cc = lax.broadcasted_iota(jnp.int32, (nb * chunk, nb * chunk), 1)
    tri = jnp.logical_and(cc <= rr, cc // chunk == rr // chunk).astype(BF16)
    y = pl.pallas_call(
        functools.partial(_rwkv_kernel, chunk=chunk, nb=nb),
        grid=(batch // nb, nch),
        in_specs=[pl.BlockSpec((nb, chunk, ca), lambda g, c: (g, c, 0)),
                  _full((1, GROUP_W)), _full((LORA_PAD, GROUP_W)),
                  _full((1, GROUP_W)), _full((LORA_PAD, GROUP_W)),
                  _full((1, GROUP_W)), _full((1, GROUP_W)),
                  _full((GROUP_W, GROUP_W)), _full((nb * chunk, nb * chunk))],
        out_specs=pl.BlockSpec((nb, chunk, GROUP_W), lambda g, c: (g, c, 0)),
        out_shape=jax.ShapeDtypeStruct((batch, seq, GROUP_W), F32),
        scratch_shapes=[pltpu.VMEM((nb * (N_HEADS // 2), 2 * HEAD_DIM, 2 * HEAD_DIM), F32)],
        compiler_params=_params("arbitrary", "arbitrary"),
        name="rwkv_scan",
    )(pr.reshape(batch, seq, ca), w0, wup, a0, aup, k_k, k_a, gsum, tri)
    return y.reshape(n, GROUP_W)


def _fox_kernel(q_ref, k_ref, vt_ref, c_ref, o_ref, acc_ref, m_ref, l_ref, kb_ref,
                sa_ref, sb_ref, *, tq, tk):
    j = pl.program_id(1)
    qi = pl.program_id(2)
    pw = 2 * HEAD_DIM
    seq = k_ref.shape[0]
    lane = lax.broadcasted_iota(jnp.int32, (1, pw), 1)

    first = lane < HEAD_DIM
    bias_lane = (HEAD_DIM, 0)

    @pl.when(qi == 0)
    def _():
        def fill(rb, carry):
            rs = pl.multiple_of(rb * tk, tk)
            cblk = c_ref[pl.ds(rs, tk), :]
            for hh in range(2):
                bias = -LOG2E * jnp.sum(jnp.where(lane == 2 * j + hh, cblk, 0.0),
                                        axis=1, keepdims=True)
                b_hi = bias.astype(BF16).astype(F32)
                b_mid = (bias - b_hi).astype(BF16).astype(F32)
                b_lo = bias - b_hi - b_mid
                l0 = bias_lane[hh]
                kb_ref[hh, pl.ds(rs, tk), :] = jnp.where(
                    lane == l0, b_hi, jnp.where(lane == l0 + 1, b_mid,
                                                jnp.where(lane == l0 + 2, b_lo, 0.0))).astype(BF16)
            return carry
        lax.fori_loop(0, seq // tk, fill, 0)

    q = q_ref[...]
    ones3 = [jnp.where(jnp.logical_and(lane >= l0, lane < l0 + 3), 1.0, 0.0).astype(BF16)
             for l0 in bias_lane]
    own = (first, jnp.logical_not(first))
    qh = tuple(jnp.where(own[hh], q, ones3[hh]) for hh in range(2))
    acc_ref[...] = jnp.zeros_like(acc_ref)
    m_ref[...] = jnp.full_like(m_ref, NEG_BIG)
    l_ref[...] = jnp.zeros_like(l_ref)
    key_minus_query = (lax.broadcasted_iota(jnp.int32, (tk, tq), 0)
                       - lax.broadcasted_iota(jnp.int32, (tk, tq), 1))
    top = lax.broadcasted_iota(jnp.int32, (pw, 1), 0) < HEAD_DIM

    def scores(kb, s_ref):
        ks = pl.multiple_of(kb * tk, tk)
        kblk = k_ref[pl.ds(ks, tk), :]
        for hh in range(2):
            k_aug = jnp.where(own[hh], kblk, kb_ref[hh, pl.ds(ks, tk), :])
            s_ref[hh] = _dot_nt(k_aug, qh[hh])

    def softmax_pv(kb, s_ref, masked):
        causal = key_minus_query <= qi * tq - kb * tk
        vt = vt_ref[kb].astype(F32)
        vts = (jnp.where(top, vt, 1.0).astype(BF16), jnp.where(top, 1.0, vt).astype(BF16))
        alphas, pvs = [], []
        for hh in range(2):
            z = s_ref[hh]
            if masked:
                z = jnp.where(causal, z, NEG_BIG)
            m_prev = m_ref[hh]
            m_new = jnp.maximum(m_prev, jnp.max(z, axis=0, keepdims=True))
            alpha = jnp.exp2(m_prev - m_new)
            p = jnp.exp2(z - m_new)
            pv = _dot(vts[hh], p.astype(BF16))
            ones_row = (1 - hh) * HEAD_DIM
            l_ref[hh] = alpha * l_ref[hh] + pv[ones_row:ones_row + 1, :]
            m_ref[hh] = m_new
            alphas.append(alpha)
            pvs.append(pv)
        acc_ref[...] = (acc_ref[...] * jnp.where(top, alphas[0], alphas[1])
                        + jnp.where(top, pvs[0], pvs[1]))

    scores(0, sa_ref)

    def body(i, carry):
        scores(2 * i + 1, sb_ref)
        softmax_pv(2 * i, sa_ref, False)
        scores(2 * i + 2, sa_ref)
        softmax_pv(2 * i + 1, sb_ref, False)
        return carry

    lax.fori_loop(0, qi, body, 0)
    scores(2 * qi + 1, sb_ref)
    softmax_pv(2 * qi, sa_ref, True)
    softmax_pv(2 * qi + 1, sb_ref, True)

    out_t = acc_ref[...] / jnp.where(top, l_ref[0], l_ref[1])
    o_ref[...] = out_t.T.astype(BF16)


def _fox(qk, vt, c, *, batch, seq, tq, tk):
    n = qk.shape[0]
    nq = seq // tq
    nk = seq // tk
    npair = N_HEADS // 2
    pw = 2 * HEAD_DIM
    assert tq == 2 * tk and vt.shape == (batch, nk, GROUP_W, tk)
    return pl.pallas_call(
        functools.partial(_fox_kernel, tq=tq, tk=tk),
        grid=(batch, npair, nq),
        in_specs=[pl.BlockSpec((tq, pw), lambda b, j, i: (b * nq + i, j)),
                  pl.BlockSpec((seq, pw), lambda b, j, i: (b, npair + j)),
                  pl.BlockSpec((None, nk, pw, tk), lambda b, j, i: (b, 0, j, 0)),
                  pl.BlockSpec((seq, LANES), lambda b, j, i: (b, 0))],
        out_specs=pl.BlockSpec((tq, pw), lambda b, j, i: (b * nq + i, j)),
        out_shape=jax.ShapeDtypeStruct((n, GROUP_W), BF16),
        scratch_shapes=[pltpu.VMEM((pw, tq), F32),
                        pltpu.VMEM((2, 1, tq), F32), pltpu.VMEM((2, 1, tq), F32),
                        pltpu.VMEM((2, seq, pw), BF16),
                        pltpu.VMEM((2, tk, tq), F32), pltpu.VMEM((2, tk, tq), F32)],
        compiler_params=_params("arbitrary", "arbitrary", "arbitrary"),
        name="fox_attention",
    )(qk, qk, vt, c)


def _mixout_kernel(x_ref, pr_ref, yr_ref, yf_ref, a0_ref, aup_ref, gup_ref, ka_ref,
                   rk_ref, gng_ref, gnb_ref, gsum_ref, wr_ref, wf_ref, lng_ref, lnb_ref,
                   o_ref):
    gw = GROUP_W
    r = pr_ref[:, 0:gw]
    k = pr_ref[:, gw:2 * gw]
    v = pr_ref[:, 2 * gw:3 * gw]
    ad = pr_ref[:, 3 * gw + LORA_PAD:3 * gw + 2 * LORA_PAD]
    gd = pr_ref[:, 3 * gw + 2 * LORA_PAD:3 * gw + 3 * LORA_PAD]
    a = _sigmoid(a0_ref[...] + _dot(ad.astype(BF16), aup_ref[...]))
    k_mod = k * (1.0 + (a - 1.0) * ka_ref[...])
    gate = _dot(_sigmoid(gd).astype(BF16), gup_ref[...])
    gsum = gsum_ref[...]

    def group_sum(t):
        return _dot(t.astype(BF16), gsum)

    y = yr_ref[...]
    y_hi = y.astype(BF16)
    y_lo = (y - y_hi.astype(F32)).astype(BF16)
    mean = (_dot(y_hi, gsum) + _dot(y_lo, gsum)) * (1.0 / HEAD_DIM)
    d = y - mean
    var = group_sum(d * d) * (1.0 / HEAD_DIM)
    yn = d * lax.rsqrt(var + GN_EPS) * gng_ref[...] + gnb_ref[...]
    bonus = group_sum(r * k_mod * rk_ref[...])
    y_rwkv = ((yn + bonus * v) * gate).astype(BF16)
    mixed = _dot(y_rwkv, wr_ref[...]) + _dot(yf_ref[...], wf_ref[...])
    o_ref[...] = _layer_norm(ALPHA * x_ref[...] + mixed, lng_ref[...], lnb_ref[...])


def _mixout(x2, pr, yr, yf, a0, aup, gup, k_a, r_k, gn_g, gn_b, w_r, w_f, ln_g, ln_b, *, tm):
    n, d = x2.shape
    ca = pr.shape[1]
    gidx = lax.broadcasted_iota(jnp.int32, (GROUP_W, GROUP_W), 0) // HEAD_DIM
    gsum = (gidx == gidx.T).astype(BF16)
    vec = _full((1, GROUP_W))
    return pl.pallas_call(
        _mixout_kernel,
        grid=(n // tm,),
        in_specs=[pl.BlockSpec((tm, d), lambda i: (i, 0)),
                  pl.BlockSpec((tm, ca), lambda i: (i, 0)),
                  pl.BlockSpec((tm, GROUP_W), lambda i: (i, 0)),
                  pl.BlockSpec((tm, GROUP_W), lambda i: (i, 0)),
                  vec, _full((LORA_PAD, GROUP_W)), _full((LORA_PAD, GROUP_W)),
                  vec, vec, vec, vec, _full((GROUP_W, GROUP_W)),
                  _full((GROUP_W, d)), _full((GROUP_W, d)),
                  _full((1, d)), _full((1, d))],
        out_specs=pl.BlockSpec((tm, d), lambda i: (i, 0)),
        out_shape=jax.ShapeDtypeStruct((n, d), F32),
        compiler_params=_params("parallel"),
        name="mix_out",
    )(x2, pr, yr, yf, a0, aup, gup, k_a, r_k, gn_g, gn_b, gsum, w_r, w_f, ln_g, ln_b)


def _ffn_kernel(x_ref, wg_ref, wu_ref, wd_ref, lng_ref, lnb_ref, o_ref, *, tf):
    xb = x_ref[...].astype(BF16)
    ff = wg_ref.shape[1]
    acc = None
    for f0 in range(0, ff, tf):
        f1 = min(f0 + tf, ff)
        g = _dot(xb, wg_ref[:, f0:f1])
        u = _dot(xb, wu_ref[:, f0:f1])
        h = (g * _sigmoid(g) * u).astype(BF16)
        part = _dot(h, wd_ref[f0:f1, :])
        acc = part if acc is None else acc + part
    o_ref[...] = _layer_norm(ALPHA * x_ref[...] + acc, lng_ref[...], lnb_ref[...])


def _ffn(x2, wg, wu, wd, ln_g, ln_b, *, tm, tf):
    n, d = x2.shape
    ff = wg.shape[1]
    once = pl.Buffered(1)

    def resident(shape):
        return pl.BlockSpec(shape, lambda i: (0, 0), pipeline_mode=once)

    return pl.pallas_call(
        functools.partial(_ffn_kernel, tf=tf),
        grid=(n // tm,),
        in_specs=[pl.BlockSpec((tm, d), lambda i: (i, 0)),
                  resident((d, ff)), resident((d, ff)), resident((ff, d)),
                  _full((1, d)), _full((1, d))],
        out_specs=pl.BlockSpec((tm, d), lambda i: (i, 0)),
        out_shape=jax.ShapeDtypeStruct((n, d), F32),
        compiler_params=_params("parallel"),
        name="ffn_swiglu",
    )(x2, wg, wu, wd, ln_g, ln_b)


def _glu_kernel(x_ref, w_ref, b_ref, o_ref):
    d = o_ref.shape[1]
    xb = x_ref[...].astype(BF16)
    val = _dot(xb, w_ref[:, 0:d]) + b_ref[:, 0:d]
    gat = _dot(xb, w_ref[:, d:2 * d]) + b_ref[:, d:2 * d]
    o_ref[...] = val * _sigmoid(gat)


def _glu(x2, w, b, *, tm):
    n, d = x2.shape
    return pl.pallas_call(
        _glu_kernel,
        grid=(n // tm,),
        in_specs=[pl.BlockSpec((tm, d), lambda i: (i, 0)), _full((d, 2 * d)), _full((1, 2 * d))],
        out_specs=pl.BlockSpec((tm, d), lambda i: (i, 0)),
        out_shape=jax.ShapeDtypeStruct((n, d), F32),
        compiler_params=_params("parallel"),
        name="conv_glu",
    )(x2, w, b)


def _top2(logits):
    lane = lax.broadcasted_iota(jnp.int32, logits.shape, 1).astype(F32)
    lg = jnp.where(lane < N_EXPERTS, logits, NEG_BIG)
    m1 = jnp.max(lg, axis=-1, keepdims=True)
    i1 = jnp.min(jnp.where(lg == m1, lane, float(LANES)), axis=-1, keepdims=True)
    lg2 = jnp.where(lane == i1, NEG_BIG, lg)
    m2 = jnp.max(lg2, axis=-1, keepdims=True)
    i2 = jnp.min(jnp.where(lg2 == m2, lane, float(LANES)), axis=-1, keepdims=True)
    e2 = jnp.exp(m2 - m1)
    w1 = 1.0 / (1.0 + e2)
    w2 = e2 / (1.0 + e2)
    return lane, i1, i2, w1, w2


def _conv_kernel(hc_ref, hp_ref, x_ref, wdw_ref, bdw_ref, lng_ref, lnb_ref, w2_ref, b2_ref,
                 pg_ref, pb_ref, wr_ref, tri_ref, x3_ref, route_ref, route_t_ref, tot_ref,
                 ext_ref, cv_ref, cnt_ref, *, tiles_per_seq):
    tm, d = x_ref.shape

    @pl.when(pl.program_id(0) == 0)
    def _():
        cnt_ref[...] = jnp.zeros_like(cnt_ref)

    first = pl.program_id(0) % tiles_per_seq == 0
    ext_ref[0, 0:CONV_HALO, :] = jnp.where(first, 0.0, hp_ref[...])
    ext_ref[0, CONV_HALO:CONV_HALO + tm, :] = hc_ref[...]
    nrows = tm + CONV_HALO
    for c0 in range(0, d, 256):
        base = ext_ref[0, :, c0:c0 + 256]
        for j in range(1, SUBLANES):
            ext_ref[j, :, c0:c0 + 256] = pltpu.roll(base, nrows - j, 0)
    off = CONV_HALO - (CONV_WIDTH - 1)
    rc, cc = 64, 256
    for r0 in range(0, tm, rc):
        for c0 in range(0, d, cc):
            acc = jnp.broadcast_to(bdw_ref[:, c0:c0 + cc], (rc, cc))
            for t in range(CONV_WIDTH):
                base, j = divmod(off + t, SUBLANES)
                rs = r0 + base * SUBLANES
                acc = acc + wdw_ref[t:t + 1, c0:c0 + cc] * ext_ref[j, rs:rs + rc, c0:c0 + cc]
            cv_ref[r0:r0 + rc, c0:c0 + cc] = acc
    hn = _layer_norm(cv_ref[...], lng_ref[...], lnb_ref[...])
    hs = (hn * _sigmoid(hn)).astype(BF16)
    conv = _dot(hs, w2_ref[...]) + b2_ref[...]
    x3 = _layer_norm(ALPHA * x_ref[...] + conv, pg_ref[...], pb_ref[...])
    x3_ref[...] = x3
    x_hi = x3.astype(BF16)
    x_lo = (x3 - x_hi.astype(F32)).astype(BF16)
    hi_part = _dot(x_hi, wr_ref[...])
    logits = hi_part[:, :LANES] + hi_part[:, LANES:] + _dot(x_lo, wr_ref[:, 0:LANES])
    lane, i1, i2, w1, w2 = _top2(logits)
    first, second = lane == i1, lane == i2
    sel = jnp.where(jnp.logical_or(first, second), 1.0, 0.0)
    pos = _dot(tri_ref[...], sel.astype(BF16)) + cnt_ref[...]
    cnt_ref[...] += jnp.sum(sel, axis=0, keepdims=True)
    tot_ref[...] = jnp.broadcast_to(cnt_ref[...], tot_ref.shape)
    rank1 = jnp.sum(jnp.where(first, pos, 0.0), axis=-1, keepdims=True)
    rank2 = jnp.sum(jnp.where(second, pos, 0.0), axis=-1, keepdims=True)
    fields = (i1, i2, w1, w2, rank1, rank2)
    record = jnp.zeros_like(logits)
    for k, field in enumerate(fields):
        record = jnp.where(lane == float(k), field, record)
    route_ref[...] = record
    route_t_ref[...] = record.T[:SUBLANES, :]


def _conv(hg, x2, w_dw, b_dw, ln_g, ln_b, w2, b2, pg, pb, w_router, *, seq, tm):
    n, d = x2.shape
    ratio = tm // CONV_HALO
    vec = _full((1, d))
    tri = (lax.broadcasted_iota(jnp.int32, (tm, tm), 1)
           < lax.broadcasted_iota(jnp.int32, (tm, tm), 0)).astype(BF16)
    return pl.pallas_call(
        functools.partial(_conv_kernel, tiles_per_seq=seq // tm),
        grid=(n // tm,),
        in_specs=[pl.BlockSpec((tm, d), lambda i: (i, 0)),
                  pl.BlockSpec((CONV_HALO, d), lambda i: (jnp.maximum(i * ratio - 1, 0), 0)),
                  pl.BlockSpec((tm, d), lambda i: (i, 0)),
                  _full((CONV_HALO, d)), vec, vec, vec, _full((d, d)), vec, vec, vec,
                  _full((d, 2 * LANES)), _full((tm, tm))],
        out_specs=[pl.BlockSpec((tm, d), lambda i: (i, 0)),
                   pl.BlockSpec((tm, LANES), lambda i: (i, 0)),
                   pl.BlockSpec((SUBLANES, tm), lambda i: (0, i)),
                   _full((SUBLANES, LANES))],
        out_shape=[jax.ShapeDtypeStruct((n, d), F32),
                   jax.ShapeDtypeStruct((n, LANES), F32),
                   jax.ShapeDtypeStruct((SUBLANES, n), F32),
                   jax.ShapeDtypeStruct((SUBLANES, LANES), F32)],
        scratch_shapes=[pltpu.VMEM((SUBLANES, tm + CONV_HALO, d), F32), pltpu.VMEM((tm, d), F32),
                        pltpu.VMEM((1, LANES), F32)],
        compiler_params=_params("arbitrary"),
        name="conv_module",
    )(hg, hg, x2, w_dw, b_dw, ln_g, ln_b, w2, b2, pg, pb, w_router, tri)


MOE_TILE = 512
SC_CORES = 2
SC_SUBCORES = 16
SC_CHUNK = 64


def _sc_row_scatter(x, slot_a, slot_b, slots):
    n, d = x.shape
    per_worker = n // (SC_CORES * SC_SUBCORES)
    mesh = plsc.VectorSubcoreMesh(core_axis_name="c", subcore_axis_name="s",
                                  num_cores=SC_CORES, num_subcores=SC_SUBCORES)

    @functools.partial(
        pl.kernel, mesh=mesh,
        out_type=jax.ShapeDtypeStruct((slots, d), x.dtype),
        scratch_types=[pltpu.VMEM((SC_CHUNK,), jnp.int32),
                       pltpu.VMEM((SC_CHUNK,), jnp.int32),
                       pltpu.VMEM((SC_CHUNK, d), x.dtype),
                       pltpu.SemaphoreType.DMA],
        name="moe_sc_row_scatter")
    def scatter(x_hbm, a_hbm, b_hbm, out_hbm, a_v, b_v, rows_v, sem):
        worker = lax.axis_index("s") * SC_CORES + lax.axis_index("c")
        base = worker * per_worker

        @pl.loop(0, per_worker // SC_CHUNK)
        def _(ci):
            off = base + ci * SC_CHUNK
            pltpu.sync_copy(a_hbm.at[pl.ds(off, SC_CHUNK)], a_v)
            pltpu.sync_copy(b_hbm.at[pl.ds(off, SC_CHUNK)], b_v)
            pltpu.sync_copy(x_hbm.at[pl.ds(off, SC_CHUNK)], rows_v)
            pltpu.async_copy(rows_v, out_hbm.at[a_v], sem).wait()
            pltpu.async_copy(rows_v, out_hbm.at[b_v], sem).wait()

    return scatter(x, slot_a, slot_b)


def _sc_row_gather(table, idx):
    rows = idx.shape[0]
    d = table.shape[1]
    per_worker = rows // (SC_CORES * SC_SUBCORES)
    mesh = plsc.VectorSubcoreMesh(core_axis_name="c", subcore_axis_name="s",
                                  num_cores=SC_CORES, num_subcores=SC_SUBCORES)

    @functools.partial(
        pl.kernel, mesh=mesh,
        out_type=jax.ShapeDtypeStruct((rows, d), table.dtype),
        scratch_types=[pltpu.VMEM((SC_CHUNK,), jnp.int32),
                       pltpu.VMEM((SC_CHUNK, d), table.dtype),
                       pltpu.SemaphoreType.DMA],
        name="moe_sc_row_gather")
    def gather(table_hbm, idx_hbm, out_hbm, idx_v, rows_v, sem):
        worker = lax.axis_index("s") * SC_CORES + lax.axis_index("c")
        base = worker * per_worker

        @pl.loop(0, per_worker // SC_CHUNK)
        def _(ci):
            off = base + ci * SC_CHUNK
            pltpu.sync_copy(idx_hbm.at[pl.ds(off, SC_CHUNK)], idx_v)
            pltpu.async_copy(table_hbm.at[idx_v], rows_v, sem).wait()
            pltpu.sync_copy(rows_v, out_hbm.at[pl.ds(off, SC_CHUNK)])

    return gather(table, idx)


def _expert_ffn_kernel(expert_ref, used_ref, x_ref, wg_ref, wu_ref, wd_ref, o_ref):
    i = pl.program_id(0)
    f = pl.program_id(1)

    @pl.when(f == 0)
    def _():
        o_ref[...] = jnp.zeros_like(o_ref)

    @pl.when(used_ref[i] > 0)
    def _():
        xb = x_ref[...].astype(BF16)
        g = _dot(xb, wg_ref[...])
        u = _dot(xb, wu_ref[...])
        h = (g * _sigmoid(g) * u).astype(BF16)
        o_ref[...] += _dot(h, wd_ref[...])


def _expert_ffn(xs, tile_expert, tile_used, wg, wu, wd, *, tf):
    slots, d = xs.shape
    ff = wg.shape[2]
    return pl.pallas_call(
        _expert_ffn_kernel,
        grid_spec=pltpu.PrefetchScalarGridSpec(
            num_scalar_prefetch=2, grid=(slots // MOE_TILE, ff // tf),
            in_specs=[pl.BlockSpec((MOE_TILE, d), lambda i, f, te, tu: (i, 0)),
                      pl.BlockSpec((None, d, tf), lambda i, f, te, tu: (te[i], 0, f)),
                      pl.BlockSpec((None, d, tf), lambda i, f, te, tu: (te[i], 0, f)),
                      pl.BlockSpec((None, tf, d), lambda i, f, te, tu: (te[i], f, 0))],
            out_specs=pl.BlockSpec((MOE_TILE, d), lambda i, f, te, tu: (i, 0))),
        out_shape=jax.ShapeDtypeStruct((slots, d), F32),
        compiler_params=_params("arbitrary", "arbitrary"),
        name="moe_expert_ffn",
    )(tile_expert, tile_used, xs, wg, wu, wd)


def _combine_kernel(x_ref, y1_ref, y2_ref, route_ref, lng_ref, lnb_ref, o_ref):
    route = route_ref[...]
    moe = route[:, 2:3] * y1_ref[...] + route[:, 3:4] * y2_ref[...]
    o_ref[...] = _layer_norm(ALPHA * x_ref[...] + moe, lng_ref[...], lnb_ref[...])


def _combine(x3, yt, route, ln_g, ln_b, *, tm):
    n, d = x3.shape
    nt = n // tm
    return pl.pallas_call(
        _combine_kernel,
        grid=(nt,),
        in_specs=[pl.BlockSpec((tm, d), lambda i: (i, 0)),
                  pl.BlockSpec((tm, d), lambda i: (i, 0)),
                  pl.BlockSpec((tm, d), lambda i: (i + nt, 0)),
                  pl.BlockSpec((tm, LANES), lambda i: (i, 0)),
                  _full((1, d)), _full((1, d))],
        out_specs=pl.BlockSpec((tm, d), lambda i: (i, 0)),
        out_shape=jax.ShapeDtypeStruct((n, d), F32),
        compiler_params=_params("parallel"),
        name="moe_combine",
    )(x3, yt, yt, route, ln_g, ln_b)


def _moe(x3, route, route_t, tot, wg, wu, wd, ln_g, ln_b, *, tm, tf):
    n, d = x3.shape
    ne = wg.shape[0]
    slots = TOP_K * n + ne * MOE_TILE
    count = tot[0, :ne].astype(jnp.int32)
    cap = (count + (MOE_TILE - 1)) // MOE_TILE * MOE_TILE
    ends = jnp.cumsum(cap)
    off = ends - cap
    e1 = route_t[0].astype(jnp.int32)
    e2 = route_t[1].astype(jnp.int32)
    slot1 = off[e1] + route_t[4].astype(jnp.int32)
    slot2 = off[e2] + route_t[5].astype(jnp.int32)
    tile_start = jnp.arange(slots // MOE_TILE, dtype=jnp.int32) * MOE_TILE
    tile_expert = jnp.minimum(jnp.sum(tile_start[:, None] >= ends[None, :], axis=1),
                              ne - 1).astype(jnp.int32)
    tile_used = (tile_start < ends[-1]).astype(jnp.int32)

    xs = _sc_row_scatter(x3, slot1, slot2, slots)
    ys = _expert_ffn(xs, tile_expert, tile_used, wg, wu, wd, tf=tf)
    yt = _sc_row_gather(ys, jnp.concatenate([slot1, slot2]))
    return _combine(x3, yt, route, ln_g, ln_b, tm=tm)


def _pad_cols(w, width):
    return jnp.pad(w, ((0, 0), (0, width - w.shape[1])))


def _pad_rows(w, height):
    return jnp.pad(w, ((0, height - w.shape[0]), (0, 0)))


def _forward(x, mix_w_in, rwkv_mu, rwkv_w0, rwkv_w_up, rwkv_a0, rwkv_a_up, rwkv_g_up,
             rwkv_k_k, rwkv_k_a, rwkv_r_k, rwkv_gn_g, rwkv_gn_b, fox_b_f, mix_w_out,
             mix_ln_g, mix_ln_b, ffn_w_gate, ffn_w_up, ffn_w_down, ffn_ln_g, ffn_ln_b,
             conv_w_pw1, conv_b_pw1, conv_w_dw, conv_b_dw, conv_ln_g, conv_ln_b,
             conv_w_pw2, conv_b_pw2, conv_post_ln_g, conv_post_ln_b,
             moe_w_router, moe_w_gate, moe_w_up, moe_w_down, moe_ln_g, moe_ln_b,
             *, tm=512, chunk=64, nb_rwkv=4, tk_fox=512, tf_ffn=1536, tf_moe=1792):
    batch, seq, d = x.shape
    n = batch * seq
    gw = GROUP_W
    x2 = x.reshape(n, d)
    row = lambda t: t.reshape(1, -1)

    w_in = mix_w_in[0]
    mu = rwkv_mu[0]
    o_w, o_a, o_g = 3 * gw, 3 * gw + DECAY_LORA, 3 * gw + DECAY_LORA + AAA_LORA
    o_fox = o_g + GATE_LORA

    def lora_layout(t):
        return jnp.concatenate([t[..., :o_w],
                                _pad_cols(t[..., o_w:o_a], LORA_PAD),
                                _pad_cols(t[..., o_a:o_g], LORA_PAD),
                                _pad_cols(t[..., o_g:o_fox], LORA_PAD)], axis=-1)

    wa = lora_layout(w_in).astype(BF16)
    mu_a = lora_layout(row(mu))
    scale = LOG2E / math.sqrt(HEAD_DIM)
    wb = jnp.concatenate([w_in[:, o_fox:o_fox + gw] * scale,
                          w_in[:, o_fox + gw:o_fox + 3 * gw]], axis=1).astype(BF16)
    wf = _pad_cols(w_in[:, o_fox + 3 * gw:], LANES).astype(BF16)
    bf = _pad_cols(row(fox_b_f[0]), LANES)

    pr, qk, vt, c = _inproj(x2, wa, wb, wf, mu_a, bf, seq=seq, tm=tk_fox)

    wup = _pad_rows(rwkv_w_up[0], LORA_PAD).astype(BF16)
    aup = _pad_rows(rwkv_a_up[0], LORA_PAD).astype(BF16)
    gup = _pad_rows(rwkv_g_up[0], LORA_PAD).astype(BF16)
    k_k, k_a, r_k = row(rwkv_k_k[0]), row(rwkv_k_a[0]), row(rwkv_r_k[0])
    yr = _rwkv(pr, row(rwkv_w0[0]), wup, row(rwkv_a0[0]), aup, k_k, k_a,
               batch=batch, seq=seq, chunk=chunk, nb=nb_rwkv)

    yf = _fox(qk, vt, c, batch=batch, seq=seq, tq=2 * tk_fox, tk=tk_fox)

    w_out = mix_w_out[0].astype(BF16)
    x1 = _mixout(x2, pr, yr, yf, row(rwkv_a0[0]), aup, gup, k_a, r_k,
                 row(rwkv_gn_g[0]), row(rwkv_gn_b[0]), w_out[:gw], w_out[gw:],
                 row(mix_ln_g[0]), row(mix_ln_b[0]), tm=tm)
    x2b = _ffn(x1, ffn_w_gate[0].astype(BF16), ffn_w_up[0].astype(BF16),
               ffn_w_down[0].astype(BF16), row(ffn_ln_g[0]), row(ffn_ln_b[0]), tm=tm, tf=tf_ffn)

    hg = _glu(x2b, conv_w_pw1[0].astype(BF16), row(conv_b_pw1[0]), tm=tm)
    w_router = _pad_cols(moe_w_router[0], LANES)
    wr_hi = w_router.astype(BF16)
    w_router = jnp.concatenate([wr_hi, (w_router - wr_hi.astype(F32)).astype(BF16)], axis=1)
    x3, route, route_t, tot = _conv(hg, x2b, _pad_rows(conv_w_dw[0], CONV_HALO), row(conv_b_dw[0]),
                                row(conv_ln_g[0]), row(conv_ln_b[0]),
                                conv_w_pw2[0].astype(BF16), row(conv_b_pw2[0]),
                                row(conv_post_ln_g[0]), row(conv_post_ln_b[0]),
                                w_router, seq=seq, tm=tm)
    out = _moe(x3, route, route_t, tot, moe_w_gate[0].astype(BF16), moe_w_up[0].astype(BF16),
               moe_w_down[0].astype(BF16), row(moe_ln_g[0]), row(moe_ln_b[0]),
               tm=tm, tf=tf_moe)
    return out.reshape(batch, seq, d)


def kernel(x, mix_w_in, rwkv_mu, rwkv_w0, rwkv_w_up, rwkv_a0, rwkv_a_up, rwkv_g_up, rwkv_k_k, rwkv_k_a, rwkv_r_k, rwkv_gn_g, rwkv_gn_b, fox_b_f, mix_w_out, mix_ln_g, mix_ln_b, ffn_w_gate, ffn_w_up, ffn_w_down, ffn_ln_g, ffn_ln_b, conv_w_pw1, conv_b_pw1, conv_w_dw, conv_b_dw, conv_ln_g, conv_ln_b, conv_w_pw2, conv_b_pw2, conv_post_ln_g, conv_post_ln_b, moe_w_router, moe_w_gate, moe_w_up, moe_w_down, moe_ln_g, moe_ln_b):
    return _forward(x, mix_w_in, rwkv_mu, rwkv_w0, rwkv_w_up, rwkv_a0, rwkv_a_up, rwkv_g_up,
                    rwkv_k_k, rwkv_k_a, rwkv_r_k, rwkv_gn_g, rwkv_gn_b, fox_b_f, mix_w_out,
                    mix_ln_g, mix_ln_b, ffn_w_gate, ffn_w_up, ffn_w_down, ffn_ln_g, ffn_ln_b,
                    conv_w_pw1, conv_b_pw1, conv_w_dw, conv_b_dw, conv_ln_g, conv_ln_b,
                    conv_w_pw2, conv_b_pw2, conv_post_ln_g, conv_post_ln_b,
                    moe_w_router, moe_w_gate, moe_w_up, moe_w_down, moe_ln_g, moe_ln_b)
```

```python
import functools
import math

import jax
import jax.numpy as jnp
from jax import lax
from jax.experimental import pallas as pl
from jax.experimental.pallas import tpu as pltpu
from jax.experimental.pallas import tpu_sc as plsc

F32 = jnp.float32
BF16 = jnp.bfloat16
LANES = 128

HEAD_DIM = 64
N_HEADS = 8
GROUP_W = N_HEADS * HEAD_DIM
LORA_PAD = LANES
DECAY_LORA = 32
AAA_LORA = 32
GATE_LORA = 96
CONV_WIDTH = 31
CONV_HALO = 32
SUBLANES = 8
N_EXPERTS = 8
TOP_K = 2
LN_EPS = 1e-5
GN_EPS = 64e-5
DEPTH = 2
ALPHA = (2.0 * DEPTH) ** 0.25
NEG_BIG = -1e30
LOG2E = math.log2(math.e)
VMEM_LIMIT = 56 * 1024 * 1024


def _dot(a, b, **kw):
    return jnp.dot(a, b, preferred_element_type=F32, **kw)


def _dot_nt(a, b):
    return lax.dot_general(a, b, (((1,), (1,)), ((), ())), preferred_element_type=F32)


def _dot_tn(a, b):
    return lax.dot_general(a, b, (((0,), (0,)), ((), ())), preferred_element_type=F32)


def _dot_exact_lhs(a, v):
    hi = v.astype(BF16)
    rem = v - hi.astype(F32)
    mid = rem.astype(BF16)
    lo = (rem - mid.astype(F32)).astype(BF16)
    w = v.shape[1]
    out = _dot(a, jnp.concatenate([hi, mid, lo], axis=1))
    return out[:, :w] + out[:, w:2 * w] + out[:, 2 * w:]


def _sigmoid(z):
    return 1.0 / (1.0 + jnp.exp(-z))


def _softplus(z):
    return jnp.maximum(z, 0.0) + jnp.log1p(jnp.exp(-jnp.abs(z)))


def _layer_norm(h, g, b):
    mu = jnp.mean(h, axis=-1, keepdims=True)
    d = h - mu
    var = jnp.mean(d * d, axis=-1, keepdims=True)
    return d * lax.rsqrt(var + LN_EPS) * g + b


def _params(*sem):
    return pltpu.CompilerParams(dimension_semantics=sem, vmem_limit_bytes=VMEM_LIMIT)


def _full(shape):
    return pl.BlockSpec(shape, lambda *_: (0,) * len(shape))


def _inproj_kernel(x_ref, wa_ref, wb_ref, wf_ref, mu_ref, bf_ref, tri_ref,
                   pr_ref, qk_ref, vt_ref, c_ref, last_ref, carry_ref, *, tiles_per_seq):
    i = pl.program_id(0)

    @pl.when(i % tiles_per_seq == 0)
    def _():
        last_ref[...] = jnp.zeros_like(last_ref)
        carry_ref[...] = jnp.zeros_like(carry_ref)

    xb = x_ref[...].astype(BF16)
    tm = xb.shape[0]
    row0 = lax.broadcasted_iota(jnp.int32, (tm, 1), 0) == 0
    ca = wa_ref.shape[1]
    for c0 in range(0, ca, GROUP_W):
        cw = min(GROUP_W, ca - c0)
        p = _dot(xb, wa_ref[:, c0:c0 + cw])
        prev = jnp.where(row0, last_ref[:, c0:c0 + cw], pltpu.roll(p, 1, 0))
        last_ref[:, c0:c0 + cw] = p[tm - 1:tm, :]
        pr_ref[:, c0:c0 + cw] = p + mu_ref[:, c0:c0 + cw] * (prev - p)
    for c0 in range(0, 2 * GROUP_W, GROUP_W):
        qk_ref[:, c0:c0 + GROUP_W] = _dot(xb, wb_ref[:, c0:c0 + GROUP_W]).astype(BF16)
    vt_ref[...] = _dot(xb, wb_ref[:, 2 * GROUP_W:3 * GROUP_W]).T.astype(BF16)
    fl = _dot(xb, wf_ref[...]) + bf_ref[...]
    log_f = jnp.minimum(fl, 0.0) - jnp.log1p(jnp.exp(-jnp.abs(fl)))
    c = _dot_exact_lhs(tri_ref[...], log_f) + carry_ref[...]
    c_ref[...] = c
    carry_ref[...] = c[tm - 1:tm, :]


def _inproj(x2, wa, wb, wf, mu, bf, *, seq, tm):
    n, d = x2.shape
    ca, cb = wa.shape[1], wb.shape[1]
    tps = seq // tm
    tri = (lax.broadcasted_iota(jnp.int32, (tm, tm), 1)
           <= lax.broadcasted_iota(jnp.int32, (tm, tm), 0)).astype(BF16)
    return pl.pallas_call(
        functools.partial(_inproj_kernel, tiles_per_seq=tps),
        grid=(n // tm,),
        in_specs=[pl.BlockSpec((tm, d), lambda i: (i, 0)),
                  _full((d, ca)), _full((d, cb)), _full((d, LANES)),
                  _full((1, ca)), _full((1, LANES)), _full((tm, tm))],
        out_specs=[pl.BlockSpec((tm, ca), lambda i: (i, 0)),
                   pl.BlockSpec((tm, 2 * GROUP_W), lambda i: (i, 0)),
                   pl.BlockSpec((None, None, GROUP_W, tm), lambda i: (i // tps, i % tps, 0, 0)),
                   pl.BlockSpec((tm, LANES), lambda i: (i, 0))],
        out_shape=[jax.ShapeDtypeStruct((n, ca), F32),
                   jax.ShapeDtypeStruct((n, 2 * GROUP_W), BF16),
                   jax.ShapeDtypeStruct((n // seq, tps, GROUP_W, tm), BF16),
                   jax.ShapeDtypeStruct((n, LANES), F32)],
        scratch_shapes=[pltpu.VMEM((1, ca), F32), pltpu.VMEM((1, LANES), F32)],
        compiler_params=_params("arbitrary"),
        name="inproj",
    )(x2, wa, wb, wf, mu, bf, tri)


def _rwkv_kernel(pr_ref, w0_ref, wup_ref, a0_ref, aup_ref, kk_ref, ka_ref,
                 gsum_ref, tri_ref, y_ref, h_ref, *, chunk, nb):
    @pl.when(pl.program_id(1) == 0)
    def _():
        h_ref[...] = jnp.zeros_like(h_ref)

    gw = GROUP_W
    pw = 2 * HEAD_DIM
    npair = N_HEADS // 2
    rows = 2 * chunk
    log_chunk = int(math.log2(chunk))
    head0 = lax.broadcasted_iota(jnp.int32, (1, pw), 1) < HEAD_DIM
    row = lax.broadcasted_iota(jnp.int32, (rows, rows), 0)
    col = lax.broadcasted_iota(jnp.int32, (rows, rows), 1)
    strict = (col & (chunk - 1)) < (row & (chunk - 1))
    incl = (col & (chunk - 1)) <= (row & (chunk - 1))
    eye = (col == row).astype(F32)
    peye = (lax.broadcasted_iota(jnp.int32, (pw, pw), 0)
            == lax.broadcasted_iota(jnp.int32, (pw, pw), 1))

    def stack(x):
        return jnp.concatenate([jnp.where(head0, x, 0.0), jnp.where(head0, 0.0, x)],
                               axis=0).astype(BF16)

    units = [(b, j) for b in range(nb) for j in range(npair)]
    nu = len(units)
    x = pr_ref[...].reshape(nb * chunk, pr_ref.shape[2])
    r = x[:, 0:gw]
    k = x[:, gw:2 * gw]
    v = x[:, 2 * gw:3 * gw]
    wd = x[:, 3 * gw:3 * gw + LORA_PAD]
    ad = x[:, 3 * gw + LORA_PAD:3 * gw + 2 * LORA_PAD]
    w_pre = w0_ref[...] + _dot(jnp.tanh(wd).astype(BF16), wup_ref[...])
    w = -_softplus(-w_pre) - 0.5
    log_decay = -jnp.exp(w)
    a = _sigmoid(a0_ref[...] + _dot(ad.astype(BF16), aup_ref[...]))
    kk = k * kk_ref[...]
    norm = jnp.sqrt(_dot((kk * kk).astype(BF16), gsum_ref[...]))
    kk = kk / jnp.maximum(norm, 1e-12)
    k_mod = k * (1.0 + (a - 1.0) * ka_ref[...])
    b_vec = kk * a
    cum = _dot_exact_lhs(tri_ref[...], log_decay)
    lasts = [cum[(b + 1) * chunk - 1:(b + 1) * chunk, :] for b in range(nb)]
    last = jnp.concatenate([jnp.broadcast_to(t, (chunk, gw)) for t in lasts], axis=0)
    p_inv = jnp.exp(-cum)
    p_tail = jnp.exp(last - cum)
    a_t = -kk * jnp.exp(cum - log_decay)
    r_t = r * jnp.exp(cum)
    b_t = b_vec * p_inv
    k_t = k_mod * p_inv
    b_h = b_vec * p_tail
    k_h = k_mod * p_tail
    ar, bk, vs, bhs, khs, pcs = [], [], [], [], [], []
    for b in range(nb):
        rb = slice(b * chunk, (b + 1) * chunk)
        p_last = jnp.exp(lasts[b])
        for j in range(npair):
            sl = slice(j * pw, (j + 1) * pw)
            ar.append(jnp.concatenate([stack(a_t[rb, sl]), stack(r_t[rb, sl])], axis=0))
            bk.append(jnp.concatenate([stack(b_t[rb, sl]), stack(k_t[rb, sl])], axis=0))
            vs.append(stack(v[rb, sl]))
            bhs.append(stack(b_h[rb, sl]))
            khs.append(stack(k_h[rb, sl]))
            pcs.append(jnp.sum(jnp.where(peye, p_last[:, sl], 0.0), axis=1, keepdims=True))

    gram = [_dot_nt(ar[u], bk[u]) for u in range(nu)]
    l_ab = [jnp.where(strict, gram[u][:rows, :rows], 0.0) for u in range(nu)]
    l_akv = [_dot(jnp.where(strict, gram[u][:rows, rows:], 0.0).astype(BF16), vs[u])
             for u in range(nu)]
    m_rb = [jnp.where(incl, gram[u][rows:, :rows], 0.0).astype(BF16) for u in range(nu)]
    m_rkv = [_dot(jnp.where(incl, gram[u][rows:, rows:], 0.0).astype(BF16), vs[u])
             for u in range(nu)]
    t_inv = [eye + l_ab[u] for u in range(nu)]
    xb = [l_ab[u].astype(BF16) for u in range(nu)]
    xp = [_dot(xb[u], xb[u]) for u in range(nu)]
    for step in range(log_chunk - 1):
        xb = [xp[u].astype(BF16) for u in range(nu)]
        if step < log_chunk - 2:
            both = [_dot(jnp.concatenate([t_inv[u].astype(BF16), xb[u]], axis=0), xb[u])
                    for u in range(nu)]
            t_inv = [t_inv[u] + both[u][:rows] for u in range(nu)]
            xp = [both[u][rows:] for u in range(nu)]
        else:
            t_inv = [t_inv[u] + _dot(t_inv[u].astype(BF16), xb[u]) for u in range(nu)]
    tw = [_dot(t_inv[u].astype(BF16),
               jnp.concatenate([ar[u][:rows], l_akv[u].astype(BF16)], axis=1)).astype(BF16)
          for u in range(nu)]
    mw = [_dot(m_rb[u], tw[u]) for u in range(nu)]
    bw = [_dot_tn(bhs[u], tw[u]) for u in range(nu)]
    kv = [_dot_tn(khs[u], vs[u]) for u in range(nu)]
    for u, (b, j) in enumerate(units):
        wy = ar[u][rows:].astype(F32) + mw[u][:, :pw]
        yc = mw[u][:, pw:] + m_rkv[u]
        hf = h_ref[u]
        yh = _dot(jnp.concatenate([wy.astype(BF16), bw[u][:, :pw].astype(BF16)], axis=0),
                  hf.astype(BF16))
        h_ref[u] = pcs[u] * hf + yh[rows:] + bw[u][:, pw:] + kv[u]
        ys = yh[:rows] + yc
        y_ref[b, :, j * pw:(j + 1) * pw] = ys[:chunk] + ys[chunk:]


def _rwkv(pr, w0, wup, a0, aup, k_k, k_a, *, batch, seq, chunk, nb):
    n, ca = pr.shape
    nch = seq // chunk
    gidx = lax.broadcasted_iota(jnp.int32, (GROUP_W, GROUP_W), 0) // HEAD_DIM
    gsum = (gidx == gidx.T).astype(BF16)
    rr = lax.broadcasted_iota(jnp.int32, (nb * chunk, nb * chunk), 0)
    cc = lax.broadcasted_iota(jnp.int32, (nb * chunk, nb * chunk), 1)
    tri = jnp.logical_and(cc <= rr, cc // chunk == rr // chunk).astype(BF16)
    y = pl.pallas_call(
        functools.partial(_rwkv_kernel, chunk=chunk, nb=nb),
        grid=(batch // nb, nch),
        in_specs=[pl.BlockSpec((nb, chunk, ca), lambda g, c: (g, c, 0)),
                  _full((1, GROUP_W)), _full((LORA_PAD, GROUP_W)),
                  _full((1, GROUP_W)), _full((LORA_PAD, GROUP_W)),
                  _full((1, GROUP_W)), _full((1, GROUP_W)),
                  _full((GROUP_W, GROUP_W)), _full((nb * chunk, nb * chunk))],
        out_specs=pl.BlockSpec((nb, chunk, GROUP_W), lambda g, c: (g, c, 0)),
        out_shape=jax.ShapeDtypeStruct((batch, seq, GROUP_W), F32),
        scratch_shapes=[pltpu.VMEM((nb * (N_HEADS // 2), 2 * HEAD_DIM, 2 * HEAD_DIM), F32)],
        compiler_params=_params("arbitrary", "arbitrary"),
        name="rwkv_scan",
    )(pr.reshape(batch, seq, ca), w0, wup, a0, aup, k_k, k_a, gsum, tri)
    return y.reshape(n, GROUP_W)


def _fox_kernel(q_ref, k_ref, vt_ref, c_ref, o_ref, acc_ref, m_ref, l_ref, kb_ref,
                sa_ref, sb_ref, *, tq, tk):
    j = pl.program_id(1)
    qi = pl.program_id(2)
    pw = 2 * HEAD_DIM
    seq = k_ref.shape[0]
    lane = lax.broadcasted_iota(jnp.int32, (1, pw), 1)

    first = lane < HEAD_DIM
    bias_lane = (HEAD_DIM, 0)

    @pl.when(qi == 0)
    def _():
        def fill(rb, carry):
            rs = pl.multiple_of(rb * tk, tk)
            cblk = c_ref[pl.ds(rs, tk), :]
            for hh in range(2):
                bias = -LOG2E * jnp.sum(jnp.where(lane == 2 * j + hh, cblk, 0.0),
                                        axis=1, keepdims=True)
                b_hi = bias.astype(BF16).astype(F32)
                b_mid = (bias - b_hi).astype(BF16).astype(F32)
                b_lo = bias - b_hi - b_mid
                l0 = bias_lane[hh]
                kb_ref[hh, pl.ds(rs, tk), :] = jnp.where(
                    lane == l0, b_hi, jnp.where(lane == l0 + 1, b_mid,
                                                jnp.where(lane == l0 + 2, b_lo, 0.0))).astype(BF16)
            return carry
        lax.fori_loop(0, seq // tk, fill, 0)

    q = q_ref[...]
    ones3 = [jnp.where(jnp.logical_and(lane >= l0, lane < l0 + 3), 1.0, 0.0).astype(BF16)
             for l0 in bias_lane]
    own = (first, jnp.logical_not(first))
    qh = tuple(jnp.where(own[hh], q, ones3[hh]) for hh in range(2))
    acc_ref[...] = jnp.zeros_like(acc_ref)
    m_ref[...] = jnp.full_like(m_ref, NEG_BIG)
    l_ref[...] = jnp.zeros_like(l_ref)
    key_minus_query = (lax.broadcasted_iota(jnp.int32, (tk, tq), 0)
                       - lax.broadcasted_iota(jnp.int32, (tk, tq), 1))
    top = lax.broadcasted_iota(jnp.int32, (pw, 1), 0) < HEAD_DIM

    def scores(kb, s_ref):
        ks = pl.multiple_of(kb * tk, tk)
        kblk = k_ref[pl.ds(ks, tk), :]
        for hh in range(2):
            k_aug = jnp.where(own[hh], kblk, kb_ref[hh, pl.ds(ks, tk), :])
            s_ref[hh] = _dot_nt(k_aug, qh[hh])

    def softmax_pv(kb, s_ref, masked):
        causal = key_minus_query <= qi * tq - kb * tk
        vt = vt_ref[kb].astype(F32)
        vts = (jnp.where(top, vt, 1.0).astype(BF16), jnp.where(top, 1.0, vt).astype(BF16))
        alphas, pvs = [], []
        for hh in range(2):
            z = s_ref[hh]
            if masked:
                z = jnp.where(causal, z, NEG_BIG)
            m_prev = m_ref[hh]
            m_new = jnp.maximum(m_prev, jnp.max(z, axis=0, keepdims=True))
            alpha = jnp.exp2(m_prev - m_new)
            p = jnp.exp2(z - m_new)
            pv = _dot(vts[hh], p.astype(BF16))
            ones_row = (1 - hh) * HEAD_DIM
            l_ref[hh] = alpha * l_ref[hh] + pv[ones_row:ones_row + 1, :]
            m_ref[hh] = m_new
            alphas.append(alpha)
            pvs.append(pv)
        acc_ref[...] = (acc_ref[...] * jnp.where(top, alphas[0], alphas[1])
                        + jnp.where(top, pvs[0], pvs[1]))

    scores(0, sa_ref)

    def body(i, carry):
        scores(2 * i + 1, sb_ref)
        softmax_pv(2 * i, sa_ref, False)
        scores(2 * i + 2, sa_ref)
        softmax_pv(2 * i + 1, sb_ref, False)
        return carry

    lax.fori_loop(0, qi, body, 0)
    scores(2 * qi + 1, sb_ref)
    softmax_pv(2 * qi, sa_ref, True)
    softmax_pv(2 * qi + 1, sb_ref, True)

    out_t = acc_ref[...] / jnp.where(top, l_ref[0], l_ref[1])
    o_ref[...] = out_t.T.astype(BF16)


def _fox(qk, vt, c, *, batch, seq, tq, tk):
    n = qk.shape[0]
    nq = seq // tq
    nk = seq // tk
    npair = N_HEADS // 2
    pw = 2 * HEAD_DIM
    assert tq == 2 * tk and vt.shape == (batch, nk, GROUP_W, tk)
    return pl.pallas_call(
        functools.partial(_fox_kernel, tq=tq, tk=tk),
        grid=(batch, npair, nq),
        in_specs=[pl.BlockSpec((tq, pw), lambda b, j, i: (b * nq + i, j)),
                  pl.BlockSpec((seq, pw), lambda b, j, i: (b, npair + j)),
                  pl.BlockSpec((None, nk, pw, tk), lambda b, j, i: (b, 0, j, 0)),
                  pl.BlockSpec((seq, LANES), lambda b, j, i: (b, 0))],
        out_specs=pl.BlockSpec((tq, pw), lambda b, j, i: (b * nq + i, j)),
        out_shape=jax.ShapeDtypeStruct((n, GROUP_W), BF16),
        scratch_shapes=[pltpu.VMEM((pw, tq), F32),
                        pltpu.VMEM((2, 1, tq), F32), pltpu.VMEM((2, 1, tq), F32),
                        pltpu.VMEM((2, seq, pw), BF16),
                        pltpu.VMEM((2, tk, tq), F32), pltpu.VMEM((2, tk, tq), F32)],
        compiler_params=_params("arbitrary", "arbitrary", "arbitrary"),
        name="fox_attention",
    )(qk, qk, vt, c)


def _mixout_kernel(x_ref, pr_ref, yr_ref, yf_ref, a0_ref, aup_ref, gup_ref, ka_ref,
                   rk_ref, gng_ref, gnb_ref, gsum_ref, wr_ref, wf_ref, lng_ref, lnb_ref,
                   o_ref):
    gw = GROUP_W
    r = pr_ref[:, 0:gw]
    k = pr_ref[:, gw:2 * gw]
    v = pr_ref[:, 2 * gw:3 * gw]
    ad = pr_ref[:, 3 * gw + LORA_PAD:3 * gw + 2 * LORA_PAD]
    gd = pr_ref[:, 3 * gw + 2 * LORA_PAD:3 * gw + 3 * LORA_PAD]
    a = _sigmoid(a0_ref[...] + _dot(ad.astype(BF16), aup_ref[...]))
    k_mod = k * (1.0 + (a - 1.0) * ka_ref[...])
    gate = _dot(_sigmoid(gd).astype(BF16), gup_ref[...])
    gsum = gsum_ref[...]

    def group_sum(t):
        return _dot(t.astype(BF16), gsum)

    y = yr_ref[...]
    y_hi = y.astype(BF16)
    y_lo = (y - y_hi.astype(F32)).astype(BF16)
    mean = (_dot(y_hi, gsum) + _dot(y_lo, gsum)) * (1.0 / HEAD_DIM)
    d = y - mean
    var = group_sum(d * d) * (1.0 / HEAD_DIM)
    yn = d * lax.rsqrt(var + GN_EPS) * gng_ref[...] + gnb_ref[...]
    bonus = group_sum(r * k_mod * rk_ref[...])
    y_rwkv = ((yn + bonus * v) * gate).astype(BF16)
    mixed = _dot(y_rwkv, wr_ref[...]) + _dot(yf_ref[...], wf_ref[...])
    o_ref[...] = _layer_norm(ALPHA * x_ref[...] + mixed, lng_ref[...], lnb_ref[...])


def _mixout(x2, pr, yr, yf, a0, aup, gup, k_a, r_k, gn_g, gn_b, w_r, w_f, ln_g, ln_b, *, tm):
    n, d = x2.shape
    ca = pr.shape[1]
    gidx = lax.broadcasted_iota(jnp.int32, (GROUP_W, GROUP_W), 0) // HEAD_DIM
    gsum = (gidx == gidx.T).astype(BF16)
    vec = _full((1, GROUP_W))
    return pl.pallas_call(
        _mixout_kernel,
        grid=(n // tm,),
        in_specs=[pl.BlockSpec((tm, d), lambda i: (i, 0)),
                  pl.BlockSpec((tm, ca), lambda i: (i, 0)),
                  pl.BlockSpec((tm, GROUP_W), lambda i: (i, 0)),
                  pl.BlockSpec((tm, GROUP_W), lambda i: (i, 0)),
                  vec, _full((LORA_PAD, GROUP_W)), _full((LORA_PAD, GROUP_W)),
                  vec, vec, vec, vec, _full((GROUP_W, GROUP_W)),
                  _full((GROUP_W, d)), _full((GROUP_W, d)),
                  _full((1, d)), _full((1, d))],
        out_specs=pl.BlockSpec((tm, d), lambda i: (i, 0)),
        out_shape=jax.ShapeDtypeStruct((n, d), F32),
        compiler_params=_params("parallel"),
        name="mix_out",
    )(x2, pr, yr, yf, a0, aup, gup, k_a, r_k, gn_g, gn_b, gsum, w_r, w_f, ln_g, ln_b)


def _ffn_kernel(x_ref, wg_ref, wu_ref, wd_ref, lng_ref, lnb_ref, o_ref, *, tf):
    xb = x_ref[...].astype(BF16)
    ff = wg_ref.shape[1]
    acc = None
    for f0 in range(0, ff, tf):
        f1 = min(f0 + tf, ff)
        g = _dot(xb, wg_ref[:, f0:f1])
        u = _dot(xb, wu_ref[:, f0:f1])
        h = (g * _sigmoid(g) * u).astype(BF16)
        part = _dot(h, wd_ref[f0:f1, :])
        acc = part if acc is None else acc + part
    o_ref[...] = _layer_norm(ALPHA * x_ref[...] + acc, lng_ref[...], lnb_ref[...])


def _ffn(x2, wg, wu, wd, ln_g, ln_b, *, tm, tf):
    n, d = x2.shape
    ff = wg.shape[1]
    once = pl.Buffered(1)

    def resident(shape):
        return pl.BlockSpec(shape, lambda i: (0, 0), pipeline_mode=once)

    return pl.pallas_call(
        functools.partial(_ffn_kernel, tf=tf),
        grid=(n // tm,),
        in_specs=[pl.BlockSpec((tm, d), lambda i: (i, 0)),
                  resident((d, ff)), resident((d, ff)), resident((ff, d)),
                  _full((1, d)), _full((1, d))],
        out_specs=pl.BlockSpec((tm, d), lambda i: (i, 0)),
        out_shape=jax.ShapeDtypeStruct((n, d), F32),
        compiler_params=_params("parallel"),
        name="ffn_swiglu",
    )(x2, wg, wu, wd, ln_g, ln_b)


def _glu_kernel(x_ref, w_ref, b_ref, o_ref):
    d = o_ref.shape[1]
    xb = x_ref[...].astype(BF16)
    val = _dot(xb, w_ref[:, 0:d]) + b_ref[:, 0:d]
    gat = _dot(xb, w_ref[:, d:2 * d]) + b_ref[:, d:2 * d]
    o_ref[...] = val * _sigmoid(gat)


def _glu(x2, w, b, *, tm):
    n, d = x2.shape
    return pl.pallas_call(
        _glu_kernel,
        grid=(n // tm,),
        in_specs=[pl.BlockSpec((tm, d), lambda i: (i, 0)), _full((d, 2 * d)), _full((1, 2 * d))],
        out_specs=pl.BlockSpec((tm, d), lambda i: (i, 0)),
        out_shape=jax.ShapeDtypeStruct((n, d), F32),
        compiler_params=_params("parallel"),
        name="conv_glu",
    )(x2, w, b)


def _top2(logits):
    lane = lax.broadcasted_iota(jnp.int32, logits.shape, 1).astype(F32)
    lg = jnp.where(lane < N_EXPERTS, logits, NEG_BIG)
    m1 = jnp.max(lg, axis=-1, keepdims=True)
    i1 = jnp.min(jnp.where(lg == m1, lane, float(LANES)), axis=-1, keepdims=True)
    lg2 = jnp.where(lane == i1, NEG_BIG, lg)
    m2 = jnp.max(lg2, axis=-1, keepdims=True)
    i2 = jnp.min(jnp.where(lg2 == m2, lane, float(LANES)), axis=-1, keepdims=True)
    e2 = jnp.exp(m2 - m1)
    w1 = 1.0 / (1.0 + e2)
    w2 = e2 / (1.0 + e2)
    return lane, i1, i2, w1, w2


def _conv_kernel(hc_ref, hp_ref, x_ref, wdw_ref, bdw_ref, lng_ref, lnb_ref, w2_ref, b2_ref,
                 pg_ref, pb_ref, wr_ref, tri_ref, x3_ref, route_ref, route_t_ref, tot_ref,
                 ext_ref, cv_ref, cnt_ref, *, tiles_per_seq):
    tm, d = x_ref.shape

    @pl.when(pl.program_id(0) == 0)
    def _():
        cnt_ref[...] = jnp.zeros_like(cnt_ref)

    first = pl.program_id(0) % tiles_per_seq == 0
    ext_ref[0, 0:CONV_HALO, :] = jnp.where(first, 0.0, hp_ref[...])
    ext_ref[0, CONV_HALO:CONV_HALO + tm, :] = hc_ref[...]
    nrows = tm + CONV_HALO
    for c0 in range(0, d, 256):
        base = ext_ref[0, :, c0:c0 + 256]
        for j in range(1, SUBLANES):
            ext_ref[j, :, c0:c0 + 256] = pltpu.roll(base, nrows - j, 0)
    off = CONV_HALO - (CONV_WIDTH - 1)
    rc, cc = 64, 256
    for r0 in range(0, tm, rc):
        for c0 in range(0, d, cc):
            acc = jnp.broadcast_to(bdw_ref[:, c0:c0 + cc], (rc, cc))
            for t in range(CONV_WIDTH):
                base, j = divmod(off + t, SUBLANES)
                rs = r0 + base * SUBLANES
                acc = acc + wdw_ref[t:t + 1, c0:c0 + cc] * ext_ref[j, rs:rs + rc, c0:c0 + cc]
            cv_ref[r0:r0 + rc, c0:c0 + cc] = acc
    hn = _layer_norm(cv_ref[...], lng_ref[...], lnb_ref[...])
    hs = (hn * _sigmoid(hn)).astype(BF16)
    conv = _dot(hs, w2_ref[...]) + b2_ref[...]
    x3 = _layer_norm(ALPHA * x_ref[...] + conv, pg_ref[...], pb_ref[...])
    x3_ref[...] = x3
    x_hi = x3.astype(BF16)
    x_lo = (x3 - x_hi.astype(F32)).astype(BF16)
    hi_part = _dot(x_hi, wr_ref[...])
    logits = hi_part[:, :LANES] + hi_part[:, LANES:] + _dot(x_lo, wr_ref[:, 0:LANES])
    lane, i1, i2, w1, w2 = _top2(logits)
    first, second = lane == i1, lane == i2
    sel = jnp.where(jnp.logical_or(first, second), 1.0, 0.0)
    pos = _dot(tri_ref[...], sel.astype(BF16)) + cnt_ref[...]
    cnt_ref[...] += jnp.sum(sel, axis=0, keepdims=True)
    tot_ref[...] = jnp.broadcast_to(cnt_ref[...], tot_ref.shape)
    rank1 = jnp.sum(jnp.where(first, pos, 0.0), axis=-1, keepdims=True)
    rank2 = jnp.sum(jnp.where(second, pos, 0.0), axis=-1, keepdims=True)
    fields = (i1, i2, w1, w2, rank1, rank2)
    record = jnp.zeros_like(logits)
    for k, field in enumerate(fields):
        record = jnp.where(lane == float(k), field, record)
    route_ref[...] = record
    route_t_ref[...] = record.T[:SUBLANES, :]


def _conv(hg, x2, w_dw, b_dw, ln_g, ln_b, w2, b2, pg, pb, w_router, *, seq, tm):
    n, d = x2.shape
    ratio = tm // CONV_HALO
    vec = _full((1, d))
    tri = (lax.broadcasted_iota(jnp.int32, (tm, tm), 1)
           < lax.broadcasted_iota(jnp.int32, (tm, tm), 0)).astype(BF16)
    return pl.pallas_call(
        functools.partial(_conv_kernel, tiles_per_seq=seq // tm),
        grid=(n // tm,),
        in_specs=[pl.BlockSpec((tm, d), lambda i: (i, 0)),
                  pl.BlockSpec((CONV_HALO, d), lambda i: (jnp.maximum(i * ratio - 1, 0), 0)),
                  pl.BlockSpec((tm, d), lambda i: (i, 0)),
                  _full((CONV_HALO, d)), vec, vec, vec, _full((d, d)), vec, vec, vec,
                  _full((d, 2 * LANES)), _full((tm, tm))],
        out_specs=[pl.BlockSpec((tm, d), lambda i: (i, 0)),
                   pl.BlockSpec((tm, LANES), lambda i: (i, 0)),
                   pl.BlockSpec((SUBLANES, tm), lambda i: (0, i)),
                   _full((SUBLANES, LANES))],
        out_shape=[jax.ShapeDtypeStruct((n, d), F32),
                   jax.ShapeDtypeStruct((n, LANES), F32),
                   jax.ShapeDtypeStruct((SUBLANES, n), F32),
                   jax.ShapeDtypeStruct((SUBLANES, LANES), F32)],
        scratch_shapes=[pltpu.VMEM((SUBLANES, tm + CONV_HALO, d), F32), pltpu.VMEM((tm, d), F32),
                        pltpu.VMEM((1, LANES), F32)],
        compiler_params=_params("arbitrary"),
        name="conv_module",
    )(hg, hg, x2, w_dw, b_dw, ln_g, ln_b, w2, b2, pg, pb, w_router, tri)


MOE_TILE = 512
SC_CORES = 2
SC_SUBCORES = 16
SC_CHUNK = 64


def _sc_row_scatter(x, slot_a, slot_b, slots):
    n, d = x.shape
    per_worker = n // (SC_CORES * SC_SUBCORES)
    mesh = plsc.VectorSubcoreMesh(core_axis_name="c", subcore_axis_name="s",
                                  num_cores=SC_CORES, num_subcores=SC_SUBCORES)

    @functools.partial(
        pl.kernel, mesh=mesh,
        out_type=jax.ShapeDtypeStruct((slots, d), x.dtype),
        scratch_types=[pltpu.VMEM((SC_CHUNK,), jnp.int32),
                       pltpu.VMEM((SC_CHUNK,), jnp.int32),
                       pltpu.VMEM((SC_CHUNK, d), x.dtype),
                       pltpu.SemaphoreType.DMA],
        name="moe_sc_row_scatter")
    def scatter(x_hbm, a_hbm, b_hbm, out_hbm, a_v, b_v, rows_v, sem):
        worker = lax.axis_index("s") * SC_CORES + lax.axis_index("c")
        base = worker * per_worker

        @pl.loop(0, per_worker // SC_CHUNK)
        def _(ci):
            off = base + ci * SC_CHUNK
            pltpu.sync_copy(a_hbm.at[pl.ds(off, SC_CHUNK)], a_v)
            pltpu.sync_copy(b_hbm.at[pl.ds(off, SC_CHUNK)], b_v)
            pltpu.sync_copy(x_hbm.at[pl.ds(off, SC_CHUNK)], rows_v)
            pltpu.async_copy(rows_v, out_hbm.at[a_v], sem).wait()
            pltpu.async_copy(rows_v, out_hbm.at[b_v], sem).wait()

    return scatter(x, slot_a, slot_b)


def _sc_row_gather(table, idx):
    rows = idx.shape[0]
    d = table.shape[1]
    per_worker = rows // (SC_CORES * SC_SUBCORES)
    mesh = plsc.VectorSubcoreMesh(core_axis_name="c", subcore_axis_name="s",
                                  num_cores=SC_CORES, num_subcores=SC_SUBCORES)

    @functools.partial(
        pl.kernel, mesh=mesh,
        out_type=jax.ShapeDtypeStruct((rows, d), table.dtype),
        scratch_types=[pltpu.VMEM((SC_CHUNK,), jnp.int32),
                       pltpu.VMEM((SC_CHUNK, d), table.dtype),
                       pltpu.SemaphoreType.DMA],
        name="moe_sc_row_gather")
    def gather(table_hbm, idx_hbm, out_hbm, idx_v, rows_v, sem):
        worker = lax.axis_index("s") * SC_CORES + lax.axis_index("c")
        base = worker * per_worker

        @pl.loop(0, per_worker // SC_CHUNK)
        def _(ci):
            off = base + ci * SC_CHUNK
            pltpu.sync_copy(idx_hbm.at[pl.ds(off, SC_CHUNK)], idx_v)
            pltpu.async_copy(table_hbm.at[idx_v], rows_v, sem).wait()
            pltpu.sync_copy(rows_v, out_hbm.at[pl.ds(off, SC_CHUNK)])

    return gather(table, idx)


def _pack_bf16_pairs(y):
    half = y.shape[1] // 2
    hi = lax.bitcast_convert_type(y[:, :half].astype(BF16).astype(F32), jnp.uint32)
    lo = lax.bitcast_convert_type(y[:, half:].astype(BF16).astype(F32), jnp.uint32)
    return lax.bitcast_convert_type(hi | (lo >> 16), F32)


def _unpack_bf16_pairs(w):
    bits = lax.bitcast_convert_type(w, jnp.uint32)
    hi = lax.bitcast_convert_type(bits & jnp.uint32(0xFFFF0000), F32)
    lo = lax.bitcast_convert_type(bits << 16, F32)
    return jnp.concatenate([hi, lo], axis=1)


def _expert_ffn_kernel(expert_ref, used_ref, x_ref, wg_ref, wu_ref, wd_ref, o_ref, acc_ref):
    i = pl.program_id(0)
    f = pl.program_id(1)

    @pl.when(f == 0)
    def _():
        acc_ref[...] = jnp.zeros_like(acc_ref)

    @pl.when(used_ref[i] > 0)
    def _():
        xb = x_ref[...].astype(BF16)
        g = _dot(xb, wg_ref[...])
        u = _dot(xb, wu_ref[...])
        h = (g * _sigmoid(g) * u).astype(BF16)
        acc_ref[...] += _dot(h, wd_ref[...])

    @pl.when(f == pl.num_programs(1) - 1)
    def _():
        o_ref[...] = _pack_bf16_pairs(acc_ref[...])


def _expert_ffn(xs, tile_expert, tile_used, wg, wu, wd, *, tf):
    slots, d = xs.shape
    ff = wg.shape[2]
    return pl.pallas_call(
        _expert_ffn_kernel,
        grid_spec=pltpu.PrefetchScalarGridSpec(
            num_scalar_prefetch=2, grid=(slots // MOE_TILE, ff // tf),
            in_specs=[pl.BlockSpec((MOE_TILE, d), lambda i, f, te, tu: (i, 0)),
                      pl.BlockSpec((None, d, tf), lambda i, f, te, tu: (te[i], 0, f)),
                      pl.BlockSpec((None, d, tf), lambda i, f, te, tu: (te[i], 0, f)),
                      pl.BlockSpec((None, tf, d), lambda i, f, te, tu: (te[i], f, 0))],
            out_specs=pl.BlockSpec((MOE_TILE, d // 2), lambda i, f, te, tu: (i, 0)),
            scratch_shapes=[pltpu.VMEM((MOE_TILE, d), F32)]),
        out_shape=jax.ShapeDtypeStruct((slots, d // 2), F32),
        compiler_params=_params("arbitrary", "arbitrary"),
        name="moe_expert_ffn",
    )(tile_expert, tile_used, xs, wg, wu, wd)


def _combine_kernel(x_ref, y1_ref, y2_ref, route_ref, lng_ref, lnb_ref, o_ref):
    route = route_ref[...]
    moe = (route[:, 2:3] * _unpack_bf16_pairs(y1_ref[...])
           + route[:, 3:4] * _unpack_bf16_pairs(y2_ref[...]))
    o_ref[...] = _layer_norm(ALPHA * x_ref[...] + moe, lng_ref[...], lnb_ref[...])


def _combine(x3, yt, route, ln_g, ln_b, *, tm):
    n, d = x3.shape
    nt = n // tm
    return pl.pallas_call(
        _combine_kernel,
        grid=(nt,),
        in_specs=[pl.BlockSpec((tm, d), lambda i: (i, 0)),
                  pl.BlockSpec((tm, d // 2), lambda i: (i, 0)),
                  pl.BlockSpec((tm, d // 2), lambda i: (i + nt, 0)),
                  pl.BlockSpec((tm, LANES), lambda i: (i, 0)),
                  _full((1, d)), _full((1, d))],
        out_specs=pl.BlockSpec((tm, d), lambda i: (i, 0)),
        out_shape=jax.ShapeDtypeStruct((n, d), F32),
        compiler_params=_params("parallel"),
        name="moe_combine",
    )(x3, yt, yt, route, ln_g, ln_b)


def _moe(x3, route, route_t, tot, wg, wu, wd, ln_g, ln_b, *, tm, tf):
    n, d = x3.shape
    ne = wg.shape[0]
    slots = TOP_K * n + ne * MOE_TILE
    count = tot[0, :ne].astype(jnp.int32)
    cap = (count + (MOE_TILE - 1)) // MOE_TILE * MOE_TILE
    ends = jnp.cumsum(cap)
    off = ends - cap
    e1 = route_t[0].astype(jnp.int32)
    e2 = route_t[1].astype(jnp.int32)
    slot1 = off[e1] + route_t[4].astype(jnp.int32)
    slot2 = off[e2] + route_t[5].astype(jnp.int32)
    tile_start = jnp.arange(slots // MOE_TILE, dtype=jnp.int32) * MOE_TILE
    tile_expert = jnp.minimum(jnp.sum(tile_start[:, None] >= ends[None, :], axis=1),
                              ne - 1).astype(jnp.int32)
    tile_used = (tile_start < ends[-1]).astype(jnp.int32)

    xs = _sc_row_scatter(x3, slot1, slot2, slots)
    ys = _expert_ffn(xs, tile_expert, tile_used, wg, wu, wd, tf=tf)
    yt = _sc_row_gather(ys, jnp.concatenate([slot1, slot2]))
    return _combine(x3, yt, route, ln_g, ln_b, tm=tm)


def _pad_cols(w, width):
    return jnp.pad(w, ((0, 0), (0, width - w.shape[1])))


def _pad_rows(w, height):
    return jnp.pad(w, ((0, height - w.shape[0]), (0, 0)))


def _forward(x, mix_w_in, rwkv_mu, rwkv_w0, rwkv_w_up, rwkv_a0, rwkv_a_up, rwkv_g_up,
             rwkv_k_k, rwkv_k_a, rwkv_r_k, rwkv_gn_g, rwkv_gn_b, fox_b_f, mix_w_out,
             mix_ln_g, mix_ln_b, ffn_w_gate, ffn_w_up, ffn_w_down, ffn_ln_g, ffn_ln_b,
             conv_w_pw1, conv_b_pw1, conv_w_dw, conv_b_dw, conv_ln_g, conv_ln_b,
             conv_w_pw2, conv_b_pw2, conv_post_ln_g, conv_post_ln_b,
             moe_w_router, moe_w_gate, moe_w_up, moe_w_down, moe_ln_g, moe_ln_b,
             *, tm=512, chunk=64, nb_rwkv=4, tk_fox=512, tf_ffn=1536, tf_moe=1792):
    batch, seq, d = x.shape
    n = batch * seq
    gw = GROUP_W
    x2 = x.reshape(n, d)
    row = lambda t: t.reshape(1, -1)

    w_in = mix_w_in[0]
    mu = rwkv_mu[0]
    o_w, o_a, o_g = 3 * gw, 3 * gw + DECAY_LORA, 3 * gw + DECAY_LORA + AAA_LORA
    o_fox = o_g + GATE_LORA

    def lora_layout(t):
        return jnp.concatenate([t[..., :o_w],
                                _pad_cols(t[..., o_w:o_a], LORA_PAD),
                                _pad_cols(t[..., o_a:o_g], LORA_PAD),
                                _pad_cols(t[..., o_g:o_fox], LORA_PAD)], axis=-1)

    wa = lora_layout(w_in).astype(BF16)
    mu_a = lora_layout(row(mu))
    scale = LOG2E / math.sqrt(HEAD_DIM)
    wb = jnp.concatenate([w_in[:, o_fox:o_fox + gw] * scale,
                          w_in[:, o_fox + gw:o_fox + 3 * gw]], axis=1).astype(BF16)
    wf = _pad_cols(w_in[:, o_fox + 3 * gw:], LANES).astype(BF16)
    bf = _pad_cols(row(fox_b_f[0]), LANES)

    pr, qk, vt, c = _inproj(x2, wa, wb, wf, mu_a, bf, seq=seq, tm=tk_fox)

    wup = _pad_rows(rwkv_w_up[0], LORA_PAD).astype(BF16)
    aup = _pad_rows(rwkv_a_up[0], LORA_PAD).astype(BF16)
    gup = _pad_rows(rwkv_g_up[0], LORA_PAD).astype(BF16)
    k_k, k_a, r_k = row(rwkv_k_k[0]), row(rwkv_k_a[0]), row(rwkv_r_k[0])
    yr = _rwkv(pr, row(rwkv_w0[0]), wup, row(rwkv_a0[0]), aup, k_k, k_a,
               batch=batch, seq=seq, chunk=chunk, nb=nb_rwkv)

    yf = _fox(qk, vt, c, batch=batch, seq=seq, tq=2 * tk_fox, tk=tk_fox)

    w_out = mix_w_out[0].astype(BF16)
    x1 = _mixout(x2, pr, yr, yf, row(rwkv_a0[0]), aup, gup, k_a, r_k,
                 row(rwkv_gn_g[0]), row(rwkv_gn_b[0]), w_out[:gw], w_out[gw:],
                 row(mix_ln_g[0]), row(mix_ln_b[0]), tm=tm)
    x2b = _ffn(x1, ffn_w_gate[0].astype(BF16), ffn_w_up[0].astype(BF16),
               ffn_w_down[0].astype(BF16), row(ffn_ln_g[0]), row(ffn_ln_b[0]), tm=tm, tf=tf_ffn)

    hg = _glu(x2b, conv_w_pw1[0].astype(BF16), row(conv_b_pw1[0]), tm=tm)
    w_router = _pad_cols(moe_w_router[0], LANES)
    wr_hi = w_router.astype(BF16)
    w_router = jnp.concatenate([wr_hi, (w_router - wr_hi.astype(F32)).astype(BF16)], axis=1)
    x3, route, route_t, tot = _conv(hg, x2b, _pad_rows(conv_w_dw[0], CONV_HALO), row(conv_b_dw[0]),
                                row(conv_ln_g[0]), row(conv_ln_b[0]),
                                conv_w_pw2[0].astype(BF16), row(conv_b_pw2[0]),
                                row(conv_post_ln_g[0]), row(conv_post_ln_b[0]),
                                w_router, seq=seq, tm=tm)
    out = _moe(x3, route, route_t, tot, moe_w_gate[0].astype(BF16), moe_w_up[0].astype(BF16),
               moe_w_down[0].astype(BF16), row(moe_ln_g[0]), row(moe_ln_b[0]),
               tm=tm, tf=tf_moe)
    return out.reshape(batch, seq, d)


def kernel(x, mix_w_in, rwkv_mu, rwkv_w0, rwkv_w_up, rwkv_a0, rwkv_a_up, rwkv_g_up, rwkv_k_k, rwkv_k_a, rwkv_r_k, rwkv_gn_g, rwkv_gn_b, fox_b_f, mix_w_out, mix_ln_g, mix_ln_b, ffn_w_gate, ffn_w_up, ffn_w_down, ffn_ln_g, ffn_ln_b, conv_w_pw1, conv_b_pw1, conv_w_dw, conv_b_dw, conv_ln_g, conv_ln_b, conv_w_pw2, conv_b_pw2, conv_post_ln_g, conv_post_ln_b, moe_w_router, moe_w_gate, moe_w_up, moe_w_down, moe_ln_g, moe_ln_b):
    return _forward(x, mix_w_in, rwkv_mu, rwkv_w0, rwkv_w_up, rwkv_a0, rwkv_a_up, rwkv_g_up,
                    rwkv_k_k, rwkv_k_a, rwkv_r_k, rwkv_gn_g, rwkv_gn_b, fox_b_f, mix_w_out,
                    mix_ln_g, mix_ln_b, ffn_w_gate, ffn_w_up, ffn_w_down, ffn_ln_g, ffn_ln_b,
                    conv_w_pw1, conv_b_pw1, conv_w_dw, conv_b_dw, conv_ln_g, conv_ln_b,
                    conv_w_pw2, conv_b_pw2, conv_post_ln_g, conv_post_ln_b,
                    moe_w_router, moe_w_gate, moe_w_up, moe_w_down, moe_ln_g, moe_ln_b)
```
